```python
import jax, jax.numpy as jnp
from jax import lax
import numpy as np

D_MODEL = 1024
BATCH = 4
SEQ = 4096
DEPTH = 1

GRID_W = 64
HEAD_DIM = 64
N_NA_HEADS = D_MODEL // 128
NA_WIDTH = N_NA_HEADS * HEAD_DIM
WIN_H = 8
WIN_W = 16
N_FOURIER_GROUPS = 4
FOURIER_WIDTH = D_MODEL // 2
FOURIER_GROUP_DIM = FOURIER_WIDTH // N_FOURIER_GROUPS
IN_WIDTH = 3 * NA_WIDTH + FOURIER_WIDTH + 2 * D_MODEL
D_FF = 4 * D_MODEL
N_MOD = 6
EPS = 1e-6

kernel_name = "hybrid_natten_fnet_gated_block"


def _rmsnorm(x, w):
    xf = x.astype(jnp.float32)
    y = xf * lax.rsqrt(jnp.mean(xf * xf, axis=-1, keepdims=True) + EPS)
    return (y * w.astype(jnp.float32)).astype(x.dtype)


def _neighbourhood_attention(q, k, v, rpb):
    b, s, h, dh = q.shape
    rows = s // GRID_W
    kh = min(WIN_H, rows)
    kw = min(WIN_W, GRID_W)
    q = q.reshape(b, rows, GRID_W, h, dh)
    k = k.reshape(b, rows, GRID_W, h, dh)
    v = v.reshape(b, rows, GRID_W, h, dh)
    cols = np.arange(GRID_W)
    col_start = np.clip(cols - kw // 2, 0, GRID_W - kw)
    col_idx = col_start[:, None] + np.arange(kw)[None, :]
    col_bias_idx = col_idx - cols[:, None] + (WIN_W - 1)
    scale = dh ** -0.5

    def row_block(r):
        rs = jnp.clip(r - kh // 2, 0, rows - kh)
        k_rows = lax.dynamic_slice_in_dim(k, rs, kh, axis=1)
        v_rows = lax.dynamic_slice_in_dim(v, rs, kh, axis=1)
        k_g = k_rows[:, :, col_idx]
        v_g = v_rows[:, :, col_idx]
        q_r = lax.dynamic_index_in_dim(q, r, axis=1, keepdims=False)
        scores = jnp.einsum('bqhd,biqjhd->bhqij', q_r, k_g).astype(jnp.float32) * scale
        row_bias_idx = rs + jnp.arange(kh) - r + (WIN_H - 1)
        bias = rpb[:, row_bias_idx[None, :, None], col_bias_idx[:, None, :]]
        scores = scores + bias.astype(jnp.float32)[None]
        p = jax.nn.softmax(scores.reshape(b, h, GRID_W, kh * kw), axis=-1)
        p = p.reshape(b, h, GRID_W, kh, kw).astype(v.dtype)
        return jnp.einsum('bhqij,biqjhd->bqhd', p, v_g)

    out = lax.map(row_block, jnp.arange(rows))
    return out.transpose(1, 0, 2, 3, 4).reshape(b, s, h * dh)


def _fourier_mix(u):
    b, s, _ = u.shape
    uf = u.astype(jnp.float32).reshape(b, s, N_FOURIER_GROUPS, FOURIER_GROUP_DIM)
    y = jnp.fft.fft2(uf, axes=(1, 3), norm="ortho").real
    return y.reshape(b, s, FOURIER_WIDTH).astype(u.dtype)


def setup_inputs(seed: int = 0) -> dict:
    key = jax.random.key(seed)
    ks = jax.random.split(key, 18)
    f32 = jnp.float32

    def nrm(k, shape, fan_in):
        return jax.random.normal(k, shape, f32) * (fan_in ** -0.5)

    L = DEPTH
    return {
        "x": jax.random.normal(ks[0], (BATCH, SEQ, D_MODEL), f32),
        "c": jax.random.normal(ks[1], (BATCH, D_MODEL), f32),
        "norm1_w": 1.0 + 0.05 * jax.random.normal(ks[2], (L, D_MODEL), f32),
        "norm2_w": 1.0 + 0.05 * jax.random.normal(ks[3], (L, D_MODEL), f32),
        "w_ada": nrm(ks[4], (L, D_MODEL, N_MOD * D_MODEL), D_MODEL) * 0.5,
        "b_ada": 0.02 * jax.random.normal(ks[5], (L, N_MOD * D_MODEL), f32),
        "w_in": nrm(ks[6], (L, D_MODEL, IN_WIDTH), D_MODEL),
        "b_in": 0.02 * jax.random.normal(ks[7], (L, IN_WIDTH), f32),
        "q_norm_w": 1.0 + 0.05 * jax.random.normal(ks[8], (L, HEAD_DIM), f32),
        "k_norm_w": 1.0 + 0.05 * jax.random.normal(ks[9], (L, HEAD_DIM), f32),
        "rpb": 0.1 * jax.random.normal(ks[10], (L, N_NA_HEADS, 2 * WIN_H - 1, 2 * WIN_W - 1), f32),
        "w_na_out": nrm(ks[11], (L, NA_WIDTH, D_MODEL), NA_WIDTH),
        "w_fn_out": nrm(ks[12], (L, FOURIER_WIDTH, D_MODEL), FOURIER_WIDTH),
        "b_fn_out": 0.02 * jax.random.normal(ks[13], (L, D_MODEL), f32),
        "w_o": nrm(ks[14], (L, D_MODEL, D_MODEL), D_MODEL),
        "w_mlp_in": nrm(ks[15], (L, D_MODEL, D_FF), D_MODEL),
        "w_mlp_out": nrm(ks[16], (L, D_FF, D_MODEL), D_FF),
    }


def reference(x, c, norm1_w, norm2_w, w_ada, b_ada, w_in, b_in, q_norm_w, k_norm_w, rpb,
              w_na_out, w_fn_out, b_fn_out, w_o, w_mlp_in, w_mlp_out):
    b, s, d = x.shape
    split_pts = [NA_WIDTH, 2 * NA_WIDTH, 3 * NA_WIDTH, 3 * NA_WIDTH + FOURIER_WIDTH,
                 3 * NA_WIDTH + FOURIER_WIDTH + D_MODEL]
    for l in range(DEPTH):
        mod = jax.nn.silu(c) @ w_ada[l] + b_ada[l]
        shift1, scale1, gate1, shift2, scale2, gate2 = [m[:, None, :] for m in jnp.split(mod, N_MOD, axis=-1)]

        h = _rmsnorm(x, norm1_w[l]) * (1.0 + scale1) + shift1
        z = h @ w_in[l] + b_in[l]
        q, k, v, u, g_a, g_b = jnp.split(z, split_pts, axis=-1)
        q = _rmsnorm(q.reshape(b, s, N_NA_HEADS, HEAD_DIM), q_norm_w[l])
        k = _rmsnorm(k.reshape(b, s, N_NA_HEADS, HEAD_DIM), k_norm_w[l])
        v = v.reshape(b, s, N_NA_HEADS, HEAD_DIM)
        y_na = _neighbourhood_attention(q, k, v, rpb[l]) @ w_na_out[l]
        y_fn = _fourier_mix(u) @ w_fn_out[l] + b_fn_out[l]
        merged = jax.nn.sigmoid(g_a) * y_na + jax.nn.sigmoid(g_b) * y_fn
        x = x + gate1 * (merged @ w_o[l])

        h2 = _rmsnorm(x, norm2_w[l]) * (1.0 + scale2) + shift2
        x = x + gate2 * (jnp.square(jax.nn.relu(h2 @ w_mlp_in[l])) @ w_mlp_out[l])
    return x
```

```python
import functools

import numpy as np
import jax
import jax.numpy as jnp
from jax import lax
from jax.experimental import pallas as pl
from jax.experimental.pallas import tpu as pltpu

F32 = jnp.float32
BF16 = jnp.bfloat16

GRID_W = 64
HEAD_DIM = 64
WIN_H = 8
WIN_W = 16
N_FOURIER_GROUPS = 4
N_MOD = 6
EPS = 1e-6
MASK_VALUE = -1e30

SUBLANES = 8
LANES = 128
VMEM_LIMIT_BYTES = 56 * 1024 * 1024


def _dot(a, b):
    return jnp.dot(a, b, preferred_element_type=F32)


def _sigmoid(x):
    return 1.0 / (1.0 + jnp.exp(-x))


def _params(*semantics):
    return pltpu.CompilerParams(dimension_semantics=semantics, vmem_limit_bytes=VMEM_LIMIT_BYTES)


def _ada_kernel(c_ref, w_ref, b_ref, o_ref):
    c = c_ref[...]
    s = c * _sigmoid(c)
    o_ref[...] = jnp.dot(s, w_ref[...], preferred_element_type=F32,
                         precision=lax.Precision.HIGHEST) + b_ref[...]


def _ada(c, w, b):
    bsz, d = c.shape
    n = w.shape[1]
    rows = -(-bsz // SUBLANES) * SUBLANES
    c_pad = jnp.pad(c, ((0, rows - bsz), (0, 0)))
    tn = n // N_MOD
    out = pl.pallas_call(
        _ada_kernel,
        grid=(n // tn,),
        in_specs=[pl.BlockSpec((rows, d), lambda j: (0, 0)),
                  pl.BlockSpec((d, tn), lambda j: (0, j)),
                  pl.BlockSpec((1, tn), lambda j: (0, j))],
        out_specs=pl.BlockSpec((rows, tn), lambda j: (0, j)),
        out_shape=jax.ShapeDtypeStruct((rows, n), F32),
        compiler_params=_params("parallel"),
        name="ada",
    )(c_pad, w, b.reshape(1, n))
    return out[:bsz]


def _inproj_kernel(x_ref, sc_ref, sh_ref, nw_ref, w_ref, b_ref, qkw_ref, gsum_ref, gexp_ref,
                   q_ref, k_ref, v_ref, u_ref, ga_ref, gb_ref, *, na_w, f_w, d):
    x = x_ref[...]
    ms = jnp.mean(x * x, axis=-1, keepdims=True)
    y = x * lax.rsqrt(ms + EPS) * nw_ref[...]
    h = (y * (1.0 + sc_ref[...]) + sh_ref[...]).astype(BF16)

    qk = 2 * na_w
    zqk = _dot(h, w_ref[:, 0:qk]) + b_ref[:, 0:qk]
    ssq = _dot((zqk * zqk).astype(BF16), gsum_ref[...])
    rinv = lax.rsqrt(ssq * (1.0 / HEAD_DIM) + EPS)
    rinv_hi = rinv.astype(BF16)
    rinv_lo = (rinv - rinv_hi.astype(F32)).astype(BF16)
    scale = _dot(rinv_hi, gexp_ref[...]) + _dot(rinv_lo, gexp_ref[...])
    zn = zqk * scale * qkw_ref[...]
    q_ref[...] = zn[:, 0:na_w].astype(BF16)
    k_ref[...] = zn[:, na_w:qk].astype(BF16)

    o = qk
    v_ref[...] = (_dot(h, w_ref[:, o:o + na_w]) + b_ref[:, o:o + na_w]).astype(BF16)
    o += na_w
    u_ref[...] = _dot(h, w_ref[:, o:o + f_w]) + b_ref[:, o:o + f_w]
    o += f_w
    ga_ref[...] = _sigmoid(_dot(h, w_ref[:, o:o + d]) + b_ref[:, o:o + d]).astype(BF16)
    o += d
    gb_ref[...] = _sigmoid(_dot(h, w_ref[:, o:o + d]) + b_ref[:, o:o + d]).astype(BF16)


def _inproj(x, scale1, shift1, norm_w, w_in, b_in, qk_w, na_w, f_w):
    bsz, s, d = x.shape
    in_w = w_in.shape[1]
    tm = 512
    qk = 2 * na_w
    n_grp = qk // HEAD_DIM
    grp = np.arange(qk) // HEAD_DIM
    gsum = jnp.asarray((grp[:, None] == np.arange(LANES)[None, :]).astype(np.float32), BF16)
    gexp = jnp.asarray((np.arange(LANES)[:, None] == grp[None, :]).astype(np.float32), BF16)
    assert n_grp <= LANES

    tok = lambda w: pl.BlockSpec((None, tm, w), lambda b, i: (b, i, 0))
    per_batch = pl.BlockSpec((None, 1, d), lambda b, i: (b, 0, 0))
    full = lambda a: pl.BlockSpec(a.shape, lambda b, i: (0,) * a.ndim)
    nw = norm_w.reshape(1, d)
    bi = b_in.reshape(1, in_w)
    out_shapes = (
        jax.ShapeDtypeStruct((bsz, s, na_w), BF16),
        jax.ShapeDtypeStruct((bsz, s, na_w), BF16),
        jax.ShapeDtypeStruct((bsz, s, na_w), BF16),
        jax.ShapeDtypeStruct((bsz, s, f_w), F32),
        jax.ShapeDtypeStruct((bsz, s, d), BF16),
        jax.ShapeDtypeStruct((bsz, s, d), BF16),
    )
    return pl.pallas_call(
        functools.partial(_inproj_kernel, na_w=na_w, f_w=f_w, d=d),
        grid=(bsz, s // tm),
        in_specs=[tok(d), per_batch, per_batch, full(nw), full(w_in), full(bi), full(qk_w),
                  full(gsum), full(gexp)],
        out_specs=(tok(na_w), tok(na_w), tok(na_w), tok(f_w), tok(d), tok(d)),
        out_shape=out_shapes,
        compiler_params=_params("parallel", "parallel"),
        name="inproj",
    )(x, scale1, shift1, nw, w_in, bi, qk_w, gsum, gexp)


def _fourier_constants(rows, f_w):
    n = rows
    r8 = SUBLANES
    k = np.arange(n)
    ang_a = 2.0 * np.pi * np.outer(k, k) / n
    eye8 = np.eye(r8)
    norm = 1.0 / np.sqrt(n)
    fa_re = np.kron(np.cos(ang_a), eye8) * norm
    fa_im = np.kron(-np.sin(ang_a), eye8) * norm
    fa = np.concatenate([fa_re, fa_im], axis=0)

    nblk = n // r8
    s2 = k[None, :]
    s2p = k[:, None]
    gr = np.zeros((n, r8, r8, n))
    gi = np.zeros((n, r8, r8, n))
    for l in range(r8):
        ang = 2.0 * np.pi * (s2 * l / (n * GRID_W) + s2 * s2p / n)
        gr[:, l, l, :] = np.cos(ang) * norm
        gi[:, l, l, :] = -np.sin(ang) * norm
    gr = gr.reshape(n * r8, r8 * n)
    gi = gi.reshape(n * r8, r8 * n)
    g = np.block([[gr, -gi], [gi, gr]])

    t_idx = np.arange(nblk)[:, None, None]
    blk_of_row = (k // r8)[None, :, None]
    j_idx = np.arange(r8)[None, None, :]
    ang_t = 2.0 * np.pi * (t_idx * r8 + j_idx) * blk_of_row * r8 / (n * GRID_W)
    ang_t = ang_t.reshape(nblk, n * r8, 1)
    tw_cos = np.broadcast_to(np.cos(ang_t), (nblk, n * r8, LANES))
    tw_sin = np.broadcast_to(np.sin(ang_t), (nblk, n * r8, LANES))

    gd = f_w // N_FOURIER_GROUPS
    c = np.arange(gd)
    ang_c = 2.0 * np.pi * np.outer(c, c) / gd
    eye_g = np.eye(N_FOURIER_GROUPS)
    cc = np.kron(eye_g, np.cos(ang_c)) / np.sqrt(gd)
    sc = np.kron(eye_g, np.sin(ang_c)) / np.sqrt(gd)
    as_f32 = lambda a: jnp.asarray(np.ascontiguousarray(a, dtype=np.float32))
    to_bf16 = lambda a: as_f32(a).astype(BF16)
    return to_bf16(fa), to_bf16(g), to_bf16(cc), to_bf16(sc), as_f32(tw_cos), as_f32(tw_sin)


def _fourier_kernel(u_ref, fa_ref, g_ref, cc_ref, sc_ref, twc_ref, tws_ref, y_ref, sre_ref, sim_ref,
                    *, nblk):
    t = pl.program_id(1)
    n, r8, f_w = u_ref.shape
    m = n * r8

    @pl.when(t < nblk)
    def _stage_a():
        xin = u_ref[...].reshape(m, f_w).astype(BF16)
        a = _dot(fa_ref[...], xin)
        reps = f_w // LANES
        cos_t = jnp.concatenate([twc_ref[...]] * reps, axis=1)
        sin_t = jnp.concatenate([tws_ref[...]] * reps, axis=1)
        a_re = a[0:m]
        a_im = a[m:2 * m]
        b_re = a_re * cos_t + a_im * sin_t
        b_im = a_im * cos_t - a_re * sin_t
        off = pl.multiple_of(t * r8, r8)
        sre_ref[:, pl.ds(off, r8), :] = b_re.reshape(n, r8, f_w)
        sim_ref[:, pl.ds(off, r8), :] = b_im.reshape(n, r8, f_w)

    @pl.when(t >= nblk)
    def _stage_b():
        off = pl.multiple_of((t - nblk) * r8, r8)
        br = sre_ref[pl.ds(off, r8)].reshape(m, f_w)
        bi = sim_ref[pl.ds(off, r8)].reshape(m, f_w)
        bcat = jnp.concatenate([br, bi], axis=0).astype(BF16)
        p = _dot(g_ref[...], bcat)
        y = _dot(p[0:m].astype(BF16), cc_ref[...]) + _dot(p[m:2 * m].astype(BF16), sc_ref[...])
        y_ref[...] = y.reshape(n, r8, f_w)


def _fourier(u):
    bsz, s, f_w = u.shape
    rows = s // GRID_W
    assert rows == GRID_W, "the two-stage position DFT assumes a square token grid"
    r8 = SUBLANES
    nblk = rows // r8
    fa, g, cc, sc, tw_cos, tw_sin = _fourier_constants(rows, f_w)
    u4 = u.reshape(bsz, rows, GRID_W, f_w)
    stage_a_blk = lambda b, t: jnp.minimum(t, nblk - 1)
    tw_spec = pl.BlockSpec((None,) + tw_cos.shape[1:], lambda b, t: (stage_a_blk(b, t), 0, 0))
    y4 = pl.pallas_call(
        functools.partial(_fourier_kernel, nblk=nblk),
        grid=(bsz, 2 * nblk),
        in_specs=[
            pl.BlockSpec((None, rows, r8, f_w), lambda b, t: (b, 0, stage_a_blk(b, t), 0)),
            pl.BlockSpec(fa.shape, lambda b, t: (0, 0)),
            pl.BlockSpec(g.shape, lambda b, t: (0, 0)),
            pl.BlockSpec(cc.shape, lambda b, t: (0, 0)),
            pl.BlockSpec(sc.shape, lambda b, t: (0, 0)),
            tw_spec, tw_spec,
        ],
        out_specs=pl.BlockSpec((None, GRID_W, r8, f_w),
                               lambda b, t: (b, 0, jnp.maximum(t - nblk, 0), 0)),
        out_shape=jax.ShapeDtypeStruct((bsz, GRID_W, rows, f_w), F32),
        scratch_shapes=[pltpu.VMEM((rows, GRID_W, f_w), F32), pltpu.VMEM((rows, GRID_W, f_w), F32)],
        compiler_params=_params("arbitrary", "arbitrary"),
        name="fourier",
    )(u4, fa, g, cc, sc, tw_cos, tw_sin)
    return y4.reshape(bsz, s, f_w)


def _bias_tables(rpb, rows):
    kw = min(WIN_W, GRID_W)
    cols = np.arange(GRID_W)
    col_start = np.clip(cols - kw // 2, 0, GRID_W - kw)
    kj = cols[None, :]
    allowed = (kj >= col_start[:, None]) & (kj < col_start[:, None] + kw)
    rel = np.clip(kj - cols[:, None] + (WIN_W - 1), 0, 2 * WIN_W - 2)
    t = jnp.take(rpb.astype(F32), jnp.asarray(rel.reshape(-1), jnp.int32), axis=2)
    t = t.reshape(rpb.shape[0], rpb.shape[1], GRID_W, GRID_W)
    t = jnp.where(jnp.asarray(allowed)[None, None], t, MASK_VALUE)
    return jnp.concatenate([t[:, :-1], t[:, 1:]], axis=-1)


def _attn_out_kernel(q_ref, k_ref, v_ref, t2_ref, yfn_ref, ga_ref, gb_ref, x_ref,
                     wna_ref, wfn_ref, bfn_ref, wo_ref, g1_ref, o_ref, yna_ref,
                     *, rows, kh, rows_per_step):
    rb = pl.program_id(1)
    n_pairs = q_ref.shape[1] // LANES
    lane = lax.broadcasted_iota(jnp.int32, (1, LANES), 1)
    low = lane < HEAD_DIM
    mask_lo = jnp.where(low, 1.0, 0.0).astype(BF16)
    mask_hi = jnp.where(low, 0.0, 1.0).astype(BF16)
    n_keys = kh * GRID_W

    def row_body(j, carry):
        r = rb * rows_per_step + j
        rs = jnp.clip(r - kh // 2, 0, rows - kh)
        ks = pl.multiple_of(rs * GRID_W, GRID_W)
        idx0 = rs - r + (WIN_H - 1)
        q_off = pl.multiple_of(j * GRID_W, GRID_W)
        outs = []
        for p in range(n_pairs):
            cs = slice(LANES * p, LANES * (p + 1))
            q2 = q_ref[pl.ds(q_off, GRID_W), cs]
            k2 = k_ref[pl.ds(ks, n_keys), cs]
            v2 = v_ref[pl.ds(ks, n_keys), cs]
            qs = jnp.concatenate([q2 * mask_lo, q2 * mask_hi], axis=0)
            s = lax.dot_general(qs, k2, (((1,), (1,)), ((), ())), preferred_element_type=F32)
            bias = jnp.concatenate(
                [jnp.concatenate([t2_ref[2 * p + hh, idx0 + 2 * m] for m in range(kh // 2)], axis=1)
                 for hh in range(2)], axis=0)
            s = s + bias
            mx = jnp.max(s, axis=-1, keepdims=True)
            e = jnp.exp(s - mx)
            denom = jnp.sum(e, axis=-1, keepdims=True)
            o = _dot(e.astype(BF16), v2) * (1.0 / denom)
            outs.append(jnp.where(low, o[0:GRID_W], o[GRID_W:2 * GRID_W]))
        yna_ref[pl.ds(q_off, GRID_W), :] = jnp.concatenate(outs, axis=1)
        return carry

    lax.fori_loop(0, rows_per_step, row_body, 0)

    ya = _dot(yna_ref[...].astype(BF16), wna_ref[...])
    yf = _dot(yfn_ref[...].astype(BF16), wfn_ref[...]) + bfn_ref[...]
    merged = ga_ref[...].astype(F32) * ya + gb_ref[...].astype(F32) * yf
    o_ref[...] = x_ref[...] + g1_ref[...] * _dot(merged.astype(BF16), wo_ref[...])


def _attn_out(q, k, v, t2, yfn, ga, gb, x, w_na, w_fn, b_fn, w_o, gate1):
    bsz, s, d = x.shape
    na_w = q.shape[-1]
    f_w = yfn.shape[-1]
    rows = s // GRID_W
    kh = min(WIN_H, rows)
    assert kh % 2 == 0 and (2 * HEAD_DIM) == LANES
    rows_per_step = 4
    tq = rows_per_step * GRID_W

    tok = lambda w: pl.BlockSpec((None, tq, w), lambda b, i: (b, i, 0))
    whole_seq = pl.BlockSpec((None, s, na_w), lambda b, i: (b, 0, 0))
    per_batch = pl.BlockSpec((None, 1, d), lambda b, i: (b, 0, 0))
    full = lambda a: pl.BlockSpec(a.shape, lambda b, i: (0,) * a.ndim)
    bfn = b_fn.reshape(1, d)
    return pl.pallas_call(
        functools.partial(_attn_out_kernel, rows=rows, kh=kh, rows_per_step=rows_per_step),
        grid=(bsz, s // tq),
        in_specs=[tok(na_w), whole_seq, whole_seq, full(t2), tok(f_w), tok(d), tok(d), tok(d),
                  full(w_na), full(w_fn), full(bfn), full(w_o), per_batch],
        out_specs=tok(d),
        out_shape=jax.ShapeDtypeStruct((bsz, s, d), F32),
        scratch_shapes=[pltpu.VMEM((tq, na_w), F32)],
        compiler_params=_params("parallel", "arbitrary"),
        name="attn_out",
    )(q, k, v, t2, yfn, ga, gb, x, w_na, w_fn, bfn, w_o, gate1)


def _mlp_kernel(x_ref, sc_ref, sh_ref, nw_ref, g2_ref, w1_ref, w2_ref, o_ref, h_ref, acc_ref):
    f = pl.program_id(2)

    @pl.when(f == 0)
    def _prologue():
        x = x_ref[...]
        ms = jnp.mean(x * x, axis=-1, keepdims=True)
        y = x * lax.rsqrt(ms + EPS) * nw_ref[...]
        h_ref[...] = (y * (1.0 + sc_ref[...]) + sh_ref[...]).astype(BF16)
        acc_ref[...] = jnp.zeros_like(acc_ref)

    a = jnp.maximum(_dot(h_ref[...], w1_ref[...]), 0.0)
    acc_ref[...] += _dot((a * a).astype(BF16), w2_ref[...])

    @pl.when(f == pl.num_programs(2) - 1)
    def _epilogue():
        o_ref[...] = x_ref[...] + g2_ref[...] * acc_ref[...]


def _mlp(x, scale2, shift2, norm_w, gate2, w1, w2):
    bsz, s, d = x.shape
    d_ff = w1.shape[1]
    tm = 1024
    tf = 1024
    tok = pl.BlockSpec((None, tm, d), lambda b, i, f: (b, i, 0))
    per_batch = pl.BlockSpec((None, 1, d), lambda b, i, f: (b, 0, 0))
    nw = norm_w.reshape(1, d)
    return pl.pallas_call(
        _mlp_kernel,
        grid=(bsz, s // tm, d_ff // tf),
        in_specs=[tok, per_batch, per_batch, pl.BlockSpec((1, d), lambda b, i, f: (0, 0)), per_batch,
                  pl.BlockSpec((d, tf), lambda b, i, f: (0, f)),
                  pl.BlockSpec((tf, d), lambda b, i, f: (f, 0))],
        out_specs=tok,
        out_shape=jax.ShapeDtypeStruct((bsz, s, d), F32),
        scratch_shapes=[pltpu.VMEM((tm, d), BF16), pltpu.VMEM((tm, d), F32)],
        compiler_params=_params("parallel", "parallel", "arbitrary"),
        name="mlp",
    )(x, scale2, shift2, nw, gate2, w1, w2)


def kernel(x, c, norm1_w, norm2_w, w_ada, b_ada, w_in, b_in, q_norm_w, k_norm_w, rpb,
           w_na_out, w_fn_out, b_fn_out, w_o, w_mlp_in, w_mlp_out):
    bsz, s, d = x.shape
    depth = w_ada.shape[0]
    na_w = w_na_out.shape[1]
    f_w = w_fn_out.shape[1]
    n_heads = na_w // HEAD_DIM
    rows = s // GRID_W
    for l in range(depth):
        mod = _ada(c, w_ada[l], b_ada[l])
        shift1, scale1, gate1, shift2, scale2, gate2 = [
            mod[:, i * d:(i + 1) * d].reshape(bsz, 1, d) for i in range(N_MOD)]

        qk_w = jnp.concatenate([jnp.tile(q_norm_w[l], n_heads) * (HEAD_DIM ** -0.5),
                                jnp.tile(k_norm_w[l], n_heads)]).reshape(1, 2 * na_w)
        q, k, v, u, ga, gb = _inproj(x, scale1, shift1, norm1_w[l], w_in[l].astype(BF16), b_in[l],
                                     qk_w, na_w, f_w)
        yfn = _fourier(u)
        t2 = _bias_tables(rpb[l], rows)
        x = _attn_out(q, k, v, t2, yfn, ga, gb, x, w_na_out[l].astype(BF16),
                      w_fn_out[l].astype(BF16), b_fn_out[l], w_o[l].astype(BF16), gate1)
        x = _mlp(x, scale2, shift2, norm2_w[l], gate2, w_mlp_in[l].astype(BF16),
                 w_mlp_out[l].astype(BF16))
    return x
```

```python
import functools

import numpy as np
import jax
import jax.numpy as jnp
from jax import lax
from jax.experimental import pallas as pl
from jax.experimental.pallas import tpu as pltpu

F32 = jnp.float32
BF16 = jnp.bfloat16

GRID_W = 64
HEAD_DIM = 64
WIN_H = 8
WIN_W = 16
N_FOURIER_GROUPS = 4
N_MOD = 6
EPS = 1e-6
MASK_VALUE = -1e30

SUBLANES = 8
LANES = 128
VMEM_LIMIT_BYTES = 56 * 1024 * 1024


def _dot(a, b):
    return jnp.dot(a, b, preferred_element_type=F32)


def _sigmoid(x):
    return 1.0 / (1.0 + jnp.exp(-x))


def _params(*semantics):
    return pltpu.CompilerParams(dimension_semantics=semantics, vmem_limit_bytes=VMEM_LIMIT_BYTES)


def _ada_kernel(c_ref, w_ref, b_ref, o_ref):
    c = c_ref[...]
    s = c * _sigmoid(c)
    o_ref[...] = jnp.dot(s, w_ref[...], preferred_element_type=F32,
                         precision=lax.Precision.HIGHEST) + b_ref[...]


def _ada(c, w, b):
    bsz, d = c.shape
    n = w.shape[1]
    rows = -(-bsz // SUBLANES) * SUBLANES
    c_pad = jnp.pad(c, ((0, rows - bsz), (0, 0)))
    tn = n // N_MOD
    out = pl.pallas_call(
        _ada_kernel,
        grid=(n // tn,),
        in_specs=[pl.BlockSpec((rows, d), lambda j: (0, 0)),
                  pl.BlockSpec((d, tn), lambda j: (0, j)),
                  pl.BlockSpec((1, tn), lambda j: (0, j))],
        out_specs=pl.BlockSpec((rows, tn), lambda j: (0, j)),
        out_shape=jax.ShapeDtypeStruct((rows, n), F32),
        compiler_params=_params("parallel"),
        name="ada",
    )(c_pad, w, b.reshape(1, n))
    return out[:bsz]


def _inproj_kernel(x_ref, sc_ref, sh_ref, nw_ref, w_ref, b_ref, qkw_ref, gsum_ref, gexp_ref,
                   q_ref, k_ref, v_ref, u_ref, ga_ref, gb_ref, *, na_w, f_w, d):
    x = x_ref[...]
    ms = jnp.mean(x * x, axis=-1, keepdims=True)
    y = x * lax.rsqrt(ms + EPS) * nw_ref[...]
    h = (y * (1.0 + sc_ref[...]) + sh_ref[...]).astype(BF16)

    qk = 2 * na_w
    zqk = _dot(h, w_ref[:, 0:qk]) + b_ref[:, 0:qk]
    ssq = _dot((zqk * zqk).astype(BF16), gsum_ref[...])
    rinv = lax.rsqrt(ssq * (1.0 / HEAD_DIM) + EPS)
    rinv_hi = rinv.astype(BF16)
    rinv_lo = (rinv - rinv_hi.astype(F32)).astype(BF16)
    scale = _dot(rinv_hi, gexp_ref[...]) + _dot(rinv_lo, gexp_ref[...])
    zn = zqk * scale * qkw_ref[...]
    q_ref[...] = zn[:, 0:na_w].astype(BF16)
    k_ref[...] = zn[:, na_w:qk].astype(BF16)

    o = qk
    v_ref[...] = (_dot(h, w_ref[:, o:o + na_w]) + b_ref[:, o:o + na_w]).astype(BF16)
    o += na_w
    u_ref[...] = _dot(h, w_ref[:, o:o + f_w]) + b_ref[:, o:o + f_w]
    o += f_w
    ga_ref[...] = _sigmoid(_dot(h, w_ref[:, o:o + d]) + b_ref[:, o:o + d]).astype(BF16)
    o += d
    gb_ref[...] = _sigmoid(_dot(h, w_ref[:, o:o + d]) + b_ref[:, o:o + d]).astype(BF16)


def _inproj(x, scale1, shift1, norm_w, w_in, b_in, qk_w, na_w, f_w):
    bsz, s, d = x.shape
    in_w = w_in.shape[1]
    tm = 512
    qk = 2 * na_w
    n_grp = qk // HEAD_DIM
    grp = np.arange(qk) // HEAD_DIM
    gsum = jnp.asarray((grp[:, None] == np.arange(LANES)[None, :]).astype(np.float32), BF16)
    gexp = jnp.asarray((np.arange(LANES)[:, None] == grp[None, :]).astype(np.float32), BF16)
    assert n_grp <= LANES

    tok = lambda w: pl.BlockSpec((None, tm, w), lambda b, i: (b, i, 0))
    per_batch = pl.BlockSpec((None, 1, d), lambda b, i: (b, 0, 0))
    full = lambda a: pl.BlockSpec(a.shape, lambda b, i: (0,) * a.ndim)
    nw = norm_w.reshape(1, d)
    bi = b_in.reshape(1, in_w)
    out_shapes = (
        jax.ShapeDtypeStruct((bsz, s, na_w), BF16),
        jax.ShapeDtypeStruct((bsz, s, na_w), BF16),
        jax.ShapeDtypeStruct((bsz, s, na_w), BF16),
        jax.ShapeDtypeStruct((bsz, s, f_w), F32),
        jax.ShapeDtypeStruct((bsz, s, d), BF16),
        jax.ShapeDtypeStruct((bsz, s, d), BF16),
    )
    return pl.pallas_call(
        functools.partial(_inproj_kernel, na_w=na_w, f_w=f_w, d=d),
        grid=(bsz, s // tm),
        in_specs=[tok(d), per_batch, per_batch, full(nw), full(w_in), full(bi), full(qk_w),
                  full(gsum), full(gexp)],
        out_specs=(tok(na_w), tok(na_w), tok(na_w), tok(f_w), tok(d), tok(d)),
        out_shape=out_shapes,
        compiler_params=_params("parallel", "parallel"),
        name="inproj",
    )(x, scale1, shift1, nw, w_in, bi, qk_w, gsum, gexp)


def _fourier_constants(rows, f_w):
    n = rows
    r8 = SUBLANES
    k = np.arange(n)
    ang_a = 2.0 * np.pi * np.outer(k, k) / n
    eye8 = np.eye(r8)
    norm = 1.0 / np.sqrt(n)
    fa_re = np.kron(np.cos(ang_a), eye8) * norm
    fa_im = np.kron(-np.sin(ang_a), eye8) * norm
    fa = np.concatenate([fa_re, fa_im], axis=0)

    nblk = n // r8
    s2 = k[None, :]
    s2p = k[:, None]
    gr = np.zeros((n, r8, r8, n))
    gi = np.zeros((n, r8, r8, n))
    for l in range(r8):
        ang = 2.0 * np.pi * (s2 * l / (n * GRID_W) + s2 * s2p / n)
        gr[:, l, l, :] = np.cos(ang) * norm
        gi[:, l, l, :] = -np.sin(ang) * norm
    gr = gr.reshape(n * r8, r8 * n)
    gi = gi.reshape(n * r8, r8 * n)
    g = np.block([[gr, -gi], [gi, gr]])

    t_idx = np.arange(nblk)[:, None, None]
    blk_of_row = (k // r8)[None, :, None]
    j_idx = np.arange(r8)[None, None, :]
    ang_t = 2.0 * np.pi * (t_idx * r8 + j_idx) * blk_of_row * r8 / (n * GRID_W)
    ang_t = ang_t.reshape(nblk, n * r8, 1)
    tw_cos = np.broadcast_to(np.cos(ang_t), (nblk, n * r8, LANES))
    tw_sin = np.broadcast_to(np.sin(ang_t), (nblk, n * r8, LANES))

    gd = f_w // N_FOURIER_GROUPS
    c = np.arange(gd)
    ang_c = 2.0 * np.pi * np.outer(c, c) / gd
    eye_g = np.eye(N_FOURIER_GROUPS)
    cc = np.kron(eye_g, np.cos(ang_c)) / np.sqrt(gd)
    sc = np.kron(eye_g, np.sin(ang_c)) / np.sqrt(gd)
    as_f32 = lambda a: jnp.asarray(np.ascontiguousarray(a, dtype=np.float32))
    to_bf16 = lambda a: as_f32(a).astype(BF16)
    return to_bf16(fa), to_bf16(g), to_bf16(cc), to_bf16(sc), as_f32(tw_cos), as_f32(tw_sin)


def _fourier_kernel(u_ref, fa_ref, g_ref, cc_ref, sc_ref, twc_ref, tws_ref, y_ref, sre_ref, sim_ref,
                    *, nblk):
    t = pl.program_id(1)
    n, r8, f_w = u_ref.shape
    m = n * r8

    @pl.when(t < nblk)
    def _stage_a():
        xin = u_ref[...].reshape(m, f_w).astype(BF16)
        a = _dot(fa_ref[...], xin)
        reps = f_w // LANES
        cos_t = jnp.concatenate([twc_ref[...]] * reps, axis=1)
        sin_t = jnp.concatenate([tws_ref[...]] * reps, axis=1)
        a_re = a[0:m]
        a_im = a[m:2 * m]
        b_re = a_re * cos_t + a_im * sin_t
        b_im = a_im * cos_t - a_re * sin_t
        off = pl.multiple_of(t * r8, r8)
        sre_ref[:, pl.ds(off, r8), :] = b_re.reshape(n, r8, f_w)
        sim_ref[:, pl.ds(off, r8), :] = b_im.reshape(n, r8, f_w)

    @pl.when(t >= nblk)
    def _stage_b():
        off = pl.multiple_of((t - nblk) * r8, r8)
        br = sre_ref[pl.ds(off, r8)].reshape(m, f_w)
        bi = sim_ref[pl.ds(off, r8)].reshape(m, f_w)
        bcat = jnp.concatenate([br, bi], axis=0).astype(BF16)
        p = _dot(g_ref[...], bcat)
        y = _dot(p[0:m].astype(BF16), cc_ref[...]) + _dot(p[m:2 * m].astype(BF16), sc_ref[...])
        y_ref[...] = y.reshape(n, r8, f_w)


def _fourier(u):
    bsz, s, f_w = u.shape
    rows = s // GRID_W
    assert rows == GRID_W, "the two-stage position DFT assumes a square token grid"
    r8 = SUBLANES
    nblk = rows // r8
    fa, g, cc, sc, tw_cos, tw_sin = _fourier_constants(rows, f_w)
    u4 = u.reshape(bsz, rows, GRID_W, f_w)
    stage_a_blk = lambda b, t: jnp.minimum(t, nblk - 1)
    tw_spec = pl.BlockSpec((None,) + tw_cos.shape[1:], lambda b, t: (stage_a_blk(b, t), 0, 0))
    y4 = pl.pallas_call(
        functools.partial(_fourier_kernel, nblk=nblk),
        grid=(bsz, 2 * nblk),
        in_specs=[
            pl.BlockSpec((None, rows, r8, f_w), lambda b, t: (b, 0, stage_a_blk(b, t), 0)),
            pl.BlockSpec(fa.shape, lambda b, t: (0, 0)),
            pl.BlockSpec(g.shape, lambda b, t: (0, 0)),
            pl.BlockSpec(cc.shape, lambda b, t: (0, 0)),
            pl.BlockSpec(sc.shape, lambda b, t: (0, 0)),
            tw_spec, tw_spec,
        ],
        out_specs=pl.BlockSpec((None, GRID_W, r8, f_w),
                               lambda b, t: (b, 0, jnp.maximum(t - nblk, 0), 0)),
        out_shape=jax.ShapeDtypeStruct((bsz, GRID_W, rows, f_w), F32),
        scratch_shapes=[pltpu.VMEM((rows, GRID_W, f_w), F32), pltpu.VMEM((rows, GRID_W, f_w), F32)],
        compiler_params=_params("arbitrary", "arbitrary"),
        name="fourier",
    )(u4, fa, g, cc, sc, tw_cos, tw_sin)
    return y4.reshape(bsz, s, f_w)


def _bias_tables(rpb, rows):
    kw = min(WIN_W, GRID_W)
    cols = np.arange(GRID_W)
    col_start = np.clip(cols - kw // 2, 0, GRID_W - kw)
    kj = cols[None, :]
    allowed = (kj >= col_start[:, None]) & (kj < col_start[:, None] + kw)
    rel = np.clip(kj - cols[:, None] + (WIN_W - 1), 0, 2 * WIN_W - 2)
    t = jnp.take(rpb.astype(F32), jnp.asarray(rel.reshape(-1), jnp.int32), axis=2)
    t = t.reshape(rpb.shape[0], rpb.shape[1], GRID_W, GRID_W)
    t = jnp.where(jnp.asarray(allowed)[None, None], t, MASK_VALUE)
    return jnp.concatenate([t[:, :-1], t[:, 1:]], axis=-1)


def _attn_out_kernel(q_ref, k_ref, v_ref, t2_ref, yfn_ref, ga_ref, gb_ref, x_ref,
                     wna_ref, wfn_ref, bfn_ref, wo_ref, g1_ref, o_ref, yna_ref, s_ref, p_ref, linv_ref,
                     *, rows, kh, rows_per_step):
    rb = pl.program_id(1)
    n_pairs = q_ref.shape[1] // LANES
    lane = lax.broadcasted_iota(jnp.int32, (1, LANES), 1)
    low = lane < HEAD_DIM
    mask_lo = jnp.where(low, 1.0, 0.0).astype(BF16)
    mask_hi = jnp.where(low, 0.0, 1.0).astype(BF16)
    n_keys = kh * GRID_W

    key_start = []
    for j in range(rows_per_step):
        r = rb * rows_per_step + j
        rs = jnp.clip(r - kh // 2, 0, rows - kh)
        key_start.append((pl.multiple_of(rs * GRID_W, GRID_W), rs - r + (WIN_H - 1)))

    for j in range(rows_per_step):
        ks, idx0 = key_start[j]
        for p in range(n_pairs):
            cs = slice(LANES * p, LANES * (p + 1))
            q2 = q_ref[j * GRID_W:(j + 1) * GRID_W, cs]
            k2 = k_ref[pl.ds(ks, n_keys), cs]
            qs = jnp.concatenate([q2 * mask_lo, q2 * mask_hi], axis=0)
            s = lax.dot_general(qs, k2, (((1,), (1,)), ((), ())), preferred_element_type=F32)
            bias = jnp.concatenate(
                [jnp.concatenate([t2_ref[2 * p + hh, idx0 + 2 * m] for m in range(kh // 2)], axis=1)
                 for hh in range(2)], axis=0)
            s_ref[j * n_pairs + p] = s + bias

    for t in range(rows_per_step * n_pairs):
        s = s_ref[t]
        e = jnp.exp(s - jnp.max(s, axis=-1, keepdims=True))
        p_ref[t] = e.astype(BF16)
        linv_ref[t] = 1.0 / jnp.sum(e, axis=-1, keepdims=True)

    for j in range(rows_per_step):
        ks, _ = key_start[j]
        outs = []
        for p in range(n_pairs):
            t = j * n_pairs + p
            v2 = v_ref[pl.ds(ks, n_keys), LANES * p:LANES * (p + 1)]
            o = _dot(p_ref[t], v2) * linv_ref[t]
            outs.append(jnp.where(low, o[0:GRID_W], o[GRID_W:2 * GRID_W]))
        yna_ref[j * GRID_W:(j + 1) * GRID_W, :] = jnp.concatenate(outs, axis=1)

    ya = _dot(yna_ref[...].astype(BF16), wna_ref[...])
    yf = _dot(yfn_ref[...].astype(BF16), wfn_ref[...]) + bfn_ref[...]
    merged = ga_ref[...].astype(F32) * ya + gb_ref[...].astype(F32) * yf
    o_ref[...] = x_ref[...] + g1_ref[...] * _dot(merged.astype(BF16), wo_ref[...])


def _attn_out(q, k, v, t2, yfn, ga, gb, x, w_na, w_fn, b_fn, w_o, gate1):
    bsz, s, d = x.shape
    na_w = q.shape[-1]
    f_w = yfn.shape[-1]
    rows = s // GRID_W
    kh = min(WIN_H, rows)
    assert kh % 2 == 0 and (2 * HEAD_DIM) == LANES
    rows_per_step = 4
    tq = rows_per_step * GRID_W
    n_tiles = rows_per_step * (na_w // LANES)

    tok = lambda w: pl.BlockSpec((None, tq, w), lambda b, i: (b, i, 0))
    whole_seq = pl.BlockSpec((None, s, na_w), lambda b, i: (b, 0, 0))
    per_batch = pl.BlockSpec((None, 1, d), lambda b, i: (b, 0, 0))
    full = lambda a: pl.BlockSpec(a.shape, lambda b, i: (0,) * a.ndim)
    bfn = b_fn.reshape(1, d)
    return pl.pallas_call(
        functools.partial(_attn_out_kernel, rows=rows, kh=kh, rows_per_step=rows_per_step),
        grid=(bsz, s // tq),
        in_specs=[tok(na_w), whole_seq, whole_seq, full(t2), tok(f_w), tok(d), tok(d), tok(d),
                  full(w_na), full(w_fn), full(bfn), full(w_o), per_batch],
        out_specs=tok(d),
        out_shape=jax.ShapeDtypeStruct((bsz, s, d), F32),
        scratch_shapes=[pltpu.VMEM((tq, na_w), F32),
                        pltpu.VMEM((n_tiles, 2 * GRID_W, kh * GRID_W), F32),
                        pltpu.VMEM((n_tiles, 2 * GRID_W, kh * GRID_W), BF16),
                        pltpu.VMEM((n_tiles, 2 * GRID_W, 1), F32)],
        compiler_params=_params("parallel", "arbitrary"),
        name="attn_out",
    )(q, k, v, t2, yfn, ga, gb, x, w_na, w_fn, bfn, w_o, gate1)


def _mlp_kernel(x_ref, sc_ref, sh_ref, nw_ref, g2_ref, w1_ref, w2_ref, o_ref, h_ref, acc_ref):
    f = pl.program_id(2)

    @pl.when(f == 0)
    def _prologue():
        x = x_ref[...]
        ms = jnp.mean(x * x, axis=-1, keepdims=True)
        y = x * lax.rsqrt(ms + EPS) * nw_ref[...]
        h_ref[...] = (y * (1.0 + sc_ref[...]) + sh_ref[...]).astype(BF16)
        acc_ref[...] = jnp.zeros_like(acc_ref)

    a = jnp.maximum(_dot(h_ref[...], w1_ref[...]), 0.0)
    acc_ref[...] += _dot((a * a).astype(BF16), w2_ref[...])

    @pl.when(f == pl.num_programs(2) - 1)
    def _epilogue():
        o_ref[...] = x_ref[...] + g2_ref[...] * acc_ref[...]


def _mlp(x, scale2, shift2, norm_w, gate2, w1, w2):
    bsz, s, d = x.shape
    d_ff = w1.shape[1]
    tm = 1024
    tf = 1024
    tok = pl.BlockSpec((None, tm, d), lambda b, i, f: (b, i, 0))
    per_batch = pl.BlockSpec((None, 1, d), lambda b, i, f: (b, 0, 0))
    nw = norm_w.reshape(1, d)
    return pl.pallas_call(
        _mlp_kernel,
        grid=(bsz, s // tm, d_ff // tf),
        in_specs=[tok, per_batch, per_batch, pl.BlockSpec((1, d), lambda b, i, f: (0, 0)), per_batch,
                  pl.BlockSpec((d, tf), lambda b, i, f: (0, f)),
                  pl.BlockSpec((tf, d), lambda b, i, f: (f, 0))],
        out_specs=tok,
        out_shape=jax.ShapeDtypeStruct((bsz, s, d), F32),
        scratch_shapes=[pltpu.VMEM((tm, d), BF16), pltpu.VMEM((tm, d), F32)],
        compiler_params=_params("parallel", "parallel", "arbitrary"),
        name="mlp",
    )(x, scale2, shift2, nw, gate2, w1, w2)


def kernel(x, c, norm1_w, norm2_w, w_ada, b_ada, w_in, b_in, q_norm_w, k_norm_w, rpb,
           w_na_out, w_fn_out, b_fn_out, w_o, w_mlp_in, w_mlp_out):
    bsz, s, d = x.shape
    depth = w_ada.shape[0]
    na_w = w_na_out.shape[1]
    f_w = w_fn_out.shape[1]
    n_heads = na_w // HEAD_DIM
    rows = s // GRID_W
    for l in range(depth):
        mod = _ada(c, w_ada[l], b_ada[l])
        shift1, scale1, gate1, shift2, scale2, gate2 = [
            mod[:, i * d:(i + 1) * d].reshape(bsz, 1, d) for i in range(N_MOD)]

        qk_w = jnp.concatenate([jnp.tile(q_norm_w[l], n_heads) * (HEAD_DIM ** -0.5),
                                jnp.tile(k_norm_w[l], n_heads)]).reshape(1, 2 * na_w)
        q, k, v, u, ga, gb = _inproj(x, scale1, shift1, norm1_w[l], w_in[l].astype(BF16), b_in[l],
                                     qk_w, na_w, f_w)
        yfn = _fourier(u)
        t2 = _bias_tables(rpb[l], rows)
        x = _attn_out(q, k, v, t2, yfn, ga, gb, x, w_na_out[l].astype(BF16),
                      w_fn_out[l].astype(BF16), b_fn_out[l], w_o[l].astype(BF16), gate1)
        x = _mlp(x, scale2, shift2, norm2_w[l], gate2, w_mlp_in[l].astype(BF16),
                 w_mlp_out[l].astype(BF16))
    return x
```

```python
import functools

import numpy as np
import jax
import jax.numpy as jnp
from jax import lax
from jax.experimental import pallas as pl
from jax.experimental.pallas import tpu as pltpu

F32 = jnp.float32
BF16 = jnp.bfloat16

GRID_W = 64
HEAD_DIM = 64
WIN_H = 8
WIN_W = 16
N_FOURIER_GROUPS = 4
N_MOD = 6
EPS = 1e-6
MASK_VALUE = -1e30

SUBLANES = 8
LANES = 128
VMEM_LIMIT_BYTES = 56 * 1024 * 1024


def _dot(a, b):
    return jnp.dot(a, b, preferred_element_type=F32)


def _sigmoid(x):
    return 1.0 / (1.0 + jnp.exp(-x))


def _params(*semantics):
    return pltpu.CompilerParams(dimension_semantics=semantics, vmem_limit_bytes=VMEM_LIMIT_BYTES)


def _ada_kernel(c_ref, w_ref, b_ref, o_ref):
    c = c_ref[...]
    s = c * _sigmoid(c)
    o_ref[...] = jnp.dot(s, w_ref[...], preferred_element_type=F32,
                         precision=lax.Precision.HIGHEST) + b_ref[...]


def _ada(c, w, b):
    bsz, d = c.shape
    n = w.shape[1]
    rows = -(-bsz // SUBLANES) * SUBLANES
    c_pad = jnp.pad(c, ((0, rows - bsz), (0, 0)))
    tn = n // N_MOD
    out = pl.pallas_call(
        _ada_kernel,
        grid=(n // tn,),
        in_specs=[pl.BlockSpec((rows, d), lambda j: (0, 0)),
                  pl.BlockSpec((d, tn), lambda j: (0, j)),
                  pl.BlockSpec((1, tn), lambda j: (0, j))],
        out_specs=pl.BlockSpec((rows, tn), lambda j: (0, j)),
        out_shape=jax.ShapeDtypeStruct((rows, n), F32),
        compiler_params=_params("parallel"),
        name="ada",
    )(c_pad, w, b.reshape(1, n))
    return out[:bsz]


def _inproj_kernel(x_ref, sc_ref, sh_ref, nw_ref, w_ref, b_ref, qkw_ref, gsum_ref, gexp_ref,
                   q_ref, k_ref, v_ref, u_ref, ga_ref, gb_ref, *, na_w, f_w, d):
    x = x_ref[...]
    ms = jnp.mean(x * x, axis=-1, keepdims=True)
    y = x * lax.rsqrt(ms + EPS) * nw_ref[...]
    h = (y * (1.0 + sc_ref[...]) + sh_ref[...]).astype(BF16)

    qk = 2 * na_w
    zqk = _dot(h, w_ref[:, 0:qk]) + b_ref[:, 0:qk]
    ssq = _dot((zqk * zqk).astype(BF16), gsum_ref[...])
    rinv = lax.rsqrt(ssq * (1.0 / HEAD_DIM) + EPS)
    rinv_hi = rinv.astype(BF16)
    rinv_lo = (rinv - rinv_hi.astype(F32)).astype(BF16)
    scale = _dot(jnp.concatenate([rinv_hi, rinv_lo], axis=1), gexp_ref[...])
    zn = zqk * scale * qkw_ref[...]
    q_ref[...] = zn[:, 0:na_w].astype(BF16)
    k_ref[...] = zn[:, na_w:qk].astype(BF16)

    o = qk
    v_ref[...] = (_dot(h, w_ref[:, o:o + na_w]) + b_ref[:, o:o + na_w]).astype(BF16)
    o += na_w
    u_ref[...] = _dot(h, w_ref[:, o:o + f_w]) + b_ref[:, o:o + f_w]
    o += f_w
    ga_ref[...] = _sigmoid(_dot(h, w_ref[:, o:o + d]) + b_ref[:, o:o + d]).astype(BF16)
    o += d
    gb_ref[...] = _sigmoid(_dot(h, w_ref[:, o:o + d]) + b_ref[:, o:o + d]).astype(BF16)


def _inproj(x, scale1, shift1, norm_w, w_in, b_in, qk_w, na_w, f_w):
    bsz, s, d = x.shape
    in_w = w_in.shape[1]
    tm = 512
    qk = 2 * na_w
    n_grp = qk // HEAD_DIM
    grp = np.arange(qk) // HEAD_DIM
    gsum = jnp.asarray((grp[:, None] == np.arange(LANES)[None, :]).astype(np.float32), BF16)
    gexp1 = (np.arange(LANES)[:, None] == grp[None, :]).astype(np.float32)
    gexp = jnp.asarray(np.concatenate([gexp1, gexp1], axis=0), BF16)
    assert n_grp <= LANES

    tok = lambda w: pl.BlockSpec((None, tm, w), lambda b, i: (b, i, 0))
    per_batch = pl.BlockSpec((None, 1, d), lambda b, i: (b, 0, 0))
    full = lambda a: pl.BlockSpec(a.shape, lambda b, i: (0,) * a.ndim)
    nw = norm_w.reshape(1, d)
    bi = b_in.reshape(1, in_w)
    out_shapes = (
        jax.ShapeDtypeStruct((bsz, s, na_w), BF16),
        jax.ShapeDtypeStruct((bsz, s, na_w), BF16),
        jax.ShapeDtypeStruct((bsz, s, na_w), BF16),
        jax.ShapeDtypeStruct((bsz, s, f_w), F32),
        jax.ShapeDtypeStruct((bsz, s, d), BF16),
        jax.ShapeDtypeStruct((bsz, s, d), BF16),
    )
    return pl.pallas_call(
        functools.partial(_inproj_kernel, na_w=na_w, f_w=f_w, d=d),
        grid=(bsz, s // tm),
        in_specs=[tok(d), per_batch, per_batch, full(nw), full(w_in), full(bi), full(qk_w),
                  full(gsum), full(gexp)],
        out_specs=(tok(na_w), tok(na_w), tok(na_w), tok(f_w), tok(d), tok(d)),
        out_shape=out_shapes,
        compiler_params=_params("parallel", "parallel"),
        name="inproj",
    )(x, scale1, shift1, nw, w_in, bi, qk_w, gsum, gexp)


def _fourier_constants(rows, f_w):
    n = rows
    r8 = SUBLANES
    k = np.arange(n)
    ang_a = 2.0 * np.pi * np.outer(k, k) / n
    eye8 = np.eye(r8)
    norm = 1.0 / np.sqrt(n)
    fa_re = np.kron(np.cos(ang_a), eye8) * norm
    fa_im = np.kron(-np.sin(ang_a), eye8) * norm
    fa = np.concatenate([fa_re, fa_im], axis=0)

    nblk = n // r8
    s2 = k[None, :]
    s2p = k[:, None]
    gr = np.zeros((n, r8, r8, n))
    gi = np.zeros((n, r8, r8, n))
    for l in range(r8):
        ang = 2.0 * np.pi * (s2 * l / (n * GRID_W) + s2 * s2p / n)
        gr[:, l, l, :] = np.cos(ang) * norm
        gi[:, l, l, :] = -np.sin(ang) * norm
    gr = gr.reshape(n * r8, r8 * n)
    gi = gi.reshape(n * r8, r8 * n)
    g = np.block([[gr, -gi], [gi, gr]])

    t_idx = np.arange(nblk)[:, None, None]
    blk_of_row = (k // r8)[None, :, None]
    j_idx = np.arange(r8)[None, None, :]
    ang_t = 2.0 * np.pi * (t_idx * r8 + j_idx) * blk_of_row * r8 / (n * GRID_W)
    ang_t = ang_t.reshape(nblk, n * r8, 1)
    tw_cos = np.broadcast_to(np.cos(ang_t), (nblk, n * r8, LANES))
    tw_sin = np.broadcast_to(np.sin(ang_t), (nblk, n * r8, LANES))

    gd = f_w // N_FOURIER_GROUPS
    c = np.arange(gd)
    ang_c = 2.0 * np.pi * np.outer(c, c) / gd
    eye_g = np.eye(N_FOURIER_GROUPS)
    cc = np.kron(eye_g, np.cos(ang_c)) / np.sqrt(gd)
    sc = np.kron(eye_g, np.sin(ang_c)) / np.sqrt(gd)
    as_f32 = lambda a: jnp.asarray(np.ascontiguousarray(a, dtype=np.float32))
    to_bf16 = lambda a: as_f32(a).astype(BF16)
    return to_bf16(fa), to_bf16(g), to_bf16(cc), to_bf16(sc), as_f32(tw_cos), as_f32(tw_sin)


def _fourier_kernel(u_ref, fa_ref, g_ref, cc_ref, sc_ref, twc_ref, tws_ref, y_ref, sre_ref, sim_ref,
                    *, nblk):
    t = pl.program_id(1)
    n, r8, f_w = u_ref.shape
    m = n * r8

    @pl.when(t < nblk)
    def _stage_a():
        xin = u_ref[...].reshape(m, f_w).astype(BF16)
        a = _dot(fa_ref[...], xin)
        reps = f_w // LANES
        cos_t = jnp.concatenate([twc_ref[...]] * reps, axis=1)
        sin_t = jnp.concatenate([tws_ref[...]] * reps, axis=1)
        a_re = a[0:m]
        a_im = a[m:2 * m]
        b_re = a_re * cos_t + a_im * sin_t
        b_im = a_im * cos_t - a_re * sin_t
        off = pl.multiple_of(t * r8, r8)
        sre_ref[:, pl.ds(off, r8), :] = b_re.reshape(n, r8, f_w)
        sim_ref[:, pl.ds(off, r8), :] = b_im.reshape(n, r8, f_w)

    @pl.when(t >= nblk)
    def _stage_b():
        off = pl.multiple_of((t - nblk) * r8, r8)
        br = sre_ref[pl.ds(off, r8)].reshape(m, f_w)
        bi = sim_ref[pl.ds(off, r8)].reshape(m, f_w)
        bcat = jnp.concatenate([br, bi], axis=0).astype(BF16)
        p = _dot(g_ref[...], bcat)
        y = _dot(p[0:m].astype(BF16), cc_ref[...]) + _dot(p[m:2 * m].astype(BF16), sc_ref[...])
        y_ref[...] = y.reshape(n, r8, f_w)


def _fourier(u):
    bsz, s, f_w = u.shape
    rows = s // GRID_W
    assert rows == GRID_W, "the two-stage position DFT assumes a square token grid"
    r8 = SUBLANES
    nblk = rows // r8
    fa, g, cc, sc, tw_cos, tw_sin = _fourier_constants(rows, f_w)
    u4 = u.reshape(bsz, rows, GRID_W, f_w)
    stage_a_blk = lambda b, t: jnp.minimum(t, nblk - 1)
    tw_spec = pl.BlockSpec((None,) + tw_cos.shape[1:], lambda b, t: (stage_a_blk(b, t), 0, 0))
    y4 = pl.pallas_call(
        functools.partial(_fourier_kernel, nblk=nblk),
        grid=(bsz, 2 * nblk),
        in_specs=[
            pl.BlockSpec((None, rows, r8, f_w), lambda b, t: (b, 0, stage_a_blk(b, t), 0)),
            pl.BlockSpec(fa.shape, lambda b, t: (0, 0)),
            pl.BlockSpec(g.shape, lambda b, t: (0, 0)),
            pl.BlockSpec(cc.shape, lambda b, t: (0, 0)),
            pl.BlockSpec(sc.shape, lambda b, t: (0, 0)),
            tw_spec, tw_spec,
        ],
        out_specs=pl.BlockSpec((None, GRID_W, r8, f_w),
                               lambda b, t: (b, 0, jnp.maximum(t - nblk, 0), 0)),
        out_shape=jax.ShapeDtypeStruct((bsz, GRID_W, rows, f_w), F32),
        scratch_shapes=[pltpu.VMEM((rows, GRID_W, f_w), F32), pltpu.VMEM((rows, GRID_W, f_w), F32)],
        compiler_params=_params("arbitrary", "arbitrary"),
        name="fourier",
    )(u4, fa, g, cc, sc, tw_cos, tw_sin)
    return y4.reshape(bsz, s, f_w)


def _bias_tables(rpb, rows):
    kw = min(WIN_W, GRID_W)
    cols = np.arange(GRID_W)
    col_start = np.clip(cols - kw // 2, 0, GRID_W - kw)
    kj = cols[None, :]
    allowed = (kj >= col_start[:, None]) & (kj < col_start[:, None] + kw)
    rel = np.clip(kj - cols[:, None] + (WIN_W - 1), 0, 2 * WIN_W - 2)
    onehot = (rel[None] == np.arange(2 * WIN_W - 1)[:, None, None]).astype(np.float32)
    t = jnp.einsum("hmd,dck->hmck", rpb.astype(F32), jnp.asarray(onehot),
                   precision=lax.Precision.HIGHEST)
    t = jnp.where(jnp.asarray(allowed)[None, None], t, MASK_VALUE)
    return jnp.concatenate([t[:, :-1], t[:, 1:]], axis=-1)


def _attn_out_kernel(q_ref, k_ref, v_ref, t2_ref, yfn_ref, ga_ref, gb_ref, x_ref,
                     wna_ref, wfn_ref, bfn_ref, wo_ref, g1_ref, sc2_ref, sh2_ref, nw2_ref,
                     o_ref, h2_ref, yna_ref, s_ref, p_ref, linv_ref, *, rows, kh, rows_per_step):
    rb = pl.program_id(1)
    n_pairs = q_ref.shape[1] // LANES
    lane = lax.broadcasted_iota(jnp.int32, (1, LANES), 1)
    low = lane < HEAD_DIM
    mask_lo = jnp.where(low, 1.0, 0.0).astype(BF16)
    mask_hi = jnp.where(low, 0.0, 1.0).astype(BF16)
    n_keys = kh * GRID_W

    key_start = []
    for j in range(rows_per_step):
        r = rb * rows_per_step + j
        rs = jnp.clip(r - kh // 2, 0, rows - kh)
        key_start.append((pl.multiple_of(rs * GRID_W, GRID_W), rs - r + (WIN_H - 1)))

    for j in range(rows_per_step):
        ks, idx0 = key_start[j]
        for p in range(n_pairs):
            cs = slice(LANES * p, LANES * (p + 1))
            q2 = q_ref[j * GRID_W:(j + 1) * GRID_W, cs]
            k2 = k_ref[pl.ds(ks, n_keys), cs]
            qs = jnp.concatenate([q2 * mask_lo, q2 * mask_hi], axis=0)
            s = lax.dot_general(qs, k2, (((1,), (1,)), ((), ())), preferred_element_type=F32)
            bias = jnp.concatenate(
                [jnp.concatenate([t2_ref[2 * p + hh, idx0 + 2 * m] for m in range(kh // 2)], axis=1)
                 for hh in range(2)], axis=0)
            s_ref[j * n_pairs + p] = s + bias

    for t in range(rows_per_step * n_pairs):
        s = s_ref[t]
        e = jnp.exp(s - jnp.max(s, axis=-1, keepdims=True))
        p_ref[t] = e.astype(BF16)
        linv_ref[t] = 1.0 / jnp.sum(e, axis=-1, keepdims=True)

    for j in range(rows_per_step):
        ks, _ = key_start[j]
        outs = []
        for p in range(n_pairs):
            t = j * n_pairs + p
            v2 = v_ref[pl.ds(ks, n_keys), LANES * p:LANES * (p + 1)]
            o = _dot(p_ref[t], v2) * linv_ref[t]
            outs.append(jnp.where(low, o[0:GRID_W], o[GRID_W:2 * GRID_W]))
        yna_ref[j * GRID_W:(j + 1) * GRID_W, :] = jnp.concatenate(outs, axis=1)

    ya = _dot(yna_ref[...].astype(BF16), wna_ref[...])
    yf = _dot(yfn_ref[...].astype(BF16), wfn_ref[...]) + bfn_ref[...]
    merged = ga_ref[...].astype(F32) * ya + gb_ref[...].astype(F32) * yf
    x1 = x_ref[...] + g1_ref[...] * _dot(merged.astype(BF16), wo_ref[...])
    o_ref[...] = x1
    ms = jnp.mean(x1 * x1, axis=-1, keepdims=True)
    y2 = x1 * lax.rsqrt(ms + EPS) * nw2_ref[...]
    h2_ref[...] = (y2 * (1.0 + sc2_ref[...]) + sh2_ref[...]).astype(BF16)


def _attn_out(q, k, v, t2, yfn, ga, gb, x, w_na, w_fn, b_fn, w_o, gate1, scale2, shift2, norm2_w):
    bsz, s, d = x.shape
    na_w = q.shape[-1]
    f_w = yfn.shape[-1]
    rows = s // GRID_W
    kh = min(WIN_H, rows)
    assert kh % 2 == 0 and (2 * HEAD_DIM) == LANES
    rows_per_step = 4
    tq = rows_per_step * GRID_W
    n_tiles = rows_per_step * (na_w // LANES)

    tok = lambda w: pl.BlockSpec((None, tq, w), lambda b, i: (b, i, 0))
    whole_seq = pl.BlockSpec((None, s, na_w), lambda b, i: (b, 0, 0))
    per_batch = pl.BlockSpec((None, 1, d), lambda b, i: (b, 0, 0))
    full = lambda a: pl.BlockSpec(a.shape, lambda b, i: (0,) * a.ndim)
    bfn = b_fn.reshape(1, d)
    nw2 = norm2_w.reshape(1, d)
    return pl.pallas_call(
        functools.partial(_attn_out_kernel, rows=rows, kh=kh, rows_per_step=rows_per_step),
        grid=(bsz, s // tq),
        in_specs=[tok(na_w), whole_seq, whole_seq, full(t2), tok(f_w), tok(d), tok(d), tok(d),
                  full(w_na), full(w_fn), full(bfn), full(w_o), per_batch, per_batch, per_batch,
                  full(nw2)],
        out_specs=(tok(d), tok(d)),
        out_shape=(jax.ShapeDtypeStruct((bsz, s, d), F32), jax.ShapeDtypeStruct((bsz, s, d), BF16)),
        scratch_shapes=[pltpu.VMEM((tq, na_w), F32),
                        pltpu.VMEM((n_tiles, 2 * GRID_W, kh * GRID_W), F32),
                        pltpu.VMEM((n_tiles, 2 * GRID_W, kh * GRID_W), BF16),
                        pltpu.VMEM((n_tiles, 2 * GRID_W, 1), F32)],
        compiler_params=_params("parallel", "arbitrary"),
        name="attn_out",
    )(q, k, v, t2, yfn, ga, gb, x, w_na, w_fn, bfn, w_o, gate1, scale2, shift2, nw2)


def _mlp_kernel(x_ref, h_ref, g2_ref, w1_ref, w2_ref, o_ref, acc_ref):
    f = pl.program_id(2)

    @pl.when(f == 0)
    def _init():
        acc_ref[...] = jnp.zeros_like(acc_ref)

    a = jnp.maximum(_dot(h_ref[...], w1_ref[...]), 0.0)
    acc_ref[...] += _dot((a * a).astype(BF16), w2_ref[...])

    @pl.when(f == pl.num_programs(2) - 1)
    def _epilogue():
        o_ref[...] = x_ref[...] + g2_ref[...] * acc_ref[...]


def _mlp(x, h, gate2, w1, w2):
    bsz, s, d = x.shape
    d_ff = w1.shape[1]
    tm = 1024
    tf = 1024
    tok = pl.BlockSpec((None, tm, d), lambda b, i, f: (b, i, 0))
    per_batch = pl.BlockSpec((None, 1, d), lambda b, i, f: (b, 0, 0))
    return pl.pallas_call(
        _mlp_kernel,
        grid=(bsz, s // tm, d_ff // tf),
        in_specs=[tok, tok, per_batch,
                  pl.BlockSpec((d, tf), lambda b, i, f: (0, f)),
                  pl.BlockSpec((tf, d), lambda b, i, f: (f, 0))],
        out_specs=tok,
        out_shape=jax.ShapeDtypeStruct((bsz, s, d), F32),
        scratch_shapes=[pltpu.VMEM((tm, d), F32)],
        compiler_params=_params("parallel", "parallel", "arbitrary"),
        name="mlp",
    )(x, h, gate2, w1, w2)


def kernel(x, c, norm1_w, norm2_w, w_ada, b_ada, w_in, b_in, q_norm_w, k_norm_w, rpb,
           w_na_out, w_fn_out, b_fn_out, w_o, w_mlp_in, w_mlp_out):
    bsz, s, d = x.shape
    depth = w_ada.shape[0]
    na_w = w_na_out.shape[1]
    f_w = w_fn_out.shape[1]
    n_heads = na_w // HEAD_DIM
    rows = s // GRID_W
    for l in range(depth):
        mod = _ada(c, w_ada[l], b_ada[l])
        shift1, scale1, gate1, shift2, scale2, gate2 = [
            mod[:, i * d:(i + 1) * d].reshape(bsz, 1, d) for i in range(N_MOD)]

        qk_w = jnp.concatenate([jnp.tile(q_norm_w[l], n_heads) * (HEAD_DIM ** -0.5),
                                jnp.tile(k_norm_w[l], n_heads)]).reshape(1, 2 * na_w)
        q, k, v, u, ga, gb = _inproj(x, scale1, shift1, norm1_w[l], w_in[l].astype(BF16), b_in[l],
                                     qk_w, na_w, f_w)
        yfn = _fourier(u)
        t2 = _bias_tables(rpb[l], rows)
        x, h2 = _attn_out(q, k, v, t2, yfn, ga, gb, x, w_na_out[l].astype(BF16),
                          w_fn_out[l].astype(BF16), b_fn_out[l], w_o[l].astype(BF16), gate1,
                          scale2, shift2, norm2_w[l])
        x = _mlp(x, h2, gate2, w_mlp_in[l].astype(BF16), w_mlp_out[l].astype(BF16))
    return x
```

```python
import functools

import numpy as np
import jax
import jax.numpy as jnp
from jax import lax
from jax.experimental import pallas as pl
from jax.experimental.pallas import tpu as pltpu

F32 = jnp.float32
BF16 = jnp.bfloat16

GRID_W = 64
HEAD_DIM = 64
WIN_H = 8
WIN_W = 16
N_FOURIER_GROUPS = 4
N_MOD = 6
EPS = 1e-6
MASK_VALUE = -1e30

SUBLANES = 8
LANES = 128
VMEM_LIMIT_BYTES = 56 * 1024 * 1024


def _dot(a, b):
    return jnp.dot(a, b, preferred_element_type=F32)


def _sigmoid(x):
    return 1.0 / (1.0 + jnp.exp(-x))


def _params(*semantics):
    return pltpu.CompilerParams(dimension_semantics=semantics, vmem_limit_bytes=VMEM_LIMIT_BYTES)


def _ada_kernel(c_ref, w_ref, b_ref, o_ref):
    c = c_ref[...]
    s = c * _sigmoid(c)
    o_ref[...] = jnp.dot(s, w_ref[...], preferred_element_type=F32,
                         precision=lax.Precision.HIGHEST) + b_ref[...]


def _ada(c, w, b):
    bsz, d = c.shape
    n = w.shape[1]
    rows = -(-bsz // SUBLANES) * SUBLANES
    c_pad = jnp.pad(c, ((0, rows - bsz), (0, 0)))
    tn = n // N_MOD
    out = pl.pallas_call(
        _ada_kernel,
        grid=(n // tn,),
        in_specs=[pl.BlockSpec((rows, d), lambda j: (0, 0)),
                  pl.BlockSpec((d, tn), lambda j: (0, j)),
                  pl.BlockSpec((1, tn), lambda j: (0, j))],
        out_specs=pl.BlockSpec((rows, tn), lambda j: (0, j)),
        out_shape=jax.ShapeDtypeStruct((rows, n), F32),
        compiler_params=_params("parallel"),
        name="ada",
    )(c_pad, w, b.reshape(1, n))
    return out[:bsz]


def _inproj_kernel(x_ref, sc_ref, sh_ref, nw_ref, w_ref, b_ref, qkw_ref, gsum_ref, gexp_ref,
                   q_ref, k_ref, v_ref, u_ref, ga_ref, gb_ref, *, na_w, f_w, d, chunk):
    for c0 in range(0, x_ref.shape[0], chunk):
        rows = slice(c0, c0 + chunk)
        x = x_ref[rows, :]
        ms = jnp.mean(x * x, axis=-1, keepdims=True)
        y = x * lax.rsqrt(ms + EPS) * nw_ref[...]
        h = (y * (1.0 + sc_ref[...]) + sh_ref[...]).astype(BF16)

        qk = 2 * na_w
        zqk = _dot(h, w_ref[:, 0:qk]) + b_ref[:, 0:qk]
        ssq = _dot((zqk * zqk).astype(BF16), gsum_ref[...])
        rinv = lax.rsqrt(ssq * (1.0 / HEAD_DIM) + EPS)
        rinv_hi = rinv.astype(BF16)
        rinv_lo = (rinv - rinv_hi.astype(F32)).astype(BF16)
        scale = _dot(jnp.concatenate([rinv_hi, rinv_lo], axis=1), gexp_ref[...])
        zn = zqk * scale * qkw_ref[...]
        q_ref[rows, :] = zn[:, 0:na_w].astype(BF16)
        k_ref[rows, :] = zn[:, na_w:qk].astype(BF16)

        o = qk
        v_ref[rows, :] = (_dot(h, w_ref[:, o:o + na_w]) + b_ref[:, o:o + na_w]).astype(BF16)
        o += na_w
        u_ref[rows, :] = _dot(h, w_ref[:, o:o + f_w]) + b_ref[:, o:o + f_w]
        o += f_w
        ga_ref[rows, :] = _sigmoid(_dot(h, w_ref[:, o:o + d]) + b_ref[:, o:o + d]).astype(BF16)
        o += d
        gb_ref[rows, :] = _sigmoid(_dot(h, w_ref[:, o:o + d]) + b_ref[:, o:o + d]).astype(BF16)


def _inproj(x, scale1, shift1, norm_w, w_in, b_in, qk_w, na_w, f_w):
    bsz, s, d = x.shape
    in_w = w_in.shape[1]
    tm = 1024
    qk = 2 * na_w
    n_grp = qk // HEAD_DIM
    grp = np.arange(qk) // HEAD_DIM
    gsum = jnp.asarray((grp[:, None] == np.arange(LANES)[None, :]).astype(np.float32), BF16)
    gexp1 = (np.arange(LANES)[:, None] == grp[None, :]).astype(np.float32)
    gexp = jnp.asarray(np.concatenate([gexp1, gexp1], axis=0), BF16)
    assert n_grp <= LANES

    tok = lambda w: pl.BlockSpec((None, tm, w), lambda b, i: (b, i, 0))
    per_batch = pl.BlockSpec((None, 1, d), lambda b, i: (b, 0, 0))
    full = lambda a: pl.BlockSpec(a.shape, lambda b, i: (0,) * a.ndim)
    nw = norm_w.reshape(1, d)
    bi = b_in.reshape(1, in_w)
    out_shapes = (
        jax.ShapeDtypeStruct((bsz, s, na_w), BF16),
        jax.ShapeDtypeStruct((bsz, s, na_w), BF16),
        jax.ShapeDtypeStruct((bsz, s, na_w), BF16),
        jax.ShapeDtypeStruct((bsz, s, f_w), F32),
        jax.ShapeDtypeStruct((bsz, s, d), BF16),
        jax.ShapeDtypeStruct((bsz, s, d), BF16),
    )
    return pl.pallas_call(
        functools.partial(_inproj_kernel, na_w=na_w, f_w=f_w, d=d, chunk=256),
        grid=(bsz, s // tm),
        in_specs=[tok(d), per_batch, per_batch, full(nw), full(w_in), full(bi), full(qk_w),
                  full(gsum), full(gexp)],
        out_specs=(tok(na_w), tok(na_w), tok(na_w), tok(f_w), tok(d), tok(d)),
        out_shape=out_shapes,
        compiler_params=_params("parallel", "parallel"),
        name="inproj",
    )(x, scale1, shift1, nw, w_in, bi, qk_w, gsum, gexp)


def _fourier_constants(rows, f_w):
    n = rows
    r8 = SUBLANES
    k = np.arange(n)
    ang_a = 2.0 * np.pi * np.outer(k, k) / n
    eye8 = np.eye(r8)
    norm = 1.0 / np.sqrt(n)
    fa_re = np.kron(np.cos(ang_a), eye8) * norm
    fa_im = np.kron(-np.sin(ang_a), eye8) * norm
    fa = np.concatenate([fa_re, fa_im], axis=0)

    nblk = n // r8
    s2 = k[None, :]
    s2p = k[:, None]
    gr = np.zeros((n, r8, r8, n))
    gi = np.zeros((n, r8, r8, n))
    for l in range(r8):
        ang = 2.0 * np.pi * (s2 * l / (n * GRID_W) + s2 * s2p / n)
        gr[:, l, l, :] = np.cos(ang) * norm
        gi[:, l, l, :] = -np.sin(ang) * norm
    gr = gr.reshape(n * r8, r8 * n)
    gi = gi.reshape(n * r8, r8 * n)
    g = np.block([[gr, -gi], [gi, gr]])

    t_idx = np.arange(nblk)[:, None, None]
    blk_of_row = (k // r8)[None, :, None]
    j_idx = np.arange(r8)[None, None, :]
    ang_t = 2.0 * np.pi * (t_idx * r8 + j_idx) * blk_of_row * r8 / (n * GRID_W)
    ang_t = ang_t.reshape(nblk, n * r8, 1)
    tw_cos = np.broadcast_to(np.cos(ang_t), (nblk, n * r8, LANES))
    tw_sin = np.broadcast_to(np.sin(ang_t), (nblk, n * r8, LANES))

    gd = f_w // N_FOURIER_GROUPS
    c = np.arange(gd)
    ang_c = 2.0 * np.pi * np.outer(c, c) / gd
    eye_g = np.eye(N_FOURIER_GROUPS)
    cc = np.kron(eye_g, np.cos(ang_c)) / np.sqrt(gd)
    sc = np.kron(eye_g, np.sin(ang_c)) / np.sqrt(gd)
    as_f32 = lambda a: jnp.asarray(np.ascontiguousarray(a, dtype=np.float32))
    to_bf16 = lambda a: as_f32(a).astype(BF16)
    return to_bf16(fa), to_bf16(g), to_bf16(cc), to_bf16(sc), as_f32(tw_cos), as_f32(tw_sin)


def _fourier_kernel(u_ref, fa_ref, g_ref, cc_ref, sc_ref, twc_ref, tws_ref, y_ref, sre_ref, sim_ref,
                    *, nblk):
    t = pl.program_id(1)
    n, r8, f_w = u_ref.shape
    m = n * r8

    @pl.when(t < nblk)
    def _stage_a():
        xin = u_ref[...].reshape(m, f_w).astype(BF16)
        a = _dot(fa_ref[...], xin)
        reps = f_w // LANES
        cos_t = jnp.concatenate([twc_ref[...]] * reps, axis=1)
        sin_t = jnp.concatenate([tws_ref[...]] * reps, axis=1)
        a_re = a[0:m]
        a_im = a[m:2 * m]
        b_re = a_re * cos_t + a_im * sin_t
        b_im = a_im * cos_t - a_re * sin_t
        off = pl.multiple_of(t * r8, r8)
        sre_ref[:, pl.ds(off, r8), :] = b_re.reshape(n, r8, f_w)
        sim_ref[:, pl.ds(off, r8), :] = b_im.reshape(n, r8, f_w)

    @pl.when(t >= nblk)
    def _stage_b():
        off = pl.multiple_of((t - nblk) * r8, r8)
        br = sre_ref[pl.ds(off, r8)].reshape(m, f_w)
        bi = sim_ref[pl.ds(off, r8)].reshape(m, f_w)
        bcat = jnp.concatenate([br, bi], axis=0).astype(BF16)
        p = _dot(g_ref[...], bcat)
        y = _dot(p[0:m].astype(BF16), cc_ref[...]) + _dot(p[m:2 * m].astype(BF16), sc_ref[...])
        y_ref[...] = y.reshape(n, r8, f_w)


def _fourier(u):
    bsz, s, f_w = u.shape
    rows = s // GRID_W
    assert rows == GRID_W, "the two-stage position DFT assumes a square token grid"
    r8 = SUBLANES
    nblk = rows // r8
    fa, g, cc, sc, tw_cos, tw_sin = _fourier_constants(rows, f_w)
    u4 = u.reshape(bsz, rows, GRID_W, f_w)
    stage_a_blk = lambda b, t: jnp.minimum(t, nblk - 1)
    tw_spec = pl.BlockSpec((None,) + tw_cos.shape[1:], lambda b, t: (stage_a_blk(b, t), 0, 0))
    y4 = pl.pallas_call(
        functools.partial(_fourier_kernel, nblk=nblk),
        grid=(bsz, 2 * nblk),
        in_specs=[
            pl.BlockSpec((None, rows, r8, f_w), lambda b, t: (b, 0, stage_a_blk(b, t), 0)),
            pl.BlockSpec(fa.shape, lambda b, t: (0, 0)),
            pl.BlockSpec(g.shape, lambda b, t: (0, 0)),
            pl.BlockSpec(cc.shape, lambda b, t: (0, 0)),
            pl.BlockSpec(sc.shape, lambda b, t: (0, 0)),
            tw_spec, tw_spec,
        ],
        out_specs=pl.BlockSpec((None, GRID_W, r8, f_w),
                               lambda b, t: (b, 0, jnp.maximum(t - nblk, 0), 0)),
        out_shape=jax.ShapeDtypeStruct((bsz, GRID_W, rows, f_w), F32),
        scratch_shapes=[pltpu.VMEM((rows, GRID_W, f_w), F32), pltpu.VMEM((rows, GRID_W, f_w), F32)],
        compiler_params=_params("arbitrary", "arbitrary"),
        name="fourier",
    )(u4, fa, g, cc, sc, tw_cos, tw_sin)
    return y4.reshape(bsz, s, f_w)


def _bias_tables(rpb, rows):
    kw = min(WIN_W, GRID_W)
    cols = np.arange(GRID_W)
    col_start = np.clip(cols - kw // 2, 0, GRID_W - kw)
    kj = cols[None, :]
    allowed = (kj >= col_start[:, None]) & (kj < col_start[:, None] + kw)
    rel = np.clip(kj - cols[:, None] + (WIN_W - 1), 0, 2 * WIN_W - 2)
    onehot = (rel[None] == np.arange(2 * WIN_W - 1)[:, None, None]).astype(np.float32)
    t = jnp.einsum("hmd,dck->hmck", rpb.astype(F32), jnp.asarray(onehot),
                   precision=lax.Precision.HIGHEST)
    t = jnp.where(jnp.asarray(allowed)[None, None], t, MASK_VALUE)
    return jnp.concatenate([t[:, :-1], t[:, 1:]], axis=-1)


def _attn_out_kernel(q_ref, k_ref, v_ref, t2_ref, yfn_ref, ga_ref, gb_ref, x_ref,
                     wna_ref, wfn_ref, bfn_ref, wo_ref, g1_ref, sc2_ref, sh2_ref, nw2_ref,
                     o_ref, h2_ref, yna_ref, s_ref, p_ref, linv_ref, *, rows, kh, rows_per_step):
    rb = pl.program_id(1)
    n_pairs = q_ref.shape[1] // LANES
    lane = lax.broadcasted_iota(jnp.int32, (1, LANES), 1)
    low = lane < HEAD_DIM
    mask_lo = jnp.where(low, 1.0, 0.0).astype(BF16)
    mask_hi = jnp.where(low, 0.0, 1.0).astype(BF16)
    n_keys = kh * GRID_W

    key_start = []
    for j in range(rows_per_step):
        r = rb * rows_per_step + j
        rs = jnp.clip(r - kh // 2, 0, rows - kh)
        key_start.append((pl.multiple_of(rs * GRID_W, GRID_W), rs - r + (WIN_H - 1)))

    for j in range(rows_per_step):
        ks, idx0 = key_start[j]
        for p in range(n_pairs):
            cs = slice(LANES * p, LANES * (p + 1))
            q2 = q_ref[j * GRID_W:(j + 1) * GRID_W, cs]
            k2 = k_ref[pl.ds(ks, n_keys), cs]
            qs = jnp.concatenate([q2 * mask_lo, q2 * mask_hi], axis=0)
            s = lax.dot_general(qs, k2, (((1,), (1,)), ((), ())), preferred_element_type=F32)
            bias = jnp.concatenate(
                [jnp.concatenate([t2_ref[2 * p + hh, idx0 + 2 * m] for m in range(kh // 2)], axis=1)
                 for hh in range(2)], axis=0)
            s_ref[j * n_pairs + p] = s + bias

    for t in range(rows_per_step * n_pairs):
        s = s_ref[t]
        e = jnp.exp(s - jnp.max(s, axis=-1, keepdims=True))
        p_ref[t] = e.astype(BF16)
        linv_ref[t] = 1.0 / jnp.sum(e, axis=-1, keepdims=True)

    for j in range(rows_per_step):
        ks, _ = key_start[j]
        outs = []
        for p in range(n_pairs):
            t = j * n_pairs + p
            v2 = v_ref[pl.ds(ks, n_keys), LANES * p:LANES * (p + 1)]
            o = _dot(p_ref[t], v2) * linv_ref[t]
            outs.append(jnp.where(low, o[0:GRID_W], o[GRID_W:2 * GRID_W]))
        yna_ref[j * GRID_W:(j + 1) * GRID_W, :] = jnp.concatenate(outs, axis=1)

    ya = _dot(yna_ref[...].astype(BF16), wna_ref[...])
    yf = _dot(yfn_ref[...].astype(BF16), wfn_ref[...]) + bfn_ref[...]
    merged = ga_ref[...].astype(F32) * ya + gb_ref[...].astype(F32) * yf
    x1 = x_ref[...] + g1_ref[...] * _dot(merged.astype(BF16), wo_ref[...])
    o_ref[...] = x1
    ms = jnp.mean(x1 * x1, axis=-1, keepdims=True)
    y2 = x1 * lax.rsqrt(ms + EPS) * nw2_ref[...]
    h2_ref[...] = (y2 * (1.0 + sc2_ref[...]) + sh2_ref[...]).astype(BF16)


def _attn_out(q, k, v, t2, yfn, ga, gb, x, w_na, w_fn, b_fn, w_o, gate1, scale2, shift2, norm2_w):
    bsz, s, d = x.shape
    na_w = q.shape[-1]
    f_w = yfn.shape[-1]
    rows = s // GRID_W
    kh = min(WIN_H, rows)
    assert kh % 2 == 0 and (2 * HEAD_DIM) == LANES
    rows_per_step = 8
    tq = rows_per_step * GRID_W
    n_tiles = rows_per_step * (na_w // LANES)

    tok = lambda w: pl.BlockSpec((None, tq, w), lambda b, i: (b, i, 0))
    whole_seq = pl.BlockSpec((None, s, na_w), lambda b, i: (b, 0, 0))
    per_batch = pl.BlockSpec((None, 1, d), lambda b, i: (b, 0, 0))
    full = lambda a: pl.BlockSpec(a.shape, lambda b, i: (0,) * a.ndim)
    bfn = b_fn.reshape(1, d)
    nw2 = norm2_w.reshape(1, d)
    return pl.pallas_call(
        functools.partial(_attn_out_kernel, rows=rows, kh=kh, rows_per_step=rows_per_step),
        grid=(bsz, s // tq),
        in_specs=[tok(na_w), whole_seq, whole_seq, full(t2), tok(f_w), tok(d), tok(d), tok(d),
                  full(w_na), full(w_fn), full(bfn), full(w_o), per_batch, per_batch, per_batch,
                  full(nw2)],
        out_specs=(tok(d), tok(d)),
        out_shape=(jax.ShapeDtypeStruct((bsz, s, d), F32), jax.ShapeDtypeStruct((bsz, s, d), BF16)),
        scratch_shapes=[pltpu.VMEM((tq, na_w), F32),
                        pltpu.VMEM((n_tiles, 2 * GRID_W, kh * GRID_W), F32),
                        pltpu.VMEM((n_tiles, 2 * GRID_W, kh * GRID_W), BF16),
                        pltpu.VMEM((n_tiles, 2 * GRID_W, 1), F32)],
        compiler_params=_params("parallel", "arbitrary"),
        name="attn_out",
    )(q, k, v, t2, yfn, ga, gb, x, w_na, w_fn, bfn, w_o, gate1, scale2, shift2, nw2)


def _mlp_kernel(x_ref, h_ref, g2_ref, w1_ref, w2_ref, o_ref, acc_ref):
    f = pl.program_id(2)

    @pl.when(f == 0)
    def _init():
        acc_ref[...] = jnp.zeros_like(acc_ref)

    a = jnp.maximum(_dot(h_ref[...], w1_ref[...]), 0.0)
    acc_ref[...] += _dot((a * a).astype(BF16), w2_ref[...])

    @pl.when(f == pl.num_programs(2) - 1)
    def _epilogue():
        o_ref[...] = x_ref[...] + g2_ref[...] * acc_ref[...]


def _mlp(x, h, gate2, w1, w2):
    bsz, s, d = x.shape
    d_ff = w1.shape[1]
    tm = 1024
    tf = 1024
    tok = pl.BlockSpec((None, tm, d), lambda b, i, f: (b, i, 0))
    per_batch = pl.BlockSpec((None, 1, d), lambda b, i, f: (b, 0, 0))
    return pl.pallas_call(
        _mlp_kernel,
        grid=(bsz, s // tm, d_ff // tf),
        in_specs=[tok, tok, per_batch,
                  pl.BlockSpec((d, tf), lambda b, i, f: (0, f)),
                  pl.BlockSpec((tf, d), lambda b, i, f: (f, 0))],
        out_specs=tok,
        out_shape=jax.ShapeDtypeStruct((bsz, s, d), F32),
        scratch_shapes=[pltpu.VMEM((tm, d), F32)],
        compiler_params=_params("parallel", "parallel", "arbitrary"),
        name="mlp",
    )(x, h, gate2, w1, w2)


def kernel(x, c, norm1_w, norm2_w, w_ada, b_ada, w_in, b_in, q_norm_w, k_norm_w, rpb,
           w_na_out, w_fn_out, b_fn_out, w_o, w_mlp_in, w_mlp_out):
    bsz, s, d = x.shape
    depth = w_ada.shape[0]
    na_w = w_na_out.shape[1]
    f_w = w_fn_out.shape[1]
    n_heads = na_w // HEAD_DIM
    rows = s // GRID_W
    for l in range(depth):
        mod = _ada(c, w_ada[l], b_ada[l])
        shift1, scale1, gate1, shift2, scale2, gate2 = [
            mod[:, i * d:(i + 1) * d].reshape(bsz, 1, d) for i in range(N_MOD)]

        qk_w = jnp.concatenate([jnp.tile(q_norm_w[l], n_heads) * (HEAD_DIM ** -0.5),
                                jnp.tile(k_norm_w[l], n_heads)]).reshape(1, 2 * na_w)
        q, k, v, u, ga, gb = _inproj(x, scale1, shift1, norm1_w[l], w_in[l].astype(BF16), b_in[l],
                                     qk_w, na_w, f_w)
        yfn = _fourier(u)
        t2 = _bias_tables(rpb[l], rows)
        x, h2 = _attn_out(q, k, v, t2, yfn, ga, gb, x, w_na_out[l].astype(BF16),
                          w_fn_out[l].astype(BF16), b_fn_out[l], w_o[l].astype(BF16), gate1,
                          scale2, shift2, norm2_w[l])
        x = _mlp(x, h2, gate2, w_mlp_in[l].astype(BF16), w_mlp_out[l].astype(BF16))
    return x
```

```python
import functools

import numpy as np
import jax
import jax.numpy as jnp
from jax import lax
from jax.experimental import pallas as pl
from jax.experimental.pallas import tpu as pltpu

F32 = jnp.float32
BF16 = jnp.bfloat16

GRID_W = 64
HEAD_DIM = 64
WIN_H = 8
WIN_W = 16
N_FOURIER_GROUPS = 4
N_MOD = 6
EPS = 1e-6
MASK_VALUE = -1e30

SUBLANES = 8
LANES = 128
VMEM_LIMIT_BYTES = 56 * 1024 * 1024


def _dot(a, b):
    return jnp.dot(a, b, preferred_element_type=F32)


def _sigmoid(x):
    return 1.0 / (1.0 + jnp.exp(-x))


def _params(*semantics):
    return pltpu.CompilerParams(dimension_semantics=semantics, vmem_limit_bytes=VMEM_LIMIT_BYTES)


def _ada_kernel(c_ref, w_ref, b_ref, o_ref):
    c = c_ref[...]
    s = c * _sigmoid(c)
    o_ref[...] = jnp.dot(s, w_ref[...], preferred_element_type=F32,
                         precision=lax.Precision.HIGHEST) + b_ref[...]


def _ada(c, w, b):
    bsz, d = c.shape
    n = w.shape[1]
    rows = -(-bsz // SUBLANES) * SUBLANES
    c_pad = jnp.pad(c, ((0, rows - bsz), (0, 0)))
    tn = n // N_MOD
    out = pl.pallas_call(
        _ada_kernel,
        grid=(n // tn,),
        in_specs=[pl.BlockSpec((rows, d), lambda j: (0, 0)),
                  pl.BlockSpec((d, tn), lambda j: (0, j)),
                  pl.BlockSpec((1, tn), lambda j: (0, j))],
        out_specs=pl.BlockSpec((rows, tn), lambda j: (0, j)),
        out_shape=jax.ShapeDtypeStruct((rows, n), F32),
        compiler_params=_params("parallel"),
        name="ada",
    )(c_pad, w, b.reshape(1, n))
    return out[:bsz]


def _inproj_kernel(x_ref, sc_ref, sh_ref, nw_ref, w_ref, b_ref, qkw_ref, gsum_ref, gexp_ref,
                   q_ref, k_ref, v_ref, u_ref, ga_ref, gb_ref, *, na_w, f_w, d, chunk):
    for c0 in range(0, x_ref.shape[0], chunk):
        rows = slice(c0, c0 + chunk)
        x = x_ref[rows, :]
        ms = jnp.mean(x * x, axis=-1, keepdims=True)
        y = x * lax.rsqrt(ms + EPS) * nw_ref[...]
        h = (y * (1.0 + sc_ref[...]) + sh_ref[...]).astype(BF16)

        qk = 2 * na_w
        zqk = _dot(h, w_ref[:, 0:qk]) + b_ref[:, 0:qk]
        ssq = _dot((zqk * zqk).astype(BF16), gsum_ref[...])
        rinv = lax.rsqrt(ssq * (1.0 / HEAD_DIM) + EPS)
        rinv_hi = rinv.astype(BF16)
        rinv_lo = (rinv - rinv_hi.astype(F32)).astype(BF16)
        scale = _dot(jnp.concatenate([rinv_hi, rinv_lo], axis=1), gexp_ref[...])
        zn = zqk * scale * qkw_ref[...]
        q_ref[rows, :] = zn[:, 0:na_w].astype(BF16)
        k_ref[rows, :] = zn[:, na_w:qk].astype(BF16)

        o = qk
        v_ref[rows, :] = (_dot(h, w_ref[:, o:o + na_w]) + b_ref[:, o:o + na_w]).astype(BF16)
        o += na_w
        u_ref[rows, :] = _dot(h, w_ref[:, o:o + f_w]) + b_ref[:, o:o + f_w]
        o += f_w
        ga_ref[rows, :] = _sigmoid(_dot(h, w_ref[:, o:o + d]) + b_ref[:, o:o + d]).astype(BF16)
        o += d
        gb_ref[rows, :] = _sigmoid(_dot(h, w_ref[:, o:o + d]) + b_ref[:, o:o + d]).astype(BF16)


def _inproj(x, scale1, shift1, norm_w, w_in, b_in, qk_w, na_w, f_w):
    bsz, s, d = x.shape
    in_w = w_in.shape[1]
    tm = 1024
    qk = 2 * na_w
    n_grp = qk // HEAD_DIM
    grp = np.arange(qk) // HEAD_DIM
    gsum = jnp.asarray((grp[:, None] == np.arange(LANES)[None, :]).astype(np.float32), BF16)
    gexp1 = (np.arange(LANES)[:, None] == grp[None, :]).astype(np.float32)
    gexp = jnp.asarray(np.concatenate([gexp1, gexp1], axis=0), BF16)
    assert n_grp <= LANES

    tok = lambda w: pl.BlockSpec((None, tm, w), lambda b, i: (b, i, 0))
    per_batch = pl.BlockSpec((None, 1, d), lambda b, i: (b, 0, 0))
    full = lambda a: pl.BlockSpec(a.shape, lambda b, i: (0,) * a.ndim)
    nw = norm_w.reshape(1, d)
    bi = b_in.reshape(1, in_w)
    out_shapes = (
        jax.ShapeDtypeStruct((bsz, s, na_w), BF16),
        jax.ShapeDtypeStruct((bsz, s, na_w), BF16),
        jax.ShapeDtypeStruct((bsz, s, na_w), BF16),
        jax.ShapeDtypeStruct((bsz, s, f_w), F32),
        jax.ShapeDtypeStruct((bsz, s, d), BF16),
        jax.ShapeDtypeStruct((bsz, s, d), BF16),
    )
    return pl.pallas_call(
        functools.partial(_inproj_kernel, na_w=na_w, f_w=f_w, d=d, chunk=256),
        grid=(bsz, s // tm),
        in_specs=[tok(d), per_batch, per_batch, full(nw), full(w_in), full(bi), full(qk_w),
                  full(gsum), full(gexp)],
        out_specs=(tok(na_w), tok(na_w), tok(na_w), tok(f_w), tok(d), tok(d)),
        out_shape=out_shapes,
        compiler_params=_params("parallel", "parallel"),
        name="inproj",
    )(x, scale1, shift1, nw, w_in, bi, qk_w, gsum, gexp)


def _fourier_constants(rows, f_w):
    n = rows
    r8 = SUBLANES
    k = np.arange(n)
    ang_a = 2.0 * np.pi * np.outer(k, k) / n
    eye8 = np.eye(r8)
    norm = 1.0 / np.sqrt(n)
    fa_re = np.kron(np.cos(ang_a), eye8) * norm
    fa_im = np.kron(-np.sin(ang_a), eye8) * norm
    fa = np.concatenate([fa_re, fa_im], axis=0)

    nblk = n // r8
    s2 = k[None, :]
    s2p = k[:, None]
    gr = np.zeros((n, r8, r8, n))
    gi = np.zeros((n, r8, r8, n))
    for l in range(r8):
        ang = 2.0 * np.pi * (s2 * l / (n * GRID_W) + s2 * s2p / n)
        gr[:, l, l, :] = np.cos(ang) * norm
        gi[:, l, l, :] = -np.sin(ang) * norm
    gr = gr.reshape(n * r8, r8 * n)
    gi = gi.reshape(n * r8, r8 * n)
    g = np.block([[gr, -gi], [gi, gr]])

    t_idx = np.arange(nblk)[:, None, None]
    blk_of_row = (k // r8)[None, :, None]
    j_idx = np.arange(r8)[None, None, :]
    ang_t = 2.0 * np.pi * (t_idx * r8 + j_idx) * blk_of_row * r8 / (n * GRID_W)
    ang_t = ang_t.reshape(nblk, n * r8, 1)
    tw_cos = np.broadcast_to(np.cos(ang_t), (nblk, n * r8, LANES))
    tw_sin = np.broadcast_to(np.sin(ang_t), (nblk, n * r8, LANES))

    gd = f_w // N_FOURIER_GROUPS
    c = np.arange(gd)
    ang_c = 2.0 * np.pi * np.outer(c, c) / gd
    eye_g = np.eye(N_FOURIER_GROUPS)
    cc = np.kron(eye_g, np.cos(ang_c)) / np.sqrt(gd)
    sc = np.kron(eye_g, np.sin(ang_c)) / np.sqrt(gd)
    as_f32 = lambda a: jnp.asarray(np.ascontiguousarray(a, dtype=np.float32))
    to_bf16 = lambda a: as_f32(a).astype(BF16)
    return to_bf16(fa), to_bf16(g), to_bf16(cc), to_bf16(sc), as_f32(tw_cos), as_f32(tw_sin)


def _fourier_kernel(u_ref, fa_ref, g_ref, cc_ref, sc_ref, twc_ref, tws_ref, y_ref, sre_ref, sim_ref,
                    *, steps_per_stage, blocks_per_step):
    t = pl.program_id(1)
    n, _, f_w = u_ref.shape
    r8 = SUBLANES
    m = n * r8

    @pl.when(t < steps_per_stage)
    def _stage_a():
        for j in range(blocks_per_step):
            xin = u_ref[:, j * r8:(j + 1) * r8, :].reshape(m, f_w).astype(BF16)
            a = _dot(fa_ref[...], xin)
            reps = f_w // LANES
            cos_t = jnp.concatenate([twc_ref[j]] * reps, axis=1)
            sin_t = jnp.concatenate([tws_ref[j]] * reps, axis=1)
            a_re = a[0:m]
            a_im = a[m:2 * m]
            b_re = a_re * cos_t + a_im * sin_t
            b_im = a_im * cos_t - a_re * sin_t
            off = pl.multiple_of((t * blocks_per_step + j) * r8, r8)
            sre_ref[:, pl.ds(off, r8), :] = b_re.reshape(n, r8, f_w)
            sim_ref[:, pl.ds(off, r8), :] = b_im.reshape(n, r8, f_w)

    @pl.when(t >= steps_per_stage)
    def _stage_b():
        for j in range(blocks_per_step):
            off = pl.multiple_of(((t - steps_per_stage) * blocks_per_step + j) * r8, r8)
            br = sre_ref[pl.ds(off, r8)].reshape(m, f_w)
            bi = sim_ref[pl.ds(off, r8)].reshape(m, f_w)
            bcat = jnp.concatenate([br, bi], axis=0).astype(BF16)
            p = _dot(g_ref[...], bcat)
            y = (_dot(p[0:m].astype(BF16), cc_ref[...])
                 + _dot(p[m:2 * m].astype(BF16), sc_ref[...]))
            y_ref[:, j * r8:(j + 1) * r8, :] = y.reshape(n, r8, f_w)


def _fourier(u):
    bsz, s, f_w = u.shape
    rows = s // GRID_W
    assert rows == GRID_W, "the two-stage position DFT assumes a square token grid"
    r8 = SUBLANES
    nblk = rows // r8
    fa, g, cc, sc, tw_cos, tw_sin = _fourier_constants(rows, f_w)
    u4 = u.reshape(bsz, rows, GRID_W, f_w)
    blocks_per_step = 4
    steps = nblk // blocks_per_step
    width = blocks_per_step * r8
    stage_a_step = lambda b, t: jnp.minimum(t, steps - 1)
    stage_b_step = lambda b, t: jnp.maximum(t - steps, 0)
    tw_spec = pl.BlockSpec((blocks_per_step,) + tw_cos.shape[1:],
                           lambda b, t: (stage_a_step(b, t), 0, 0))
    y4 = pl.pallas_call(
        functools.partial(_fourier_kernel, steps_per_stage=steps, blocks_per_step=blocks_per_step),
        grid=(bsz, 2 * steps),
        in_specs=[
            pl.BlockSpec((None, rows, width, f_w), lambda b, t: (b, 0, stage_a_step(b, t), 0)),
            pl.BlockSpec(fa.shape, lambda b, t: (0, 0)),
            pl.BlockSpec(g.shape, lambda b, t: (0, 0)),
            pl.BlockSpec(cc.shape, lambda b, t: (0, 0)),
            pl.BlockSpec(sc.shape, lambda b, t: (0, 0)),
            tw_spec, tw_spec,
        ],
        out_specs=pl.BlockSpec((None, GRID_W, width, f_w),
                               lambda b, t: (b, 0, stage_b_step(b, t), 0)),
        out_shape=jax.ShapeDtypeStruct((bsz, GRID_W, rows, f_w), F32),
        scratch_shapes=[pltpu.VMEM((rows, GRID_W, f_w), F32), pltpu.VMEM((rows, GRID_W, f_w), F32)],
        compiler_params=_params("arbitrary", "arbitrary"),
        name="fourier",
    )(u4, fa, g, cc, sc, tw_cos, tw_sin)
    return y4.reshape(bsz, s, f_w)


def _bias_tables(rpb, rows):
    kw = min(WIN_W, GRID_W)
    cols = np.arange(GRID_W)
    col_start = np.clip(cols - kw // 2, 0, GRID_W - kw)
    kj = cols[None, :]
    allowed = (kj >= col_start[:, None]) & (kj < col_start[:, None] + kw)
    rel = np.clip(kj - cols[:, None] + (WIN_W - 1), 0, 2 * WIN_W - 2)
    onehot = (rel[None] == np.arange(2 * WIN_W - 1)[:, None, None]).astype(np.float32)
    t = jnp.einsum("hmd,dck->hmck", rpb.astype(F32), jnp.asarray(onehot),
                   precision=lax.Precision.HIGHEST)
    t = jnp.where(jnp.asarray(allowed)[None, None], t, MASK_VALUE)
    return jnp.concatenate([t[:, :-1], t[:, 1:]], axis=-1)


def _attn_out_kernel(q_ref, k_ref, v_ref, t2_ref, yfn_ref, ga_ref, gb_ref, x_ref,
                     wna_ref, wfn_ref, bfn_ref, wo_ref, g1_ref, sc2_ref, sh2_ref, nw2_ref,
                     o_ref, h2_ref, yna_ref, s_ref, p_ref, linv_ref, *, rows, kh, rows_per_step):
    rb = pl.program_id(1)
    n_pairs = q_ref.shape[1] // LANES
    lane = lax.broadcasted_iota(jnp.int32, (1, LANES), 1)
    low = lane < HEAD_DIM
    mask_lo = jnp.where(low, 1.0, 0.0).astype(BF16)
    mask_hi = jnp.where(low, 0.0, 1.0).astype(BF16)
    n_keys = kh * GRID_W

    key_start = []
    for j in range(rows_per_step):
        r = rb * rows_per_step + j
        rs = jnp.clip(r - kh // 2, 0, rows - kh)
        key_start.append((pl.multiple_of(rs * GRID_W, GRID_W), rs - r + (WIN_H - 1)))

    for j in range(rows_per_step):
        ks, idx0 = key_start[j]
        for p in range(n_pairs):
            cs = slice(LANES * p, LANES * (p + 1))
            q2 = q_ref[j * GRID_W:(j + 1) * GRID_W, cs]
            k2 = k_ref[pl.ds(ks, n_keys), cs]
            qs = jnp.concatenate([q2 * mask_lo, q2 * mask_hi], axis=0)
            s = lax.dot_general(qs, k2, (((1,), (1,)), ((), ())), preferred_element_type=F32)
            bias = jnp.concatenate(
                [jnp.concatenate([t2_ref[2 * p + hh, idx0 + 2 * m] for m in range(kh // 2)], axis=1)
                 for hh in range(2)], axis=0)
            s_ref[j * n_pairs + p] = s + bias

    sub = 32
    for t in range(rows_per_step * n_pairs):
        for r0 in range(0, 2 * GRID_W, sub):
            s = s_ref[t, r0:r0 + sub, :]
            e = jnp.exp(s - jnp.max(s, axis=-1, keepdims=True))
            p_ref[t, r0:r0 + sub, :] = e.astype(BF16)
            linv_ref[t, r0:r0 + sub, :] = 1.0 / jnp.sum(e, axis=-1, keepdims=True)

    for j in range(rows_per_step):
        ks, _ = key_start[j]
        outs = []
        for p in range(n_pairs):
            t = j * n_pairs + p
            v2 = v_ref[pl.ds(ks, n_keys), LANES * p:LANES * (p + 1)]
            o = _dot(p_ref[t], v2) * linv_ref[t]
            outs.append(jnp.where(low, o[0:GRID_W], o[GRID_W:2 * GRID_W]))
        yna_ref[j * GRID_W:(j + 1) * GRID_W, :] = jnp.concatenate(outs, axis=1)

    ya = _dot(yna_ref[...].astype(BF16), wna_ref[...])
    yf = _dot(yfn_ref[...].astype(BF16), wfn_ref[...]) + bfn_ref[...]
    merged = ga_ref[...].astype(F32) * ya + gb_ref[...].astype(F32) * yf
    x1 = x_ref[...] + g1_ref[...] * _dot(merged.astype(BF16), wo_ref[...])
    o_ref[...] = x1
    ms = jnp.mean(x1 * x1, axis=-1, keepdims=True)
    y2 = x1 * lax.rsqrt(ms + EPS) * nw2_ref[...]
    h2_ref[...] = (y2 * (1.0 + sc2_ref[...]) + sh2_ref[...]).astype(BF16)


def _attn_out(q, k, v, t2, yfn, ga, gb, x, w_na, w_fn, b_fn, w_o, gate1, scale2, shift2, norm2_w):
    bsz, s, d = x.shape
    na_w = q.shape[-1]
    f_w = yfn.shape[-1]
    rows = s // GRID_W
    kh = min(WIN_H, rows)
    assert kh % 2 == 0 and (2 * HEAD_DIM) == LANES
    rows_per_step = 8
    tq = rows_per_step * GRID_W
    n_tiles = rows_per_step * (na_w // LANES)

    tok = lambda w: pl.BlockSpec((None, tq, w), lambda b, i: (b, i, 0))
    whole_seq = pl.BlockSpec((None, s, na_w), lambda b, i: (b, 0, 0))
    per_batch = pl.BlockSpec((None, 1, d), lambda b, i: (b, 0, 0))
    full = lambda a: pl.BlockSpec(a.shape, lambda b, i: (0,) * a.ndim)
    bfn = b_fn.reshape(1, d)
    nw2 = norm2_w.reshape(1, d)
    return pl.pallas_call(
        functools.partial(_attn_out_kernel, rows=rows, kh=kh, rows_per_step=rows_per_step),
        grid=(bsz, s // tq),
        in_specs=[tok(na_w), whole_seq, whole_seq, full(t2), tok(f_w), tok(d), tok(d), tok(d),
                  full(w_na), full(w_fn), full(bfn), full(w_o), per_batch, per_batch, per_batch,
                  full(nw2)],
        out_specs=(tok(d), tok(d)),
        out_shape=(jax.ShapeDtypeStruct((bsz, s, d), F32), jax.ShapeDtypeStruct((bsz, s, d), BF16)),
        scratch_shapes=[pltpu.VMEM((tq, na_w), F32),
                        pltpu.VMEM((n_tiles, 2 * GRID_W, kh * GRID_W), F32),
                        pltpu.VMEM((n_tiles, 2 * GRID_W, kh * GRID_W), BF16),
                        pltpu.VMEM((n_tiles, 2 * GRID_W, 1), F32)],
        compiler_params=_params("parallel", "arbitrary"),
        name="attn_out",
    )(q, k, v, t2, yfn, ga, gb, x, w_na, w_fn, bfn, w_o, gate1, scale2, shift2, nw2)


def _mlp_kernel(x_ref, h_ref, g2_ref, w1_ref, w2_ref, o_ref, acc_ref, *, chunk):
    f = pl.program_id(2)

    @pl.when(f == 0)
    def _init():
        acc_ref[...] = jnp.zeros_like(acc_ref)

    h = h_ref[...]
    part = None
    for c0 in range(0, w1_ref.shape[1], chunk):
        a = jnp.maximum(_dot(h, w1_ref[:, c0:c0 + chunk]), 0.0)
        term = _dot((a * a).astype(BF16), w2_ref[c0:c0 + chunk, :])
        part = term if part is None else part + term
    acc_ref[...] += part

    @pl.when(f == pl.num_programs(2) - 1)
    def _epilogue():
        o_ref[...] = x_ref[...] + g2_ref[...] * acc_ref[...]


def _mlp(x, h, gate2, w1, w2):
    bsz, s, d = x.shape
    d_ff = w1.shape[1]
    tm = 1024
    tf = 2048
    tok = pl.BlockSpec((None, tm, d), lambda b, i, f: (b, i, 0))
    per_batch = pl.BlockSpec((None, 1, d), lambda b, i, f: (b, 0, 0))
    return pl.pallas_call(
        functools.partial(_mlp_kernel, chunk=1024),
        grid=(bsz, s // tm, d_ff // tf),
        in_specs=[tok, tok, per_batch,
                  pl.BlockSpec((d, tf), lambda b, i, f: (0, f)),
                  pl.BlockSpec((tf, d), lambda b, i, f: (f, 0))],
        out_specs=tok,
        out_shape=jax.ShapeDtypeStruct((bsz, s, d), F32),
        scratch_shapes=[pltpu.VMEM((tm, d), F32)],
        compiler_params=_params("parallel", "parallel", "arbitrary"),
        name="mlp",
    )(x, h, gate2, w1, w2)


def kernel(x, c, norm1_w, norm2_w, w_ada, b_ada, w_in, b_in, q_norm_w, k_norm_w, rpb,
           w_na_out, w_fn_out, b_fn_out, w_o, w_mlp_in, w_mlp_out):
    bsz, s, d = x.shape
    depth = w_ada.shape[0]
    na_w = w_na_out.shape[1]
    f_w = w_fn_out.shape[1]
    n_heads = na_w // HEAD_DIM
    rows = s // GRID_W
    for l in range(depth):
        mod = _ada(c, w_ada[l], b_ada[l])
        shift1, scale1, gate1, shift2, scale2, gate2 = [
            mod[:, i * d:(i + 1) * d].reshape(bsz, 1, d) for i in range(N_MOD)]

        qk_w = jnp.concatenate([jnp.tile(q_norm_w[l], n_heads) * (HEAD_DIM ** -0.5),
                                jnp.tile(k_norm_w[l], n_heads)]).reshape(1, 2 * na_w)
        q, k, v, u, ga, gb = _inproj(x, scale1, shift1, norm1_w[l], w_in[l].astype(BF16), b_in[l],
                                     qk_w, na_w, f_w)
        yfn = _fourier(u)
        t2 = _bias_tables(rpb[l], rows)
        x, h2 = _attn_out(q, k, v, t2, yfn, ga, gb, x, w_na_out[l].astype(BF16),
                          w_fn_out[l].astype(BF16), b_fn_out[l], w_o[l].astype(BF16), gate1,
                          scale2, shift2, norm2_w[l])
        x = _mlp(x, h2, gate2, w_mlp_in[l].astype(BF16), w_mlp_out[l].astype(BF16))
    return x
```

```python
import functools

import numpy as np
import jax
import jax.numpy as jnp
from jax import lax
from jax.experimental import pallas as pl
from jax.experimental.pallas import tpu as pltpu

F32 = jnp.float32
BF16 = jnp.bfloat16

GRID_W = 64
HEAD_DIM = 64
WIN_H = 8
WIN_W = 16
N_FOURIER_GROUPS = 4
N_MOD = 6
EPS = 1e-6
MASK_VALUE = -1e30

SUBLANES = 8
LANES = 128
VMEM_LIMIT_BYTES = 56 * 1024 * 1024


def _dot(a, b):
    return jnp.dot(a, b, preferred_element_type=F32)


def _sigmoid(x):
    return 1.0 / (1.0 + jnp.exp(-x))


def _params(*semantics):
    return pltpu.CompilerParams(dimension_semantics=semantics, vmem_limit_bytes=VMEM_LIMIT_BYTES)


def _ada_kernel(ct_ref, w_ref, b_ref, o_ref):
    ct = ct_ref[...]
    st = ct * _sigmoid(ct)
    w = w_ref[...]
    rows = [jnp.sum(w * st[:, b:b + 1], axis=0, keepdims=True) for b in range(ct.shape[1])]
    o_ref[...] = jnp.concatenate(rows, axis=0) + b_ref[...]


def _ada(c, w, b):
    bsz, d = c.shape
    n = w.shape[1]
    tn = n // N_MOD
    return pl.pallas_call(
        _ada_kernel,
        grid=(n // tn,),
        in_specs=[pl.BlockSpec((d, bsz), lambda j: (0, 0)),
                  pl.BlockSpec((d, tn), lambda j: (0, j)),
                  pl.BlockSpec((1, tn), lambda j: (0, j))],
        out_specs=pl.BlockSpec((bsz, tn), lambda j: (0, j)),
        out_shape=jax.ShapeDtypeStruct((bsz, n), F32),
        compiler_params=_params("parallel"),
        name="ada",
    )(c.T, w, b.reshape(1, n))


def _inproj_kernel(x_ref, sc_ref, sh_ref, nw_ref, w_ref, b_ref, qkw_ref, gsum_ref, gexp_ref,
                   q_ref, k_ref, v_ref, u_ref, ga_ref, gb_ref, *, na_w, f_w, d, chunk):
    for c0 in range(0, x_ref.shape[0], chunk):
        rows = slice(c0, c0 + chunk)
        x = x_ref[rows, :]
        ms = jnp.mean(x * x, axis=-1, keepdims=True)
        y = x * lax.rsqrt(ms + EPS) * nw_ref[...]
        h = (y * (1.0 + sc_ref[...]) + sh_ref[...]).astype(BF16)

        qk = 2 * na_w
        zqk = _dot(h, w_ref[:, 0:qk]) + b_ref[:, 0:qk]
        ssq = _dot((zqk * zqk).astype(BF16), gsum_ref[...])
        rinv = lax.rsqrt(ssq * (1.0 / HEAD_DIM) + EPS)
        rinv_hi = rinv.astype(BF16)
        rinv_lo = (rinv - rinv_hi.astype(F32)).astype(BF16)
        scale = _dot(jnp.concatenate([rinv_hi, rinv_lo], axis=1), gexp_ref[...])
        zn = zqk * scale * qkw_ref[...]
        q_ref[rows, :] = zn[:, 0:na_w].astype(BF16)
        k_ref[rows, :] = zn[:, na_w:qk].astype(BF16)

        o = qk
        v_ref[rows, :] = (_dot(h, w_ref[:, o:o + na_w]) + b_ref[:, o:o + na_w]).astype(BF16)
        o += na_w
        u_ref[rows, :] = _dot(h, w_ref[:, o:o + f_w]) + b_ref[:, o:o + f_w]
        o += f_w
        ga_ref[rows, :] = _sigmoid(_dot(h, w_ref[:, o:o + d]) + b_ref[:, o:o + d]).astype(BF16)
        o += d
        gb_ref[rows, :] = _sigmoid(_dot(h, w_ref[:, o:o + d]) + b_ref[:, o:o + d]).astype(BF16)


def _inproj(x, scale1, shift1, norm_w, w_in, b_in, qk_w, na_w, f_w):
    bsz, s, d = x.shape
    in_w = w_in.shape[1]
    tm = 1024
    qk = 2 * na_w
    n_grp = qk // HEAD_DIM
    grp = np.arange(qk) // HEAD_DIM
    gsum = jnp.asarray((grp[:, None] == np.arange(LANES)[None, :]).astype(np.float32), BF16)
    gexp1 = (np.arange(LANES)[:, None] == grp[None, :]).astype(np.float32)
    gexp = jnp.asarray(np.concatenate([gexp1, gexp1], axis=0), BF16)
    assert n_grp <= LANES

    tok = lambda w: pl.BlockSpec((None, tm, w), lambda b, i: (b, i, 0))
    per_batch = pl.BlockSpec((None, 1, d), lambda b, i: (b, 0, 0))
    full = lambda a: pl.BlockSpec(a.shape, lambda b, i: (0,) * a.ndim)
    nw = norm_w.reshape(1, d)
    bi = b_in.reshape(1, in_w)
    out_shapes = (
        jax.ShapeDtypeStruct((bsz, s, na_w), BF16),
        jax.ShapeDtypeStruct((bsz, s, na_w), BF16),
        jax.ShapeDtypeStruct((bsz, s, na_w), BF16),
        jax.ShapeDtypeStruct((bsz, s, f_w), F32),
        jax.ShapeDtypeStruct((bsz, s, d), BF16),
        jax.ShapeDtypeStruct((bsz, s, d), BF16),
    )
    return pl.pallas_call(
        functools.partial(_inproj_kernel, na_w=na_w, f_w=f_w, d=d, chunk=256),
        grid=(bsz, s // tm),
        in_specs=[tok(d), per_batch, per_batch, full(nw), full(w_in), full(bi), full(qk_w),
                  full(gsum), full(gexp)],
        out_specs=(tok(na_w), tok(na_w), tok(na_w), tok(f_w), tok(d), tok(d)),
        out_shape=out_shapes,
        compiler_params=_params("parallel", "parallel"),
        name="inproj",
    )(x, scale1, shift1, nw, w_in, bi, qk_w, gsum, gexp)


def _fourier_constants(rows, f_w):
    n = rows
    r8 = SUBLANES
    k = np.arange(n)
    ang_a = 2.0 * np.pi * np.outer(k, k) / n
    eye8 = np.eye(r8)
    norm = 1.0 / np.sqrt(n)
    fa_re = np.kron(np.cos(ang_a), eye8) * norm
    fa_im = np.kron(-np.sin(ang_a), eye8) * norm
    fa = np.concatenate([fa_re, fa_im], axis=0)

    nblk = n // r8
    s2 = k[None, :]
    s2p = k[:, None]
    g = np.zeros((r8 // 2, 4 * n, 4 * n))
    for pp in range(r8 // 2):
        gr = np.zeros((2, n, 2, n))
        gi = np.zeros((2, n, 2, n))
        for ll in range(2):
            l = 2 * pp + ll
            ang = 2.0 * np.pi * (s2 * l / (n * GRID_W) + s2 * s2p / n)
            gr[ll, :, ll, :] = np.cos(ang) * norm
            gi[ll, :, ll, :] = -np.sin(ang) * norm
        gr = gr.reshape(2 * n, 2 * n)
        gi = gi.reshape(2 * n, 2 * n)
        g[pp] = np.block([[gr, -gi], [gi, gr]])

    perm = np.zeros((n * r8, n * r8))
    s1p = np.repeat(k, r8)
    jj = np.tile(np.arange(r8), n)
    perm[jj * n + s1p, s1p * r8 + jj] = 1.0

    t_idx = np.arange(nblk)[:, None, None]
    blk_of_row = (k // r8)[None, :, None]
    j_idx = np.arange(r8)[None, None, :]
    ang_t = 2.0 * np.pi * (t_idx * r8 + j_idx) * blk_of_row * r8 / (n * GRID_W)
    ang_t = ang_t.reshape(nblk, n * r8, 1)
    tw_cos = np.broadcast_to(np.cos(ang_t), (nblk, n * r8, LANES))
    tw_sin = np.broadcast_to(np.sin(ang_t), (nblk, n * r8, LANES))

    gd = f_w // N_FOURIER_GROUPS
    c = np.arange(gd)
    ang_c = 2.0 * np.pi * np.outer(c, c) / gd
    eye_g = np.eye(N_FOURIER_GROUPS)
    cc = np.kron(eye_g, np.cos(ang_c)) / np.sqrt(gd)
    sc = np.kron(eye_g, np.sin(ang_c)) / np.sqrt(gd)
    as_f32 = lambda a: jnp.asarray(np.ascontiguousarray(a, dtype=np.float32))
    to_bf16 = lambda a: as_f32(a).astype(BF16)
    return (to_bf16(fa), to_bf16(g), to_bf16(cc), to_bf16(sc), to_bf16(perm),
            as_f32(tw_cos), as_f32(tw_sin))


def _fourier_kernel(u_ref, fa_ref, g_ref, cc_ref, sc_ref, perm_ref, twc_ref, tws_ref, y_ref,
                    sre_ref, sim_ref, *, steps_per_stage, blocks_per_step):
    t = pl.program_id(1)
    n, _, f_w = u_ref.shape
    r8 = SUBLANES
    m = n * r8

    @pl.when(t < steps_per_stage)
    def _stage_a():
        for j in range(blocks_per_step):
            xin = u_ref[:, j * r8:(j + 1) * r8, :].reshape(m, f_w).astype(BF16)
            a = _dot(fa_ref[...], xin)
            reps = f_w // LANES
            cos_t = jnp.concatenate([twc_ref[j]] * reps, axis=1)
            sin_t = jnp.concatenate([tws_ref[j]] * reps, axis=1)
            a_re = a[0:m]
            a_im = a[m:2 * m]
            b_re = a_re * cos_t + a_im * sin_t
            b_im = a_im * cos_t - a_re * sin_t
            off = pl.multiple_of((t * blocks_per_step + j) * r8, r8)
            sre_ref[:, pl.ds(off, r8), :] = b_re.reshape(n, r8, f_w)
            sim_ref[:, pl.ds(off, r8), :] = b_im.reshape(n, r8, f_w)

    @pl.when((t >= steps_per_stage) & (t < 2 * steps_per_stage))
    def _stage_b():
        pair_rows = 2 * n
        for j in range(blocks_per_step):
            off = pl.multiple_of(((t - steps_per_stage) * blocks_per_step + j) * r8, r8)
            br = sre_ref[pl.ds(off, r8)].reshape(m, f_w)
            bi = sim_ref[pl.ds(off, r8)].reshape(m, f_w)
            p_re, p_im = [], []
            for pp in range(r8 // 2):
                rs_ = slice(pp * pair_rows, (pp + 1) * pair_rows)
                bcat = jnp.concatenate([br[rs_], bi[rs_]], axis=0).astype(BF16)
                p = _dot(g_ref[pp], bcat)
                p_re.append(p[0:pair_rows])
                p_im.append(p[pair_rows:2 * pair_rows])
            p_re = jnp.concatenate(p_re, axis=0).astype(BF16)
            p_im = jnp.concatenate(p_im, axis=0).astype(BF16)
            y = _dot(p_re, cc_ref[...]) + _dot(p_im, sc_ref[...])
            sre_ref[pl.ds(off, r8)] = y.reshape(r8, n, f_w)

    @pl.when(t >= 2 * steps_per_stage)
    def _stage_c():
        for j in range(blocks_per_step):
            off = pl.multiple_of(((t - 2 * steps_per_stage) * blocks_per_step + j) * r8, r8)
            yin = sre_ref[:, pl.ds(off, r8), :].reshape(m, f_w).astype(BF16)
            y_ref[j * m:(j + 1) * m, :] = _dot(perm_ref[...], yin).astype(y_ref.dtype)


def _fourier(u):
    bsz, s, f_w = u.shape
    rows = s // GRID_W
    assert rows == GRID_W, "the two-stage position DFT assumes a square token grid"
    r8 = SUBLANES
    nblk = rows // r8
    fa, g, cc, sc, perm, tw_cos, tw_sin = _fourier_constants(rows, f_w)
    u4 = u.reshape(bsz, rows, GRID_W, f_w)
    blocks_per_step = 4
    steps = nblk // blocks_per_step
    width = blocks_per_step * r8
    stage_a_step = lambda b, t: jnp.minimum(t, steps - 1)
    stage_c_step = lambda b, t: jnp.maximum(t - 2 * steps, 0)
    full = lambda a: pl.BlockSpec(a.shape, lambda b, t: (0,) * a.ndim)
    tw_spec = pl.BlockSpec((blocks_per_step,) + tw_cos.shape[1:],
                           lambda b, t: (stage_a_step(b, t), 0, 0))
    return pl.pallas_call(
        functools.partial(_fourier_kernel, steps_per_stage=steps, blocks_per_step=blocks_per_step),
        grid=(bsz, 3 * steps),
        in_specs=[
            pl.BlockSpec((None, rows, width, f_w), lambda b, t: (b, 0, stage_a_step(b, t), 0)),
            full(fa), full(g), full(cc), full(sc), full(perm), tw_spec, tw_spec,
        ],
        out_specs=pl.BlockSpec((None, width * GRID_W, f_w), lambda b, t: (b, stage_c_step(b, t), 0)),
        out_shape=jax.ShapeDtypeStruct((bsz, s, f_w), BF16),
        scratch_shapes=[pltpu.VMEM((rows, GRID_W, f_w), F32), pltpu.VMEM((rows, GRID_W, f_w), F32)],
        compiler_params=_params("arbitrary", "arbitrary"),
        name="fourier",
    )(u4, fa, g, cc, sc, perm, tw_cos, tw_sin)


def _bias_tables(rpb, rows):
    kw = min(WIN_W, GRID_W)
    cols = np.arange(GRID_W)
    col_start = np.clip(cols - kw // 2, 0, GRID_W - kw)
    kj = cols[None, :]
    allowed = (kj >= col_start[:, None]) & (kj < col_start[:, None] + kw)
    rel = np.clip(kj - cols[:, None] + (WIN_W - 1), 0, 2 * WIN_W - 2)
    onehot = (rel[None] == np.arange(2 * WIN_W - 1)[:, None, None]).astype(np.float32)
    t = jnp.einsum("hmd,dck->hmck", rpb.astype(F32), jnp.asarray(onehot),
                   precision=lax.Precision.HIGHEST)
    t = jnp.where(jnp.asarray(allowed)[None, None], t, MASK_VALUE)
    return jnp.concatenate([t[:, :-1], t[:, 1:]], axis=-1)


def _attn_out_kernel(q_ref, k_ref, v_ref, t2_ref, yfn_ref, ga_ref, gb_ref, x_ref,
                     wna_ref, wfn_ref, bfn_ref, wo_ref, g1_ref, sc2_ref, sh2_ref, nw2_ref,
                     o_ref, h2_ref, yna_ref, s_ref, p_ref, linv_ref,
                     *, rows, kh, rows_per_step, out_chunk):
    rb = pl.program_id(1)
    n_pairs = q_ref.shape[1] // LANES
    lane = lax.broadcasted_iota(jnp.int32, (1, LANES), 1)
    low = lane < HEAD_DIM
    mask_lo = jnp.where(low, 1.0, 0.0).astype(BF16)
    mask_hi = jnp.where(low, 0.0, 1.0).astype(BF16)
    n_keys = kh * GRID_W

    key_start = []
    for j in range(rows_per_step):
        r = rb * rows_per_step + j
        rs = jnp.clip(r - kh // 2, 0, rows - kh)
        key_start.append((pl.multiple_of(rs * GRID_W, GRID_W), rs - r + (WIN_H - 1)))

    for j in range(rows_per_step):
        ks, idx0 = key_start[j]
        for p in range(n_pairs):
            cs = slice(LANES * p, LANES * (p + 1))
            q2 = q_ref[j * GRID_W:(j + 1) * GRID_W, cs]
            k2 = k_ref[pl.ds(ks, n_keys), cs]
            qs = jnp.concatenate([q2 * mask_lo, q2 * mask_hi], axis=0)
            s = lax.dot_general(qs, k2, (((1,), (1,)), ((), ())), preferred_element_type=F32)
            bias = jnp.concatenate(
                [jnp.concatenate([t2_ref[2 * p + hh, idx0 + 2 * m] for m in range(kh // 2)], axis=1)
                 for hh in range(2)], axis=0)
            s_ref[j * n_pairs + p] = s + bias

    sub = 32
    for t in range(rows_per_step * n_pairs):
        for r0 in range(0, 2 * GRID_W, sub):
            s = s_ref[t, r0:r0 + sub, :]
            e = jnp.exp(s - jnp.max(s, axis=-1, keepdims=True))
            p_ref[t, r0:r0 + sub, :] = e.astype(BF16)
            linv_ref[t, r0:r0 + sub, :] = 1.0 / jnp.sum(e, axis=-1, keepdims=True)

    for j in range(rows_per_step):
        ks, _ = key_start[j]
        outs = []
        for p in range(n_pairs):
            t = j * n_pairs + p
            v2 = v_ref[pl.ds(ks, n_keys), LANES * p:LANES * (p + 1)]
            o = _dot(p_ref[t], v2) * linv_ref[t]
            outs.append(jnp.where(low, o[0:GRID_W], o[GRID_W:2 * GRID_W]))
        yna_ref[j * GRID_W:(j + 1) * GRID_W, :] = jnp.concatenate(outs, axis=1)

    for c0 in range(0, o_ref.shape[0], out_chunk):
        rs_ = slice(c0, c0 + out_chunk)
        ya = _dot(yna_ref[rs_, :].astype(BF16), wna_ref[...])
        yf = _dot(yfn_ref[rs_, :], wfn_ref[...]) + bfn_ref[...]
        merged = ga_ref[rs_, :].astype(F32) * ya + gb_ref[rs_, :].astype(F32) * yf
        x1 = x_ref[rs_, :] + g1_ref[...] * _dot(merged.astype(BF16), wo_ref[...])
        o_ref[rs_, :] = x1
        ms = jnp.mean(x1 * x1, axis=-1, keepdims=True)
        y2 = x1 * lax.rsqrt(ms + EPS) * nw2_ref[...]
        h2_ref[rs_, :] = (y2 * (1.0 + sc2_ref[...]) + sh2_ref[...]).astype(BF16)


def _attn_out(q, k, v, t2, yfn, ga, gb, x, w_na, w_fn, b_fn, w_o, gate1, scale2, shift2, norm2_w):
    bsz, s, d = x.shape
    na_w = q.shape[-1]
    f_w = yfn.shape[-1]
    rows = s // GRID_W
    kh = min(WIN_H, rows)
    assert kh % 2 == 0 and (2 * HEAD_DIM) == LANES
    rows_per_step = 8
    tq = rows_per_step * GRID_W
    n_tiles = rows_per_step * (na_w // LANES)

    tok = lambda w: pl.BlockSpec((None, tq, w), lambda b, i: (b, i, 0))
    whole_seq = pl.BlockSpec((None, s, na_w), lambda b, i: (b, 0, 0))
    per_batch = pl.BlockSpec((None, 1, d), lambda b, i: (b, 0, 0))
    full = lambda a: pl.BlockSpec(a.shape, lambda b, i: (0,) * a.ndim)
    bfn = b_fn.reshape(1, d)
    nw2 = norm2_w.reshape(1, d)
    return pl.pallas_call(
        functools.partial(_attn_out_kernel, rows=rows, kh=kh, rows_per_step=rows_per_step,
                          out_chunk=tq),
        grid=(bsz, s // tq),
        in_specs=[tok(na_w), whole_seq, whole_seq, full(t2), tok(f_w), tok(d), tok(d), tok(d),
                  full(w_na), full(w_fn), full(bfn), full(w_o), per_batch, per_batch, per_batch,
                  full(nw2)],
        out_specs=(tok(d), tok(d)),
        out_shape=(jax.ShapeDtypeStruct((bsz, s, d), F32), jax.ShapeDtypeStruct((bsz, s, d), BF16)),
        scratch_shapes=[pltpu.VMEM((tq, na_w), F32),
                        pltpu.VMEM((n_tiles, 2 * GRID_W, kh * GRID_W), F32),
                        pltpu.VMEM((n_tiles, 2 * GRID_W, kh * GRID_W), BF16),
                        pltpu.VMEM((n_tiles, 2 * GRID_W, 1), F32)],
        compiler_params=_params("parallel", "arbitrary"),
        name="attn_out",
    )(q, k, v, t2, yfn, ga, gb, x, w_na, w_fn, bfn, w_o, gate1, scale2, shift2, nw2)


def _mlp_kernel(x_ref, h_ref, g2_ref, w1_ref, w2_ref, o_ref, acc_ref, *, chunk):
    f = pl.program_id(2)

    @pl.when(f == 0)
    def _init():
        acc_ref[...] = jnp.zeros_like(acc_ref)

    h = h_ref[...]
    part = None
    for c0 in range(0, w1_ref.shape[1], chunk):
        a = jnp.maximum(_dot(h, w1_ref[:, c0:c0 + chunk]), 0.0)
        term = _dot((a * a).astype(BF16), w2_ref[c0:c0 + chunk, :])
        part = term if part is None else part + term
    acc_ref[...] += part

    @pl.when(f == pl.num_programs(2) - 1)
    def _epilogue():
        o_ref[...] = x_ref[...] + g2_ref[...] * acc_ref[...]


def _mlp(x, h, gate2, w1, w2):
    bsz, s, d = x.shape
    d_ff = w1.shape[1]
    tm = 1024
    tf = 2048
    tok = pl.BlockSpec((None, tm, d), lambda b, i, f: (b, i, 0))
    per_batch = pl.BlockSpec((None, 1, d), lambda b, i, f: (b, 0, 0))
    return pl.pallas_call(
        functools.partial(_mlp_kernel, chunk=1024),
        grid=(bsz, s // tm, d_ff // tf),
        in_specs=[tok, tok, per_batch,
                  pl.BlockSpec((d, tf), lambda b, i, f: (0, f)),
                  pl.BlockSpec((tf, d), lambda b, i, f: (f, 0))],
        out_specs=tok,
        out_shape=jax.ShapeDtypeStruct((bsz, s, d), F32),
        scratch_shapes=[pltpu.VMEM((tm, d), F32)],
        compiler_params=_params("parallel", "parallel", "arbitrary"),
        name="mlp",
    )(x, h, gate2, w1, w2)


def kernel(x, c, norm1_w, norm2_w, w_ada, b_ada, w_in, b_in, q_norm_w, k_norm_w, rpb,
           w_na_out, w_fn_out, b_fn_out, w_o, w_mlp_in, w_mlp_out):
    bsz, s, d = x.shape
    depth = w_ada.shape[0]
    na_w = w_na_out.shape[1]
    f_w = w_fn_out.shape[1]
    n_heads = na_w // HEAD_DIM
    rows = s // GRID_W
    for l in range(depth):
        mod = _ada(c, w_ada[l], b_ada[l])
        shift1, scale1, gate1, shift2, scale2, gate2 = [
            mod[:, i * d:(i + 1) * d].reshape(bsz, 1, d) for i in range(N_MOD)]

        qk_w = jnp.concatenate([jnp.tile(q_norm_w[l], n_heads) * (HEAD_DIM ** -0.5),
                                jnp.tile(k_norm_w[l], n_heads)]).reshape(1, 2 * na_w)
        q, k, v, u, ga, gb = _inproj(x, scale1, shift1, norm1_w[l], w_in[l].astype(BF16), b_in[l],
                                     qk_w, na_w, f_w)
        yfn = _fourier(u)
        t2 = _bias_tables(rpb[l], rows)
        x, h2 = _attn_out(q, k, v, t2, yfn, ga, gb, x, w_na_out[l].astype(BF16),
                          w_fn_out[l].astype(BF16), b_fn_out[l], w_o[l].astype(BF16), gate1,
                          scale2, shift2, norm2_w[l])
        x = _mlp(x, h2, gate2, w_mlp_in[l].astype(BF16), w_mlp_out[l].astype(BF16))
    return x
```

```python
import functools

import numpy as np
import jax
import jax.numpy as jnp
from jax import lax
from jax.experimental import pallas as pl
from jax.experimental.pallas import tpu as pltpu

F32 = jnp.float32
BF16 = jnp.bfloat16

GRID_W = 64
HEAD_DIM = 64
WIN_H = 8
WIN_W = 16
N_FOURIER_GROUPS = 4
N_MOD = 6
EPS = 1e-6
MASK_VALUE = -1e30

SUBLANES = 8
LANES = 128
VMEM_LIMIT_BYTES = 56 * 1024 * 1024


def _dot(a, b):
    return jnp.dot(a, b, preferred_element_type=F32)


def _sigmoid(x):
    return 1.0 / (1.0 + jnp.exp(-x))


def _params(*semantics):
    return pltpu.CompilerParams(dimension_semantics=semantics, vmem_limit_bytes=VMEM_LIMIT_BYTES)


def _ada_kernel(ct_ref, w_ref, b_ref, o_ref):
    ct = ct_ref[...]
    st = ct * _sigmoid(ct)
    w = w_ref[...]
    rows = [jnp.sum(w * st[:, b:b + 1], axis=0, keepdims=True) for b in range(ct.shape[1])]
    o_ref[...] = jnp.concatenate(rows, axis=0) + b_ref[...]


def _ada(c, w, b):
    bsz, d = c.shape
    n = w.shape[1]
    tn = n // N_MOD
    return pl.pallas_call(
        _ada_kernel,
        grid=(n // tn,),
        in_specs=[pl.BlockSpec((d, bsz), lambda j: (0, 0)),
                  pl.BlockSpec((d, tn), lambda j: (0, j)),
                  pl.BlockSpec((1, tn), lambda j: (0, j))],
        out_specs=pl.BlockSpec((bsz, tn), lambda j: (0, j)),
        out_shape=jax.ShapeDtypeStruct((bsz, n), F32),
        compiler_params=_params("parallel"),
        name="ada",
    )(c.T, w, b.reshape(1, n))


def _inproj_kernel(x_ref, sc_ref, sh_ref, nw_ref, w_ref, b_ref, qkw_ref, gsum_ref, gexp_ref,
                   q_ref, k_ref, v_ref, u_ref, ga_ref, gb_ref, *, na_w, f_w, d, chunk):
    for c0 in range(0, x_ref.shape[0], chunk):
        rows = slice(c0, c0 + chunk)
        x = x_ref[rows, :]
        ms = jnp.mean(x * x, axis=-1, keepdims=True)
        y = x * lax.rsqrt(ms + EPS) * nw_ref[...]
        h = (y * (1.0 + sc_ref[...]) + sh_ref[...]).astype(BF16)

        qk = 2 * na_w
        zqk = _dot(h, w_ref[:, 0:qk]) + b_ref[:, 0:qk]
        ssq = _dot((zqk * zqk).astype(BF16), gsum_ref[...])
        rinv = lax.rsqrt(ssq * (1.0 / HEAD_DIM) + EPS)
        rinv_hi = rinv.astype(BF16)
        rinv_lo = (rinv - rinv_hi.astype(F32)).astype(BF16)
        scale = _dot(jnp.concatenate([rinv_hi, rinv_lo], axis=1), gexp_ref[...])
        zn = zqk * scale * qkw_ref[...]
        q_ref[rows, :] = zn[:, 0:na_w].astype(BF16)
        k_ref[rows, :] = zn[:, na_w:qk].astype(BF16)

        o = qk
        v_ref[rows, :] = (_dot(h, w_ref[:, o:o + na_w]) + b_ref[:, o:o + na_w]).astype(BF16)
        o += na_w
        u_ref[rows, :] = _dot(h, w_ref[:, o:o + f_w]) + b_ref[:, o:o + f_w]
        o += f_w
        ga_ref[rows, :] = _sigmoid(_dot(h, w_ref[:, o:o + d]) + b_ref[:, o:o + d]).astype(BF16)
        o += d
        gb_ref[rows, :] = _sigmoid(_dot(h, w_ref[:, o:o + d]) + b_ref[:, o:o + d]).astype(BF16)


def _inproj(x, scale1, shift1, norm_w, w_in, b_in, qk_w, na_w, f_w):
    bsz, s, d = x.shape
    in_w = w_in.shape[1]
    tm = 1024
    qk = 2 * na_w
    n_grp = qk // HEAD_DIM
    grp = np.arange(qk) // HEAD_DIM
    gsum = jnp.asarray((grp[:, None] == np.arange(LANES)[None, :]).astype(np.float32), BF16)
    gexp1 = (np.arange(LANES)[:, None] == grp[None, :]).astype(np.float32)
    gexp = jnp.asarray(np.concatenate([gexp1, gexp1], axis=0), BF16)
    assert n_grp <= LANES

    tok = lambda w: pl.BlockSpec((None, tm, w), lambda b, i: (b, i, 0))
    per_batch = pl.BlockSpec((None, 1, d), lambda b, i: (b, 0, 0))
    full = lambda a: pl.BlockSpec(a.shape, lambda b, i: (0,) * a.ndim)
    nw = norm_w.reshape(1, d)
    bi = b_in.reshape(1, in_w)
    out_shapes = (
        jax.ShapeDtypeStruct((bsz, s, na_w), BF16),
        jax.ShapeDtypeStruct((bsz, s, na_w), BF16),
        jax.ShapeDtypeStruct((bsz, s, na_w), BF16),
        jax.ShapeDtypeStruct((bsz, s, f_w), F32),
        jax.ShapeDtypeStruct((bsz, s, d), BF16),
        jax.ShapeDtypeStruct((bsz, s, d), BF16),
    )
    return pl.pallas_call(
        functools.partial(_inproj_kernel, na_w=na_w, f_w=f_w, d=d, chunk=256),
        grid=(bsz, s // tm),
        in_specs=[tok(d), per_batch, per_batch, full(nw), full(w_in), full(bi), full(qk_w),
                  full(gsum), full(gexp)],
        out_specs=(tok(na_w), tok(na_w), tok(na_w), tok(f_w), tok(d), tok(d)),
        out_shape=out_shapes,
        compiler_params=_params("parallel", "parallel"),
        name="inproj",
    )(x, scale1, shift1, nw, w_in, bi, qk_w, gsum, gexp)


def _fourier_constants(rows, f_w):
    n = rows
    r8 = SUBLANES
    k = np.arange(n)
    ang_a = 2.0 * np.pi * np.outer(k, k) / n
    eye8 = np.eye(r8)
    norm = 1.0 / np.sqrt(n)
    fa_re = np.kron(np.cos(ang_a), eye8) * norm
    fa_im = np.kron(-np.sin(ang_a), eye8) * norm
    fa = np.concatenate([fa_re, fa_im], axis=0)

    nblk = n // r8
    s2 = k[None, :]
    s2p = k[:, None]
    g = np.zeros((r8 // 2, 4 * n, 4 * n))
    for pp in range(r8 // 2):
        gr = np.zeros((2, n, 2, n))
        gi = np.zeros((2, n, 2, n))
        for ll in range(2):
            l = 2 * pp + ll
            ang = 2.0 * np.pi * (s2 * l / (n * GRID_W) + s2 * s2p / n)
            gr[ll, :, ll, :] = np.cos(ang) * norm
            gi[ll, :, ll, :] = -np.sin(ang) * norm
        gr = gr.reshape(2 * n, 2 * n)
        gi = gi.reshape(2 * n, 2 * n)
        g[pp] = np.block([[gr, -gi], [gi, gr]])

    perm = np.zeros((n * r8, n * r8))
    s1p = np.repeat(k, r8)
    jj = np.tile(np.arange(r8), n)
    perm[jj * n + s1p, s1p * r8 + jj] = 1.0

    t_idx = np.arange(nblk)[:, None, None]
    blk_of_row = (k // r8)[None, :, None]
    j_idx = np.arange(r8)[None, None, :]
    ang_t = 2.0 * np.pi * (t_idx * r8 + j_idx) * blk_of_row * r8 / (n * GRID_W)
    ang_t = ang_t.reshape(nblk, n * r8, 1)
    tw_cos = np.broadcast_to(np.cos(ang_t), (nblk, n * r8, LANES))
    tw_sin = np.broadcast_to(np.sin(ang_t), (nblk, n * r8, LANES))

    gd = f_w // N_FOURIER_GROUPS
    c = np.arange(gd)
    ang_c = 2.0 * np.pi * np.outer(c, c) / gd
    eye_g = np.eye(N_FOURIER_GROUPS)
    cc = np.kron(eye_g, np.cos(ang_c)) / np.sqrt(gd)
    sc = np.kron(eye_g, np.sin(ang_c)) / np.sqrt(gd)
    as_f32 = lambda a: jnp.asarray(np.ascontiguousarray(a, dtype=np.float32))
    to_bf16 = lambda a: as_f32(a).astype(BF16)
    return (to_bf16(fa), to_bf16(g), to_bf16(cc), to_bf16(sc), to_bf16(perm),
            as_f32(tw_cos), as_f32(tw_sin))


def _fourier_kernel(u_ref, fa_ref, g_ref, cc_ref, sc_ref, perm_ref, twc_ref, tws_ref, y_ref,
                    sre_ref, sim_ref, *, steps_per_stage, blocks_per_step):
    t = pl.program_id(1)
    n, _, f_w = u_ref.shape
    r8 = SUBLANES
    m = n * r8

    @pl.when(t < steps_per_stage)
    def _stage_a():
        for j in range(blocks_per_step):
            xin = u_ref[:, j * r8:(j + 1) * r8, :].reshape(m, f_w).astype(BF16)
            a = _dot(fa_ref[...], xin)
            reps = f_w // LANES
            cos_t = jnp.concatenate([twc_ref[j]] * reps, axis=1)
            sin_t = jnp.concatenate([tws_ref[j]] * reps, axis=1)
            a_re = a[0:m]
            a_im = a[m:2 * m]
            b_re = a_re * cos_t + a_im * sin_t
            b_im = a_im * cos_t - a_re * sin_t
            off = pl.multiple_of((t * blocks_per_step + j) * r8, r8)
            sre_ref[:, pl.ds(off, r8), :] = b_re.reshape(n, r8, f_w)
            sim_ref[:, pl.ds(off, r8), :] = b_im.reshape(n, r8, f_w)

    @pl.when((t >= steps_per_stage) & (t < 2 * steps_per_stage))
    def _stage_b():
        pair_rows = 2 * n
        for j in range(blocks_per_step):
            off = pl.multiple_of(((t - steps_per_stage) * blocks_per_step + j) * r8, r8)
            br = sre_ref[pl.ds(off, r8)].reshape(m, f_w)
            bi = sim_ref[pl.ds(off, r8)].reshape(m, f_w)
            p_re, p_im = [], []
            for pp in range(r8 // 2):
                rs_ = slice(pp * pair_rows, (pp + 1) * pair_rows)
                bcat = jnp.concatenate([br[rs_], bi[rs_]], axis=0).astype(BF16)
                p = _dot(g_ref[pp], bcat)
                p_re.append(p[0:pair_rows])
                p_im.append(p[pair_rows:2 * pair_rows])
            p_re = jnp.concatenate(p_re, axis=0).astype(BF16)
            p_im = jnp.concatenate(p_im, axis=0).astype(BF16)
            y = _dot(p_re, cc_ref[...]) + _dot(p_im, sc_ref[...])
            sre_ref[pl.ds(off, r8)] = y.reshape(r8, n, f_w)

    @pl.when(t >= 2 * steps_per_stage)
    def _stage_c():
        for j in range(blocks_per_step):
            off = pl.multiple_of(((t - 2 * steps_per_stage) * blocks_per_step + j) * r8, r8)
            yin = sre_ref[:, pl.ds(off, r8), :].reshape(m, f_w).astype(BF16)
            y_ref[j * m:(j + 1) * m, :] = _dot(perm_ref[...], yin).astype(y_ref.dtype)


def _fourier(u):
    bsz, s, f_w = u.shape
    rows = s // GRID_W
    assert rows == GRID_W, "the two-stage position DFT assumes a square token grid"
    r8 = SUBLANES
    nblk = rows // r8
    fa, g, cc, sc, perm, tw_cos, tw_sin = _fourier_constants(rows, f_w)
    u4 = u.reshape(bsz, rows, GRID_W, f_w)
    blocks_per_step = 4
    steps = nblk // blocks_per_step
    width = blocks_per_step * r8
    stage_a_step = lambda b, t: jnp.minimum(t, steps - 1)
    stage_c_step = lambda b, t: jnp.maximum(t - 2 * steps, 0)
    full = lambda a: pl.BlockSpec(a.shape, lambda b, t: (0,) * a.ndim)
    tw_spec = pl.BlockSpec((blocks_per_step,) + tw_cos.shape[1:],
                           lambda b, t: (stage_a_step(b, t), 0, 0))
    return pl.pallas_call(
        functools.partial(_fourier_kernel, steps_per_stage=steps, blocks_per_step=blocks_per_step),
        grid=(bsz, 3 * steps),
        in_specs=[
            pl.BlockSpec((None, rows, width, f_w), lambda b, t: (b, 0, stage_a_step(b, t), 0)),
            full(fa), full(g), full(cc), full(sc), full(perm), tw_spec, tw_spec,
        ],
        out_specs=pl.BlockSpec((None, width * GRID_W, f_w), lambda b, t: (b, stage_c_step(b, t), 0)),
        out_shape=jax.ShapeDtypeStruct((bsz, s, f_w), BF16),
        scratch_shapes=[pltpu.VMEM((rows, GRID_W, f_w), F32), pltpu.VMEM((rows, GRID_W, f_w), F32)],
        compiler_params=_params("arbitrary", "arbitrary"),
        name="fourier",
    )(u4, fa, g, cc, sc, perm, tw_cos, tw_sin)


def _bias_tables(rpb, rows):
    kw = min(WIN_W, GRID_W)
    cols = np.arange(GRID_W)
    col_start = np.clip(cols - kw // 2, 0, GRID_W - kw)
    kj = cols[None, :]
    allowed = (kj >= col_start[:, None]) & (kj < col_start[:, None] + kw)
    rel = np.clip(kj - cols[:, None] + (WIN_W - 1), 0, 2 * WIN_W - 2)
    onehot = (rel[None] == np.arange(2 * WIN_W - 1)[:, None, None]).astype(np.float32)
    t = jnp.einsum("hmd,dck->hmck", rpb.astype(F32), jnp.asarray(onehot),
                   precision=lax.Precision.HIGHEST)
    t = jnp.where(jnp.asarray(allowed)[None, None], t, MASK_VALUE)
    return jnp.concatenate([t[:, :-1], t[:, 1:]], axis=-1)


def _attn_out_kernel(q_ref, k_ref, v_ref, t2_ref, yfn_ref, ga_ref, gb_ref, x_ref,
                     wna_ref, wfn_ref, bfn_ref, wo_ref, g1_ref, sc2_ref, sh2_ref, nw2_ref,
                     o_ref, h2_ref, yna_ref, s_ref, p_ref,
                     *, rows, kh, rows_per_step, out_chunk):
    rb = pl.program_id(1)
    n_pairs = q_ref.shape[1] // LANES
    lane = lax.broadcasted_iota(jnp.int32, (1, LANES), 1)
    low = lane < HEAD_DIM
    mask_lo = jnp.where(low, 1.0, 0.0).astype(BF16)
    mask_hi = jnp.where(low, 0.0, 1.0).astype(BF16)
    n_keys = kh * GRID_W

    key_start = []
    for j in range(rows_per_step):
        r = rb * rows_per_step + j
        rs = jnp.clip(r - kh // 2, 0, rows - kh)
        key_start.append((pl.multiple_of(rs * GRID_W, GRID_W), rs - r + (WIN_H - 1)))

    for j in range(rows_per_step):
        ks, idx0 = key_start[j]
        for p in range(n_pairs):
            cs = slice(LANES * p, LANES * (p + 1))
            q2 = q_ref[j * GRID_W:(j + 1) * GRID_W, cs]
            k2 = k_ref[pl.ds(ks, n_keys), cs]
            qs = jnp.concatenate([q2 * mask_lo, q2 * mask_hi], axis=0)
            s = lax.dot_general(qs, k2, (((1,), (1,)), ((), ())), preferred_element_type=F32)
            bias = jnp.concatenate(
                [jnp.concatenate([t2_ref[2 * p + hh, idx0 + 2 * m] for m in range(kh // 2)], axis=1)
                 for hh in range(2)], axis=0)
            s_ref[j * n_pairs + p] = s + bias

    for t in range(rows_per_step * n_pairs):
        s = s_ref[t]
        p_ref[t] = jnp.exp(s - jnp.max(s, axis=-1, keepdims=True)).astype(BF16)

    ones = jnp.ones((n_keys, LANES), BF16)
    for j in range(rows_per_step):
        ks, _ = key_start[j]
        outs = []
        for p in range(n_pairs):
            t = j * n_pairs + p
            v2 = v_ref[pl.ds(ks, n_keys), LANES * p:LANES * (p + 1)]
            o = _dot(p_ref[t], jnp.concatenate([v2, ones], axis=1))
            o = o[:, 0:LANES] * (1.0 / o[:, LANES:2 * LANES])
            outs.append(jnp.where(low, o[0:GRID_W], o[GRID_W:2 * GRID_W]))
        yna_ref[j * GRID_W:(j + 1) * GRID_W, :] = jnp.concatenate(outs, axis=1)

    for c0 in range(0, o_ref.shape[0], out_chunk):
        rs_ = slice(c0, c0 + out_chunk)
        ya = _dot(yna_ref[rs_, :].astype(BF16), wna_ref[...])
        yf = _dot(yfn_ref[rs_, :], wfn_ref[...]) + bfn_ref[...]
        merged = ga_ref[rs_, :].astype(F32) * ya + gb_ref[rs_, :].astype(F32) * yf
        x1 = x_ref[rs_, :] + g1_ref[...] * _dot(merged.astype(BF16), wo_ref[...])
        o_ref[rs_, :] = x1
        ms = jnp.mean(x1 * x1, axis=-1, keepdims=True)
        y2 = x1 * lax.rsqrt(ms + EPS) * nw2_ref[...]
        h2_ref[rs_, :] = (y2 * (1.0 + sc2_ref[...]) + sh2_ref[...]).astype(BF16)


def _attn_out(q, k, v, t2, yfn, ga, gb, x, w_na, w_fn, b_fn, w_o, gate1, scale2, shift2, norm2_w):
    bsz, s, d = x.shape
    na_w = q.shape[-1]
    f_w = yfn.shape[-1]
    rows = s // GRID_W
    kh = min(WIN_H, rows)
    assert kh % 2 == 0 and (2 * HEAD_DIM) == LANES
    rows_per_step = 8
    tq = rows_per_step * GRID_W
    n_tiles = rows_per_step * (na_w // LANES)

    tok = lambda w: pl.BlockSpec((None, tq, w), lambda b, i: (b, i, 0))
    whole_seq = pl.BlockSpec((None, s, na_w), lambda b, i: (b, 0, 0))
    per_batch = pl.BlockSpec((None, 1, d), lambda b, i: (b, 0, 0))
    full = lambda a: pl.BlockSpec(a.shape, lambda b, i: (0,) * a.ndim)
    bfn = b_fn.reshape(1, d)
    nw2 = norm2_w.reshape(1, d)
    return pl.pallas_call(
        functools.partial(_attn_out_kernel, rows=rows, kh=kh, rows_per_step=rows_per_step,
                          out_chunk=tq),
        grid=(bsz, s // tq),
        in_specs=[tok(na_w), whole_seq, whole_seq, full(t2), tok(f_w), tok(d), tok(d), tok(d),
                  full(w_na), full(w_fn), full(bfn), full(w_o), per_batch, per_batch, per_batch,
                  full(nw2)],
        out_specs=(tok(d), tok(d)),
        out_shape=(jax.ShapeDtypeStruct((bsz, s, d), F32), jax.ShapeDtypeStruct((bsz, s, d), BF16)),
        scratch_shapes=[pltpu.VMEM((tq, na_w), F32),
                        pltpu.VMEM((n_tiles, 2 * GRID_W, kh * GRID_W), F32),
                        pltpu.VMEM((n_tiles, 2 * GRID_W, kh * GRID_W), BF16)],
        compiler_params=_params("parallel", "arbitrary"),
        name="attn_out",
    )(q, k, v, t2, yfn, ga, gb, x, w_na, w_fn, bfn, w_o, gate1, scale2, shift2, nw2)


def _mlp_kernel(x_ref, h_ref, g2_ref, w1_ref, w2_ref, o_ref, acc_ref, *, chunk):
    f = pl.program_id(2)

    @pl.when(f == 0)
    def _init():
        acc_ref[...] = jnp.zeros_like(acc_ref)

    h = h_ref[...]
    part = None
    for c0 in range(0, w1_ref.shape[1], chunk):
        a = jnp.maximum(_dot(h, w1_ref[:, c0:c0 + chunk]), 0.0)
        term = _dot((a * a).astype(BF16), w2_ref[c0:c0 + chunk, :])
        part = term if part is None else part + term
    acc_ref[...] += part

    @pl.when(f == pl.num_programs(2) - 1)
    def _epilogue():
        o_ref[...] = x_ref[...] + g2_ref[...] * acc_ref[...]


def _mlp(x, h, gate2, w1, w2):
    bsz, s, d = x.shape
    d_ff = w1.shape[1]
    tm = 1024
    tf = 2048
    tok = pl.BlockSpec((None, tm, d), lambda b, i, f: (b, i, 0))
    per_batch = pl.BlockSpec((None, 1, d), lambda b, i, f: (b, 0, 0))
    return pl.pallas_call(
        functools.partial(_mlp_kernel, chunk=1024),
        grid=(bsz, s // tm, d_ff // tf),
        in_specs=[tok, tok, per_batch,
                  pl.BlockSpec((d, tf), lambda b, i, f: (0, f)),
                  pl.BlockSpec((tf, d), lambda b, i, f: (f, 0))],
        out_specs=tok,
        out_shape=jax.ShapeDtypeStruct((bsz, s, d), F32),
        scratch_shapes=[pltpu.VMEM((tm, d), F32)],
        compiler_params=_params("parallel", "parallel", "arbitrary"),
        name="mlp",
    )(x, h, gate2, w1, w2)


def kernel(x, c, norm1_w, norm2_w, w_ada, b_ada, w_in, b_in, q_norm_w, k_norm_w, rpb,
           w_na_out, w_fn_out, b_fn_out, w_o, w_mlp_in, w_mlp_out):
    bsz, s, d = x.shape
    depth = w_ada.shape[0]
    na_w = w_na_out.shape[1]
    f_w = w_fn_out.shape[1]
    n_heads = na_w // HEAD_DIM
    rows = s // GRID_W
    for l in range(depth):
        mod = _ada(c, w_ada[l], b_ada[l])
        shift1, scale1, gate1, shift2, scale2, gate2 = [
            mod[:, i * d:(i + 1) * d].reshape(bsz, 1, d) for i in range(N_MOD)]

        qk_w = jnp.concatenate([jnp.tile(q_norm_w[l], n_heads) * (HEAD_DIM ** -0.5),
                                jnp.tile(k_norm_w[l], n_heads)]).reshape(1, 2 * na_w)
        q, k, v, u, ga, gb = _inproj(x, scale1, shift1, norm1_w[l], w_in[l].astype(BF16), b_in[l],
                                     qk_w, na_w, f_w)
        yfn = _fourier(u)
        t2 = _bias_tables(rpb[l], rows)
        x, h2 = _attn_out(q, k, v, t2, yfn, ga, gb, x, w_na_out[l].astype(BF16),
                          w_fn_out[l].astype(BF16), b_fn_out[l], w_o[l].astype(BF16), gate1,
                          scale2, shift2, norm2_w[l])
        x = _mlp(x, h2, gate2, w_mlp_in[l].astype(BF16), w_mlp_out[l].astype(BF16))
    return x
```

```python
import functools

import numpy as np
import jax
import jax.numpy as jnp
from jax import lax
from jax.experimental import pallas as pl
from jax.experimental.pallas import tpu as pltpu

F32 = jnp.float32
BF16 = jnp.bfloat16

GRID_W = 64
HEAD_DIM = 64
WIN_H = 8
WIN_W = 16
N_FOURIER_GROUPS = 4
N_MOD = 6
EPS = 1e-6
MASK_VALUE = -1e30

SUBLANES = 8
LANES = 128
VMEM_LIMIT_BYTES = 56 * 1024 * 1024


def _dot(a, b):
    return jnp.dot(a, b, preferred_element_type=F32)


def _sigmoid(x):
    return 1.0 / (1.0 + jnp.exp(-x))


def _params(*semantics):
    return pltpu.CompilerParams(dimension_semantics=semantics, vmem_limit_bytes=VMEM_LIMIT_BYTES)


def _ada_kernel(ct_ref, w_ref, b_ref, o_ref):
    ct = ct_ref[...]
    st = ct * _sigmoid(ct)
    w = w_ref[...]
    rows = [jnp.sum(w * st[:, b:b + 1], axis=0, keepdims=True) for b in range(ct.shape[1])]
    o_ref[...] = jnp.concatenate(rows, axis=0) + b_ref[...]


def _ada(c, w, b):
    bsz, d = c.shape
    n = w.shape[1]
    tn = n // N_MOD
    return pl.pallas_call(
        _ada_kernel,
        grid=(n // tn,),
        in_specs=[pl.BlockSpec((d, bsz), lambda j: (0, 0)),
                  pl.BlockSpec((d, tn), lambda j: (0, j)),
                  pl.BlockSpec((1, tn), lambda j: (0, j))],
        out_specs=pl.BlockSpec((bsz, tn), lambda j: (0, j)),
        out_shape=jax.ShapeDtypeStruct((bsz, n), F32),
        compiler_params=_params("parallel"),
        name="ada",
    )(c.T, w, b.reshape(1, n))


def _inproj_kernel(x_ref, sc_ref, sh_ref, nw_ref, w_ref, b_ref, qkw_ref, gsum_ref, gexp_ref, *refs,
                   na_w, f_w, d, chunk, n_cast):
    cast_in = refs[:n_cast]
    q_ref, k_ref, v_ref, u_ref, ga_ref, gb_ref = refs[n_cast:n_cast + 6]
    cast_out = refs[n_cast + 6:]
    for src, dst in zip(cast_in, cast_out):
        dst[...] = src[...].astype(BF16)

    for c0 in range(0, x_ref.shape[0], chunk):
        rows = slice(c0, c0 + chunk)
        x = x_ref[rows, :]
        ms = jnp.mean(x * x, axis=-1, keepdims=True)
        y = x * lax.rsqrt(ms + EPS) * nw_ref[...]
        h = (y * (1.0 + sc_ref[...]) + sh_ref[...]).astype(BF16)

        qk = 2 * na_w
        zqk = _dot(h, w_ref[:, 0:qk]) + b_ref[:, 0:qk]
        ssq = _dot((zqk * zqk).astype(BF16), gsum_ref[...])
        rinv = lax.rsqrt(ssq * (1.0 / HEAD_DIM) + EPS)
        rinv_hi = rinv.astype(BF16)
        rinv_lo = (rinv - rinv_hi.astype(F32)).astype(BF16)
        scale = _dot(jnp.concatenate([rinv_hi, rinv_lo], axis=1), gexp_ref[...])
        zn = zqk * scale * qkw_ref[...]
        q_ref[rows, :] = zn[:, 0:na_w].astype(BF16)
        k_ref[rows, :] = zn[:, na_w:qk].astype(BF16)

        o = qk
        v_ref[rows, :] = (_dot(h, w_ref[:, o:o + na_w]) + b_ref[:, o:o + na_w]).astype(BF16)
        o += na_w
        u_ref[rows, :] = _dot(h, w_ref[:, o:o + f_w]) + b_ref[:, o:o + f_w]
        o += f_w
        ga_ref[rows, :] = _sigmoid(_dot(h, w_ref[:, o:o + d]) + b_ref[:, o:o + d]).astype(BF16)
        o += d
        gb_ref[rows, :] = _sigmoid(_dot(h, w_ref[:, o:o + d]) + b_ref[:, o:o + d]).astype(BF16)


def _inproj(x, scale1, shift1, norm_w, w_in, b_in, qk_w, na_w, f_w, later_weights):
    bsz, s, d = x.shape
    in_w = w_in.shape[1]
    tm = 1024
    qk = 2 * na_w
    n_grp = qk // HEAD_DIM
    grp = np.arange(qk) // HEAD_DIM
    gsum = jnp.asarray((grp[:, None] == np.arange(LANES)[None, :]).astype(np.float32), BF16)
    gexp1 = (np.arange(LANES)[:, None] == grp[None, :]).astype(np.float32)
    gexp = jnp.asarray(np.concatenate([gexp1, gexp1], axis=0), BF16)
    assert n_grp <= LANES

    tok = lambda w: pl.BlockSpec((None, tm, w), lambda b, i: (b, i, 0))
    per_batch = pl.BlockSpec((None, 1, d), lambda b, i: (b, 0, 0))
    full = lambda a: pl.BlockSpec(a.shape, lambda b, i: (0,) * a.ndim)
    nw = norm_w.reshape(1, d)
    bi = b_in.reshape(1, in_w)
    out_shapes = (
        jax.ShapeDtypeStruct((bsz, s, na_w), BF16),
        jax.ShapeDtypeStruct((bsz, s, na_w), BF16),
        jax.ShapeDtypeStruct((bsz, s, na_w), BF16),
        jax.ShapeDtypeStruct((bsz, s, f_w), F32),
        jax.ShapeDtypeStruct((bsz, s, d), BF16),
        jax.ShapeDtypeStruct((bsz, s, d), BF16),
    )
    n_steps = bsz * (s // tm)
    slab = lambda a: pl.BlockSpec((a.shape[0] // n_steps, a.shape[1]),
                                  lambda b, i: (b * (s // tm) + i, 0))
    for a in later_weights:
        assert a.shape[0] % (n_steps * 2 * SUBLANES) == 0, a.shape
    outs = pl.pallas_call(
        functools.partial(_inproj_kernel, na_w=na_w, f_w=f_w, d=d, chunk=256,
                          n_cast=len(later_weights)),
        grid=(bsz, s // tm),
        in_specs=[tok(d), per_batch, per_batch, full(nw), full(w_in), full(bi), full(qk_w),
                  full(gsum), full(gexp)] + [slab(a) for a in later_weights],
        out_specs=(tok(na_w), tok(na_w), tok(na_w), tok(f_w), tok(d), tok(d))
        + tuple(slab(a) for a in later_weights),
        out_shape=out_shapes + tuple(jax.ShapeDtypeStruct(a.shape, BF16) for a in later_weights),
        compiler_params=_params("parallel", "parallel"),
        name="inproj",
    )(x, scale1, shift1, nw, w_in, bi, qk_w, gsum, gexp, *later_weights)
    return outs[:6], outs[6:]


def _fourier_constants(rows, f_w):
    n = rows
    r8 = SUBLANES
    k = np.arange(n)
    ang_a = 2.0 * np.pi * np.outer(k, k) / n
    eye8 = np.eye(r8)
    norm = 1.0 / np.sqrt(n)
    fa_re = np.kron(np.cos(ang_a), eye8) * norm
    fa_im = np.kron(-np.sin(ang_a), eye8) * norm
    fa = np.concatenate([fa_re, fa_im], axis=0)

    nblk = n // r8
    s2 = k[None, :]
    s2p = k[:, None]
    g = np.zeros((r8 // 2, 4 * n, 4 * n))
    for pp in range(r8 // 2):
        gr = np.zeros((2, n, 2, n))
        gi = np.zeros((2, n, 2, n))
        for ll in range(2):
            l = 2 * pp + ll
            ang = 2.0 * np.pi * (s2 * l / (n * GRID_W) + s2 * s2p / n)
            gr[ll, :, ll, :] = np.cos(ang) * norm
            gi[ll, :, ll, :] = -np.sin(ang) * norm
        gr = gr.reshape(2 * n, 2 * n)
        gi = gi.reshape(2 * n, 2 * n)
        g[pp] = np.block([[gr, -gi], [gi, gr]])

    perm = np.zeros((n * r8, n * r8))
    s1p = np.repeat(k, r8)
    jj = np.tile(np.arange(r8), n)
    perm[jj * n + s1p, s1p * r8 + jj] = 1.0

    t_idx = np.arange(nblk)[:, None, None]
    blk_of_row = (k // r8)[None, :, None]
    j_idx = np.arange(r8)[None, None, :]
    ang_t = 2.0 * np.pi * (t_idx * r8 + j_idx) * blk_of_row * r8 / (n * GRID_W)
    ang_t = ang_t.reshape(nblk, n * r8, 1)
    tw_cos = np.broadcast_to(np.cos(ang_t), (nblk, n * r8, LANES))
    tw_sin = np.broadcast_to(np.sin(ang_t), (nblk, n * r8, LANES))

    gd = f_w // N_FOURIER_GROUPS
    c = np.arange(gd)
    ang_c = 2.0 * np.pi * np.outer(c, c) / gd
    eye_g = np.eye(N_FOURIER_GROUPS)
    cc = np.kron(eye_g, np.cos(ang_c)) / np.sqrt(gd)
    sc = np.kron(eye_g, np.sin(ang_c)) / np.sqrt(gd)
    as_f32 = lambda a: jnp.asarray(np.ascontiguousarray(a, dtype=np.float32))
    to_bf16 = lambda a: as_f32(a).astype(BF16)
    return (to_bf16(fa), to_bf16(g), to_bf16(cc), to_bf16(sc), to_bf16(perm),
            as_f32(tw_cos), as_f32(tw_sin))


def _fourier_kernel(u_ref, fa_ref, g_ref, cc_ref, sc_ref, perm_ref, twc_ref, tws_ref, y_ref,
                    sre_ref, sim_ref, *, steps_per_stage, blocks_per_step):
    t = pl.program_id(1)
    n, _, f_w = u_ref.shape
    r8 = SUBLANES
    m = n * r8

    @pl.when(t < steps_per_stage)
    def _stage_a():
        for j in range(blocks_per_step):
            xin = u_ref[:, j * r8:(j + 1) * r8, :].reshape(m, f_w).astype(BF16)
            a = _dot(fa_ref[...], xin)
            reps = f_w // LANES
            cos_t = jnp.concatenate([twc_ref[j]] * reps, axis=1)
            sin_t = jnp.concatenate([tws_ref[j]] * reps, axis=1)
            a_re = a[0:m]
            a_im = a[m:2 * m]
            b_re = a_re * cos_t + a_im * sin_t
            b_im = a_im * cos_t - a_re * sin_t
            off = pl.multiple_of((t * blocks_per_step + j) * r8, r8)
            sre_ref[:, pl.ds(off, r8), :] = b_re.reshape(n, r8, f_w)
            sim_ref[:, pl.ds(off, r8), :] = b_im.reshape(n, r8, f_w)

    @pl.when((t >= steps_per_stage) & (t < 2 * steps_per_stage))
    def _stage_b():
        pair_rows = 2 * n
        for j in range(blocks_per_step):
            off = pl.multiple_of(((t - steps_per_stage) * blocks_per_step + j) * r8, r8)
            br = sre_ref[pl.ds(off, r8)].reshape(m, f_w)
            bi = sim_ref[pl.ds(off, r8)].reshape(m, f_w)
            p_re, p_im = [], []
            for pp in range(r8 // 2):
                rs_ = slice(pp * pair_rows, (pp + 1) * pair_rows)
                bcat = jnp.concatenate([br[rs_], bi[rs_]], axis=0).astype(BF16)
                p = _dot(g_ref[pp], bcat)
                p_re.append(p[0:pair_rows])
                p_im.append(p[pair_rows:2 * pair_rows])
            p_re = jnp.concatenate(p_re, axis=0).astype(BF16)
            p_im = jnp.concatenate(p_im, axis=0).astype(BF16)
            y = _dot(p_re, cc_ref[...]) + _dot(p_im, sc_ref[...])
            sre_ref[pl.ds(off, r8)] = y.reshape(r8, n, f_w)

    @pl.when(t >= 2 * steps_per_stage)
    def _stage_c():
        for j in range(blocks_per_step):
            off = pl.multiple_of(((t - 2 * steps_per_stage) * blocks_per_step + j) * r8, r8)
            yin = sre_ref[:, pl.ds(off, r8), :].reshape(m, f_w).astype(BF16)
            y_ref[j * m:(j + 1) * m, :] = _dot(perm_ref[...], yin).astype(y_ref.dtype)


def _fourier(u):
    bsz, s, f_w = u.shape
    rows = s // GRID_W
    assert rows == GRID_W, "the two-stage position DFT assumes a square token grid"
    r8 = SUBLANES
    nblk = rows // r8
    fa, g, cc, sc, perm, tw_cos, tw_sin = _fourier_constants(rows, f_w)
    u4 = u.reshape(bsz, rows, GRID_W, f_w)
    blocks_per_step = 4
    steps = nblk // blocks_per_step
    width = blocks_per_step * r8
    stage_a_step = lambda b, t: jnp.minimum(t, steps - 1)
    stage_c_step = lambda b, t: jnp.maximum(t - 2 * steps, 0)
    full = lambda a: pl.BlockSpec(a.shape, lambda b, t: (0,) * a.ndim)
    tw_spec = pl.BlockSpec((blocks_per_step,) + tw_cos.shape[1:],
                           lambda b, t: (stage_a_step(b, t), 0, 0))
    return pl.pallas_call(
        functools.partial(_fourier_kernel, steps_per_stage=steps, blocks_per_step=blocks_per_step),
        grid=(bsz, 3 * steps),
        in_specs=[
            pl.BlockSpec((None, rows, width, f_w), lambda b, t: (b, 0, stage_a_step(b, t), 0)),
            full(fa), full(g), full(cc), full(sc), full(perm), tw_spec, tw_spec,
        ],
        out_specs=pl.BlockSpec((None, width * GRID_W, f_w), lambda b, t: (b, stage_c_step(b, t), 0)),
        out_shape=jax.ShapeDtypeStruct((bsz, s, f_w), BF16),
        scratch_shapes=[pltpu.VMEM((rows, GRID_W, f_w), F32), pltpu.VMEM((rows, GRID_W, f_w), F32)],
        compiler_params=_params("arbitrary", "arbitrary"),
        name="fourier",
    )(u4, fa, g, cc, sc, perm, tw_cos, tw_sin)


def _bias_tables(rpb, rows):
    kw = min(WIN_W, GRID_W)
    cols = np.arange(GRID_W)
    col_start = np.clip(cols - kw // 2, 0, GRID_W - kw)
    kj = cols[None, :]
    allowed = (kj >= col_start[:, None]) & (kj < col_start[:, None] + kw)
    rel = kj - cols[:, None] + (WIN_W - 1)
    nrel = 2 * WIN_W - 1
    onehot = ((rel[None] == np.arange(nrel)[:, None, None]) & allowed[None]).astype(np.float32)
    sel = np.zeros((2 * nrel + 1, GRID_W, 2 * GRID_W), np.float32)
    sel[0:nrel, :, 0:GRID_W] = onehot
    sel[nrel:2 * nrel, :, GRID_W:] = onehot
    sel[2 * nrel] = np.tile(~allowed, (1, 2))
    rpb = rpb.astype(F32)
    mask_col = jnp.full(rpb.shape[:1] + (rpb.shape[1] - 1, 1), MASK_VALUE, F32)
    src = jnp.concatenate([rpb[:, :-1], rpb[:, 1:], mask_col], axis=-1)
    return jnp.einsum("hmd,dcx->hmcx", src, jnp.asarray(sel), precision=lax.Precision.HIGHEST)


def _attn_out_kernel(q_ref, k_ref, v_ref, t2_ref, yfn_ref, ga_ref, gb_ref, x_ref,
                     wna_ref, wfn_ref, bfn_ref, wo_ref, g1_ref, sc2_ref, sh2_ref, nw2_ref,
                     o_ref, h2_ref, yna_ref, s_ref, p_ref,
                     *, rows, kh, rows_per_step, out_chunk):
    rb = pl.program_id(1)
    n_pairs = q_ref.shape[1] // LANES
    lane = lax.broadcasted_iota(jnp.int32, (1, LANES), 1)
    low = lane < HEAD_DIM
    mask_lo = jnp.where(low, 1.0, 0.0).astype(BF16)
    mask_hi = jnp.where(low, 0.0, 1.0).astype(BF16)
    n_keys = kh * GRID_W

    key_start = []
    for j in range(rows_per_step):
        r = rb * rows_per_step + j
        rs = jnp.clip(r - kh // 2, 0, rows - kh)
        key_start.append((pl.multiple_of(rs * GRID_W, GRID_W), rs - r + (WIN_H - 1)))

    for j in range(rows_per_step):
        ks, idx0 = key_start[j]
        for p in range(n_pairs):
            cs = slice(LANES * p, LANES * (p + 1))
            q2 = q_ref[j * GRID_W:(j + 1) * GRID_W, cs]
            k2 = k_ref[pl.ds(ks, n_keys), cs]
            qs = jnp.concatenate([q2 * mask_lo, q2 * mask_hi], axis=0)
            s = lax.dot_general(qs, k2, (((1,), (1,)), ((), ())), preferred_element_type=F32)
            bias = jnp.concatenate(
                [jnp.concatenate([t2_ref[2 * p + hh, idx0 + 2 * m] for m in range(kh // 2)], axis=1)
                 for hh in range(2)], axis=0)
            s_ref[j * n_pairs + p] = s + bias

    for t in range(rows_per_step * n_pairs):
        s = s_ref[t]
        p_ref[t] = jnp.exp(s - jnp.max(s, axis=-1, keepdims=True)).astype(BF16)

    ones = jnp.ones((n_keys, LANES), BF16)
    for j in range(rows_per_step):
        ks, _ = key_start[j]
        outs = []
        for p in range(n_pairs):
            t = j * n_pairs + p
            v2 = v_ref[pl.ds(ks, n_keys), LANES * p:LANES * (p + 1)]
            o = _dot(p_ref[t], jnp.concatenate([v2, ones], axis=1))
            o = o[:, 0:LANES] * (1.0 / o[:, LANES:2 * LANES])
            outs.append(jnp.where(low, o[0:GRID_W], o[GRID_W:2 * GRID_W]))
        yna_ref[j * GRID_W:(j + 1) * GRID_W, :] = jnp.concatenate(outs, axis=1)

    for c0 in range(0, o_ref.shape[0], out_chunk):
        rs_ = slice(c0, c0 + out_chunk)
        ya = _dot(yna_ref[rs_, :].astype(BF16), wna_ref[...])
        yf = _dot(yfn_ref[rs_, :], wfn_ref[...]) + bfn_ref[...]
        merged = ga_ref[rs_, :].astype(F32) * ya + gb_ref[rs_, :].astype(F32) * yf
        x1 = x_ref[rs_, :] + g1_ref[...] * _dot(merged.astype(BF16), wo_ref[...])
        o_ref[rs_, :] = x1
        ms = jnp.mean(x1 * x1, axis=-1, keepdims=True)
        y2 = x1 * lax.rsqrt(ms + EPS) * nw2_ref[...]
        h2_ref[rs_, :] = (y2 * (1.0 + sc2_ref[...]) + sh2_ref[...]).astype(BF16)


def _attn_out(q, k, v, t2, yfn, ga, gb, x, w_na, w_fn, b_fn, w_o, gate1, scale2, shift2, norm2_w):
    bsz, s, d = x.shape
    na_w = q.shape[-1]
    f_w = yfn.shape[-1]
    rows = s // GRID_W
    kh = min(WIN_H, rows)
    assert kh % 2 == 0 and (2 * HEAD_DIM) == LANES
    rows_per_step = 8
    tq = rows_per_step * GRID_W
    n_tiles = rows_per_step * (na_w // LANES)

    tok = lambda w: pl.BlockSpec((None, tq, w), lambda b, i: (b, i, 0))
    whole_seq = pl.BlockSpec((None, s, na_w), lambda b, i: (b, 0, 0))
    per_batch = pl.BlockSpec((None, 1, d), lambda b, i: (b, 0, 0))
    full = lambda a: pl.BlockSpec(a.shape, lambda b, i: (0,) * a.ndim)
    bfn = b_fn.reshape(1, d)
    nw2 = norm2_w.reshape(1, d)
    return pl.pallas_call(
        functools.partial(_attn_out_kernel, rows=rows, kh=kh, rows_per_step=rows_per_step,
                          out_chunk=tq),
        grid=(bsz, s // tq),
        in_specs=[tok(na_w), whole_seq, whole_seq, full(t2), tok(f_w), tok(d), tok(d), tok(d),
                  full(w_na), full(w_fn), full(bfn), full(w_o), per_batch, per_batch, per_batch,
                  full(nw2)],
        out_specs=(tok(d), tok(d)),
        out_shape=(jax.ShapeDtypeStruct((bsz, s, d), F32), jax.ShapeDtypeStruct((bsz, s, d), BF16)),
        scratch_shapes=[pltpu.VMEM((tq, na_w), F32),
                        pltpu.VMEM((n_tiles, 2 * GRID_W, kh * GRID_W), F32),
                        pltpu.VMEM((n_tiles, 2 * GRID_W, kh * GRID_W), BF16)],
        compiler_params=_params("parallel", "arbitrary"),
        name="attn_out",
    )(q, k, v, t2, yfn, ga, gb, x, w_na, w_fn, bfn, w_o, gate1, scale2, shift2, nw2)


def _mlp_kernel(x_ref, h_ref, g2_ref, w1_ref, w2_ref, o_ref, acc_ref, *, chunk):
    f = pl.program_id(2)

    @pl.when(f == 0)
    def _init():
        acc_ref[...] = jnp.zeros_like(acc_ref)

    h = h_ref[...]
    part = None
    for c0 in range(0, w1_ref.shape[1], chunk):
        a = jnp.maximum(_dot(h, w1_ref[:, c0:c0 + chunk]), 0.0)
        term = _dot((a * a).astype(BF16), w2_ref[c0:c0 + chunk, :])
        part = term if part is None else part + term
    acc_ref[...] += part

    @pl.when(f == pl.num_programs(2) - 1)
    def _epilogue():
        o_ref[...] = x_ref[...] + g2_ref[...] * acc_ref[...]


def _mlp(x, h, gate2, w1, w2):
    bsz, s, d = x.shape
    d_ff = w1.shape[1]
    tm = 1024
    tf = 2048
    tok = pl.BlockSpec((None, tm, d), lambda b, i, f: (b, i, 0))
    per_batch = pl.BlockSpec((None, 1, d), lambda b, i, f: (b, 0, 0))
    return pl.pallas_call(
        functools.partial(_mlp_kernel, chunk=1024),
        grid=(bsz, s // tm, d_ff // tf),
        in_specs=[tok, tok, per_batch,
                  pl.BlockSpec((d, tf), lambda b, i, f: (0, f)),
                  pl.BlockSpec((tf, d), lambda b, i, f: (f, 0))],
        out_specs=tok,
        out_shape=jax.ShapeDtypeStruct((bsz, s, d), F32),
        scratch_shapes=[pltpu.VMEM((tm, d), F32)],
        compiler_params=_params("parallel", "parallel", "arbitrary"),
        name="mlp",
    )(x, h, gate2, w1, w2)


def kernel(x, c, norm1_w, norm2_w, w_ada, b_ada, w_in, b_in, q_norm_w, k_norm_w, rpb,
           w_na_out, w_fn_out, b_fn_out, w_o, w_mlp_in, w_mlp_out):
    bsz, s, d = x.shape
    depth = w_ada.shape[0]
    na_w = w_na_out.shape[1]
    f_w = w_fn_out.shape[1]
    n_heads = na_w // HEAD_DIM
    rows = s // GRID_W
    for l in range(depth):
        mod = _ada(c, w_ada[l], b_ada[l])
        shift1, scale1, gate1, shift2, scale2, gate2 = [
            mod[:, i * d:(i + 1) * d].reshape(bsz, 1, d) for i in range(N_MOD)]

        qk_w = jnp.concatenate([jnp.tile(q_norm_w[l], n_heads) * (HEAD_DIM ** -0.5),
                                jnp.tile(k_norm_w[l], n_heads)]).reshape(1, 2 * na_w)
        (q, k, v, u, ga, gb), (w_na, w_fn, w_o_b, w_m1, w_m2) = _inproj(
            x, scale1, shift1, norm1_w[l], w_in[l].astype(BF16), b_in[l], qk_w, na_w, f_w,
            (w_na_out[l], w_fn_out[l], w_o[l], w_mlp_in[l], w_mlp_out[l]))
        yfn = _fourier(u)
        t2 = _bias_tables(rpb[l], rows)
        x, h2 = _attn_out(q, k, v, t2, yfn, ga, gb, x, w_na, w_fn, b_fn_out[l], w_o_b, gate1,
                          scale2, shift2, norm2_w[l])
        x = _mlp(x, h2, gate2, w_m1, w_m2)
    return x
```

```python
import functools

import numpy as np
import jax
import jax.numpy as jnp
from jax import lax
from jax.experimental import pallas as pl
from jax.experimental.pallas import tpu as pltpu

F32 = jnp.float32
BF16 = jnp.bfloat16

GRID_W = 64
HEAD_DIM = 64
WIN_H = 8
WIN_W = 16
N_FOURIER_GROUPS = 4
N_MOD = 6
EPS = 1e-6
MASK_VALUE = -1e30

SUBLANES = 8
LANES = 128
VMEM_LIMIT_BYTES = 56 * 1024 * 1024


def _dot(a, b):
    return jnp.dot(a, b, preferred_element_type=F32)


def _sigmoid(x):
    return 1.0 / (1.0 + jnp.exp(-x))


def _params(*semantics):
    return pltpu.CompilerParams(dimension_semantics=semantics, vmem_limit_bytes=VMEM_LIMIT_BYTES)


def _ada_kernel(ct_ref, w_ref, b_ref, o_ref):
    ct = ct_ref[...]
    st = ct * _sigmoid(ct)
    w = w_ref[...]
    rows = [jnp.sum(w * st[:, b:b + 1], axis=0, keepdims=True) for b in range(ct.shape[1])]
    o_ref[...] = jnp.concatenate(rows, axis=0) + b_ref[...]


def _ada(c, w, b):
    bsz, d = c.shape
    n = w.shape[1]
    tn = n // N_MOD
    return pl.pallas_call(
        _ada_kernel,
        grid=(n // tn,),
        in_specs=[pl.BlockSpec((d, bsz), lambda j: (0, 0)),
                  pl.BlockSpec((d, tn), lambda j: (0, j)),
                  pl.BlockSpec((1, tn), lambda j: (0, j))],
        out_specs=pl.BlockSpec((bsz, tn), lambda j: (0, j)),
        out_shape=jax.ShapeDtypeStruct((bsz, n), F32),
        compiler_params=_params("parallel"),
        name="ada",
    )(c.T, w, b.reshape(1, n))


def _inproj_kernel(x_ref, sc_ref, sh_ref, nw_ref, w_ref, b_ref, qkw_ref, gsum_ref, gexp_ref, *refs,
                   na_w, f_w, d, chunk, n_cast):
    cast_in = refs[:n_cast]
    q_ref, k_ref, v_ref, u_ref, ga_ref, gb_ref = refs[n_cast:n_cast + 6]
    cast_out = refs[n_cast + 6:]
    for src, dst in zip(cast_in, cast_out):
        dst[...] = src[...].astype(BF16)

    for c0 in range(0, x_ref.shape[0], chunk):
        rows = slice(c0, c0 + chunk)
        x = x_ref[rows, :]
        ms = jnp.mean(x * x, axis=-1, keepdims=True)
        y = x * lax.rsqrt(ms + EPS) * nw_ref[...]
        h = (y * (1.0 + sc_ref[...]) + sh_ref[...]).astype(BF16)

        qk = 2 * na_w
        zqk = _dot(h, w_ref[:, 0:qk]) + b_ref[:, 0:qk]
        ssq = _dot((zqk * zqk).astype(BF16), gsum_ref[...])
        rinv = lax.rsqrt(ssq * (1.0 / HEAD_DIM) + EPS)
        rinv_hi = rinv.astype(BF16)
        rinv_lo = (rinv - rinv_hi.astype(F32)).astype(BF16)
        scale = _dot(jnp.concatenate([rinv_hi, rinv_lo], axis=1), gexp_ref[...])
        zn = zqk * scale * qkw_ref[...]
        q_ref[rows, :] = zn[:, 0:na_w].astype(BF16)
        k_ref[rows, :] = zn[:, na_w:qk].astype(BF16)

        o = qk
        v_ref[rows, :] = (_dot(h, w_ref[:, o:o + na_w]) + b_ref[:, o:o + na_w]).astype(BF16)
        o += na_w
        u_ref[rows, :] = _dot(h, w_ref[:, o:o + f_w]) + b_ref[:, o:o + f_w]
        o += f_w
        ga_ref[rows, :] = _sigmoid(_dot(h, w_ref[:, o:o + d]) + b_ref[:, o:o + d]).astype(BF16)
        o += d
        gb_ref[rows, :] = _sigmoid(_dot(h, w_ref[:, o:o + d]) + b_ref[:, o:o + d]).astype(BF16)


def _inproj(x, scale1, shift1, norm_w, w_in, b_in, qk_w, na_w, f_w, later_weights):
    bsz, s, d = x.shape
    in_w = w_in.shape[1]
    tm = 1024
    qk = 2 * na_w
    n_grp = qk // HEAD_DIM
    grp = np.arange(qk) // HEAD_DIM
    gsum = jnp.asarray((grp[:, None] == np.arange(LANES)[None, :]).astype(np.float32), BF16)
    gexp1 = (np.arange(LANES)[:, None] == grp[None, :]).astype(np.float32)
    gexp = jnp.asarray(np.concatenate([gexp1, gexp1], axis=0), BF16)
    assert n_grp <= LANES

    tok = lambda w: pl.BlockSpec((None, tm, w), lambda b, i: (b, i, 0))
    per_batch = pl.BlockSpec((None, 1, d), lambda b, i: (b, 0, 0))
    full = lambda a: pl.BlockSpec(a.shape, lambda b, i: (0,) * a.ndim)
    nw = norm_w.reshape(1, d)
    bi = b_in.reshape(1, in_w)
    out_shapes = (
        jax.ShapeDtypeStruct((bsz, s, na_w), BF16),
        jax.ShapeDtypeStruct((bsz, s, na_w), BF16),
        jax.ShapeDtypeStruct((bsz, s, na_w), BF16),
        jax.ShapeDtypeStruct((bsz, s, f_w), F32),
        jax.ShapeDtypeStruct((bsz, s, d), BF16),
        jax.ShapeDtypeStruct((bsz, s, d), BF16),
    )
    n_steps = bsz * (s // tm)
    slab = lambda a: pl.BlockSpec((a.shape[0] // n_steps, a.shape[1]),
                                  lambda b, i: (b * (s // tm) + i, 0))
    for a in later_weights:
        assert a.shape[0] % (n_steps * 2 * SUBLANES) == 0, a.shape
    outs = pl.pallas_call(
        functools.partial(_inproj_kernel, na_w=na_w, f_w=f_w, d=d, chunk=256,
                          n_cast=len(later_weights)),
        grid=(bsz, s // tm),
        in_specs=[tok(d), per_batch, per_batch, full(nw), full(w_in), full(bi), full(qk_w),
                  full(gsum), full(gexp)] + [slab(a) for a in later_weights],
        out_specs=(tok(na_w), tok(na_w), tok(na_w), tok(f_w), tok(d), tok(d))
        + tuple(slab(a) for a in later_weights),
        out_shape=out_shapes + tuple(jax.ShapeDtypeStruct(a.shape, BF16) for a in later_weights),
        compiler_params=_params("parallel", "parallel"),
        name="inproj",
    )(x, scale1, shift1, nw, w_in, bi, qk_w, gsum, gexp, *later_weights)
    return outs[:6], outs[6:]


def _fourier_constants(rows, f_w):
    n = rows
    r8 = SUBLANES
    k = np.arange(n)
    ang_a = 2.0 * np.pi * np.outer(k, k) / n
    eye8 = np.eye(r8)
    norm = 1.0 / np.sqrt(n)
    fa_re = np.kron(np.cos(ang_a), eye8) * norm
    fa_im = np.kron(-np.sin(ang_a), eye8) * norm
    fa = np.concatenate([fa_re, fa_im], axis=0)

    nblk = n // r8
    s2 = k[None, :]
    s2p = k[:, None]
    g = np.zeros((r8 // 2, 4 * n, 4 * n))
    for pp in range(r8 // 2):
        gr = np.zeros((2, n, 2, n))
        gi = np.zeros((2, n, 2, n))
        for ll in range(2):
            l = 2 * pp + ll
            ang = 2.0 * np.pi * (s2 * l / (n * GRID_W) + s2 * s2p / n)
            gr[ll, :, ll, :] = np.cos(ang) * norm
            gi[ll, :, ll, :] = -np.sin(ang) * norm
        gr = gr.reshape(2 * n, 2 * n)
        gi = gi.reshape(2 * n, 2 * n)
        g[pp] = np.block([[gr, -gi], [gi, gr]])

    perm = np.zeros((n * r8, n * r8))
    s1p = np.repeat(k, r8)
    jj = np.tile(np.arange(r8), n)
    perm[jj * n + s1p, s1p * r8 + jj] = 1.0

    t_idx = np.arange(nblk)[:, None, None]
    blk_of_row = (k // r8)[None, :, None]
    j_idx = np.arange(r8)[None, None, :]
    ang_t = 2.0 * np.pi * (t_idx * r8 + j_idx) * blk_of_row * r8 / (n * GRID_W)
    ang_t = ang_t.reshape(nblk, n * r8, 1)
    tw_cos = np.broadcast_to(np.cos(ang_t), (nblk, n * r8, LANES))
    tw_sin = np.broadcast_to(np.sin(ang_t), (nblk, n * r8, LANES))

    gd = f_w // N_FOURIER_GROUPS
    c = np.arange(gd)
    ang_c = 2.0 * np.pi * np.outer(c, c) / gd
    eye_g = np.eye(N_FOURIER_GROUPS)
    cc = np.kron(eye_g, np.cos(ang_c)) / np.sqrt(gd)
    sc = np.kron(eye_g, np.sin(ang_c)) / np.sqrt(gd)
    as_f32 = lambda a: jnp.asarray(np.ascontiguousarray(a, dtype=np.float32))
    to_bf16 = lambda a: as_f32(a).astype(BF16)
    return (to_bf16(fa), to_bf16(g), to_bf16(cc), to_bf16(sc), to_bf16(perm),
            as_f32(tw_cos), as_f32(tw_sin))


def _fourier_kernel(u_ref, fa_ref, g_ref, cc_ref, sc_ref, perm_ref, twc_ref, tws_ref, y_ref,
                    sre_ref, sim_ref, *, steps_per_stage, blocks_per_step):
    t = pl.program_id(1)
    n, _, f_w = u_ref.shape
    r8 = SUBLANES
    m = n * r8

    @pl.when(t < steps_per_stage)
    def _stage_a():
        for j in range(blocks_per_step):
            xin = u_ref[:, j * r8:(j + 1) * r8, :].reshape(m, f_w).astype(BF16)
            a = _dot(fa_ref[...], xin)
            reps = f_w // LANES
            cos_t = jnp.concatenate([twc_ref[j]] * reps, axis=1)
            sin_t = jnp.concatenate([tws_ref[j]] * reps, axis=1)
            a_re = a[0:m]
            a_im = a[m:2 * m]
            b_re = a_re * cos_t + a_im * sin_t
            b_im = a_im * cos_t - a_re * sin_t
            off = pl.multiple_of((t * blocks_per_step + j) * r8, r8)
            sre_ref[:, pl.ds(off, r8), :] = b_re.reshape(n, r8, f_w)
            sim_ref[:, pl.ds(off, r8), :] = b_im.reshape(n, r8, f_w)

    @pl.when((t >= steps_per_stage) & (t < 2 * steps_per_stage))
    def _stage_b():
        pair_rows = 2 * n
        for j in range(blocks_per_step):
            off = pl.multiple_of(((t - steps_per_stage) * blocks_per_step + j) * r8, r8)
            br = sre_ref[pl.ds(off, r8)].reshape(m, f_w)
            bi = sim_ref[pl.ds(off, r8)].reshape(m, f_w)
            p_re, p_im = [], []
            for pp in range(r8 // 2):
                rs_ = slice(pp * pair_rows, (pp + 1) * pair_rows)
                bcat = jnp.concatenate([br[rs_], bi[rs_]], axis=0).astype(BF16)
                p = _dot(g_ref[pp], bcat)
                p_re.append(p[0:pair_rows])
                p_im.append(p[pair_rows:2 * pair_rows])
            p_re = jnp.concatenate(p_re, axis=0).astype(BF16)
            p_im = jnp.concatenate(p_im, axis=0).astype(BF16)
            y = _dot(p_re, cc_ref[...]) + _dot(p_im, sc_ref[...])
            sre_ref[pl.ds(off, r8)] = y.reshape(r8, n, f_w)

    @pl.when(t >= 2 * steps_per_stage)
    def _stage_c():
        for j in range(blocks_per_step):
            off = pl.multiple_of(((t - 2 * steps_per_stage) * blocks_per_step + j) * r8, r8)
            yin = sre_ref[:, pl.ds(off, r8), :].reshape(m, f_w).astype(BF16)
            y_ref[j * m:(j + 1) * m, :] = _dot(perm_ref[...], yin).astype(y_ref.dtype)


def _fourier(u):
    bsz, s, f_w = u.shape
    rows = s // GRID_W
    assert rows == GRID_W, "the two-stage position DFT assumes a square token grid"
    r8 = SUBLANES
    nblk = rows // r8
    fa, g, cc, sc, perm, tw_cos, tw_sin = _fourier_constants(rows, f_w)
    u4 = u.reshape(bsz, rows, GRID_W, f_w)
    blocks_per_step = 4
    steps = nblk // blocks_per_step
    width = blocks_per_step * r8
    stage_a_step = lambda b, t: jnp.minimum(t, steps - 1)
    stage_c_step = lambda b, t: jnp.maximum(t - 2 * steps, 0)
    full = lambda a: pl.BlockSpec(a.shape, lambda b, t: (0,) * a.ndim)
    tw_spec = pl.BlockSpec((blocks_per_step,) + tw_cos.shape[1:],
                           lambda b, t: (stage_a_step(b, t), 0, 0))
    return pl.pallas_call(
        functools.partial(_fourier_kernel, steps_per_stage=steps, blocks_per_step=blocks_per_step),
        grid=(bsz, 3 * steps),
        in_specs=[
            pl.BlockSpec((None, rows, width, f_w), lambda b, t: (b, 0, stage_a_step(b, t), 0)),
            full(fa), full(g), full(cc), full(sc), full(perm), tw_spec, tw_spec,
        ],
        out_specs=pl.BlockSpec((None, width * GRID_W, f_w), lambda b, t: (b, stage_c_step(b, t), 0)),
        out_shape=jax.ShapeDtypeStruct((bsz, s, f_w), BF16),
        scratch_shapes=[pltpu.VMEM((rows, GRID_W, f_w), F32), pltpu.VMEM((rows, GRID_W, f_w), F32)],
        compiler_params=_params("arbitrary", "arbitrary"),
        name="fourier",
    )(u4, fa, g, cc, sc, perm, tw_cos, tw_sin)


def _bias_tables(rpb, rows):
    kw = min(WIN_W, GRID_W)
    cols = np.arange(GRID_W)
    col_start = np.clip(cols - kw // 2, 0, GRID_W - kw)
    kj = cols[None, :]
    allowed = (kj >= col_start[:, None]) & (kj < col_start[:, None] + kw)
    rel = kj - cols[:, None] + (WIN_W - 1)
    nrel = 2 * WIN_W - 1
    onehot = ((rel[None] == np.arange(nrel)[:, None, None]) & allowed[None]).astype(np.float32)
    sel = np.zeros((2 * nrel + 1, GRID_W, 2 * GRID_W), np.float32)
    sel[0:nrel, :, 0:GRID_W] = onehot
    sel[nrel:2 * nrel, :, GRID_W:] = onehot
    sel[2 * nrel] = np.tile(~allowed, (1, 2))
    rpb = rpb.astype(F32)
    mask_col = jnp.full(rpb.shape[:1] + (rpb.shape[1] - 1, 1), MASK_VALUE, F32)
    src = jnp.concatenate([rpb[:, :-1], rpb[:, 1:], mask_col], axis=-1)
    return jnp.einsum("hmd,dcx->hmcx", src, jnp.asarray(sel), precision=lax.Precision.HIGHEST)


def _attn_out_kernel(q_ref, k_ref, v_ref, t2_ref, yfn_ref, ga_ref, gb_ref, x_ref,
                     wna_ref, wfn_ref, bfn_ref, wo_ref, g1_ref, sc2_ref, sh2_ref, nw2_ref,
                     o_ref, h2_ref, yna_ref, s_ref, p_ref,
                     *, rows, kh, rows_per_step, out_chunk):
    rb = pl.program_id(1)
    n_pairs = q_ref.shape[1] // LANES
    lane = lax.broadcasted_iota(jnp.int32, (1, LANES), 1)
    low = lane < HEAD_DIM
    mask_lo = jnp.where(low, 1.0, 0.0).astype(BF16)
    mask_hi = jnp.where(low, 0.0, 1.0).astype(BF16)
    n_keys = kh * GRID_W

    key_start = []
    for j in range(rows_per_step):
        r = rb * rows_per_step + j
        rs = jnp.clip(r - kh // 2, 0, rows - kh)
        key_start.append((pl.multiple_of(rs * GRID_W, GRID_W), rs - r + (WIN_H - 1)))

    for j in range(rows_per_step):
        ks, idx0 = key_start[j]
        for p in range(n_pairs):
            cs = slice(LANES * p, LANES * (p + 1))
            q2 = q_ref[j * GRID_W:(j + 1) * GRID_W, cs]
            k2 = k_ref[pl.ds(ks, n_keys), cs]
            qs = jnp.concatenate([q2 * mask_lo, q2 * mask_hi], axis=0)
            s = lax.dot_general(qs, k2, (((1,), (1,)), ((), ())), preferred_element_type=F32)
            bias = jnp.concatenate(
                [jnp.concatenate([t2_ref[2 * p + hh, idx0 + 2 * m] for m in range(kh // 2)], axis=1)
                 for hh in range(2)], axis=0)
            s_ref[j * n_pairs + p] = s + bias

    for t in range(rows_per_step * n_pairs):
        s = s_ref[t]
        p_ref[t] = jnp.exp(s - jnp.max(s, axis=-1, keepdims=True)).astype(BF16)

    ones = jnp.ones((n_keys, LANES), BF16)
    for j in range(rows_per_step):
        ks, _ = key_start[j]
        outs = []
        for p in range(n_pairs):
            t = j * n_pairs + p
            v2 = v_ref[pl.ds(ks, n_keys), LANES * p:LANES * (p + 1)]
            o = _dot(p_ref[t], jnp.concatenate([v2, ones], axis=1))
            o = o[:, 0:LANES] * (1.0 / o[:, LANES:2 * LANES])
            outs.append(jnp.where(low, o[0:GRID_W], o[GRID_W:2 * GRID_W]))
        yna_ref[j * GRID_W:(j + 1) * GRID_W, :] = jnp.concatenate(outs, axis=1)

    for c0 in range(0, o_ref.shape[0], out_chunk):
        rs_ = slice(c0, c0 + out_chunk)
        ya = _dot(yna_ref[rs_, :].astype(BF16), wna_ref[...])
        yf = _dot(yfn_ref[rs_, :], wfn_ref[...]) + bfn_ref[...]
        merged = ga_ref[rs_, :].astype(F32) * ya + gb_ref[rs_, :].astype(F32) * yf
        x1 = x_ref[rs_, :] + g1_ref[...] * _dot(merged.astype(BF16), wo_ref[...])
        o_ref[rs_, :] = x1
        ms = jnp.mean(x1 * x1, axis=-1, keepdims=True)
        y2 = x1 * lax.rsqrt(ms + EPS) * nw2_ref[...]
        h2_ref[rs_, :] = (y2 * (1.0 + sc2_ref[...]) + sh2_ref[...]).astype(BF16)


def _attn_out(q, k, v, t2, yfn, ga, gb, x, w_na, w_fn, b_fn, w_o, gate1, scale2, shift2, norm2_w):
    bsz, s, d = x.shape
    na_w = q.shape[-1]
    f_w = yfn.shape[-1]
    rows = s // GRID_W
    kh = min(WIN_H, rows)
    assert kh % 2 == 0 and (2 * HEAD_DIM) == LANES
    rows_per_step = 8
    tq = rows_per_step * GRID_W
    n_tiles = rows_per_step * (na_w // LANES)

    tok = lambda w: pl.BlockSpec((None, tq, w), lambda b, i: (b, i, 0))
    whole_seq = pl.BlockSpec((None, s, na_w), lambda b, i: (b, 0, 0))
    per_batch = pl.BlockSpec((None, 1, d), lambda b, i: (b, 0, 0))
    full = lambda a: pl.BlockSpec(a.shape, lambda b, i: (0,) * a.ndim)
    bfn = b_fn.reshape(1, d)
    nw2 = norm2_w.reshape(1, d)
    return pl.pallas_call(
        functools.partial(_attn_out_kernel, rows=rows, kh=kh, rows_per_step=rows_per_step,
                          out_chunk=tq),
        grid=(bsz, s // tq),
        in_specs=[tok(na_w), whole_seq, whole_seq, full(t2), tok(f_w), tok(d), tok(d), tok(d),
                  full(w_na), full(w_fn), full(bfn), full(w_o), per_batch, per_batch, per_batch,
                  full(nw2)],
        out_specs=(tok(d), tok(d)),
        out_shape=(jax.ShapeDtypeStruct((bsz, s, d), F32), jax.ShapeDtypeStruct((bsz, s, d), BF16)),
        scratch_shapes=[pltpu.VMEM((tq, na_w), F32),
                        pltpu.VMEM((n_tiles, 2 * GRID_W, kh * GRID_W), F32),
                        pltpu.VMEM((n_tiles, 2 * GRID_W, kh * GRID_W), BF16)],
        compiler_params=_params("parallel", "arbitrary"),
        name="attn_out",
    )(q, k, v, t2, yfn, ga, gb, x, w_na, w_fn, bfn, w_o, gate1, scale2, shift2, nw2)


def _mlp_kernel(x_ref, h_ref, g2_ref, w1_ref, w2_ref, o_ref, *, chunk):
    h = h_ref[...]
    acc = None
    for c0 in range(0, w1_ref.shape[1], chunk):
        a = jnp.maximum(_dot(h, w1_ref[:, c0:c0 + chunk]), 0.0)
        term = _dot((a * a).astype(BF16), w2_ref[c0:c0 + chunk, :])
        acc = term if acc is None else acc + term
    o_ref[...] = x_ref[...] + g2_ref[...] * acc


def _mlp(x, h, gate2, w1, w2):
    bsz, s, d = x.shape
    tm = 1024
    tok = pl.BlockSpec((None, tm, d), lambda b, i: (b, i, 0))
    per_batch = pl.BlockSpec((None, 1, d), lambda b, i: (b, 0, 0))
    full = lambda a: pl.BlockSpec(a.shape, lambda b, i: (0,) * a.ndim)
    return pl.pallas_call(
        functools.partial(_mlp_kernel, chunk=1024),
        grid=(bsz, s // tm),
        in_specs=[tok, tok, per_batch, full(w1), full(w2)],
        out_specs=tok,
        out_shape=jax.ShapeDtypeStruct((bsz, s, d), F32),
        compiler_params=_params("parallel", "parallel"),
        name="mlp",
    )(x, h, gate2, w1, w2)


def kernel(x, c, norm1_w, norm2_w, w_ada, b_ada, w_in, b_in, q_norm_w, k_norm_w, rpb,
           w_na_out, w_fn_out, b_fn_out, w_o, w_mlp_in, w_mlp_out):
    bsz, s, d = x.shape
    depth = w_ada.shape[0]
    na_w = w_na_out.shape[1]
    f_w = w_fn_out.shape[1]
    n_heads = na_w // HEAD_DIM
    rows = s // GRID_W
    for l in range(depth):
        mod = _ada(c, w_ada[l], b_ada[l])
        shift1, scale1, gate1, shift2, scale2, gate2 = [
            mod[:, i * d:(i + 1) * d].reshape(bsz, 1, d) for i in range(N_MOD)]

        qk_w = jnp.concatenate([jnp.tile(q_norm_w[l], n_heads) * (HEAD_DIM ** -0.5),
                                jnp.tile(k_norm_w[l], n_heads)]).reshape(1, 2 * na_w)
        (q, k, v, u, ga, gb), (w_na, w_fn, w_o_b, w_m1, w_m2) = _inproj(
            x, scale1, shift1, norm1_w[l], w_in[l].astype(BF16), b_in[l], qk_w, na_w, f_w,
            (w_na_out[l], w_fn_out[l], w_o[l], w_mlp_in[l], w_mlp_out[l]))
        yfn = _fourier(u)
        t2 = _bias_tables(rpb[l], rows)
        x, h2 = _attn_out(q, k, v, t2, yfn, ga, gb, x, w_na, w_fn, b_fn_out[l], w_o_b, gate1,
                          scale2, shift2, norm2_w[l])
        x = _mlp(x, h2, gate2, w_m1, w_m2)
    return x
```

```python
import functools

import numpy as np
import jax
import jax.numpy as jnp
from jax import lax
from jax.experimental import pallas as pl
from jax.experimental.pallas import tpu as pltpu

F32 = jnp.float32
BF16 = jnp.bfloat16

GRID_W = 64
HEAD_DIM = 64
WIN_H = 8
WIN_W = 16
N_FOURIER_GROUPS = 4
N_MOD = 6
EPS = 1e-6
MASK_VALUE = -1e30

SUBLANES = 8
LANES = 128
MXU_DIM = 256
VMEM_LIMIT_BYTES = 56 * 1024 * 1024


def _dot(a, b):
    return jnp.dot(a, b, preferred_element_type=F32)


def _sigmoid(x):
    return 1.0 / (1.0 + jnp.exp(-x))


def _params(*semantics):
    return pltpu.CompilerParams(dimension_semantics=semantics, vmem_limit_bytes=VMEM_LIMIT_BYTES)


def _ada_kernel(ct_ref, w_ref, b_ref, o_ref):
    ct = ct_ref[...]
    st = ct * _sigmoid(ct)
    w = w_ref[...]
    rows = [jnp.sum(w * st[:, b:b + 1], axis=0, keepdims=True) for b in range(ct.shape[1])]
    o_ref[...] = jnp.concatenate(rows, axis=0) + b_ref[...]


def _ada(c, w, b):
    bsz, d = c.shape
    n = w.shape[1]
    tn = n // N_MOD
    return pl.pallas_call(
        _ada_kernel,
        grid=(n // tn,),
        in_specs=[pl.BlockSpec((d, bsz), lambda j: (0, 0)),
                  pl.BlockSpec((d, tn), lambda j: (0, j)),
                  pl.BlockSpec((1, tn), lambda j: (0, j))],
        out_specs=pl.BlockSpec((bsz, tn), lambda j: (0, j)),
        out_shape=jax.ShapeDtypeStruct((bsz, n), F32),
        compiler_params=_params("parallel"),
        name="ada",
    )(c.T, w, b.reshape(1, n))


def _inproj_kernel(x_ref, sc_ref, sh_ref, nw_ref, w_ref, b_ref, qkw_ref, gsum_ref, gexp_ref, *refs,
                   na_w, f_w, d, chunk, n_cast):
    cast_in = refs[:n_cast]
    q_ref, k_ref, v_ref, u_ref, ga_ref, gb_ref = refs[n_cast:n_cast + 6]
    cast_out = refs[n_cast + 6:]
    for src, dst in zip(cast_in, cast_out):
        dst[...] = src[...].astype(BF16)

    for c0 in range(0, x_ref.shape[0], chunk):
        rows = slice(c0, c0 + chunk)
        x = x_ref[rows, :]
        ms = jnp.mean(x * x, axis=-1, keepdims=True)
        y = x * lax.rsqrt(ms + EPS) * nw_ref[...]
        h = (y * (1.0 + sc_ref[...]) + sh_ref[...]).astype(BF16)

        qk = 2 * na_w
        zqk = _dot(h, w_ref[:, 0:qk]) + b_ref[:, 0:qk]
        ssq = _dot((zqk * zqk).astype(BF16), gsum_ref[...])
        rinv = lax.rsqrt(ssq * (1.0 / HEAD_DIM) + EPS)
        rinv_hi = rinv.astype(BF16)
        rinv_lo = (rinv - rinv_hi.astype(F32)).astype(BF16)
        scale = _dot(jnp.concatenate([rinv_hi, rinv_lo], axis=1), gexp_ref[...])
        zn = zqk * scale * qkw_ref[...]
        q_ref[rows, :] = zn[:, 0:na_w].astype(BF16)
        k_ref[rows, :] = zn[:, na_w:qk].astype(BF16)

        o = qk
        v_ref[rows, :] = (_dot(h, w_ref[:, o:o + na_w]) + b_ref[:, o:o + na_w]).astype(BF16)
        o += na_w
        u_ref[rows, :] = _dot(h, w_ref[:, o:o + f_w]) + b_ref[:, o:o + f_w]
        o += f_w
        ga_ref[rows, :] = _sigmoid(_dot(h, w_ref[:, o:o + d]) + b_ref[:, o:o + d]).astype(BF16)
        o += d
        gb_ref[rows, :] = _sigmoid(_dot(h, w_ref[:, o:o + d]) + b_ref[:, o:o + d]).astype(BF16)


def _inproj(x, scale1, shift1, norm_w, w_in, b_in, qk_w, na_w, f_w, later_weights):
    bsz, s, d = x.shape
    in_w = w_in.shape[1]
    tm = 1024
    qk = 2 * na_w
    n_grp = qk // HEAD_DIM
    grp = np.arange(qk) // HEAD_DIM
    gsum = jnp.asarray((grp[:, None] == np.arange(LANES)[None, :]).astype(np.float32), BF16)
    gexp1 = (np.arange(LANES)[:, None] == grp[None, :]).astype(np.float32)
    gexp = jnp.asarray(np.concatenate([gexp1, gexp1], axis=0), BF16)
    assert n_grp <= LANES

    tok = lambda w: pl.BlockSpec((None, tm, w), lambda b, i: (b, i, 0))
    per_batch = pl.BlockSpec((None, 1, d), lambda b, i: (b, 0, 0))
    full = lambda a: pl.BlockSpec(a.shape, lambda b, i: (0,) * a.ndim)
    nw = norm_w.reshape(1, d)
    bi = b_in.reshape(1, in_w)
    out_shapes = (
        jax.ShapeDtypeStruct((bsz, s, na_w), BF16),
        jax.ShapeDtypeStruct((bsz, s, na_w), BF16),
        jax.ShapeDtypeStruct((bsz, s, na_w), BF16),
        jax.ShapeDtypeStruct((bsz, s, f_w), F32),
        jax.ShapeDtypeStruct((bsz, s, d), BF16),
        jax.ShapeDtypeStruct((bsz, s, d), BF16),
    )
    n_steps = bsz * (s // tm)
    slab = lambda a: pl.BlockSpec((a.shape[0] // n_steps, a.shape[1]),
                                  lambda b, i: (b * (s // tm) + i, 0))
    for a in later_weights:
        assert a.shape[0] % (n_steps * 2 * SUBLANES) == 0, a.shape
    outs = pl.pallas_call(
        functools.partial(_inproj_kernel, na_w=na_w, f_w=f_w, d=d, chunk=256,
                          n_cast=len(later_weights)),
        grid=(bsz, s // tm),
        in_specs=[tok(d), per_batch, per_batch, full(nw), full(w_in), full(bi), full(qk_w),
                  full(gsum), full(gexp)] + [slab(a) for a in later_weights],
        out_specs=(tok(na_w), tok(na_w), tok(na_w), tok(f_w), tok(d), tok(d))
        + tuple(slab(a) for a in later_weights),
        out_shape=out_shapes + tuple(jax.ShapeDtypeStruct(a.shape, BF16) for a in later_weights),
        compiler_params=_params("parallel", "parallel"),
        name="inproj",
    )(x, scale1, shift1, nw, w_in, bi, qk_w, gsum, gexp, *later_weights)
    return outs[:6], outs[6:]


def _fourier_constants(rows, f_w):
    n = rows
    r8 = SUBLANES
    k = np.arange(n)
    ang_a = 2.0 * np.pi * np.outer(k, k) / n
    eye8 = np.eye(r8)
    norm = 1.0 / np.sqrt(n)
    fa_re = np.kron(np.cos(ang_a), eye8) * norm
    fa_im = np.kron(-np.sin(ang_a), eye8) * norm
    fa = np.concatenate([fa_re, fa_im], axis=0)

    nblk = n // r8
    s2 = k[None, :]
    s2p = k[:, None]
    g = np.zeros((r8 // 2, 4 * n, 4 * n))
    for pp in range(r8 // 2):
        gr = np.zeros((2, n, 2, n))
        gi = np.zeros((2, n, 2, n))
        for ll in range(2):
            l = 2 * pp + ll
            ang = 2.0 * np.pi * (s2 * l / (n * GRID_W) + s2 * s2p / n)
            gr[ll, :, ll, :] = np.cos(ang) * norm
            gi[ll, :, ll, :] = -np.sin(ang) * norm
        gr = gr.reshape(2 * n, 2 * n)
        gi = gi.reshape(2 * n, 2 * n)
        g[pp] = np.block([[gr, -gi], [gi, gr]])

    perm = np.zeros((n * r8, n * r8))
    s1p = np.repeat(k, r8)
    jj = np.tile(np.arange(r8), n)
    perm[jj * n + s1p, s1p * r8 + jj] = 1.0

    t_idx = np.arange(nblk)[:, None, None]
    blk_of_row = (k // r8)[None, :, None]
    j_idx = np.arange(r8)[None, None, :]
    ang_t = 2.0 * np.pi * (t_idx * r8 + j_idx) * blk_of_row * r8 / (n * GRID_W)
    ang_t = ang_t.reshape(nblk, n * r8, 1)
    tw_cos = np.broadcast_to(np.cos(ang_t), (nblk, n * r8, LANES))
    tw_sin = np.broadcast_to(np.sin(ang_t), (nblk, n * r8, LANES))

    gd = f_w // N_FOURIER_GROUPS
    c = np.arange(gd)
    ang_c = 2.0 * np.pi * np.outer(c, c) / gd
    eye_g = np.eye(N_FOURIER_GROUPS)
    cc = np.kron(eye_g, np.cos(ang_c)) / np.sqrt(gd)
    sc = np.kron(eye_g, np.sin(ang_c)) / np.sqrt(gd)
    as_f32 = lambda a: jnp.asarray(np.ascontiguousarray(a, dtype=np.float32))
    to_bf16 = lambda a: as_f32(a).astype(BF16)
    return (to_bf16(fa), to_bf16(g), to_bf16(cc), to_bf16(sc), to_bf16(perm),
            as_f32(tw_cos), as_f32(tw_sin))


def _fourier_kernel(u_ref, fa_ref, g_ref, cc_ref, sc_ref, perm_ref, twc_ref, tws_ref, y_ref,
                    sre_ref, sim_ref, *, steps_per_stage, blocks_per_step):
    t = pl.program_id(1)
    n, _, f_w = u_ref.shape
    r8 = SUBLANES
    m = n * r8

    @pl.when(t < steps_per_stage)
    def _stage_a():
        for j in range(blocks_per_step):
            xin = u_ref[:, j * r8:(j + 1) * r8, :].reshape(m, f_w).astype(BF16)
            a = _dot(fa_ref[...], xin)
            reps = f_w // LANES
            cos_t = jnp.concatenate([twc_ref[j]] * reps, axis=1)
            sin_t = jnp.concatenate([tws_ref[j]] * reps, axis=1)
            a_re = a[0:m]
            a_im = a[m:2 * m]
            b_re = a_re * cos_t + a_im * sin_t
            b_im = a_im * cos_t - a_re * sin_t
            off = pl.multiple_of((t * blocks_per_step + j) * r8, r8)
            sre_ref[:, pl.ds(off, r8), :] = b_re.reshape(n, r8, f_w)
            sim_ref[:, pl.ds(off, r8), :] = b_im.reshape(n, r8, f_w)

    @pl.when((t >= steps_per_stage) & (t < 2 * steps_per_stage))
    def _stage_b():
        pair_rows = 2 * n
        for j in range(blocks_per_step):
            off = pl.multiple_of(((t - steps_per_stage) * blocks_per_step + j) * r8, r8)
            br = sre_ref[pl.ds(off, r8)].reshape(m, f_w)
            bi = sim_ref[pl.ds(off, r8)].reshape(m, f_w)
            p_re, p_im = [], []
            for pp in range(r8 // 2):
                rs_ = slice(pp * pair_rows, (pp + 1) * pair_rows)
                bcat = jnp.concatenate([br[rs_], bi[rs_]], axis=0).astype(BF16)
                p = _dot(g_ref[pp], bcat)
                p_re.append(p[0:pair_rows])
                p_im.append(p[pair_rows:2 * pair_rows])
            p_re = jnp.concatenate(p_re, axis=0).astype(BF16)
            p_im = jnp.concatenate(p_im, axis=0).astype(BF16)
            y = _dot(p_re, cc_ref[...]) + _dot(p_im, sc_ref[...])
            sre_ref[pl.ds(off, r8)] = y.reshape(r8, n, f_w)

    @pl.when(t >= 2 * steps_per_stage)
    def _stage_c():
        for j in range(blocks_per_step):
            off = pl.multiple_of(((t - 2 * steps_per_stage) * blocks_per_step + j) * r8, r8)
            yin = sre_ref[:, pl.ds(off, r8), :].reshape(m, f_w).astype(BF16)
            y_ref[j * m:(j + 1) * m, :] = _dot(perm_ref[...], yin).astype(y_ref.dtype)


def _fourier(u):
    bsz, s, f_w = u.shape
    rows = s // GRID_W
    assert rows == GRID_W, "the two-stage position DFT assumes a square token grid"
    r8 = SUBLANES
    nblk = rows // r8
    fa, g, cc, sc, perm, tw_cos, tw_sin = _fourier_constants(rows, f_w)
    u4 = u.reshape(bsz, rows, GRID_W, f_w)
    blocks_per_step = 4
    steps = nblk // blocks_per_step
    width = blocks_per_step * r8
    stage_a_step = lambda b, t: jnp.minimum(t, steps - 1)
    stage_c_step = lambda b, t: jnp.maximum(t - 2 * steps, 0)
    full = lambda a: pl.BlockSpec(a.shape, lambda b, t: (0,) * a.ndim)
    tw_spec = pl.BlockSpec((blocks_per_step,) + tw_cos.shape[1:],
                           lambda b, t: (stage_a_step(b, t), 0, 0))
    return pl.pallas_call(
        functools.partial(_fourier_kernel, steps_per_stage=steps, blocks_per_step=blocks_per_step),
        grid=(bsz, 3 * steps),
        in_specs=[
            pl.BlockSpec((None, rows, width, f_w), lambda b, t: (b, 0, stage_a_step(b, t), 0)),
            full(fa), full(g), full(cc), full(sc), full(perm), tw_spec, tw_spec,
        ],
        out_specs=pl.BlockSpec((None, width * GRID_W, f_w), lambda b, t: (b, stage_c_step(b, t), 0)),
        out_shape=jax.ShapeDtypeStruct((bsz, s, f_w), BF16),
        scratch_shapes=[pltpu.VMEM((rows, GRID_W, f_w), F32), pltpu.VMEM((rows, GRID_W, f_w), F32)],
        compiler_params=_params("arbitrary", "arbitrary"),
        name="fourier",
    )(u4, fa, g, cc, sc, perm, tw_cos, tw_sin)


def _bias_tables(rpb, rows):
    kw = min(WIN_W, GRID_W)
    cols = np.arange(GRID_W)
    col_start = np.clip(cols - kw // 2, 0, GRID_W - kw)
    kj = cols[None, :]
    allowed = (kj >= col_start[:, None]) & (kj < col_start[:, None] + kw)
    rel = kj - cols[:, None] + (WIN_W - 1)
    nrel = 2 * WIN_W - 1
    onehot = ((rel[None] == np.arange(nrel)[:, None, None]) & allowed[None]).astype(np.float32)
    sel = np.zeros((2 * nrel + 1, GRID_W, 2 * GRID_W), np.float32)
    sel[0:nrel, :, 0:GRID_W] = onehot
    sel[nrel:2 * nrel, :, GRID_W:] = onehot
    sel[2 * nrel] = np.tile(~allowed, (1, 2))
    rpb = rpb.astype(F32)
    mask_col = jnp.full(rpb.shape[:1] + (rpb.shape[1] - 1, 1), MASK_VALUE, F32)
    src = jnp.concatenate([rpb[:, :-1], rpb[:, 1:], mask_col], axis=-1)
    return jnp.einsum("hmd,dcx->hmcx", src, jnp.asarray(sel), precision=lax.Precision.HIGHEST)


def _attn_out_kernel(q_ref, k_ref, v_ref, t2_ref, yfn_ref, ga_ref, gb_ref, x_ref,
                     wna_ref, wfn_ref, bfn_ref, wo_ref, g1_ref, sc2_ref, sh2_ref, nw2_ref,
                     o_ref, h2_ref, yna_ref, s_ref, p_ref,
                     *, rows, kh, rows_per_step):
    t = pl.program_id(0)
    last_tile = pl.num_programs(0) - 2

    @pl.when(t == 0)
    def _init():
        yna_ref[...] = jnp.zeros_like(yna_ref)

    yna_prev = yna_ref[...].astype(BF16)
    yfn_prev = yfn_ref[...]

    rb = lax.rem(jnp.minimum(t, last_tile), rows // rows_per_step)
    n_pairs = q_ref.shape[1] // LANES
    lane = lax.broadcasted_iota(jnp.int32, (1, LANES), 1)
    low = lane < HEAD_DIM
    mask_lo = jnp.where(low, 1.0, 0.0).astype(BF16)
    mask_hi = jnp.where(low, 0.0, 1.0).astype(BF16)
    n_keys = kh * GRID_W

    key_start = []
    for j in range(rows_per_step):
        r = rb * rows_per_step + j
        rs = jnp.clip(r - kh // 2, 0, rows - kh)
        key_start.append((pl.multiple_of(rs * GRID_W, GRID_W), rs - r + (WIN_H - 1)))

    for j in range(rows_per_step):
        ks, idx0 = key_start[j]
        for p in range(n_pairs):
            cs = slice(LANES * p, LANES * (p + 1))
            q2 = q_ref[j * GRID_W:(j + 1) * GRID_W, cs]
            k2 = k_ref[pl.ds(ks, n_keys), cs]
            qs = jnp.concatenate([q2 * mask_lo, q2 * mask_hi], axis=0)
            s = lax.dot_general(qs, k2, (((1,), (1,)), ((), ())), preferred_element_type=F32)
            bias = jnp.concatenate(
                [jnp.concatenate([t2_ref[2 * p + hh, idx0 + 2 * m] for m in range(kh // 2)], axis=1)
                 for hh in range(2)], axis=0)
            s_ref[j * n_pairs + p] = s + bias

    tiles_left = list(range(rows_per_step * n_pairs))

    def softmax_next_tile():
        if tiles_left:
            ti = tiles_left.pop(0)
            s = s_ref[ti]
            p_ref[ti] = jnp.exp(s - jnp.max(s, axis=-1, keepdims=True)).astype(BF16)

    def dot_interleaved(lhs, w_ref, cols):
        acc = None
        for k0 in range(0, lhs.shape[1], MXU_DIM):
            term = _dot(lhs[:, k0:k0 + MXU_DIM], w_ref[k0:k0 + MXU_DIM, cols])
            acc = term if acc is None else acc + term
            softmax_next_tile()
        return acc

    d = o_ref.shape[1]
    merged = []
    for c0 in range(0, d, MXU_DIM):
        cols = slice(c0, c0 + MXU_DIM)
        ya = dot_interleaved(yna_prev, wna_ref, cols)
        yf = dot_interleaved(yfn_prev, wfn_ref, cols) + bfn_ref[:, cols]
        merged.append((ga_ref[:, cols].astype(F32) * ya
                       + gb_ref[:, cols].astype(F32) * yf).astype(BF16))
    merged = jnp.concatenate(merged, axis=1)
    for c0 in range(0, d, MXU_DIM):
        cols = slice(c0, c0 + MXU_DIM)
        o_ref[:, cols] = (x_ref[:, cols]
                          + g1_ref[:, cols] * dot_interleaved(merged, wo_ref, cols))
    while tiles_left:
        softmax_next_tile()

    ones = jnp.ones((n_keys, LANES), BF16)
    norm_rows = o_ref.shape[0] // rows_per_step
    for j in range(rows_per_step):
        ks, _ = key_start[j]
        outs = []
        for p in range(n_pairs):
            ti = j * n_pairs + p
            v2 = v_ref[pl.ds(ks, n_keys), LANES * p:LANES * (p + 1)]
            o = _dot(p_ref[ti], jnp.concatenate([v2, ones], axis=1))
            o = o[:, 0:LANES] * (1.0 / o[:, LANES:2 * LANES])
            outs.append(jnp.where(low, o[0:GRID_W], o[GRID_W:2 * GRID_W]))
        yna_ref[j * GRID_W:(j + 1) * GRID_W, :] = jnp.concatenate(outs, axis=1)

        nr = slice(j * norm_rows, (j + 1) * norm_rows)
        x1 = o_ref[nr, :]
        ms = jnp.mean(x1 * x1, axis=-1, keepdims=True)
        y2 = x1 * lax.rsqrt(ms + EPS) * nw2_ref[...]
        h2_ref[nr, :] = (y2 * (1.0 + sc2_ref[...]) + sh2_ref[...]).astype(BF16)


def _attn_out(q, k, v, t2, yfn, ga, gb, x, w_na, w_fn, b_fn, w_o, gate1, scale2, shift2, norm2_w):
    bsz, s, d = x.shape
    na_w = q.shape[-1]
    f_w = yfn.shape[-1]
    rows = s // GRID_W
    kh = min(WIN_H, rows)
    assert kh % 2 == 0 and (2 * HEAD_DIM) == LANES
    rows_per_step = 8
    tq = rows_per_step * GRID_W
    n_tiles = rows_per_step * (na_w // LANES)

    per_seq = s // tq
    n_steps = bsz * per_seq + 1
    attn_tile = lambda t: jnp.minimum(t, n_steps - 2)
    out_tile = lambda t: jnp.maximum(t - 1, 0)
    tok_a = lambda w: pl.BlockSpec(
        (None, tq, w), lambda t: (attn_tile(t) // per_seq, attn_tile(t) % per_seq, 0))
    tok_o = lambda w: pl.BlockSpec(
        (None, tq, w), lambda t: (out_tile(t) // per_seq, out_tile(t) % per_seq, 0))
    whole_seq = pl.BlockSpec((None, s, na_w), lambda t: (attn_tile(t) // per_seq, 0, 0))
    per_batch = pl.BlockSpec((None, 1, d), lambda t: (out_tile(t) // per_seq, 0, 0))
    full = lambda a: pl.BlockSpec(a.shape, lambda t: (0,) * a.ndim)
    bfn = b_fn.reshape(1, d)
    nw2 = norm2_w.reshape(1, d)
    return pl.pallas_call(
        functools.partial(_attn_out_kernel, rows=rows, kh=kh, rows_per_step=rows_per_step),
        grid=(n_steps,),
        in_specs=[tok_a(na_w), whole_seq, whole_seq, full(t2), tok_o(f_w), tok_o(d), tok_o(d),
                  tok_o(d), full(w_na), full(w_fn), full(bfn), full(w_o), per_batch, per_batch,
                  per_batch, full(nw2)],
        out_specs=(tok_o(d), tok_o(d)),
        out_shape=(jax.ShapeDtypeStruct((bsz, s, d), F32), jax.ShapeDtypeStruct((bsz, s, d), BF16)),
        scratch_shapes=[pltpu.VMEM((tq, na_w), F32),
                        pltpu.VMEM((n_tiles, 2 * GRID_W, kh * GRID_W), F32),
                        pltpu.VMEM((n_tiles, 2 * GRID_W, kh * GRID_W), BF16)],
        compiler_params=_params("arbitrary"),
        name="attn_out",
    )(q, k, v, t2, yfn, ga, gb, x, w_na, w_fn, bfn, w_o, gate1, scale2, shift2, nw2)


def _mlp_kernel(x_ref, h_ref, g2_ref, w1_ref, w2_ref, o_ref, *, chunk):
    h = h_ref[...]
    acc = None
    for c0 in range(0, w1_ref.shape[1], chunk):
        a = jnp.maximum(_dot(h, w1_ref[:, c0:c0 + chunk]), 0.0)
        term = _dot((a * a).astype(BF16), w2_ref[c0:c0 + chunk, :])
        acc = term if acc is None else acc + term
    o_ref[...] = x_ref[...] + g2_ref[...] * acc


def _mlp(x, h, gate2, w1, w2):
    bsz, s, d = x.shape
    tm = 1024
    tok = pl.BlockSpec((None, tm, d), lambda b, i: (b, i, 0))
    per_batch = pl.BlockSpec((None, 1, d), lambda b, i: (b, 0, 0))
    full = lambda a: pl.BlockSpec(a.shape, lambda b, i: (0,) * a.ndim)
    return pl.pallas_call(
        functools.partial(_mlp_kernel, chunk=1024),
        grid=(bsz, s // tm),
        in_specs=[tok, tok, per_batch, full(w1), full(w2)],
        out_specs=tok,
        out_shape=jax.ShapeDtypeStruct((bsz, s, d), F32),
        compiler_params=_params("parallel", "parallel"),
        name="mlp",
    )(x, h, gate2, w1, w2)


def kernel(x, c, norm1_w, norm2_w, w_ada, b_ada, w_in, b_in, q_norm_w, k_norm_w, rpb,
           w_na_out, w_fn_out, b_fn_out, w_o, w_mlp_in, w_mlp_out):
    bsz, s, d = x.shape
    depth = w_ada.shape[0]
    na_w = w_na_out.shape[1]
    f_w = w_fn_out.shape[1]
    n_heads = na_w // HEAD_DIM
    rows = s // GRID_W
    for l in range(depth):
        mod = _ada(c, w_ada[l], b_ada[l])
        shift1, scale1, gate1, shift2, scale2, gate2 = [
            mod[:, i * d:(i + 1) * d].reshape(bsz, 1, d) for i in range(N_MOD)]

        qk_w = jnp.concatenate([jnp.tile(q_norm_w[l], n_heads) * (HEAD_DIM ** -0.5),
                                jnp.tile(k_norm_w[l], n_heads)]).reshape(1, 2 * na_w)
        (q, k, v, u, ga, gb), (w_na, w_fn, w_o_b, w_m1, w_m2) = _inproj(
            x, scale1, shift1, norm1_w[l], w_in[l].astype(BF16), b_in[l], qk_w, na_w, f_w,
            (w_na_out[l], w_fn_out[l], w_o[l], w_mlp_in[l], w_mlp_out[l]))
        yfn = _fourier(u)
        t2 = _bias_tables(rpb[l], rows)
        x, h2 = _attn_out(q, k, v, t2, yfn, ga, gb, x, w_na, w_fn, b_fn_out[l], w_o_b, gate1,
                          scale2, shift2, norm2_w[l])
        x = _mlp(x, h2, gate2, w_m1, w_m2)
    return x
```

```python
import functools

import numpy as np
import jax
import jax.numpy as jnp
from jax import lax
from jax.experimental import pallas as pl
from jax.experimental.pallas import tpu as pltpu

F32 = jnp.float32
BF16 = jnp.bfloat16

GRID_W = 64
HEAD_DIM = 64
WIN_H = 8
WIN_W = 16
N_FOURIER_GROUPS = 4
N_MOD = 6
EPS = 1e-6
MASK_VALUE = -1e30

SUBLANES = 8
LANES = 128
MXU_DIM = 256
SOFTMAX_SKEW = 4
PV_SKEW = 8
OUT_UNIT_EVERY = 4
VMEM_LIMIT_BYTES = 56 * 1024 * 1024


def _dot(a, b):
    return jnp.dot(a, b, preferred_element_type=F32)


def _sigmoid(x):
    return 1.0 / (1.0 + jnp.exp(-x))


def _params(*semantics):
    return pltpu.CompilerParams(dimension_semantics=semantics, vmem_limit_bytes=VMEM_LIMIT_BYTES)


def _ada_kernel(ct_ref, w_ref, b_ref, o_ref):
    ct = ct_ref[...]
    st = ct * _sigmoid(ct)
    w = w_ref[...]
    rows = [jnp.sum(w * st[:, b:b + 1], axis=0, keepdims=True) for b in range(ct.shape[1])]
    o_ref[...] = jnp.concatenate(rows, axis=0) + b_ref[...]


def _ada(c, w, b):
    bsz, d = c.shape
    n = w.shape[1]
    tn = n // N_MOD
    return pl.pallas_call(
        _ada_kernel,
        grid=(n // tn,),
        in_specs=[pl.BlockSpec((d, bsz), lambda j: (0, 0)),
                  pl.BlockSpec((d, tn), lambda j: (0, j)),
                  pl.BlockSpec((1, tn), lambda j: (0, j))],
        out_specs=pl.BlockSpec((bsz, tn), lambda j: (0, j)),
        out_shape=jax.ShapeDtypeStruct((bsz, n), F32),
        compiler_params=_params("parallel"),
        name="ada",
    )(c.T, w, b.reshape(1, n))


def _inproj_kernel(x_ref, sc_ref, sh_ref, nw_ref, w_ref, b_ref, qkw_ref, gsum_ref, gexp_ref, *refs,
                   na_w, f_w, d, chunk, n_cast):
    cast_in = refs[:n_cast]
    q_ref, k_ref, v_ref, u_ref, ga_ref, gb_ref = refs[n_cast:n_cast + 6]
    cast_out = refs[n_cast + 6:]
    for src, dst in zip(cast_in, cast_out):
        dst[...] = src[...].astype(BF16)

    for c0 in range(0, x_ref.shape[0], chunk):
        rows = slice(c0, c0 + chunk)
        x = x_ref[rows, :]
        ms = jnp.mean(x * x, axis=-1, keepdims=True)
        y = x * lax.rsqrt(ms + EPS) * nw_ref[...]
        h = (y * (1.0 + sc_ref[...]) + sh_ref[...]).astype(BF16)

        qk = 2 * na_w
        zqk = _dot(h, w_ref[:, 0:qk]) + b_ref[:, 0:qk]
        ssq = _dot((zqk * zqk).astype(BF16), gsum_ref[...])
        rinv = lax.rsqrt(ssq * (1.0 / HEAD_DIM) + EPS)
        rinv_hi = rinv.astype(BF16)
        rinv_lo = (rinv - rinv_hi.astype(F32)).astype(BF16)
        scale = _dot(jnp.concatenate([rinv_hi, rinv_lo], axis=1), gexp_ref[...])
        zn = zqk * scale * qkw_ref[...]
        q_ref[rows, :] = zn[:, 0:na_w].astype(BF16)
        k_ref[rows, :] = zn[:, na_w:qk].astype(BF16)

        o = qk
        v_ref[rows, :] = (_dot(h, w_ref[:, o:o + na_w]) + b_ref[:, o:o + na_w]).astype(BF16)
        o += na_w
        u_ref[rows, :] = _dot(h, w_ref[:, o:o + f_w]) + b_ref[:, o:o + f_w]
        o += f_w
        ga_ref[rows, :] = _sigmoid(_dot(h, w_ref[:, o:o + d]) + b_ref[:, o:o + d]).astype(BF16)
        o += d
        gb_ref[rows, :] = _sigmoid(_dot(h, w_ref[:, o:o + d]) + b_ref[:, o:o + d]).astype(BF16)


def _inproj(x, scale1, shift1, norm_w, w_in, b_in, qk_w, na_w, f_w, later_weights):
    bsz, s, d = x.shape
    in_w = w_in.shape[1]
    tm = 1024
    qk = 2 * na_w
    n_grp = qk // HEAD_DIM
    grp = np.arange(qk) // HEAD_DIM
    gsum = jnp.asarray((grp[:, None] == np.arange(LANES)[None, :]).astype(np.float32), BF16)
    gexp1 = (np.arange(LANES)[:, None] == grp[None, :]).astype(np.float32)
    gexp = jnp.asarray(np.concatenate([gexp1, gexp1], axis=0), BF16)
    assert n_grp <= LANES

    tok = lambda w: pl.BlockSpec((None, tm, w), lambda b, i: (b, i, 0))
    per_batch = pl.BlockSpec((None, 1, d), lambda b, i: (b, 0, 0))
    full = lambda a: pl.BlockSpec(a.shape, lambda b, i: (0,) * a.ndim)
    nw = norm_w.reshape(1, d)
    bi = b_in.reshape(1, in_w)
    out_shapes = (
        jax.ShapeDtypeStruct((bsz, s, na_w), BF16),
        jax.ShapeDtypeStruct((bsz, s, na_w), BF16),
        jax.ShapeDtypeStruct((bsz, s, na_w), BF16),
        jax.ShapeDtypeStruct((bsz, s, f_w), F32),
        jax.ShapeDtypeStruct((bsz, s, d), BF16),
        jax.ShapeDtypeStruct((bsz, s, d), BF16),
    )
    n_steps = bsz * (s // tm)
    slab = lambda a: pl.BlockSpec((a.shape[0] // n_steps, a.shape[1]),
                                  lambda b, i: (b * (s // tm) + i, 0))
    for a in later_weights:
        assert a.shape[0] % (n_steps * 2 * SUBLANES) == 0, a.shape
    outs = pl.pallas_call(
        functools.partial(_inproj_kernel, na_w=na_w, f_w=f_w, d=d, chunk=256,
                          n_cast=len(later_weights)),
        grid=(bsz, s // tm),
        in_specs=[tok(d), per_batch, per_batch, full(nw), full(w_in), full(bi), full(qk_w),
                  full(gsum), full(gexp)] + [slab(a) for a in later_weights],
        out_specs=(tok(na_w), tok(na_w), tok(na_w), tok(f_w), tok(d), tok(d))
        + tuple(slab(a) for a in later_weights),
        out_shape=out_shapes + tuple(jax.ShapeDtypeStruct(a.shape, BF16) for a in later_weights),
        compiler_params=_params("parallel", "parallel"),
        name="inproj",
    )(x, scale1, shift1, nw, w_in, bi, qk_w, gsum, gexp, *later_weights)
    return outs[:6], outs[6:]


def _fourier_constants(rows, f_w):
    n = rows
    r8 = SUBLANES
    k = np.arange(n)
    ang_a = 2.0 * np.pi * np.outer(k, k) / n
    eye8 = np.eye(r8)
    norm = 1.0 / np.sqrt(n)
    fa_re = np.kron(np.cos(ang_a), eye8) * norm
    fa_im = np.kron(-np.sin(ang_a), eye8) * norm
    fa = np.concatenate([fa_re, fa_im], axis=0)

    nblk = n // r8
    s2 = k[None, :]
    s2p = k[:, None]
    g = np.zeros((r8 // 2, 4 * n, 4 * n))
    for pp in range(r8 // 2):
        gr = np.zeros((2, n, 2, n))
        gi = np.zeros((2, n, 2, n))
        for ll in range(2):
            l = 2 * pp + ll
            ang = 2.0 * np.pi * (s2 * l / (n * GRID_W) + s2 * s2p / n)
            gr[ll, :, ll, :] = np.cos(ang) * norm
            gi[ll, :, ll, :] = -np.sin(ang) * norm
        gr = gr.reshape(2 * n, 2 * n)
        gi = gi.reshape(2 * n, 2 * n)
        g[pp] = np.block([[gr, -gi], [gi, gr]])

    perm = np.zeros((n * r8, n * r8))
    s1p = np.repeat(k, r8)
    jj = np.tile(np.arange(r8), n)
    perm[jj * n + s1p, s1p * r8 + jj] = 1.0

    t_idx = np.arange(nblk)[:, None, None]
    blk_of_row = (k // r8)[None, :, None]
    j_idx = np.arange(r8)[None, None, :]
    ang_t = 2.0 * np.pi * (t_idx * r8 + j_idx) * blk_of_row * r8 / (n * GRID_W)
    ang_t = ang_t.reshape(nblk, n * r8, 1)
    tw_cos = np.broadcast_to(np.cos(ang_t), (nblk, n * r8, LANES))
    tw_sin = np.broadcast_to(np.sin(ang_t), (nblk, n * r8, LANES))

    gd = f_w // N_FOURIER_GROUPS
    c = np.arange(gd)
    ang_c = 2.0 * np.pi * np.outer(c, c) / gd
    eye_g = np.eye(N_FOURIER_GROUPS)
    cc = np.kron(eye_g, np.cos(ang_c)) / np.sqrt(gd)
    sc = np.kron(eye_g, np.sin(ang_c)) / np.sqrt(gd)
    as_f32 = lambda a: jnp.asarray(np.ascontiguousarray(a, dtype=np.float32))
    to_bf16 = lambda a: as_f32(a).astype(BF16)
    return (to_bf16(fa), to_bf16(g), to_bf16(cc), to_bf16(sc), to_bf16(perm),
            as_f32(tw_cos), as_f32(tw_sin))


def _fourier_kernel(u_ref, fa_ref, g_ref, cc_ref, sc_ref, perm_ref, twc_ref, tws_ref, y_ref,
                    sre_ref, sim_ref, *, steps_per_stage, blocks_per_step):
    t = pl.program_id(1)
    n, _, f_w = u_ref.shape
    r8 = SUBLANES
    m = n * r8

    @pl.when(t < steps_per_stage)
    def _stage_a():
        for j in range(blocks_per_step):
            xin = u_ref[:, j * r8:(j + 1) * r8, :].reshape(m, f_w).astype(BF16)
            a = _dot(fa_ref[...], xin)
            reps = f_w // LANES
            cos_t = jnp.concatenate([twc_ref[j]] * reps, axis=1)
            sin_t = jnp.concatenate([tws_ref[j]] * reps, axis=1)
            a_re = a[0:m]
            a_im = a[m:2 * m]
            b_re = a_re * cos_t + a_im * sin_t
            b_im = a_im * cos_t - a_re * sin_t
            off = pl.multiple_of((t * blocks_per_step + j) * r8, r8)
            sre_ref[:, pl.ds(off, r8), :] = b_re.reshape(n, r8, f_w)
            sim_ref[:, pl.ds(off, r8), :] = b_im.reshape(n, r8, f_w)

    @pl.when((t >= steps_per_stage) & (t < 2 * steps_per_stage))
    def _stage_b():
        pair_rows = 2 * n
        for j in range(blocks_per_step):
            off = pl.multiple_of(((t - steps_per_stage) * blocks_per_step + j) * r8, r8)
            br = sre_ref[pl.ds(off, r8)].reshape(m, f_w)
            bi = sim_ref[pl.ds(off, r8)].reshape(m, f_w)
            p_re, p_im = [], []
            for pp in range(r8 // 2):
                rs_ = slice(pp * pair_rows, (pp + 1) * pair_rows)
                bcat = jnp.concatenate([br[rs_], bi[rs_]], axis=0).astype(BF16)
                p = _dot(g_ref[pp], bcat)
                p_re.append(p[0:pair_rows])
                p_im.append(p[pair_rows:2 * pair_rows])
            p_re = jnp.concatenate(p_re, axis=0).astype(BF16)
            p_im = jnp.concatenate(p_im, axis=0).astype(BF16)
            y = _dot(p_re, cc_ref[...]) + _dot(p_im, sc_ref[...])
            sre_ref[pl.ds(off, r8)] = y.reshape(r8, n, f_w)

    @pl.when(t >= 2 * steps_per_stage)
    def _stage_c():
        for j in range(blocks_per_step):
            off = pl.multiple_of(((t - 2 * steps_per_stage) * blocks_per_step + j) * r8, r8)
            yin = sre_ref[:, pl.ds(off, r8), :].reshape(m, f_w).astype(BF16)
            y_ref[j * m:(j + 1) * m, :] = _dot(perm_ref[...], yin).astype(y_ref.dtype)


def _fourier(u):
    bsz, s, f_w = u.shape
    rows = s // GRID_W
    assert rows == GRID_W, "the two-stage position DFT assumes a square token grid"
    r8 = SUBLANES
    nblk = rows // r8
    fa, g, cc, sc, perm, tw_cos, tw_sin = _fourier_constants(rows, f_w)
    u4 = u.reshape(bsz, rows, GRID_W, f_w)
    blocks_per_step = 4
    steps = nblk // blocks_per_step
    width = blocks_per_step * r8
    stage_a_step = lambda b, t: jnp.minimum(t, steps - 1)
    stage_c_step = lambda b, t: jnp.maximum(t - 2 * steps, 0)
    full = lambda a: pl.BlockSpec(a.shape, lambda b, t: (0,) * a.ndim)
    tw_spec = pl.BlockSpec((blocks_per_step,) + tw_cos.shape[1:],
                           lambda b, t: (stage_a_step(b, t), 0, 0))
    return pl.pallas_call(
        functools.partial(_fourier_kernel, steps_per_stage=steps, blocks_per_step=blocks_per_step),
        grid=(bsz, 3 * steps),
        in_specs=[
            pl.BlockSpec((None, rows, width, f_w), lambda b, t: (b, 0, stage_a_step(b, t), 0)),
            full(fa), full(g), full(cc), full(sc), full(perm), tw_spec, tw_spec,
        ],
        out_specs=pl.BlockSpec((None, width * GRID_W, f_w), lambda b, t: (b, stage_c_step(b, t), 0)),
        out_shape=jax.ShapeDtypeStruct((bsz, s, f_w), BF16),
        scratch_shapes=[pltpu.VMEM((rows, GRID_W, f_w), F32), pltpu.VMEM((rows, GRID_W, f_w), F32)],
        compiler_params=_params("arbitrary", "arbitrary"),
        name="fourier",
    )(u4, fa, g, cc, sc, perm, tw_cos, tw_sin)


def _bias_tables(rpb, rows):
    kw = min(WIN_W, GRID_W)
    cols = np.arange(GRID_W)
    col_start = np.clip(cols - kw // 2, 0, GRID_W - kw)
    kj = cols[None, :]
    allowed = (kj >= col_start[:, None]) & (kj < col_start[:, None] + kw)
    rel = kj - cols[:, None] + (WIN_W - 1)
    nrel = 2 * WIN_W - 1
    onehot = ((rel[None] == np.arange(nrel)[:, None, None]) & allowed[None]).astype(np.float32)
    sel = np.zeros((2 * nrel + 1, GRID_W, 2 * GRID_W), np.float32)
    sel[0:nrel, :, 0:GRID_W] = onehot
    sel[nrel:2 * nrel, :, GRID_W:] = onehot
    sel[2 * nrel] = np.tile(~allowed, (1, 2))
    rpb = rpb.astype(F32)
    mask_col = jnp.full(rpb.shape[:1] + (rpb.shape[1] - 1, 1), MASK_VALUE, F32)
    src = jnp.concatenate([rpb[:, :-1], rpb[:, 1:], mask_col], axis=-1)
    return jnp.einsum("hmd,dcx->hmcx", src, jnp.asarray(sel), precision=lax.Precision.HIGHEST)


def _attn_out_kernel(q_ref, k_ref, v_ref, t2_ref, yfn_ref, ga_ref, gb_ref, x_ref,
                     wna_ref, wfn_ref, bfn_ref, wo_ref, g1_ref, sc2_ref, sh2_ref, nw2_ref,
                     o_ref, h2_ref, yna_ref, s_ref, p_ref,
                     *, rows, kh, rows_per_step):
    t = pl.program_id(0)
    last_tile = pl.num_programs(0) - 2

    @pl.when(t == 0)
    def _init():
        yna_ref[...] = jnp.zeros_like(yna_ref)

    yna_prev = yna_ref[...].astype(BF16)
    yfn_prev = yfn_ref[...]

    rb = lax.rem(jnp.minimum(t, last_tile), rows // rows_per_step)
    n_pairs = q_ref.shape[1] // LANES
    lane = lax.broadcasted_iota(jnp.int32, (1, LANES), 1)
    low = lane < HEAD_DIM
    mask_lo = jnp.where(low, 1.0, 0.0).astype(BF16)
    mask_hi = jnp.where(low, 0.0, 1.0).astype(BF16)
    n_keys = kh * GRID_W

    key_start = []
    for j in range(rows_per_step):
        r = rb * rows_per_step + j
        rs = jnp.clip(r - kh // 2, 0, rows - kh)
        key_start.append((pl.multiple_of(rs * GRID_W, GRID_W), rs - r + (WIN_H - 1)))

    def qk_tile(ti):
        j, p = divmod(ti, n_pairs)
        ks, idx0 = key_start[j]
        cs = slice(LANES * p, LANES * (p + 1))
        q2 = q_ref[j * GRID_W:(j + 1) * GRID_W, cs]
        k2 = k_ref[pl.ds(ks, n_keys), cs]
        qs = jnp.concatenate([q2 * mask_lo, q2 * mask_hi], axis=0)
        s = lax.dot_general(qs, k2, (((1,), (1,)), ((), ())), preferred_element_type=F32)
        bias = jnp.concatenate(
            [jnp.concatenate([t2_ref[2 * p + hh, idx0 + 2 * m] for m in range(kh // 2)], axis=1)
             for hh in range(2)], axis=0)
        s_ref[ti] = s + bias

    def softmax_tile(ti):
        s = s_ref[ti]
        p_ref[ti] = jnp.exp(s - jnp.max(s, axis=-1, keepdims=True)).astype(BF16)

    ones = jnp.ones((n_keys, LANES), BF16)
    row_outs = []

    def pv_tile(ti):
        j, p = divmod(ti, n_pairs)
        ks, _ = key_start[j]
        v2 = v_ref[pl.ds(ks, n_keys), LANES * p:LANES * (p + 1)]
        o = _dot(p_ref[ti], jnp.concatenate([v2, ones], axis=1))
        o = o[:, 0:LANES] * (1.0 / o[:, LANES:2 * LANES])
        row_outs.append(jnp.where(low, o[0:GRID_W], o[GRID_W:2 * GRID_W]))
        if p == n_pairs - 1:
            yna_ref[j * GRID_W:(j + 1) * GRID_W, :] = jnp.concatenate(row_outs, axis=1)
            row_outs.clear()

    d = o_ref.shape[1]
    merged = []

    def merge_chunk(c0):
        cols = slice(c0, c0 + MXU_DIM)
        ya = _dot(yna_prev, wna_ref[:, cols])
        yf = _dot(yfn_prev, wfn_ref[:, cols]) + bfn_ref[:, cols]
        merged.append((ga_ref[:, cols].astype(F32) * ya
                       + gb_ref[:, cols].astype(F32) * yf).astype(BF16))

    def out_chunk(c0):
        cols = slice(c0, c0 + MXU_DIM)
        lhs = jnp.concatenate(merged, axis=1)
        o_ref[:, cols] = x_ref[:, cols] + g1_ref[:, cols] * _dot(lhs, wo_ref[:, cols])

    norm_rows = o_ref.shape[0] // rows_per_step

    def norm_chunk(j):
        nr = slice(j * norm_rows, (j + 1) * norm_rows)
        x1 = o_ref[nr, :]
        ms = jnp.mean(x1 * x1, axis=-1, keepdims=True)
        y2 = x1 * lax.rsqrt(ms + EPS) * nw2_ref[...]
        h2_ref[nr, :] = (y2 * (1.0 + sc2_ref[...]) + sh2_ref[...]).astype(BF16)

    matmul_units = ([functools.partial(merge_chunk, c0) for c0 in range(0, d, MXU_DIM)]
                    + [functools.partial(out_chunk, c0) for c0 in range(0, d, MXU_DIM)])
    norm_units = [functools.partial(norm_chunk, j) for j in range(rows_per_step)]

    n_tiles = rows_per_step * n_pairs
    for u in range(n_tiles + PV_SKEW):
        if u < n_tiles:
            qk_tile(u)
        if 0 <= u - SOFTMAX_SKEW < n_tiles:
            softmax_tile(u - SOFTMAX_SKEW)
        if 0 <= u - PV_SKEW < n_tiles:
            pv_tile(u - PV_SKEW)
        if matmul_units and u % OUT_UNIT_EVERY == 1:
            matmul_units.pop(0)()
        elif not matmul_units and norm_units:
            norm_units.pop(0)()
    assert not matmul_units and not norm_units


def _attn_out(q, k, v, t2, yfn, ga, gb, x, w_na, w_fn, b_fn, w_o, gate1, scale2, shift2, norm2_w):
    bsz, s, d = x.shape
    na_w = q.shape[-1]
    f_w = yfn.shape[-1]
    rows = s // GRID_W
    kh = min(WIN_H, rows)
    assert kh % 2 == 0 and (2 * HEAD_DIM) == LANES
    rows_per_step = 8
    tq = rows_per_step * GRID_W
    n_tiles = rows_per_step * (na_w // LANES)

    per_seq = s // tq
    n_steps = bsz * per_seq + 1
    attn_tile = lambda t: jnp.minimum(t, n_steps - 2)
    out_tile = lambda t: jnp.maximum(t - 1, 0)
    tok_a = lambda w: pl.BlockSpec(
        (None, tq, w), lambda t: (attn_tile(t) // per_seq, attn_tile(t) % per_seq, 0))
    tok_o = lambda w: pl.BlockSpec(
        (None, tq, w), lambda t: (out_tile(t) // per_seq, out_tile(t) % per_seq, 0))
    whole_seq = pl.BlockSpec((None, s, na_w), lambda t: (attn_tile(t) // per_seq, 0, 0))
    per_batch = pl.BlockSpec((None, 1, d), lambda t: (out_tile(t) // per_seq, 0, 0))
    full = lambda a: pl.BlockSpec(a.shape, lambda t: (0,) * a.ndim)
    bfn = b_fn.reshape(1, d)
    nw2 = norm2_w.reshape(1, d)
    return pl.pallas_call(
        functools.partial(_attn_out_kernel, rows=rows, kh=kh, rows_per_step=rows_per_step),
        grid=(n_steps,),
        in_specs=[tok_a(na_w), whole_seq, whole_seq, full(t2), tok_o(f_w), tok_o(d), tok_o(d),
                  tok_o(d), full(w_na), full(w_fn), full(bfn), full(w_o), per_batch, per_batch,
                  per_batch, full(nw2)],
        out_specs=(tok_o(d), tok_o(d)),
        out_shape=(jax.ShapeDtypeStruct((bsz, s, d), F32), jax.ShapeDtypeStruct((bsz, s, d), BF16)),
        scratch_shapes=[pltpu.VMEM((tq, na_w), F32),
                        pltpu.VMEM((n_tiles, 2 * GRID_W, kh * GRID_W), F32),
                        pltpu.VMEM((n_tiles, 2 * GRID_W, kh * GRID_W), BF16)],
        compiler_params=_params("arbitrary"),
        name="attn_out",
    )(q, k, v, t2, yfn, ga, gb, x, w_na, w_fn, bfn, w_o, gate1, scale2, shift2, nw2)


def _mlp_kernel(x_ref, h_ref, g2_ref, w1_ref, w2_ref, o_ref, *, chunk):
    h = h_ref[...]
    acc = None
    for c0 in range(0, w1_ref.shape[1], chunk):
        a = jnp.maximum(_dot(h, w1_ref[:, c0:c0 + chunk]), 0.0)
        term = _dot((a * a).astype(BF16), w2_ref[c0:c0 + chunk, :])
        acc = term if acc is None else acc + term
    o_ref[...] = x_ref[...] + g2_ref[...] * acc


def _mlp(x, h, gate2, w1, w2):
    bsz, s, d = x.shape
    tm = 1024
    tok = pl.BlockSpec((None, tm, d), lambda b, i: (b, i, 0))
    per_batch = pl.BlockSpec((None, 1, d), lambda b, i: (b, 0, 0))
    full = lambda a: pl.BlockSpec(a.shape, lambda b, i: (0,) * a.ndim)
    return pl.pallas_call(
        functools.partial(_mlp_kernel, chunk=1024),
        grid=(bsz, s // tm),
        in_specs=[tok, tok, per_batch, full(w1), full(w2)],
        out_specs=tok,
        out_shape=jax.ShapeDtypeStruct((bsz, s, d), F32),
        compiler_params=_params("parallel", "parallel"),
        name="mlp",
    )(x, h, gate2, w1, w2)


def kernel(x, c, norm1_w, norm2_w, w_ada, b_ada, w_in, b_in, q_norm_w, k_norm_w, rpb,
           w_na_out, w_fn_out, b_fn_out, w_o, w_mlp_in, w_mlp_out):
    bsz, s, d = x.shape
    depth = w_ada.shape[0]
    na_w = w_na_out.shape[1]
    f_w = w_fn_out.shape[1]
    n_heads = na_w // HEAD_DIM
    rows = s // GRID_W
    for l in range(depth):
        mod = _ada(c, w_ada[l], b_ada[l])
        shift1, scale1, gate1, shift2, scale2, gate2 = [
            mod[:, i * d:(i + 1) * d].reshape(bsz, 1, d) for i in range(N_MOD)]

        qk_w = jnp.concatenate([jnp.tile(q_norm_w[l], n_heads) * (HEAD_DIM ** -0.5),
                                jnp.tile(k_norm_w[l], n_heads)]).reshape(1, 2 * na_w)
        (q, k, v, u, ga, gb), (w_na, w_fn, w_o_b, w_m1, w_m2) = _inproj(
            x, scale1, shift1, norm1_w[l], w_in[l].astype(BF16), b_in[l], qk_w, na_w, f_w,
            (w_na_out[l], w_fn_out[l], w_o[l], w_mlp_in[l], w_mlp_out[l]))
        yfn = _fourier(u)
        t2 = _bias_tables(rpb[l], rows)
        x, h2 = _attn_out(q, k, v, t2, yfn, ga, gb, x, w_na, w_fn, b_fn_out[l], w_o_b, gate1,
                          scale2, shift2, norm2_w[l])
        x = _mlp(x, h2, gate2, w_m1, w_m2)
    return x
```

```python
import functools

import numpy as np
import jax
import jax.numpy as jnp
from jax import lax
from jax.experimental import pallas as pl
from jax.experimental.pallas import tpu as pltpu

F32 = jnp.float32
BF16 = jnp.bfloat16

GRID_W = 64
HEAD_DIM = 64
WIN_H = 8
WIN_W = 16
N_FOURIER_GROUPS = 4
N_MOD = 6
EPS = 1e-6
MASK_VALUE = -1e30

SUBLANES = 8
LANES = 128
MXU_DIM = 256
SOFTMAX_SKEW = 2
PV_SKEW = 4
OUT_UNIT_EVERY = 4
NORM_UNIT_EVERY = 1
VMEM_LIMIT_BYTES = 56 * 1024 * 1024


def _dot(a, b):
    return jnp.dot(a, b, preferred_element_type=F32)


def _sigmoid(x):
    return 0.5 * jnp.tanh(0.5 * x) + 0.5


def _params(*semantics):
    return pltpu.CompilerParams(dimension_semantics=semantics, vmem_limit_bytes=VMEM_LIMIT_BYTES)


def _ada_kernel(ct_ref, w_ref, b_ref, o_ref):
    ct = ct_ref[...]
    st = ct * _sigmoid(ct)
    w = w_ref[...]
    rows = [jnp.sum(w * st[:, b:b + 1], axis=0, keepdims=True) for b in range(ct.shape[1])]
    o_ref[...] = jnp.concatenate(rows, axis=0) + b_ref[...]


def _ada(c, w, b):
    bsz, d = c.shape
    n = w.shape[1]
    tn = n // N_MOD
    return pl.pallas_call(
        _ada_kernel,
        grid=(n // tn,),
        in_specs=[pl.BlockSpec((d, bsz), lambda j: (0, 0)),
                  pl.BlockSpec((d, tn), lambda j: (0, j)),
                  pl.BlockSpec((1, tn), lambda j: (0, j))],
        out_specs=pl.BlockSpec((bsz, tn), lambda j: (0, j)),
        out_shape=jax.ShapeDtypeStruct((bsz, n), F32),
        compiler_params=_params("parallel"),
        name="ada",
    )(c.T, w, b.reshape(1, n))


def _inproj_kernel(x_ref, sc_ref, sh_ref, nw_ref, w_ref, b_ref, qkw_ref, gsum_ref, gexp_ref, *refs,
                   na_w, f_w, d, chunk, n_cast):
    cast_in = refs[:n_cast]
    q_ref, k_ref, v_ref, u_ref, ga_ref, gb_ref = refs[n_cast:n_cast + 6]
    cast_out = refs[n_cast + 6:]
    for src, dst in zip(cast_in, cast_out):
        dst[...] = src[...].astype(BF16)

    def normed(c0):
        x = x_ref[c0:c0 + chunk, :]
        ms = jnp.mean(x * x, axis=-1, keepdims=True)
        y = x * lax.rsqrt(ms + EPS) * nw_ref[...]
        return (y * (1.0 + sc_ref[...]) + sh_ref[...]).astype(BF16)

    def proj(h, lo, width):
        return _dot(h, w_ref[:, lo:lo + width]) + b_ref[:, lo:lo + width]

    qk = 2 * na_w
    o_v, o_u, o_ga, o_gb = qk, qk + na_w, qk + na_w + f_w, qk + na_w + f_w + d
    n_rows = x_ref.shape[0]
    h = normed(0)
    for c0 in range(0, n_rows, chunk):
        rows = slice(c0, c0 + chunk)
        zqk = proj(h, 0, qk)
        h_next = normed(c0 + chunk) if c0 + chunk < n_rows else None
        v_ref[rows, :] = proj(h, o_v, na_w).astype(BF16)
        ssq = _dot((zqk * zqk).astype(BF16), gsum_ref[...])
        u_ref[rows, :] = proj(h, o_u, f_w)
        rinv = lax.rsqrt(ssq * (1.0 / HEAD_DIM) + EPS)
        rinv_hi = rinv.astype(BF16)
        rinv_lo = (rinv - rinv_hi.astype(F32)).astype(BF16)
        scale = _dot(jnp.concatenate([rinv_hi, rinv_lo], axis=1), gexp_ref[...])
        ga_ref[rows, :] = _sigmoid(proj(h, o_ga, d)).astype(BF16)
        zn = zqk * scale * qkw_ref[...]
        q_ref[rows, :] = zn[:, 0:na_w].astype(BF16)
        k_ref[rows, :] = zn[:, na_w:qk].astype(BF16)
        gb_ref[rows, :] = _sigmoid(proj(h, o_gb, d)).astype(BF16)
        h = h_next


def _inproj(x, scale1, shift1, norm_w, w_in, b_in, qk_w, na_w, f_w, later_weights):
    bsz, s, d = x.shape
    in_w = w_in.shape[1]
    tm = 1024
    qk = 2 * na_w
    n_grp = qk // HEAD_DIM
    grp = np.arange(qk) // HEAD_DIM
    gsum = jnp.asarray((grp[:, None] == np.arange(LANES)[None, :]).astype(np.float32), BF16)
    gexp1 = (np.arange(LANES)[:, None] == grp[None, :]).astype(np.float32)
    gexp = jnp.asarray(np.concatenate([gexp1, gexp1], axis=0), BF16)
    assert n_grp <= LANES

    tok = lambda w: pl.BlockSpec((None, tm, w), lambda b, i: (b, i, 0))
    per_batch = pl.BlockSpec((None, 1, d), lambda b, i: (b, 0, 0))
    full = lambda a: pl.BlockSpec(a.shape, lambda b, i: (0,) * a.ndim)
    nw = norm_w.reshape(1, d)
    bi = b_in.reshape(1, in_w)
    out_shapes = (
        jax.ShapeDtypeStruct((bsz, s, na_w), BF16),
        jax.ShapeDtypeStruct((bsz, s, na_w), BF16),
        jax.ShapeDtypeStruct((bsz, s, na_w), BF16),
        jax.ShapeDtypeStruct((bsz, s, f_w), F32),
        jax.ShapeDtypeStruct((bsz, s, d), BF16),
        jax.ShapeDtypeStruct((bsz, s, d), BF16),
    )
    n_steps = bsz * (s // tm)
    slab = lambda a: pl.BlockSpec((a.shape[0] // n_steps, a.shape[1]),
                                  lambda b, i: (b * (s // tm) + i, 0))
    for a in later_weights:
        assert a.shape[0] % (n_steps * 2 * SUBLANES) == 0, a.shape
    outs = pl.pallas_call(
        functools.partial(_inproj_kernel, na_w=na_w, f_w=f_w, d=d, chunk=256,
                          n_cast=len(later_weights)),
        grid=(bsz, s // tm),
        in_specs=[tok(d), per_batch, per_batch, full(nw), full(w_in), full(bi), full(qk_w),
                  full(gsum), full(gexp)] + [slab(a) for a in later_weights],
        out_specs=(tok(na_w), tok(na_w), tok(na_w), tok(f_w), tok(d), tok(d))
        + tuple(slab(a) for a in later_weights),
        out_shape=out_shapes + tuple(jax.ShapeDtypeStruct(a.shape, BF16) for a in later_weights),
        compiler_params=_params("parallel", "parallel"),
        name="inproj",
    )(x, scale1, shift1, nw, w_in, bi, qk_w, gsum, gexp, *later_weights)
    return outs[:6], outs[6:]


def _fourier_constants(rows, f_w):
    n = rows
    r8 = SUBLANES
    k = np.arange(n)
    ang_a = 2.0 * np.pi * np.outer(k, k) / n
    eye8 = np.eye(r8)
    norm = 1.0 / np.sqrt(n)
    fa_re = np.kron(np.cos(ang_a), eye8) * norm
    fa_im = np.kron(-np.sin(ang_a), eye8) * norm
    fa = np.concatenate([fa_re, fa_im], axis=0)

    nblk = n // r8
    s2 = k[None, :]
    s2p = k[:, None]
    g = np.zeros((r8 // 2, 4 * n, 4 * n))
    for pp in range(r8 // 2):
        gr = np.zeros((2, n, 2, n))
        gi = np.zeros((2, n, 2, n))
        for ll in range(2):
            l = 2 * pp + ll
            ang = 2.0 * np.pi * (s2 * l / (n * GRID_W) + s2 * s2p / n)
            gr[ll, :, ll, :] = np.cos(ang) * norm
            gi[ll, :, ll, :] = -np.sin(ang) * norm
        gr = gr.reshape(2 * n, 2 * n)
        gi = gi.reshape(2 * n, 2 * n)
        g[pp] = np.block([[gr, -gi], [gi, gr]])

    perm = np.zeros((n * r8, n * r8))
    s1p = np.repeat(k, r8)
    jj = np.tile(np.arange(r8), n)
    perm[jj * n + s1p, s1p * r8 + jj] = 1.0

    t_idx = np.arange(nblk)[:, None, None]
    blk_of_row = (k // r8)[None, :, None]
    j_idx = np.arange(r8)[None, None, :]
    ang_t = 2.0 * np.pi * (t_idx * r8 + j_idx) * blk_of_row * r8 / (n * GRID_W)
    ang_t = ang_t.reshape(nblk, n * r8, 1)
    tw_cos = np.broadcast_to(np.cos(ang_t), (nblk, n * r8, LANES))
    tw_sin = np.broadcast_to(np.sin(ang_t), (nblk, n * r8, LANES))

    gd = f_w // N_FOURIER_GROUPS
    c = np.arange(gd)
    ang_c = 2.0 * np.pi * np.outer(c, c) / gd
    eye_g = np.eye(N_FOURIER_GROUPS)
    cc = np.kron(eye_g, np.cos(ang_c)) / np.sqrt(gd)
    sc = np.kron(eye_g, np.sin(ang_c)) / np.sqrt(gd)
    as_f32 = lambda a: jnp.asarray(np.ascontiguousarray(a, dtype=np.float32))
    to_bf16 = lambda a: as_f32(a).astype(BF16)
    return (to_bf16(fa), to_bf16(g), to_bf16(cc), to_bf16(sc), to_bf16(perm),
            as_f32(tw_cos), as_f32(tw_sin))


def _fourier_kernel(u_ref, fa_ref, g_ref, cc_ref, sc_ref, perm_ref, twc_ref, tws_ref, y_ref,
                    sre_ref, sim_ref, *, steps_per_stage, blocks_per_step):
    t = pl.program_id(1)
    n, _, f_w = u_ref.shape
    r8 = SUBLANES
    m = n * r8

    @pl.when(t < steps_per_stage)
    def _stage_a():
        for j in range(blocks_per_step):
            xin = u_ref[:, j * r8:(j + 1) * r8, :].reshape(m, f_w).astype(BF16)
            a = _dot(fa_ref[...], xin)
            reps = f_w // LANES
            cos_t = jnp.concatenate([twc_ref[j]] * reps, axis=1)
            sin_t = jnp.concatenate([tws_ref[j]] * reps, axis=1)
            a_re = a[0:m]
            a_im = a[m:2 * m]
            b_re = a_re * cos_t + a_im * sin_t
            b_im = a_im * cos_t - a_re * sin_t
            off = pl.multiple_of((t * blocks_per_step + j) * r8, r8)
            sre_ref[:, pl.ds(off, r8), :] = b_re.reshape(n, r8, f_w)
            sim_ref[:, pl.ds(off, r8), :] = b_im.reshape(n, r8, f_w)

    @pl.when((t >= steps_per_stage) & (t < 2 * steps_per_stage))
    def _stage_b():
        pair_rows = 2 * n
        for j in range(blocks_per_step):
            off = pl.multiple_of(((t - steps_per_stage) * blocks_per_step + j) * r8, r8)
            br = sre_ref[pl.ds(off, r8)].reshape(m, f_w)
            bi = sim_ref[pl.ds(off, r8)].reshape(m, f_w)
            p_re, p_im = [], []
            for pp in range(r8 // 2):
                rs_ = slice(pp * pair_rows, (pp + 1) * pair_rows)
                bcat = jnp.concatenate([br[rs_], bi[rs_]], axis=0).astype(BF16)
                p = _dot(g_ref[pp], bcat)
                p_re.append(p[0:pair_rows])
                p_im.append(p[pair_rows:2 * pair_rows])
            p_re = jnp.concatenate(p_re, axis=0).astype(BF16)
            p_im = jnp.concatenate(p_im, axis=0).astype(BF16)
            y = _dot(p_re, cc_ref[...]) + _dot(p_im, sc_ref[...])
            sre_ref[pl.ds(off, r8)] = y.reshape(r8, n, f_w)

    @pl.when(t >= 2 * steps_per_stage)
    def _stage_c():
        for j in range(blocks_per_step):
            off = pl.multiple_of(((t - 2 * steps_per_stage) * blocks_per_step + j) * r8, r8)
            yin = sre_ref[:, pl.ds(off, r8), :].reshape(m, f_w).astype(BF16)
            y_ref[j * m:(j + 1) * m, :] = _dot(perm_ref[...], yin).astype(y_ref.dtype)


def _fourier(u):
    bsz, s, f_w = u.shape
    rows = s // GRID_W
    assert rows == GRID_W, "the two-stage position DFT assumes a square token grid"
    r8 = SUBLANES
    nblk = rows // r8
    fa, g, cc, sc, perm, tw_cos, tw_sin = _fourier_constants(rows, f_w)
    u4 = u.reshape(bsz, rows, GRID_W, f_w)
    blocks_per_step = 8
    steps = nblk // blocks_per_step
    width = blocks_per_step * r8
    stage_a_step = lambda b, t: jnp.minimum(t, steps - 1)
    stage_c_step = lambda b, t: jnp.maximum(t - 2 * steps, 0)
    full = lambda a: pl.BlockSpec(a.shape, lambda b, t: (0,) * a.ndim)
    tw_spec = pl.BlockSpec((blocks_per_step,) + tw_cos.shape[1:],
                           lambda b, t: (stage_a_step(b, t), 0, 0))
    return pl.pallas_call(
        functools.partial(_fourier_kernel, steps_per_stage=steps, blocks_per_step=blocks_per_step),
        grid=(bsz, 3 * steps),
        in_specs=[
            pl.BlockSpec((None, rows, width, f_w), lambda b, t: (b, 0, stage_a_step(b, t), 0)),
            full(fa), full(g), full(cc), full(sc), full(perm), tw_spec, tw_spec,
        ],
        out_specs=pl.BlockSpec((None, width * GRID_W, f_w), lambda b, t: (b, stage_c_step(b, t), 0)),
        out_shape=jax.ShapeDtypeStruct((bsz, s, f_w), BF16),
        scratch_shapes=[pltpu.VMEM((rows, GRID_W, f_w), F32), pltpu.VMEM((rows, GRID_W, f_w), F32)],
        compiler_params=_params("arbitrary", "arbitrary"),
        name="fourier",
    )(u4, fa, g, cc, sc, perm, tw_cos, tw_sin)


def _bias_tables(rpb, rows):
    kw = min(WIN_W, GRID_W)
    cols = np.arange(GRID_W)
    col_start = np.clip(cols - kw // 2, 0, GRID_W - kw)
    kj = cols[None, :]
    allowed = (kj >= col_start[:, None]) & (kj < col_start[:, None] + kw)
    rel = kj - cols[:, None] + (WIN_W - 1)
    nrel = 2 * WIN_W - 1
    onehot = ((rel[None] == np.arange(nrel)[:, None, None]) & allowed[None]).astype(np.float32)
    sel = np.zeros((2 * nrel + 1, GRID_W, 2 * GRID_W), np.float32)
    sel[0:nrel, :, 0:GRID_W] = onehot
    sel[nrel:2 * nrel, :, GRID_W:] = onehot
    sel[2 * nrel] = np.tile(~allowed, (1, 2))
    rpb = rpb.astype(F32)
    mask_col = jnp.full(rpb.shape[:1] + (rpb.shape[1] - 1, 1), MASK_VALUE, F32)
    src = jnp.concatenate([rpb[:, :-1], rpb[:, 1:], mask_col], axis=-1)
    return jnp.einsum("hmd,dcx->hmcx", src, jnp.asarray(sel), precision=lax.Precision.HIGHEST)


def _attn_out_kernel(q_ref, k_ref, v_ref, t2_ref, yfn_ref, ga_ref, gb_ref, x_ref,
                     wna_ref, wfn_ref, bfn_ref, wo_ref, g1_ref, sc2_ref, sh2_ref, nw2_ref,
                     o_ref, h2_ref, yna_ref, s_ref, p_ref,
                     *, rows, kh, rows_per_step):
    t = pl.program_id(0)
    last_tile = pl.num_programs(0) - 2

    @pl.when(t == 0)
    def _init():
        yna_ref[...] = jnp.zeros_like(yna_ref)

    yna_prev = yna_ref[...].astype(BF16)
    yfn_prev = yfn_ref[...]

    rb = lax.rem(jnp.minimum(t, last_tile), rows // rows_per_step)
    n_pairs = q_ref.shape[1] // LANES
    lane = lax.broadcasted_iota(jnp.int32, (1, LANES), 1)
    low = lane < HEAD_DIM
    mask_lo = jnp.where(low, 1.0, 0.0).astype(BF16)
    mask_hi = jnp.where(low, 0.0, 1.0).astype(BF16)
    n_keys = kh * GRID_W

    key_start = []
    for j in range(rows_per_step):
        r = rb * rows_per_step + j
        rs = jnp.clip(r - kh // 2, 0, rows - kh)
        key_start.append((pl.multiple_of(rs * GRID_W, GRID_W), rs - r + (WIN_H - 1)))

    def qk_tile(ti):
        j, p = divmod(ti, n_pairs)
        ks, idx0 = key_start[j]
        cs = slice(LANES * p, LANES * (p + 1))
        q2 = q_ref[j * GRID_W:(j + 1) * GRID_W, cs]
        k2 = k_ref[pl.ds(ks, n_keys), cs]
        qs = jnp.concatenate([q2 * mask_lo, q2 * mask_hi], axis=0)
        s = lax.dot_general(qs, k2, (((1,), (1,)), ((), ())), preferred_element_type=F32)
        bias = jnp.concatenate(
            [jnp.concatenate([t2_ref[2 * p + hh, idx0 + 2 * m] for m in range(kh // 2)], axis=1)
             for hh in range(2)], axis=0)
        s_ref[ti] = s + bias

    def softmax_tile(ti):
        s = s_ref[ti]
        p_ref[ti] = jnp.exp(s - jnp.max(s, axis=-1, keepdims=True)).astype(BF16)

    ones = jnp.ones((n_keys, LANES), BF16)
    row_outs = []

    def pv_tile(ti):
        j, p = divmod(ti, n_pairs)
        ks, _ = key_start[j]
        v2 = v_ref[pl.ds(ks, n_keys), LANES * p:LANES * (p + 1)]
        o = _dot(p_ref[ti], jnp.concatenate([v2, ones], axis=1))
        o = o[:, 0:LANES] * (1.0 / o[:, LANES:2 * LANES])
        row_outs.append(jnp.where(low, o[0:GRID_W], o[GRID_W:2 * GRID_W]))
        if p == n_pairs - 1:
            yna_ref[j * GRID_W:(j + 1) * GRID_W, :] = jnp.concatenate(row_outs, axis=1)
            row_outs.clear()

    d = o_ref.shape[1]
    merged = []

    def merge_chunk(c0):
        cols = slice(c0, c0 + MXU_DIM)
        ya = _dot(yna_prev, wna_ref[:, cols])
        yf = _dot(yfn_prev, wfn_ref[:, cols]) + bfn_ref[:, cols]
        merged.append((ga_ref[:, cols].astype(F32) * ya
                       + gb_ref[:, cols].astype(F32) * yf).astype(BF16))

    def out_chunk(c0):
        cols = slice(c0, c0 + MXU_DIM)
        lhs = jnp.concatenate(merged, axis=1)
        o_ref[:, cols] = x_ref[:, cols] + g1_ref[:, cols] * _dot(lhs, wo_ref[:, cols])

    norm_rows = o_ref.shape[0] // rows_per_step

    def norm_chunk(j):
        nr = slice(j * norm_rows, (j + 1) * norm_rows)
        x1 = o_ref[nr, :]
        ms = jnp.mean(x1 * x1, axis=-1, keepdims=True)
        y2 = x1 * lax.rsqrt(ms + EPS) * nw2_ref[...]
        h2_ref[nr, :] = (y2 * (1.0 + sc2_ref[...]) + sh2_ref[...]).astype(BF16)

    matmul_units = ([functools.partial(merge_chunk, c0) for c0 in range(0, d, MXU_DIM)]
                    + [functools.partial(out_chunk, c0) for c0 in range(0, d, MXU_DIM)])
    norm_units = [functools.partial(norm_chunk, j) for j in range(rows_per_step)]

    n_tiles = rows_per_step * n_pairs
    for u in range(n_tiles + PV_SKEW):
        if u < n_tiles:
            qk_tile(u)
        if 0 <= u - SOFTMAX_SKEW < n_tiles:
            softmax_tile(u - SOFTMAX_SKEW)
        if 0 <= u - PV_SKEW < n_tiles:
            pv_tile(u - PV_SKEW)
        if matmul_units:
            if u % OUT_UNIT_EVERY == 1:
                matmul_units.pop(0)()
        elif norm_units and u % NORM_UNIT_EVERY == 0:
            norm_units.pop(0)()
    assert not matmul_units
    while norm_units:
        norm_units.pop(0)()


def _attn_out(q, k, v, t2, yfn, ga, gb, x, w_na, w_fn, b_fn, w_o, gate1, scale2, shift2, norm2_w):
    bsz, s, d = x.shape
    na_w = q.shape[-1]
    f_w = yfn.shape[-1]
    rows = s // GRID_W
    kh = min(WIN_H, rows)
    assert kh % 2 == 0 and (2 * HEAD_DIM) == LANES
    rows_per_step = 8
    tq = rows_per_step * GRID_W
    n_tiles = rows_per_step * (na_w // LANES)

    per_seq = s // tq
    n_steps = bsz * per_seq + 1
    attn_tile = lambda t: jnp.minimum(t, n_steps - 2)
    out_tile = lambda t: jnp.maximum(t - 1, 0)
    tok_a = lambda w: pl.BlockSpec(
        (None, tq, w), lambda t: (attn_tile(t) // per_seq, attn_tile(t) % per_seq, 0))
    tok_o = lambda w: pl.BlockSpec(
        (None, tq, w), lambda t: (out_tile(t) // per_seq, out_tile(t) % per_seq, 0))
    whole_seq = pl.BlockSpec((None, s, na_w), lambda t: (attn_tile(t) // per_seq, 0, 0))
    per_batch = pl.BlockSpec((None, 1, d), lambda t: (out_tile(t) // per_seq, 0, 0))
    full = lambda a: pl.BlockSpec(a.shape, lambda t: (0,) * a.ndim)
    bfn = b_fn.reshape(1, d)
    nw2 = norm2_w.reshape(1, d)
    return pl.pallas_call(
        functools.partial(_attn_out_kernel, rows=rows, kh=kh, rows_per_step=rows_per_step),
        grid=(n_steps,),
        in_specs=[tok_a(na_w), whole_seq, whole_seq, full(t2), tok_o(f_w), tok_o(d), tok_o(d),
                  tok_o(d), full(w_na), full(w_fn), full(bfn), full(w_o), per_batch, per_batch,
                  per_batch, full(nw2)],
        out_specs=(tok_o(d), tok_o(d)),
        out_shape=(jax.ShapeDtypeStruct((bsz, s, d), F32), jax.ShapeDtypeStruct((bsz, s, d), BF16)),
        scratch_shapes=[pltpu.VMEM((tq, na_w), F32),
                        pltpu.VMEM((n_tiles, 2 * GRID_W, kh * GRID_W), F32),
                        pltpu.VMEM((n_tiles, 2 * GRID_W, kh * GRID_W), BF16)],
        compiler_params=_params("arbitrary"),
        name="attn_out",
    )(q, k, v, t2, yfn, ga, gb, x, w_na, w_fn, bfn, w_o, gate1, scale2, shift2, nw2)


def _mlp_kernel(x_ref, h_ref, g2_ref, w1_ref, w2_ref, o_ref, *, chunk):
    h = h_ref[...]
    acc = None
    for c0 in range(0, w1_ref.shape[1], chunk):
        a = jnp.maximum(_dot(h, w1_ref[:, c0:c0 + chunk]), 0.0)
        term = _dot((a * a).astype(BF16), w2_ref[c0:c0 + chunk, :])
        acc = term if acc is None else acc + term
    o_ref[...] = x_ref[...] + g2_ref[...] * acc


def _mlp(x, h, gate2, w1, w2):
    bsz, s, d = x.shape
    tm = 1024
    tok = pl.BlockSpec((None, tm, d), lambda b, i: (b, i, 0))
    per_batch = pl.BlockSpec((None, 1, d), lambda b, i: (b, 0, 0))
    full = lambda a: pl.BlockSpec(a.shape, lambda b, i: (0,) * a.ndim)
    return pl.pallas_call(
        functools.partial(_mlp_kernel, chunk=1024),
        grid=(bsz, s // tm),
        in_specs=[tok, tok, per_batch, full(w1), full(w2)],
        out_specs=tok,
        out_shape=jax.ShapeDtypeStruct((bsz, s, d), F32),
        compiler_params=_params("parallel", "parallel"),
        name="mlp",
    )(x, h, gate2, w1, w2)


def kernel(x, c, norm1_w, norm2_w, w_ada, b_ada, w_in, b_in, q_norm_w, k_norm_w, rpb,
           w_na_out, w_fn_out, b_fn_out, w_o, w_mlp_in, w_mlp_out):
    bsz, s, d = x.shape
    depth = w_ada.shape[0]
    na_w = w_na_out.shape[1]
    f_w = w_fn_out.shape[1]
    n_heads = na_w // HEAD_DIM
    rows = s // GRID_W
    for l in range(depth):
        mod = _ada(c, w_ada[l], b_ada[l])
        shift1, scale1, gate1, shift2, scale2, gate2 = [
            mod[:, i * d:(i + 1) * d].reshape(bsz, 1, d) for i in range(N_MOD)]

        qk_w = jnp.concatenate([jnp.tile(q_norm_w[l], n_heads) * (HEAD_DIM ** -0.5),
                                jnp.tile(k_norm_w[l], n_heads)]).reshape(1, 2 * na_w)
        (q, k, v, u, ga, gb), (w_na, w_fn, w_o_b, w_m1, w_m2) = _inproj(
            x, scale1, shift1, norm1_w[l], w_in[l].astype(BF16), b_in[l], qk_w, na_w, f_w,
            (w_na_out[l], w_fn_out[l], w_o[l], w_mlp_in[l], w_mlp_out[l]))
        yfn = _fourier(u)
        t2 = _bias_tables(rpb[l], rows)
        x, h2 = _attn_out(q, k, v, t2, yfn, ga, gb, x, w_na, w_fn, b_fn_out[l], w_o_b, gate1,
                          scale2, shift2, norm2_w[l])
        x = _mlp(x, h2, gate2, w_m1, w_m2)
    return x
```

```python
import functools

import numpy as np
import jax
import jax.numpy as jnp
from jax import lax
from jax.experimental import pallas as pl
from jax.experimental.pallas import tpu as pltpu

F32 = jnp.float32
BF16 = jnp.bfloat16

GRID_W = 64
HEAD_DIM = 64
WIN_H = 8
WIN_W = 16
N_FOURIER_GROUPS = 4
N_MOD = 6
EPS = 1e-6
MASK_VALUE = -1e30

SUBLANES = 8
LANES = 128
MXU_DIM = 256
SOFTMAX_SKEW = 2
PV_SKEW = 4
OUT_UNIT_EVERY = 4
NORM_UNIT_EVERY = 1
VMEM_LIMIT_BYTES = 56 * 1024 * 1024

INPROJ_TOKENS = 1024
INPROJ_CHUNK = 512
FOURIER_BLOCKS_PER_STEP = 8
ATTN_ROWS_PER_STEP = 8
MLP_TOKENS = 1024
MLP_FF_CHUNK = 1024


def _dot(a, b):
    return jnp.dot(a, b, preferred_element_type=F32)


def _sigmoid(x):
    return 0.5 * jnp.tanh(0.5 * x) + 0.5


def _params(*semantics):
    return pltpu.CompilerParams(dimension_semantics=semantics, vmem_limit_bytes=VMEM_LIMIT_BYTES)


def _ada_kernel(ct_ref, w_ref, b_ref, o_ref):
    ct = ct_ref[...]
    st = ct * _sigmoid(ct)
    w = w_ref[...]
    rows = [jnp.sum(w * st[:, b:b + 1], axis=0, keepdims=True) for b in range(ct.shape[1])]
    o_ref[...] = jnp.concatenate(rows, axis=0) + b_ref[...]


def _ada(c, w, b):
    bsz, d = c.shape
    n = w.shape[1]
    tn = n // N_MOD
    return pl.pallas_call(
        _ada_kernel,
        grid=(n // tn,),
        in_specs=[pl.BlockSpec((d, bsz), lambda j: (0, 0)),
                  pl.BlockSpec((d, tn), lambda j: (0, j)),
                  pl.BlockSpec((1, tn), lambda j: (0, j))],
        out_specs=pl.BlockSpec((bsz, tn), lambda j: (0, j)),
        out_shape=jax.ShapeDtypeStruct((bsz, n), F32),
        compiler_params=_params("parallel"),
        name="ada",
    )(c.T, w, b.reshape(1, n))


def _inproj_kernel(x_ref, sc_ref, sh_ref, nw_ref, w_ref, b_ref, qkw_ref, gsum_ref, gexp_ref, *refs,
                   na_w, f_w, d, chunk, n_cast):
    cast_in = refs[:n_cast]
    q_ref, k_ref, v_ref, u_ref, ga_ref, gb_ref = refs[n_cast:n_cast + 6]
    cast_out = refs[n_cast + 6:]
    for src, dst in zip(cast_in, cast_out):
        dst[...] = src[...].astype(BF16)

    gain = nw_ref[...] * (1.0 + sc_ref[...])

    def normed(c0):
        x = x_ref[c0:c0 + chunk, :]
        ms = jnp.mean(x * x, axis=-1, keepdims=True)
        return (x * lax.rsqrt(ms + EPS) * gain + sh_ref[...]).astype(BF16)

    def proj(h, lo, width):
        return _dot(h, w_ref[:, lo:lo + width]) + b_ref[:, lo:lo + width]

    qk = 2 * na_w
    o_v, o_u, o_ga, o_gb = qk, qk + na_w, qk + na_w + f_w, qk + na_w + f_w + d
    n_rows = x_ref.shape[0]
    h = normed(0)
    for c0 in range(0, n_rows, chunk):
        rows = slice(c0, c0 + chunk)
        zqk = proj(h, 0, qk)
        h_next = normed(c0 + chunk) if c0 + chunk < n_rows else None
        v_ref[rows, :] = proj(h, o_v, na_w).astype(BF16)
        ssq = _dot((zqk * zqk).astype(BF16), gsum_ref[...])
        u_ref[rows, :] = proj(h, o_u, f_w)
        rinv = lax.rsqrt(ssq * (1.0 / HEAD_DIM) + EPS)
        rinv_hi = rinv.astype(BF16)
        rinv_lo = (rinv - rinv_hi.astype(F32)).astype(BF16)
        scale = _dot(jnp.concatenate([rinv_hi, rinv_lo], axis=1), gexp_ref[...])
        ga_ref[rows, :] = _sigmoid(proj(h, o_ga, d)).astype(BF16)
        zn = zqk * scale * qkw_ref[...]
        q_ref[rows, :] = zn[:, 0:na_w].astype(BF16)
        k_ref[rows, :] = zn[:, na_w:qk].astype(BF16)
        gb_ref[rows, :] = _sigmoid(proj(h, o_gb, d)).astype(BF16)
        h = h_next


def _inproj(x, scale1, shift1, norm_w, w_in, b_in, qk_w, na_w, f_w, later_weights):
    bsz, s, d = x.shape
    in_w = w_in.shape[1]
    tm = INPROJ_TOKENS
    qk = 2 * na_w
    n_grp = qk // HEAD_DIM
    grp = np.arange(qk) // HEAD_DIM
    gsum = jnp.asarray((grp[:, None] == np.arange(LANES)[None, :]).astype(np.float32), BF16)
    gexp1 = (np.arange(LANES)[:, None] == grp[None, :]).astype(np.float32)
    gexp = jnp.asarray(np.concatenate([gexp1, gexp1], axis=0), BF16)
    assert n_grp <= LANES

    tok = lambda w: pl.BlockSpec((None, tm, w), lambda b, i: (b, i, 0))
    per_batch = pl.BlockSpec((None, 1, d), lambda b, i: (b, 0, 0))
    full = lambda a: pl.BlockSpec(a.shape, lambda b, i: (0,) * a.ndim)
    nw = norm_w.reshape(1, d)
    bi = b_in.reshape(1, in_w)
    out_shapes = (
        jax.ShapeDtypeStruct((bsz, s, na_w), BF16),
        jax.ShapeDtypeStruct((bsz, s, na_w), BF16),
        jax.ShapeDtypeStruct((bsz, s, na_w), BF16),
        jax.ShapeDtypeStruct((bsz, s, f_w), F32),
        jax.ShapeDtypeStruct((bsz, s, d), BF16),
        jax.ShapeDtypeStruct((bsz, s, d), BF16),
    )
    n_steps = bsz * (s // tm)
    slab = lambda a: pl.BlockSpec((a.shape[0] // n_steps, a.shape[1]),
                                  lambda b, i: (b * (s // tm) + i, 0))
    for a in later_weights:
        assert a.shape[0] % (n_steps * 2 * SUBLANES) == 0, a.shape
    outs = pl.pallas_call(
        functools.partial(_inproj_kernel, na_w=na_w, f_w=f_w, d=d, chunk=INPROJ_CHUNK,
                          n_cast=len(later_weights)),
        grid=(bsz, s // tm),
        in_specs=[tok(d), per_batch, per_batch, full(nw), full(w_in), full(bi), full(qk_w),
                  full(gsum), full(gexp)] + [slab(a) for a in later_weights],
        out_specs=(tok(na_w), tok(na_w), tok(na_w), tok(f_w), tok(d), tok(d))
        + tuple(slab(a) for a in later_weights),
        out_shape=out_shapes + tuple(jax.ShapeDtypeStruct(a.shape, BF16) for a in later_weights),
        compiler_params=_params("parallel", "parallel"),
        name="inproj",
    )(x, scale1, shift1, nw, w_in, bi, qk_w, gsum, gexp, *later_weights)
    return outs[:6], outs[6:]


def _fourier_constants(rows, f_w):
    n = rows
    r8 = SUBLANES
    k = np.arange(n)
    ang_a = 2.0 * np.pi * np.outer(k, k) / n
    eye8 = np.eye(r8)
    norm = 1.0 / np.sqrt(n)
    fa_re = np.kron(np.cos(ang_a), eye8) * norm
    fa_im = np.kron(-np.sin(ang_a), eye8) * norm
    fa = np.concatenate([fa_re, fa_im], axis=0)

    nblk = n // r8
    s2 = k[None, :]
    s2p = k[:, None]
    g = np.zeros((r8 // 2, 4 * n, 4 * n))
    for pp in range(r8 // 2):
        gr = np.zeros((2, n, 2, n))
        gi = np.zeros((2, n, 2, n))
        for ll in range(2):
            l = 2 * pp + ll
            ang = 2.0 * np.pi * (s2 * l / (n * GRID_W) + s2 * s2p / n)
            gr[ll, :, ll, :] = np.cos(ang) * norm
            gi[ll, :, ll, :] = -np.sin(ang) * norm
        gr = gr.reshape(2 * n, 2 * n)
        gi = gi.reshape(2 * n, 2 * n)
        g[pp] = np.block([[gr, -gi], [gi, gr]])

    perm = np.zeros((n * r8, n * r8))
    s1p = np.repeat(k, r8)
    jj = np.tile(np.arange(r8), n)
    perm[jj * n + s1p, s1p * r8 + jj] = 1.0

    t_idx = np.arange(nblk)[:, None, None]
    blk_of_row = (k // r8)[None, :, None]
    j_idx = np.arange(r8)[None, None, :]
    ang_t = 2.0 * np.pi * (t_idx * r8 + j_idx) * blk_of_row * r8 / (n * GRID_W)
    ang_t = ang_t.reshape(nblk, n * r8, 1)
    tw_cos = np.broadcast_to(np.cos(ang_t), (nblk, n * r8, LANES))
    tw_sin = np.broadcast_to(np.sin(ang_t), (nblk, n * r8, LANES))

    gd = f_w // N_FOURIER_GROUPS
    c = np.arange(gd)
    ang_c = 2.0 * np.pi * np.outer(c, c) / gd
    eye_g = np.eye(N_FOURIER_GROUPS)
    cc = np.kron(eye_g, np.cos(ang_c)) / np.sqrt(gd)
    sc = np.kron(eye_g, np.sin(ang_c)) / np.sqrt(gd)
    as_f32 = lambda a: jnp.asarray(np.ascontiguousarray(a, dtype=np.float32))
    to_bf16 = lambda a: as_f32(a).astype(BF16)
    return (to_bf16(fa), to_bf16(g), to_bf16(cc), to_bf16(sc), to_bf16(perm),
            as_f32(tw_cos), as_f32(tw_sin))


def _fourier_kernel(u_ref, fa_ref, g_ref, cc_ref, sc_ref, perm_ref, twc_ref, tws_ref, y_ref,
                    sre_ref, sim_ref, *, steps_per_stage, blocks_per_step):
    t = pl.program_id(1)
    n, _, f_w = u_ref.shape
    r8 = SUBLANES
    m = n * r8

    @pl.when(t < steps_per_stage)
    def _stage_a():
        for j in range(blocks_per_step):
            xin = u_ref[:, j * r8:(j + 1) * r8, :].reshape(m, f_w).astype(BF16)
            a = _dot(fa_ref[...], xin)
            reps = f_w // LANES
            cos_t = jnp.concatenate([twc_ref[j]] * reps, axis=1)
            sin_t = jnp.concatenate([tws_ref[j]] * reps, axis=1)
            a_re = a[0:m]
            a_im = a[m:2 * m]
            b_re = a_re * cos_t + a_im * sin_t
            b_im = a_im * cos_t - a_re * sin_t
            off = pl.multiple_of((t * blocks_per_step + j) * r8, r8)
            sre_ref[:, pl.ds(off, r8), :] = b_re.reshape(n, r8, f_w)
            sim_ref[:, pl.ds(off, r8), :] = b_im.reshape(n, r8, f_w)

    @pl.when((t >= steps_per_stage) & (t < 2 * steps_per_stage))
    def _stage_b():
        pair_rows = 2 * n
        for j in range(blocks_per_step):
            off = pl.multiple_of(((t - steps_per_stage) * blocks_per_step + j) * r8, r8)
            br = sre_ref[pl.ds(off, r8)].reshape(m, f_w)
            bi = sim_ref[pl.ds(off, r8)].reshape(m, f_w)
            p_re, p_im = [], []
            for pp in range(r8 // 2):
                rs_ = slice(pp * pair_rows, (pp + 1) * pair_rows)
                bcat = jnp.concatenate([br[rs_], bi[rs_]], axis=0).astype(BF16)
                p = _dot(g_ref[pp], bcat)
                p_re.append(p[0:pair_rows])
                p_im.append(p[pair_rows:2 * pair_rows])
            p_re = jnp.concatenate(p_re, axis=0).astype(BF16)
            p_im = jnp.concatenate(p_im, axis=0).astype(BF16)
            y = _dot(p_re, cc_ref[...]) + _dot(p_im, sc_ref[...])
            sre_ref[pl.ds(off, r8)] = y.reshape(r8, n, f_w)

    @pl.when(t >= 2 * steps_per_stage)
    def _stage_c():
        for j in range(blocks_per_step):
            off = pl.multiple_of(((t - 2 * steps_per_stage) * blocks_per_step + j) * r8, r8)
            yin = sre_ref[:, pl.ds(off, r8), :].reshape(m, f_w).astype(BF16)
            y_ref[j * m:(j + 1) * m, :] = _dot(perm_ref[...], yin).astype(y_ref.dtype)


def _fourier(u):
    bsz, s, f_w = u.shape
    rows = s // GRID_W
    assert rows == GRID_W, "the two-stage position DFT assumes a square token grid"
    r8 = SUBLANES
    nblk = rows // r8
    fa, g, cc, sc, perm, tw_cos, tw_sin = _fourier_constants(rows, f_w)
    u4 = u.reshape(bsz, rows, GRID_W, f_w)
    blocks_per_step = FOURIER_BLOCKS_PER_STEP
    steps = nblk // blocks_per_step
    width = blocks_per_step * r8
    stage_a_step = lambda b, t: jnp.minimum(t, steps - 1)
    stage_c_step = lambda b, t: jnp.maximum(t - 2 * steps, 0)
    full = lambda a: pl.BlockSpec(a.shape, lambda b, t: (0,) * a.ndim)
    tw_spec = pl.BlockSpec((blocks_per_step,) + tw_cos.shape[1:],
                           lambda b, t: (stage_a_step(b, t), 0, 0))
    return pl.pallas_call(
        functools.partial(_fourier_kernel, steps_per_stage=steps, blocks_per_step=blocks_per_step),
        grid=(bsz, 3 * steps),
        in_specs=[
            pl.BlockSpec((None, rows, width, f_w), lambda b, t: (b, 0, stage_a_step(b, t), 0)),
            full(fa), full(g), full(cc), full(sc), full(perm), tw_spec, tw_spec,
        ],
        out_specs=pl.BlockSpec((None, width * GRID_W, f_w), lambda b, t: (b, stage_c_step(b, t), 0)),
        out_shape=jax.ShapeDtypeStruct((bsz, s, f_w), BF16),
        scratch_shapes=[pltpu.VMEM((rows, GRID_W, f_w), F32), pltpu.VMEM((rows, GRID_W, f_w), F32)],
        compiler_params=_params("arbitrary", "arbitrary"),
        name="fourier",
    )(u4, fa, g, cc, sc, perm, tw_cos, tw_sin)


def _bias_tables(rpb, rows):
    kw = min(WIN_W, GRID_W)
    cols = np.arange(GRID_W)
    col_start = np.clip(cols - kw // 2, 0, GRID_W - kw)
    kj = cols[None, :]
    allowed = (kj >= col_start[:, None]) & (kj < col_start[:, None] + kw)
    rel = kj - cols[:, None] + (WIN_W - 1)
    nrel = 2 * WIN_W - 1
    onehot = ((rel[None] == np.arange(nrel)[:, None, None]) & allowed[None]).astype(np.float32)
    sel = np.zeros((2 * nrel + 1, GRID_W, 2 * GRID_W), np.float32)
    sel[0:nrel, :, 0:GRID_W] = onehot
    sel[nrel:2 * nrel, :, GRID_W:] = onehot
    sel[2 * nrel] = np.tile(~allowed, (1, 2))
    rpb = rpb.astype(F32)
    mask_col = jnp.full(rpb.shape[:1] + (rpb.shape[1] - 1, 1), MASK_VALUE, F32)
    src = jnp.concatenate([rpb[:, :-1], rpb[:, 1:], mask_col], axis=-1)
    return jnp.einsum("hmd,dcx->hmcx", src, jnp.asarray(sel), precision=lax.Precision.HIGHEST)


def _attn_out_kernel(q_ref, k_ref, v_ref, t2_ref, yfn_ref, ga_ref, gb_ref, x_ref,
                     wna_ref, wfn_ref, bfn_ref, wo_ref, g1_ref, sc2_ref, sh2_ref, nw2_ref,
                     o_ref, h2_ref, yna_ref, s_ref, p_ref,
                     *, rows, kh, rows_per_step):
    t = pl.program_id(0)
    last_tile = pl.num_programs(0) - 2

    @pl.when(t == 0)
    def _init():
        yna_ref[...] = jnp.zeros_like(yna_ref)

    yna_prev = yna_ref[...].astype(BF16)
    yfn_prev = yfn_ref[...]

    rb = lax.rem(jnp.minimum(t, last_tile), rows // rows_per_step)
    n_pairs = q_ref.shape[1] // LANES
    lane = lax.broadcasted_iota(jnp.int32, (1, LANES), 1)
    low = lane < HEAD_DIM
    mask_lo = jnp.where(low, 1.0, 0.0).astype(BF16)
    mask_hi = jnp.where(low, 0.0, 1.0).astype(BF16)
    n_keys = kh * GRID_W

    key_start = []
    for j in range(rows_per_step):
        r = rb * rows_per_step + j
        rs = jnp.clip(r - kh // 2, 0, rows - kh)
        key_start.append((pl.multiple_of(rs * GRID_W, GRID_W), rs - r + (WIN_H - 1)))

    def qk_tile(ti):
        j, p = divmod(ti, n_pairs)
        ks, idx0 = key_start[j]
        cs = slice(LANES * p, LANES * (p + 1))
        q2 = q_ref[j * GRID_W:(j + 1) * GRID_W, cs]
        k2 = k_ref[pl.ds(ks, n_keys), cs]
        qs = jnp.concatenate([q2 * mask_lo, q2 * mask_hi], axis=0)
        s = lax.dot_general(qs, k2, (((1,), (1,)), ((), ())), preferred_element_type=F32)
        bias = jnp.concatenate(
            [jnp.concatenate([t2_ref[2 * p + hh, idx0 + 2 * m] for m in range(kh // 2)], axis=1)
             for hh in range(2)], axis=0)
        s_ref[ti] = s + bias

    def softmax_tile(ti):
        s = s_ref[ti]
        p_ref[ti] = jnp.exp(s - jnp.max(s, axis=-1, keepdims=True)).astype(BF16)

    ones = jnp.ones((n_keys, LANES), BF16)
    row_outs = []

    def pv_tile(ti):
        j, p = divmod(ti, n_pairs)
        ks, _ = key_start[j]
        v2 = v_ref[pl.ds(ks, n_keys), LANES * p:LANES * (p + 1)]
        o = _dot(p_ref[ti], jnp.concatenate([v2, ones], axis=1))
        o = o[:, 0:LANES] * (1.0 / o[:, LANES:2 * LANES])
        row_outs.append(jnp.where(low, o[0:GRID_W], o[GRID_W:2 * GRID_W]))
        if p == n_pairs - 1:
            yna_ref[j * GRID_W:(j + 1) * GRID_W, :] = jnp.concatenate(row_outs, axis=1)
            row_outs.clear()

    d = o_ref.shape[1]
    merged = []

    def merge_chunk(c0):
        cols = slice(c0, c0 + MXU_DIM)
        ya = _dot(yna_prev, wna_ref[:, cols])
        yf = _dot(yfn_prev, wfn_ref[:, cols]) + bfn_ref[:, cols]
        merged.append((ga_ref[:, cols].astype(F32) * ya
                       + gb_ref[:, cols].astype(F32) * yf).astype(BF16))

    def out_chunk(c0):
        cols = slice(c0, c0 + MXU_DIM)
        lhs = jnp.concatenate(merged, axis=1)
        o_ref[:, cols] = x_ref[:, cols] + g1_ref[:, cols] * _dot(lhs, wo_ref[:, cols])

    norm_rows = o_ref.shape[0] // rows_per_step
    gain2 = nw2_ref[...] * (1.0 + sc2_ref[...])

    def norm_chunk(j):
        nr = slice(j * norm_rows, (j + 1) * norm_rows)
        x1 = o_ref[nr, :]
        ms = jnp.mean(x1 * x1, axis=-1, keepdims=True)
        h2_ref[nr, :] = (x1 * lax.rsqrt(ms + EPS) * gain2 + sh2_ref[...]).astype(BF16)

    matmul_units = ([functools.partial(merge_chunk, c0) for c0 in range(0, d, MXU_DIM)]
                    + [functools.partial(out_chunk, c0) for c0 in range(0, d, MXU_DIM)])
    norm_units = [functools.partial(norm_chunk, j) for j in range(rows_per_step)]

    n_tiles = rows_per_step * n_pairs
    for u in range(n_tiles + PV_SKEW):
        if u < n_tiles:
            qk_tile(u)
        if 0 <= u - SOFTMAX_SKEW < n_tiles:
            softmax_tile(u - SOFTMAX_SKEW)
        if 0 <= u - PV_SKEW < n_tiles:
            pv_tile(u - PV_SKEW)
        if matmul_units:
            if u % OUT_UNIT_EVERY == 1:
                matmul_units.pop(0)()
        elif norm_units and u % NORM_UNIT_EVERY == 0:
            norm_units.pop(0)()
    assert not matmul_units
    while norm_units:
        norm_units.pop(0)()


def _attn_out(q, k, v, t2, yfn, ga, gb, x, w_na, w_fn, b_fn, w_o, gate1, scale2, shift2, norm2_w):
    bsz, s, d = x.shape
    na_w = q.shape[-1]
    f_w = yfn.shape[-1]
    rows = s // GRID_W
    kh = min(WIN_H, rows)
    assert kh % 2 == 0 and (2 * HEAD_DIM) == LANES
    rows_per_step = ATTN_ROWS_PER_STEP
    tq = rows_per_step * GRID_W
    n_tiles = rows_per_step * (na_w // LANES)

    per_seq = s // tq
    n_steps = bsz * per_seq + 1
    attn_tile = lambda t: jnp.minimum(t, n_steps - 2)
    out_tile = lambda t: jnp.maximum(t - 1, 0)
    tok_a = lambda w: pl.BlockSpec(
        (None, tq, w), lambda t: (attn_tile(t) // per_seq, attn_tile(t) % per_seq, 0))
    tok_o = lambda w: pl.BlockSpec(
        (None, tq, w), lambda t: (out_tile(t) // per_seq, out_tile(t) % per_seq, 0))
    whole_seq = pl.BlockSpec((None, s, na_w), lambda t: (attn_tile(t) // per_seq, 0, 0))
    per_batch = pl.BlockSpec((None, 1, d), lambda t: (out_tile(t) // per_seq, 0, 0))
    full = lambda a: pl.BlockSpec(a.shape, lambda t: (0,) * a.ndim)
    bfn = b_fn.reshape(1, d)
    nw2 = norm2_w.reshape(1, d)
    return pl.pallas_call(
        functools.partial(_attn_out_kernel, rows=rows, kh=kh, rows_per_step=rows_per_step),
        grid=(n_steps,),
        in_specs=[tok_a(na_w), whole_seq, whole_seq, full(t2), tok_o(f_w), tok_o(d), tok_o(d),
                  tok_o(d), full(w_na), full(w_fn), full(bfn), full(w_o), per_batch, per_batch,
                  per_batch, full(nw2)],
        out_specs=(tok_o(d), tok_o(d)),
        out_shape=(jax.ShapeDtypeStruct((bsz, s, d), F32), jax.ShapeDtypeStruct((bsz, s, d), BF16)),
        scratch_shapes=[pltpu.VMEM((tq, na_w), F32),
                        pltpu.VMEM((n_tiles, 2 * GRID_W, kh * GRID_W), F32),
                        pltpu.VMEM((n_tiles, 2 * GRID_W, kh * GRID_W), BF16)],
        compiler_params=_params("arbitrary"),
        name="attn_out",
    )(q, k, v, t2, yfn, ga, gb, x, w_na, w_fn, bfn, w_o, gate1, scale2, shift2, nw2)


def _mlp_kernel(x_ref, h_ref, g2_ref, w1_ref, w2_ref, o_ref, *, chunk):
    h = h_ref[...]
    acc = None
    for c0 in range(0, w1_ref.shape[1], chunk):
        a = jnp.maximum(_dot(h, w1_ref[:, c0:c0 + chunk]), 0.0)
        term = _dot((a * a).astype(BF16), w2_ref[c0:c0 + chunk, :])
        acc = term if acc is None else acc + term
    o_ref[...] = x_ref[...] + g2_ref[...] * acc


def _mlp(x, h, gate2, w1, w2):
    bsz, s, d = x.shape
    tm = MLP_TOKENS
    tok = pl.BlockSpec((None, tm, d), lambda b, i: (b, i, 0))
    per_batch = pl.BlockSpec((None, 1, d), lambda b, i: (b, 0, 0))
    full = lambda a: pl.BlockSpec(a.shape, lambda b, i: (0,) * a.ndim)
    return pl.pallas_call(
        functools.partial(_mlp_kernel, chunk=MLP_FF_CHUNK),
        grid=(bsz, s // tm),
        in_specs=[tok, tok, per_batch, full(w1), full(w2)],
        out_specs=tok,
        out_shape=jax.ShapeDtypeStruct((bsz, s, d), F32),
        compiler_params=_params("parallel", "parallel"),
        name="mlp",
    )(x, h, gate2, w1, w2)


def kernel(x, c, norm1_w, norm2_w, w_ada, b_ada, w_in, b_in, q_norm_w, k_norm_w, rpb,
           w_na_out, w_fn_out, b_fn_out, w_o, w_mlp_in, w_mlp_out):
    bsz, s, d = x.shape
    depth = w_ada.shape[0]
    na_w = w_na_out.shape[1]
    f_w = w_fn_out.shape[1]
    n_heads = na_w // HEAD_DIM
    rows = s // GRID_W
    for l in range(depth):
        mod = _ada(c, w_ada[l], b_ada[l])
        shift1, scale1, gate1, shift2, scale2, gate2 = [
            mod[:, i * d:(i + 1) * d].reshape(bsz, 1, d) for i in range(N_MOD)]

        qk_w = jnp.concatenate([jnp.tile(q_norm_w[l], n_heads) * (HEAD_DIM ** -0.5),
                                jnp.tile(k_norm_w[l], n_heads)]).reshape(1, 2 * na_w)
        (q, k, v, u, ga, gb), (w_na, w_fn, w_o_b, w_m1, w_m2) = _inproj(
            x, scale1, shift1, norm1_w[l], w_in[l].astype(BF16), b_in[l], qk_w, na_w, f_w,
            (w_na_out[l], w_fn_out[l], w_o[l], w_mlp_in[l], w_mlp_out[l]))
        yfn = _fourier(u)
        t2 = _bias_tables(rpb[l], rows)
        x, h2 = _attn_out(q, k, v, t2, yfn, ga, gb, x, w_na, w_fn, b_fn_out[l], w_o_b, gate1,
                          scale2, shift2, norm2_w[l])
        x = _mlp(x, h2, gate2, w_m1, w_m2)
    return x
```

```python
import functools

import numpy as np
import jax
import jax.numpy as jnp
from jax import lax
from jax.experimental import pallas as pl
from jax.experimental.pallas import tpu as pltpu

F32 = jnp.float32
BF16 = jnp.bfloat16

GRID_W = 64
HEAD_DIM = 64
WIN_H = 8
WIN_W = 16
N_FOURIER_GROUPS = 4
N_MOD = 6
EPS = 1e-6
MASK_VALUE = -1e30

SUBLANES = 8
LANES = 128
MXU_DIM = 256
SOFTMAX_SKEW = 2
PV_SKEW = 4
OUT_UNIT_EVERY = 4
NORM_UNIT_EVERY = 1
VMEM_LIMIT_BYTES = 56 * 1024 * 1024

INPROJ_TOKENS = 1024
INPROJ_CHUNK = 1024
FOURIER_BLOCKS_PER_STEP = 8
ATTN_ROWS_PER_STEP = 8
MLP_TOKENS = 1024
MLP_FF_CHUNK = 1024


def _dot(a, b):
    return jnp.dot(a, b, preferred_element_type=F32)


def _sigmoid(x):
    return 0.5 * jnp.tanh(0.5 * x) + 0.5


def _params(*semantics):
    return pltpu.CompilerParams(dimension_semantics=semantics, vmem_limit_bytes=VMEM_LIMIT_BYTES)


def _ada_kernel(ct_ref, w_ref, b_ref, o_ref):
    ct = ct_ref[...]
    st = ct * _sigmoid(ct)
    w = w_ref[...]
    rows = [jnp.sum(w * st[:, b:b + 1], axis=0, keepdims=True) for b in range(ct.shape[1])]
    o_ref[...] = jnp.concatenate(rows, axis=0) + b_ref[...]


def _ada(c, w, b):
    bsz, d = c.shape
    n = w.shape[1]
    tn = n // N_MOD
    return pl.pallas_call(
        _ada_kernel,
        grid=(n // tn,),
        in_specs=[pl.BlockSpec((d, bsz), lambda j: (0, 0)),
                  pl.BlockSpec((d, tn), lambda j: (0, j)),
                  pl.BlockSpec((1, tn), lambda j: (0, j))],
        out_specs=pl.BlockSpec((bsz, tn), lambda j: (0, j)),
        out_shape=jax.ShapeDtypeStruct((bsz, n), F32),
        compiler_params=_params("parallel"),
        name="ada",
    )(c.T, w, b.reshape(1, n))


def _inproj_kernel(x_ref, sc_ref, sh_ref, nw_ref, w_ref, b_ref, qkw_ref, gsum_ref, gexp_ref, *refs,
                   na_w, f_w, d, chunk, n_cast):
    cast_in = refs[:n_cast]
    q_ref, k_ref, v_ref, u_ref, ga_ref, gb_ref = refs[n_cast:n_cast + 6]
    cast_out = refs[n_cast + 6:]
    for src, dst in zip(cast_in, cast_out):
        dst[...] = src[...].astype(BF16)

    gain = nw_ref[...] * (1.0 + sc_ref[...])

    def normed(c0):
        x = x_ref[c0:c0 + chunk, :]
        ms = jnp.mean(x * x, axis=-1, keepdims=True)
        return (x * lax.rsqrt(ms + EPS) * gain + sh_ref[...]).astype(BF16)

    def proj(h, lo, width):
        return _dot(h, w_ref[:, lo:lo + width]) + b_ref[:, lo:lo + width]

    qk = 2 * na_w
    o_v, o_u, o_ga, o_gb = qk, qk + na_w, qk + na_w + f_w, qk + na_w + f_w + d
    n_rows = x_ref.shape[0]
    h = normed(0)
    for c0 in range(0, n_rows, chunk):
        rows = slice(c0, c0 + chunk)
        zqk = proj(h, 0, qk)
        h_next = normed(c0 + chunk) if c0 + chunk < n_rows else None
        v_ref[rows, :] = proj(h, o_v, na_w).astype(BF16)
        ssq = _dot((zqk * zqk).astype(BF16), gsum_ref[...])
        u_ref[rows, :] = proj(h, o_u, f_w)
        rinv = lax.rsqrt(ssq * (1.0 / HEAD_DIM) + EPS)
        rinv_hi = rinv.astype(BF16)
        rinv_lo = (rinv - rinv_hi.astype(F32)).astype(BF16)
        scale = _dot(jnp.concatenate([rinv_hi, rinv_lo], axis=1), gexp_ref[...])
        ga_ref[rows, :] = _sigmoid(proj(h, o_ga, d)).astype(BF16)
        zn = zqk * scale * qkw_ref[...]
        q_ref[rows, :] = zn[:, 0:na_w].astype(BF16)
        k_ref[rows, :] = zn[:, na_w:qk].astype(BF16)
        gb_ref[rows, :] = _sigmoid(proj(h, o_gb, d)).astype(BF16)
        h = h_next


def _inproj(x, scale1, shift1, norm_w, w_in, b_in, qk_w, na_w, f_w, later_weights):
    bsz, s, d = x.shape
    in_w = w_in.shape[1]
    tm = INPROJ_TOKENS
    qk = 2 * na_w
    n_grp = qk // HEAD_DIM
    grp = np.arange(qk) // HEAD_DIM
    gsum = jnp.asarray((grp[:, None] == np.arange(LANES)[None, :]).astype(np.float32), BF16)
    gexp1 = (np.arange(LANES)[:, None] == grp[None, :]).astype(np.float32)
    gexp = jnp.asarray(np.concatenate([gexp1, gexp1], axis=0), BF16)
    assert n_grp <= LANES

    tok = lambda w: pl.BlockSpec((None, tm, w), lambda b, i: (b, i, 0))
    per_batch = pl.BlockSpec((None, 1, d), lambda b, i: (b, 0, 0))
    full = lambda a: pl.BlockSpec(a.shape, lambda b, i: (0,) * a.ndim)
    nw = norm_w.reshape(1, d)
    bi = b_in.reshape(1, in_w)
    out_shapes = (
        jax.ShapeDtypeStruct((bsz, s, na_w), BF16),
        jax.ShapeDtypeStruct((bsz, s, na_w), BF16),
        jax.ShapeDtypeStruct((bsz, s, na_w), BF16),
        jax.ShapeDtypeStruct((bsz, s, f_w), F32),
        jax.ShapeDtypeStruct((bsz, s, d), BF16),
        jax.ShapeDtypeStruct((bsz, s, d), BF16),
    )
    n_steps = bsz * (s // tm)
    slab = lambda a: pl.BlockSpec((a.shape[0] // n_steps, a.shape[1]),
                                  lambda b, i: (b * (s // tm) + i, 0))
    for a in later_weights:
        assert a.shape[0] % (n_steps * 2 * SUBLANES) == 0, a.shape
    outs = pl.pallas_call(
        functools.partial(_inproj_kernel, na_w=na_w, f_w=f_w, d=d, chunk=INPROJ_CHUNK,
                          n_cast=len(later_weights)),
        grid=(bsz, s // tm),
        in_specs=[tok(d), per_batch, per_batch, full(nw), full(w_in), full(bi), full(qk_w),
                  full(gsum), full(gexp)] + [slab(a) for a in later_weights],
        out_specs=(tok(na_w), tok(na_w), tok(na_w), tok(f_w), tok(d), tok(d))
        + tuple(slab(a) for a in later_weights),
        out_shape=out_shapes + tuple(jax.ShapeDtypeStruct(a.shape, BF16) for a in later_weights),
        compiler_params=_params("parallel", "parallel"),
        name="inproj",
    )(x, scale1, shift1, nw, w_in, bi, qk_w, gsum, gexp, *later_weights)
    return outs[:6], outs[6:]


def _fourier_constants(rows, f_w):
    n = rows
    r8 = SUBLANES
    k = np.arange(n)
    ang_a = 2.0 * np.pi * np.outer(k, k) / n
    eye8 = np.eye(r8)
    norm = 1.0 / np.sqrt(n)
    fa_re = np.kron(np.cos(ang_a), eye8) * norm
    fa_im = np.kron(-np.sin(ang_a), eye8) * norm
    fa = np.concatenate([fa_re, fa_im], axis=0)

    nblk = n // r8
    s2 = k[None, :]
    s2p = k[:, None]
    g = np.zeros((r8 // 2, 4 * n, 4 * n))
    for pp in range(r8 // 2):
        gr = np.zeros((2, n, 2, n))
        gi = np.zeros((2, n, 2, n))
        for ll in range(2):
            l = 2 * pp + ll
            ang = 2.0 * np.pi * (s2 * l / (n * GRID_W) + s2 * s2p / n)
            gr[ll, :, ll, :] = np.cos(ang) * norm
            gi[ll, :, ll, :] = -np.sin(ang) * norm
        gr = gr.reshape(2 * n, 2 * n)
        gi = gi.reshape(2 * n, 2 * n)
        g[pp] = np.block([[gr, -gi], [gi, gr]])

    perm = np.zeros((n * r8, n * r8))
    s1p = np.repeat(k, r8)
    jj = np.tile(np.arange(r8), n)
    perm[jj * n + s1p, s1p * r8 + jj] = 1.0

    t_idx = np.arange(nblk)[:, None, None]
    blk_of_row = (k // r8)[None, :, None]
    j_idx = np.arange(r8)[None, None, :]
    ang_t = 2.0 * np.pi * (t_idx * r8 + j_idx) * blk_of_row * r8 / (n * GRID_W)
    ang_t = ang_t.reshape(nblk, n * r8, 1)
    tw_cos = np.broadcast_to(np.cos(ang_t), (nblk, n * r8, LANES))
    tw_sin = np.broadcast_to(np.sin(ang_t), (nblk, n * r8, LANES))

    gd = f_w // N_FOURIER_GROUPS
    c = np.arange(gd)
    ang_c = 2.0 * np.pi * np.outer(c, c) / gd
    eye_g = np.eye(N_FOURIER_GROUPS)
    cc = np.kron(eye_g, np.cos(ang_c)) / np.sqrt(gd)
    sc = np.kron(eye_g, np.sin(ang_c)) / np.sqrt(gd)
    as_f32 = lambda a: jnp.asarray(np.ascontiguousarray(a, dtype=np.float32))
    to_bf16 = lambda a: as_f32(a).astype(BF16)
    return (to_bf16(fa), to_bf16(g), to_bf16(cc), to_bf16(sc), to_bf16(perm),
            as_f32(tw_cos), as_f32(tw_sin))


def _fourier_kernel(u_ref, fa_ref, g_ref, cc_ref, sc_ref, perm_ref, twc_ref, tws_ref, y_ref,
                    sre_ref, sim_ref, *, steps_per_stage, blocks_per_step):
    t = pl.program_id(1)
    n, _, f_w = u_ref.shape
    r8 = SUBLANES
    m = n * r8

    @pl.when(t < steps_per_stage)
    def _stage_a():
        for j in range(blocks_per_step):
            xin = u_ref[:, j * r8:(j + 1) * r8, :].reshape(m, f_w).astype(BF16)
            a = _dot(fa_ref[...], xin)
            reps = f_w // LANES
            cos_t = jnp.concatenate([twc_ref[j]] * reps, axis=1)
            sin_t = jnp.concatenate([tws_ref[j]] * reps, axis=1)
            a_re = a[0:m]
            a_im = a[m:2 * m]
            b_re = a_re * cos_t + a_im * sin_t
            b_im = a_im * cos_t - a_re * sin_t
            off = pl.multiple_of((t * blocks_per_step + j) * r8, r8)
            sre_ref[:, pl.ds(off, r8), :] = b_re.reshape(n, r8, f_w)
            sim_ref[:, pl.ds(off, r8), :] = b_im.reshape(n, r8, f_w)

    @pl.when((t >= steps_per_stage) & (t < 2 * steps_per_stage))
    def _stage_b():
        pair_rows = 2 * n
        for j in range(blocks_per_step):
            off = pl.multiple_of(((t - steps_per_stage) * blocks_per_step + j) * r8, r8)
            br = sre_ref[pl.ds(off, r8)].reshape(m, f_w)
            bi = sim_ref[pl.ds(off, r8)].reshape(m, f_w)
            p_re, p_im = [], []
            for pp in range(r8 // 2):
                rs_ = slice(pp * pair_rows, (pp + 1) * pair_rows)
                bcat = jnp.concatenate([br[rs_], bi[rs_]], axis=0).astype(BF16)
                p = _dot(g_ref[pp], bcat)
                p_re.append(p[0:pair_rows])
                p_im.append(p[pair_rows:2 * pair_rows])
            p_re = jnp.concatenate(p_re, axis=0).astype(BF16)
            p_im = jnp.concatenate(p_im, axis=0).astype(BF16)
            y = _dot(p_re, cc_ref[...]) + _dot(p_im, sc_ref[...])
            sre_ref[pl.ds(off, r8)] = y.reshape(r8, n, f_w)

    @pl.when(t >= 2 * steps_per_stage)
    def _stage_c():
        for j in range(blocks_per_step):
            off = pl.multiple_of(((t - 2 * steps_per_stage) * blocks_per_step + j) * r8, r8)
            yin = sre_ref[:, pl.ds(off, r8), :].reshape(m, f_w).astype(BF16)
            y_ref[j * m:(j + 1) * m, :] = _dot(perm_ref[...], yin).astype(y_ref.dtype)


def _fourier(u):
    bsz, s, f_w = u.shape
    rows = s // GRID_W
    assert rows == GRID_W, "the two-stage position DFT assumes a square token grid"
    r8 = SUBLANES
    nblk = rows // r8
    fa, g, cc, sc, perm, tw_cos, tw_sin = _fourier_constants(rows, f_w)
    u4 = u.reshape(bsz, rows, GRID_W, f_w)
    blocks_per_step = FOURIER_BLOCKS_PER_STEP
    steps = nblk // blocks_per_step
    width = blocks_per_step * r8
    stage_a_step = lambda b, t: jnp.minimum(t, steps - 1)
    stage_c_step = lambda b, t: jnp.maximum(t - 2 * steps, 0)
    full = lambda a: pl.BlockSpec(a.shape, lambda b, t: (0,) * a.ndim)
    tw_spec = pl.BlockSpec((blocks_per_step,) + tw_cos.shape[1:],
                           lambda b, t: (stage_a_step(b, t), 0, 0))
    return pl.pallas_call(
        functools.partial(_fourier_kernel, steps_per_stage=steps, blocks_per_step=blocks_per_step),
        grid=(bsz, 3 * steps),
        in_specs=[
            pl.BlockSpec((None, rows, width, f_w), lambda b, t: (b, 0, stage_a_step(b, t), 0)),
            full(fa), full(g), full(cc), full(sc), full(perm), tw_spec, tw_spec,
        ],
        out_specs=pl.BlockSpec((None, width * GRID_W, f_w), lambda b, t: (b, stage_c_step(b, t), 0)),
        out_shape=jax.ShapeDtypeStruct((bsz, s, f_w), BF16),
        scratch_shapes=[pltpu.VMEM((rows, GRID_W, f_w), F32), pltpu.VMEM((rows, GRID_W, f_w), F32)],
        compiler_params=_params("arbitrary", "arbitrary"),
        name="fourier",
    )(u4, fa, g, cc, sc, perm, tw_cos, tw_sin)


def _bias_tables(rpb, rows):
    kw = min(WIN_W, GRID_W)
    cols = np.arange(GRID_W)
    col_start = np.clip(cols - kw // 2, 0, GRID_W - kw)
    kj = cols[None, :]
    allowed = (kj >= col_start[:, None]) & (kj < col_start[:, None] + kw)
    rel = kj - cols[:, None] + (WIN_W - 1)
    nrel = 2 * WIN_W - 1
    onehot = ((rel[None] == np.arange(nrel)[:, None, None]) & allowed[None]).astype(np.float32)
    sel = np.zeros((2 * nrel + 1, GRID_W, 2 * GRID_W), np.float32)
    sel[0:nrel, :, 0:GRID_W] = onehot
    sel[nrel:2 * nrel, :, GRID_W:] = onehot
    sel[2 * nrel] = np.tile(~allowed, (1, 2))
    rpb = rpb.astype(F32)
    mask_col = jnp.full(rpb.shape[:1] + (rpb.shape[1] - 1, 1), MASK_VALUE, F32)
    src = jnp.concatenate([rpb[:, :-1], rpb[:, 1:], mask_col], axis=-1)
    return jnp.einsum("hmd,dcx->hmcx", src, jnp.asarray(sel), precision=lax.Precision.HIGHEST)


def _attn_out_kernel(q_ref, k_ref, v_ref, t2_ref, yfn_ref, ga_ref, gb_ref, x_ref,
                     wna_ref, wfn_ref, bfn_ref, wo_ref, g1_ref, sc2_ref, sh2_ref, nw2_ref,
                     o_ref, h2_ref, yna_ref, s_ref, p_ref,
                     *, rows, kh, rows_per_step):
    t = pl.program_id(0)
    last_tile = pl.num_programs(0) - 2

    @pl.when(t == 0)
    def _init():
        yna_ref[...] = jnp.zeros_like(yna_ref)

    yna_prev = yna_ref[...].astype(BF16)
    yfn_prev = yfn_ref[...]

    rb = lax.rem(jnp.minimum(t, last_tile), rows // rows_per_step)
    n_pairs = q_ref.shape[1] // LANES
    lane = lax.broadcasted_iota(jnp.int32, (1, LANES), 1)
    low = lane < HEAD_DIM
    mask_lo = jnp.where(low, 1.0, 0.0).astype(BF16)
    mask_hi = jnp.where(low, 0.0, 1.0).astype(BF16)
    n_keys = kh * GRID_W

    key_start = []
    for j in range(rows_per_step):
        r = rb * rows_per_step + j
        rs = jnp.clip(r - kh // 2, 0, rows - kh)
        key_start.append((pl.multiple_of(rs * GRID_W, GRID_W), rs - r + (WIN_H - 1)))

    def qk_tile(ti):
        j, p = divmod(ti, n_pairs)
        ks, idx0 = key_start[j]
        cs = slice(LANES * p, LANES * (p + 1))
        q2 = q_ref[j * GRID_W:(j + 1) * GRID_W, cs]
        k2 = k_ref[pl.ds(ks, n_keys), cs]
        qs = jnp.concatenate([q2 * mask_lo, q2 * mask_hi], axis=0)
        s = lax.dot_general(qs, k2, (((1,), (1,)), ((), ())), preferred_element_type=F32)
        bias = jnp.concatenate(
            [jnp.concatenate([t2_ref[2 * p + hh, idx0 + 2 * m] for m in range(kh // 2)], axis=1)
             for hh in range(2)], axis=0)
        s_ref[ti] = s + bias

    def softmax_tile(ti):
        s = s_ref[ti]
        p_ref[ti] = jnp.exp(s - jnp.max(s, axis=-1, keepdims=True)).astype(BF16)

    ones = jnp.ones((n_keys, LANES), BF16)
    row_outs = []

    def pv_tile(ti):
        j, p = divmod(ti, n_pairs)
        ks, _ = key_start[j]
        v2 = v_ref[pl.ds(ks, n_keys), LANES * p:LANES * (p + 1)]
        o = _dot(p_ref[ti], jnp.concatenate([v2, ones], axis=1))
        o = o[:, 0:LANES] * (1.0 / o[:, LANES:2 * LANES])
        row_outs.append(jnp.where(low, o[0:GRID_W], o[GRID_W:2 * GRID_W]))
        if p == n_pairs - 1:
            yna_ref[j * GRID_W:(j + 1) * GRID_W, :] = jnp.concatenate(row_outs, axis=1)
            row_outs.clear()

    d = o_ref.shape[1]
    merged = []

    def merge_chunk(c0):
        cols = slice(c0, c0 + MXU_DIM)
        ya = _dot(yna_prev, wna_ref[:, cols])
        yf = _dot(yfn_prev, wfn_ref[:, cols]) + bfn_ref[:, cols]
        merged.append((ga_ref[:, cols].astype(F32) * ya
                       + gb_ref[:, cols].astype(F32) * yf).astype(BF16))

    def out_chunk(c0):
        cols = slice(c0, c0 + MXU_DIM)
        lhs = jnp.concatenate(merged, axis=1)
        o_ref[:, cols] = x_ref[:, cols] + g1_ref[:, cols] * _dot(lhs, wo_ref[:, cols])

    norm_rows = o_ref.shape[0] // rows_per_step
    gain2 = nw2_ref[...] * (1.0 + sc2_ref[...])

    def norm_chunk(j):
        nr = slice(j * norm_rows, (j + 1) * norm_rows)
        x1 = o_ref[nr, :]
        ms = jnp.mean(x1 * x1, axis=-1, keepdims=True)
        h2_ref[nr, :] = (x1 * lax.rsqrt(ms + EPS) * gain2 + sh2_ref[...]).astype(BF16)

    matmul_units = ([functools.partial(merge_chunk, c0) for c0 in range(0, d, MXU_DIM)]
                    + [functools.partial(out_chunk, c0) for c0 in range(0, d, MXU_DIM)])
    norm_units = [functools.partial(norm_chunk, j) for j in range(rows_per_step)]

    n_tiles = rows_per_step * n_pairs
    for u in range(n_tiles + PV_SKEW):
        if u < n_tiles:
            qk_tile(u)
        if 0 <= u - SOFTMAX_SKEW < n_tiles:
            softmax_tile(u - SOFTMAX_SKEW)
        if 0 <= u - PV_SKEW < n_tiles:
            pv_tile(u - PV_SKEW)
        if matmul_units:
            if u % OUT_UNIT_EVERY == 1:
                matmul_units.pop(0)()
        elif norm_units and u % NORM_UNIT_EVERY == 0:
            norm_units.pop(0)()
    assert not matmul_units
    while norm_units:
        norm_units.pop(0)()


def _attn_out(q, k, v, t2, yfn, ga, gb, x, w_na, w_fn, b_fn, w_o, gate1, scale2, shift2, norm2_w):
    bsz, s, d = x.shape
    na_w = q.shape[-1]
    f_w = yfn.shape[-1]
    rows = s // GRID_W
    kh = min(WIN_H, rows)
    assert kh % 2 == 0 and (2 * HEAD_DIM) == LANES
    rows_per_step = ATTN_ROWS_PER_STEP
    tq = rows_per_step * GRID_W
    n_tiles = rows_per_step * (na_w // LANES)

    per_seq = s // tq
    n_steps = bsz * per_seq + 1
    attn_tile = lambda t: jnp.minimum(t, n_steps - 2)
    out_tile = lambda t: jnp.maximum(t - 1, 0)
    tok_a = lambda w: pl.BlockSpec(
        (None, tq, w), lambda t: (attn_tile(t) // per_seq, attn_tile(t) % per_seq, 0))
    tok_o = lambda w: pl.BlockSpec(
        (None, tq, w), lambda t: (out_tile(t) // per_seq, out_tile(t) % per_seq, 0))
    whole_seq = pl.BlockSpec((None, s, na_w), lambda t: (attn_tile(t) // per_seq, 0, 0))
    per_batch = pl.BlockSpec((None, 1, d), lambda t: (out_tile(t) // per_seq, 0, 0))
    full = lambda a: pl.BlockSpec(a.shape, lambda t: (0,) * a.ndim)
    bfn = b_fn.reshape(1, d)
    nw2 = norm2_w.reshape(1, d)
    return pl.pallas_call(
        functools.partial(_attn_out_kernel, rows=rows, kh=kh, rows_per_step=rows_per_step),
        grid=(n_steps,),
        in_specs=[tok_a(na_w), whole_seq, whole_seq, full(t2), tok_o(f_w), tok_o(d), tok_o(d),
                  tok_o(d), full(w_na), full(w_fn), full(bfn), full(w_o), per_batch, per_batch,
                  per_batch, full(nw2)],
        out_specs=(tok_o(d), tok_o(d)),
        out_shape=(jax.ShapeDtypeStruct((bsz, s, d), F32), jax.ShapeDtypeStruct((bsz, s, d), BF16)),
        scratch_shapes=[pltpu.VMEM((tq, na_w), F32),
                        pltpu.VMEM((n_tiles, 2 * GRID_W, kh * GRID_W), F32),
                        pltpu.VMEM((n_tiles, 2 * GRID_W, kh * GRID_W), BF16)],
        compiler_params=_params("arbitrary"),
        name="attn_out",
    )(q, k, v, t2, yfn, ga, gb, x, w_na, w_fn, bfn, w_o, gate1, scale2, shift2, nw2)


def _mlp_kernel(x_ref, h_ref, g2_ref, w1_ref, w2_ref, o_ref, *, chunk):
    h = h_ref[...]
    acc = None
    for c0 in range(0, w1_ref.shape[1], chunk):
        a = jnp.maximum(_dot(h, w1_ref[:, c0:c0 + chunk]), 0.0)
        term = _dot((a * a).astype(BF16), w2_ref[c0:c0 + chunk, :])
        acc = term if acc is None else acc + term
    o_ref[...] = x_ref[...] + g2_ref[...] * acc


def _mlp(x, h, gate2, w1, w2):
    bsz, s, d = x.shape
    tm = MLP_TOKENS
    tok = pl.BlockSpec((None, tm, d), lambda b, i: (b, i, 0))
    per_batch = pl.BlockSpec((None, 1, d), lambda b, i: (b, 0, 0))
    full = lambda a: pl.BlockSpec(a.shape, lambda b, i: (0,) * a.ndim)
    return pl.pallas_call(
        functools.partial(_mlp_kernel, chunk=MLP_FF_CHUNK),
        grid=(bsz, s // tm),
        in_specs=[tok, tok, per_batch, full(w1), full(w2)],
        out_specs=tok,
        out_shape=jax.ShapeDtypeStruct((bsz, s, d), F32),
        compiler_params=_params("parallel", "parallel"),
        name="mlp",
    )(x, h, gate2, w1, w2)


def kernel(x, c, norm1_w, norm2_w, w_ada, b_ada, w_in, b_in, q_norm_w, k_norm_w, rpb,
           w_na_out, w_fn_out, b_fn_out, w_o, w_mlp_in, w_mlp_out):
    bsz, s, d = x.shape
    depth = w_ada.shape[0]
    na_w = w_na_out.shape[1]
    f_w = w_fn_out.shape[1]
    n_heads = na_w // HEAD_DIM
    rows = s // GRID_W
    for l in range(depth):
        mod = _ada(c, w_ada[l], b_ada[l])
        shift1, scale1, gate1, shift2, scale2, gate2 = [
            mod[:, i * d:(i + 1) * d].reshape(bsz, 1, d) for i in range(N_MOD)]

        qk_w = jnp.concatenate([jnp.tile(q_norm_w[l], n_heads) * (HEAD_DIM ** -0.5),
                                jnp.tile(k_norm_w[l], n_heads)]).reshape(1, 2 * na_w)
        (q, k, v, u, ga, gb), (w_na, w_fn, w_o_b, w_m1, w_m2) = _inproj(
            x, scale1, shift1, norm1_w[l], w_in[l].astype(BF16), b_in[l], qk_w, na_w, f_w,
            (w_na_out[l], w_fn_out[l], w_o[l], w_mlp_in[l], w_mlp_out[l]))
        yfn = _fourier(u)
        t2 = _bias_tables(rpb[l], rows)
        x, h2 = _attn_out(q, k, v, t2, yfn, ga, gb, x, w_na, w_fn, b_fn_out[l], w_o_b, gate1,
                          scale2, shift2, norm2_w[l])
        x = _mlp(x, h2, gate2, w_m1, w_m2)
    return x
```

```python
import functools

import numpy as np
import jax
import jax.numpy as jnp
from jax import lax
from jax.experimental import pallas as pl
from jax.experimental.pallas import tpu as pltpu

F32 = jnp.float32
BF16 = jnp.bfloat16

GRID_W = 64
HEAD_DIM = 64
WIN_H = 8
WIN_W = 16
N_FOURIER_GROUPS = 4
N_MOD = 6
EPS = 1e-6
MASK_VALUE = -1e30

SUBLANES = 8
LANES = 128
MXU_DIM = 256
SOFTMAX_SKEW = 2
PV_SKEW = 4
OUT_UNIT_EVERY = 4
NORM_UNIT_EVERY = 1
VMEM_LIMIT_BYTES = 56 * 1024 * 1024

INPROJ_TOKENS = 1024
INPROJ_CHUNK = 1024
FOURIER_BLOCKS_PER_STEP = 8
ATTN_ROWS_PER_STEP = 8
MLP_TOKENS = 1024
MLP_FF_CHUNK = 1024


def _dot(a, b):
    return jnp.dot(a, b, preferred_element_type=F32)


def _sigmoid(x):
    return 0.5 * jnp.tanh(0.5 * x) + 0.5


def _params(*semantics):
    return pltpu.CompilerParams(dimension_semantics=semantics, vmem_limit_bytes=VMEM_LIMIT_BYTES)


def _ada_kernel(ct_ref, w_ref, b_ref, o_ref):
    ct = ct_ref[...]
    st = ct * _sigmoid(ct)
    w = w_ref[...]
    rows = [jnp.sum(w * st[:, b:b + 1], axis=0, keepdims=True) for b in range(ct.shape[1])]
    o_ref[...] = jnp.concatenate(rows, axis=0) + b_ref[...]


def _ada(c, w, b):
    bsz, d = c.shape
    n = w.shape[1]
    tn = n // N_MOD
    return pl.pallas_call(
        _ada_kernel,
        grid=(n // tn,),
        in_specs=[pl.BlockSpec((d, bsz), lambda j: (0, 0)),
                  pl.BlockSpec((d, tn), lambda j: (0, j)),
                  pl.BlockSpec((1, tn), lambda j: (0, j))],
        out_specs=pl.BlockSpec((bsz, tn), lambda j: (0, j)),
        out_shape=jax.ShapeDtypeStruct((bsz, n), F32),
        compiler_params=_params("parallel"),
        name="ada",
    )(c.T, w, b.reshape(1, n))


def _inproj_kernel(x_ref, sc_ref, sh_ref, nw_ref, w_ref, b_ref, qkw_ref, gsum_ref, gexp_ref, *refs,
                   na_w, f_w, d, chunk, n_cast):
    cast_in = refs[:n_cast]
    q_ref, k_ref, v_ref, u_ref, ga_ref, gb_ref = refs[n_cast:n_cast + 6]
    cast_out = refs[n_cast + 6:]
    for src, dst in zip(cast_in, cast_out):
        dst[...] = src[...].astype(BF16)

    gain = nw_ref[...] * (1.0 + sc_ref[...])

    def normed(c0):
        x = x_ref[c0:c0 + chunk, :]
        ms = jnp.mean(x * x, axis=-1, keepdims=True)
        return (x * lax.rsqrt(ms + EPS) * gain + sh_ref[...]).astype(BF16)

    def proj(h, lo, width):
        return _dot(h, w_ref[:, lo:lo + width]) + b_ref[:, lo:lo + width]

    qk = 2 * na_w
    o_v, o_u, o_ga, o_gb = qk, qk + na_w, qk + na_w + f_w, qk + na_w + f_w + d
    n_rows = x_ref.shape[0]
    h = normed(0)
    for c0 in range(0, n_rows, chunk):
        rows = slice(c0, c0 + chunk)
        zqk = proj(h, 0, qk)
        h_next = normed(c0 + chunk) if c0 + chunk < n_rows else None
        v_ref[rows, :] = proj(h, o_v, na_w).astype(BF16)
        ssq = _dot((zqk * zqk).astype(BF16), gsum_ref[...])
        u_ref[rows, :] = proj(h, o_u, f_w)
        rinv = lax.rsqrt(ssq * (1.0 / HEAD_DIM) + EPS)
        rinv_hi = rinv.astype(BF16)
        rinv_lo = (rinv - rinv_hi.astype(F32)).astype(BF16)
        scale = _dot(jnp.concatenate([rinv_hi, rinv_lo], axis=1), gexp_ref[...])
        ga_ref[rows, :] = _sigmoid(proj(h, o_ga, d)).astype(BF16)
        zn = zqk * scale * qkw_ref[...]
        q_ref[rows, :] = zn[:, 0:na_w].astype(BF16)
        k_ref[rows, :] = zn[:, na_w:qk].astype(BF16)
        gb_ref[rows, :] = _sigmoid(proj(h, o_gb, d)).astype(BF16)
        h = h_next


def _inproj(x, scale1, shift1, norm_w, w_in, b_in, qk_w, na_w, f_w, later_weights):
    bsz, s, d = x.shape
    in_w = w_in.shape[1]
    tm = INPROJ_TOKENS
    qk = 2 * na_w
    n_grp = qk // HEAD_DIM
    grp = np.arange(qk) // HEAD_DIM
    gsum = jnp.asarray((grp[:, None] == np.arange(LANES)[None, :]).astype(np.float32), BF16)
    gexp1 = (np.arange(LANES)[:, None] == grp[None, :]).astype(np.float32)
    gexp = jnp.asarray(np.concatenate([gexp1, gexp1], axis=0), BF16)
    assert n_grp <= LANES

    tok = lambda w: pl.BlockSpec((None, tm, w), lambda b, i: (b, i, 0))
    per_batch = pl.BlockSpec((None, 1, d), lambda b, i: (b, 0, 0))
    full = lambda a: pl.BlockSpec(a.shape, lambda b, i: (0,) * a.ndim)
    nw = norm_w.reshape(1, d)
    bi = b_in.reshape(1, in_w)
    out_shapes = (
        jax.ShapeDtypeStruct((bsz, s, na_w), BF16),
        jax.ShapeDtypeStruct((bsz, s, na_w), BF16),
        jax.ShapeDtypeStruct((bsz, s, na_w), BF16),
        jax.ShapeDtypeStruct((bsz, s, f_w), F32),
        jax.ShapeDtypeStruct((bsz, s, d), BF16),
        jax.ShapeDtypeStruct((bsz, s, d), BF16),
    )
    n_steps = bsz * (s // tm)
    slab = lambda a: pl.BlockSpec((a.shape[0] // n_steps, a.shape[1]),
                                  lambda b, i: (b * (s // tm) + i, 0))
    for a in later_weights:
        assert a.shape[0] % (n_steps * 2 * SUBLANES) == 0, a.shape
    outs = pl.pallas_call(
        functools.partial(_inproj_kernel, na_w=na_w, f_w=f_w, d=d, chunk=INPROJ_CHUNK,
                          n_cast=len(later_weights)),
        grid=(bsz, s // tm),
        in_specs=[tok(d), per_batch, per_batch, full(nw), full(w_in), full(bi), full(qk_w),
                  full(gsum), full(gexp)] + [slab(a) for a in later_weights],
        out_specs=(tok(na_w), tok(na_w), tok(na_w), tok(f_w), tok(d), tok(d))
        + tuple(slab(a) for a in later_weights),
        out_shape=out_shapes + tuple(jax.ShapeDtypeStruct(a.shape, BF16) for a in later_weights),
        compiler_params=_params("parallel", "parallel"),
        name="inproj",
    )(x, scale1, shift1, nw, w_in, bi, qk_w, gsum, gexp, *later_weights)
    return outs[:6], outs[6:]


def _fourier_constants(rows, f_w):
    n = rows
    r8 = SUBLANES
    k = np.arange(n)
    ang_a = 2.0 * np.pi * np.outer(k, k) / n
    eye8 = np.eye(r8)
    norm = 1.0 / np.sqrt(n)
    half = n // 2 + 1
    fa_re = np.kron(np.cos(ang_a[:half]), eye8) * norm
    fa_im = np.kron(-np.sin(ang_a[:half]), eye8) * norm
    fa = np.concatenate([fa_re, fa_im], axis=0)

    nblk = n // r8
    s2 = k[None, :]
    s2p = k[:, None]
    g = np.zeros((r8 // 2, 4 * n, 4 * n))
    for pp in range(r8 // 2):
        gr = np.zeros((2, n, 2, n))
        gi = np.zeros((2, n, 2, n))
        for ll in range(2):
            l = 2 * pp + ll
            ang = 2.0 * np.pi * (s2 * l / (n * GRID_W) + s2 * s2p / n)
            gr[ll, :, ll, :] = np.cos(ang) * norm
            gi[ll, :, ll, :] = -np.sin(ang) * norm
        gr = gr.reshape(2 * n, 2 * n)
        gi = gi.reshape(2 * n, 2 * n)
        g[pp] = np.block([[gr, -gi], [gi, gr]])

    perm = np.zeros((n * r8, n * r8))
    s1p = np.repeat(k, r8)
    jj = np.tile(np.arange(r8), n)
    perm[jj * n + s1p, s1p * r8 + jj] = 1.0

    t_idx = np.arange(nblk)[:, None, None]
    blk_of_row = (k // r8)[None, :, None]
    j_idx = np.arange(r8)[None, None, :]
    ang_t = 2.0 * np.pi * (t_idx * r8 + j_idx) * blk_of_row * r8 / (n * GRID_W)
    ang_t = ang_t.reshape(nblk, n * r8, 1)
    tw_cos = np.broadcast_to(np.cos(ang_t), (nblk, n * r8, LANES))
    tw_sin = np.broadcast_to(np.sin(ang_t), (nblk, n * r8, LANES))

    gd = f_w // N_FOURIER_GROUPS
    c = np.arange(gd)
    ang_c = 2.0 * np.pi * np.outer(c, c) / gd
    assert MXU_DIM % gd == 0 and f_w % MXU_DIM == 0
    eye_g = np.eye(MXU_DIM // gd)
    cs = np.concatenate([np.kron(eye_g, np.cos(ang_c)), np.kron(eye_g, np.sin(ang_c))],
                        axis=0) / np.sqrt(gd)
    as_f32 = lambda a: jnp.asarray(np.ascontiguousarray(a, dtype=np.float32))
    to_bf16 = lambda a: as_f32(a).astype(BF16)
    return (to_bf16(fa), to_bf16(g), to_bf16(cs), to_bf16(perm), as_f32(tw_cos), as_f32(tw_sin))


def _fourier_kernel(u_ref, fa_ref, g_ref, cs_ref, perm_ref, twc_ref, tws_ref, y_ref,
                    sre_ref, sim_ref, *, steps_per_stage, blocks_per_step):
    t = pl.program_id(1)
    n, _, f_w = u_ref.shape
    r8 = SUBLANES
    m = n * r8

    @pl.when(t < steps_per_stage)
    def _stage_a():
        for j in range(blocks_per_step):
            xin = u_ref[:, j * r8:(j + 1) * r8, :].reshape(m, f_w).astype(BF16)
            a = _dot(fa_ref[...], xin)
            reps = f_w // LANES
            cos_t = jnp.concatenate([twc_ref[j]] * reps, axis=1)
            sin_t = jnp.concatenate([tws_ref[j]] * reps, axis=1)
            mh = (n // 2 + 1) * r8
            mirror = lambda z: [z[(n - s1) * r8:(n - s1 + 1) * r8] for s1 in range(n // 2 + 1, n)]
            a_re = jnp.concatenate([a[0:mh]] + mirror(a[0:mh]), axis=0)
            a_im = jnp.concatenate([a[mh:2 * mh]] + [-z for z in mirror(a[mh:2 * mh])], axis=0)
            b_re = a_re * cos_t + a_im * sin_t
            b_im = a_im * cos_t - a_re * sin_t
            off = pl.multiple_of((t * blocks_per_step + j) * r8, r8)
            sre_ref[:, pl.ds(off, r8), :] = b_re.reshape(n, r8, f_w)
            sim_ref[:, pl.ds(off, r8), :] = b_im.reshape(n, r8, f_w)

    @pl.when((t >= steps_per_stage) & (t < 2 * steps_per_stage))
    def _stage_b():
        pair_rows = 2 * n
        for j in range(blocks_per_step):
            off = pl.multiple_of(((t - steps_per_stage) * blocks_per_step + j) * r8, r8)
            br = sre_ref[pl.ds(off, r8)].reshape(m, f_w)
            bi = sim_ref[pl.ds(off, r8)].reshape(m, f_w)
            p_re, p_im = [], []
            for pp in range(r8 // 2):
                rs_ = slice(pp * pair_rows, (pp + 1) * pair_rows)
                bcat = jnp.concatenate([br[rs_], bi[rs_]], axis=0).astype(BF16)
                p = _dot(g_ref[pp], bcat)
                p_re.append(p[0:pair_rows])
                p_im.append(p[pair_rows:2 * pair_rows])
            p_re = jnp.concatenate(p_re, axis=0).astype(BF16)
            p_im = jnp.concatenate(p_im, axis=0).astype(BF16)
            y = jnp.concatenate(
                [_dot(jnp.concatenate([p_re[:, c0:c0 + MXU_DIM], p_im[:, c0:c0 + MXU_DIM]], axis=1),
                      cs_ref[...]) for c0 in range(0, f_w, MXU_DIM)], axis=1)
            sre_ref[pl.ds(off, r8)] = y.reshape(r8, n, f_w)

    @pl.when(t >= 2 * steps_per_stage)
    def _stage_c():
        for j in range(blocks_per_step):
            off = pl.multiple_of(((t - 2 * steps_per_stage) * blocks_per_step + j) * r8, r8)
            yin = sre_ref[:, pl.ds(off, r8), :].reshape(m, f_w).astype(BF16)
            y_ref[j * m:(j + 1) * m, :] = _dot(perm_ref[...], yin).astype(y_ref.dtype)


def _fourier(u):
    bsz, s, f_w = u.shape
    rows = s // GRID_W
    assert rows == GRID_W, "the two-stage position DFT assumes a square token grid"
    r8 = SUBLANES
    nblk = rows // r8
    fa, g, cs, perm, tw_cos, tw_sin = _fourier_constants(rows, f_w)
    u4 = u.reshape(bsz, rows, GRID_W, f_w)
    blocks_per_step = FOURIER_BLOCKS_PER_STEP
    steps = nblk // blocks_per_step
    width = blocks_per_step * r8
    stage_a_step = lambda b, t: jnp.minimum(t, steps - 1)
    stage_c_step = lambda b, t: jnp.maximum(t - 2 * steps, 0)
    full = lambda a: pl.BlockSpec(a.shape, lambda b, t: (0,) * a.ndim)
    tw_spec = pl.BlockSpec((blocks_per_step,) + tw_cos.shape[1:],
                           lambda b, t: (stage_a_step(b, t), 0, 0))
    return pl.pallas_call(
        functools.partial(_fourier_kernel, steps_per_stage=steps, blocks_per_step=blocks_per_step),
        grid=(bsz, 3 * steps),
        in_specs=[
            pl.BlockSpec((None, rows, width, f_w), lambda b, t: (b, 0, stage_a_step(b, t), 0)),
            full(fa), full(g), full(cs), full(perm), tw_spec, tw_spec,
        ],
        out_specs=pl.BlockSpec((None, width * GRID_W, f_w), lambda b, t: (b, stage_c_step(b, t), 0)),
        out_shape=jax.ShapeDtypeStruct((bsz, s, f_w), BF16),
        scratch_shapes=[pltpu.VMEM((rows, GRID_W, f_w), F32), pltpu.VMEM((rows, GRID_W, f_w), F32)],
        compiler_params=_params("arbitrary", "arbitrary"),
        name="fourier",
    )(u4, fa, g, cs, perm, tw_cos, tw_sin)


def _bias_tables(rpb, rows):
    kw = min(WIN_W, GRID_W)
    cols = np.arange(GRID_W)
    col_start = np.clip(cols - kw // 2, 0, GRID_W - kw)
    kj = cols[None, :]
    allowed = (kj >= col_start[:, None]) & (kj < col_start[:, None] + kw)
    rel = kj - cols[:, None] + (WIN_W - 1)
    nrel = 2 * WIN_W - 1
    onehot = ((rel[None] == np.arange(nrel)[:, None, None]) & allowed[None]).astype(np.float32)
    sel = np.zeros((2 * nrel + 1, GRID_W, 2 * GRID_W), np.float32)
    sel[0:nrel, :, 0:GRID_W] = onehot
    sel[nrel:2 * nrel, :, GRID_W:] = onehot
    sel[2 * nrel] = np.tile(~allowed, (1, 2))
    rpb = rpb.astype(F32)
    mask_col = jnp.full(rpb.shape[:1] + (rpb.shape[1] - 1, 1), MASK_VALUE, F32)
    src = jnp.concatenate([rpb[:, :-1], rpb[:, 1:], mask_col], axis=-1)
    return jnp.einsum("hmd,dcx->hmcx", src, jnp.asarray(sel), precision=lax.Precision.HIGHEST)


def _attn_out_kernel(q_ref, k_ref, v_ref, t2_ref, yfn_ref, ga_ref, gb_ref, x_ref,
                     wna_ref, wfn_ref, bfn_ref, wo_ref, g1_ref, sc2_ref, sh2_ref, nw2_ref,
                     o_ref, h2_ref, yna_ref, s_ref, p_ref,
                     *, rows, kh, rows_per_step):
    t = pl.program_id(0)
    last_tile = pl.num_programs(0) - 2

    @pl.when(t == 0)
    def _init():
        yna_ref[...] = jnp.zeros_like(yna_ref)

    yna_prev = yna_ref[...].astype(BF16)
    yfn_prev = yfn_ref[...]

    rb = lax.rem(jnp.minimum(t, last_tile), rows // rows_per_step)
    n_pairs = q_ref.shape[1] // LANES
    lane = lax.broadcasted_iota(jnp.int32, (1, LANES), 1)
    low = lane < HEAD_DIM
    mask_lo = jnp.where(low, 1.0, 0.0).astype(BF16)
    mask_hi = jnp.where(low, 0.0, 1.0).astype(BF16)
    n_keys = kh * GRID_W

    key_start = []
    for j in range(rows_per_step):
        r = rb * rows_per_step + j
        rs = jnp.clip(r - kh // 2, 0, rows - kh)
        key_start.append((pl.multiple_of(rs * GRID_W, GRID_W), rs - r + (WIN_H - 1)))

    def qk_tile(ti):
        j, p = divmod(ti, n_pairs)
        ks, idx0 = key_start[j]
        cs = slice(LANES * p, LANES * (p + 1))
        q2 = q_ref[j * GRID_W:(j + 1) * GRID_W, cs]
        k2 = k_ref[pl.ds(ks, n_keys), cs]
        qs = jnp.concatenate([q2 * mask_lo, q2 * mask_hi], axis=0)
        s = lax.dot_general(qs, k2, (((1,), (1,)), ((), ())), preferred_element_type=F32)
        bias = jnp.concatenate(
            [jnp.concatenate([t2_ref[2 * p + hh, idx0 + 2 * m] for m in range(kh // 2)], axis=1)
             for hh in range(2)], axis=0)
        s_ref[ti] = s + bias

    def softmax_tile(ti):
        s = s_ref[ti]
        p_ref[ti] = jnp.exp(s - jnp.max(s, axis=-1, keepdims=True)).astype(BF16)

    ones = jnp.ones((n_keys, LANES), BF16)
    row_outs = []

    def pv_tile(ti):
        j, p = divmod(ti, n_pairs)
        ks, _ = key_start[j]
        v2 = v_ref[pl.ds(ks, n_keys), LANES * p:LANES * (p + 1)]
        o = _dot(p_ref[ti], jnp.concatenate([v2, ones], axis=1))
        o = o[:, 0:LANES] * (1.0 / o[:, LANES:2 * LANES])
        row_outs.append(jnp.where(low, o[0:GRID_W], o[GRID_W:2 * GRID_W]))
        if p == n_pairs - 1:
            yna_ref[j * GRID_W:(j + 1) * GRID_W, :] = jnp.concatenate(row_outs, axis=1)
            row_outs.clear()

    d = o_ref.shape[1]
    merged = []

    def merge_chunk(c0):
        cols = slice(c0, c0 + MXU_DIM)
        ya = _dot(yna_prev, wna_ref[:, cols])
        yf = _dot(yfn_prev, wfn_ref[:, cols]) + bfn_ref[:, cols]
        merged.append((ga_ref[:, cols].astype(F32) * ya
                       + gb_ref[:, cols].astype(F32) * yf).astype(BF16))

    def out_chunk(c0):
        cols = slice(c0, c0 + MXU_DIM)
        lhs = jnp.concatenate(merged, axis=1)
        o_ref[:, cols] = x_ref[:, cols] + g1_ref[:, cols] * _dot(lhs, wo_ref[:, cols])

    norm_rows = o_ref.shape[0] // rows_per_step
    gain2 = nw2_ref[...] * (1.0 + sc2_ref[...])

    def norm_chunk(j):
        nr = slice(j * norm_rows, (j + 1) * norm_rows)
        x1 = o_ref[nr, :]
        ms = jnp.mean(x1 * x1, axis=-1, keepdims=True)
        h2_ref[nr, :] = (x1 * lax.rsqrt(ms + EPS) * gain2 + sh2_ref[...]).astype(BF16)

    matmul_units = ([functools.partial(merge_chunk, c0) for c0 in range(0, d, MXU_DIM)]
                    + [functools.partial(out_chunk, c0) for c0 in range(0, d, MXU_DIM)])
    norm_units = [functools.partial(norm_chunk, j) for j in range(rows_per_step)]

    n_tiles = rows_per_step * n_pairs
    for u in range(n_tiles + PV_SKEW):
        if u < n_tiles:
            qk_tile(u)
        if 0 <= u - SOFTMAX_SKEW < n_tiles:
            softmax_tile(u - SOFTMAX_SKEW)
        if 0 <= u - PV_SKEW < n_tiles:
            pv_tile(u - PV_SKEW)
        if matmul_units:
            if u % OUT_UNIT_EVERY == 1:
                matmul_units.pop(0)()
        elif norm_units and u % NORM_UNIT_EVERY == 0:
            norm_units.pop(0)()
    assert not matmul_units
    while norm_units:
        norm_units.pop(0)()


def _attn_out(q, k, v, t2, yfn, ga, gb, x, w_na, w_fn, b_fn, w_o, gate1, scale2, shift2, norm2_w):
    bsz, s, d = x.shape
    na_w = q.shape[-1]
    f_w = yfn.shape[-1]
    rows = s // GRID_W
    kh = min(WIN_H, rows)
    assert kh % 2 == 0 and (2 * HEAD_DIM) == LANES
    rows_per_step = ATTN_ROWS_PER_STEP
    tq = rows_per_step * GRID_W
    n_tiles = rows_per_step * (na_w // LANES)

    per_seq = s // tq
    n_steps = bsz * per_seq + 1
    attn_tile = lambda t: jnp.minimum(t, n_steps - 2)
    out_tile = lambda t: jnp.maximum(t - 1, 0)
    tok_a = lambda w: pl.BlockSpec(
        (None, tq, w), lambda t: (attn_tile(t) // per_seq, attn_tile(t) % per_seq, 0))
    tok_o = lambda w: pl.BlockSpec(
        (None, tq, w), lambda t: (out_tile(t) // per_seq, out_tile(t) % per_seq, 0))
    whole_seq = pl.BlockSpec((None, s, na_w), lambda t: (attn_tile(t) // per_seq, 0, 0))
    per_batch = pl.BlockSpec((None, 1, d), lambda t: (out_tile(t) // per_seq, 0, 0))
    full = lambda a: pl.BlockSpec(a.shape, lambda t: (0,) * a.ndim)
    bfn = b_fn.reshape(1, d)
    nw2 = norm2_w.reshape(1, d)
    return pl.pallas_call(
        functools.partial(_attn_out_kernel, rows=rows, kh=kh, rows_per_step=rows_per_step),
        grid=(n_steps,),
        in_specs=[tok_a(na_w), whole_seq, whole_seq, full(t2), tok_o(f_w), tok_o(d), tok_o(d),
                  tok_o(d), full(w_na), full(w_fn), full(bfn), full(w_o), per_batch, per_batch,
                  per_batch, full(nw2)],
        out_specs=(tok_o(d), tok_o(d)),
        out_shape=(jax.ShapeDtypeStruct((bsz, s, d), F32), jax.ShapeDtypeStruct((bsz, s, d), BF16)),
        scratch_shapes=[pltpu.VMEM((tq, na_w), F32),
                        pltpu.VMEM((n_tiles, 2 * GRID_W, kh * GRID_W), F32),
                        pltpu.VMEM((n_tiles, 2 * GRID_W, kh * GRID_W), BF16)],
        compiler_params=_params("arbitrary"),
        name="attn_out",
    )(q, k, v, t2, yfn, ga, gb, x, w_na, w_fn, bfn, w_o, gate1, scale2, shift2, nw2)


def _mlp_kernel(x_ref, h_ref, g2_ref, w1_ref, w2_ref, o_ref, *, chunk):
    h = h_ref[...]
    acc = None
    for c0 in range(0, w1_ref.shape[1], chunk):
        a = jnp.maximum(_dot(h, w1_ref[:, c0:c0 + chunk]), 0.0)
        term = _dot((a * a).astype(BF16), w2_ref[c0:c0 + chunk, :])
        acc = term if acc is None else acc + term
    o_ref[...] = x_ref[...] + g2_ref[...] * acc


def _mlp(x, h, gate2, w1, w2):
    bsz, s, d = x.shape
    tm = MLP_TOKENS
    tok = pl.BlockSpec((None, tm, d), lambda b, i: (b, i, 0))
    per_batch = pl.BlockSpec((None, 1, d), lambda b, i: (b, 0, 0))
    full = lambda a: pl.BlockSpec(a.shape, lambda b, i: (0,) * a.ndim)
    return pl.pallas_call(
        functools.partial(_mlp_kernel, chunk=MLP_FF_CHUNK),
        grid=(bsz, s // tm),
        in_specs=[tok, tok, per_batch, full(w1), full(w2)],
        out_specs=tok,
        out_shape=jax.ShapeDtypeStruct((bsz, s, d), F32),
        compiler_params=_params("parallel", "parallel"),
        name="mlp",
    )(x, h, gate2, w1, w2)


def kernel(x, c, norm1_w, norm2_w, w_ada, b_ada, w_in, b_in, q_norm_w, k_norm_w, rpb,
           w_na_out, w_fn_out, b_fn_out, w_o, w_mlp_in, w_mlp_out):
    bsz, s, d = x.shape
    depth = w_ada.shape[0]
    na_w = w_na_out.shape[1]
    f_w = w_fn_out.shape[1]
    n_heads = na_w // HEAD_DIM
    rows = s // GRID_W
    for l in range(depth):
        mod = _ada(c, w_ada[l], b_ada[l])
        shift1, scale1, gate1, shift2, scale2, gate2 = [
            mod[:, i * d:(i + 1) * d].reshape(bsz, 1, d) for i in range(N_MOD)]

        qk_w = jnp.concatenate([jnp.tile(q_norm_w[l], n_heads) * (HEAD_DIM ** -0.5),
                                jnp.tile(k_norm_w[l], n_heads)]).reshape(1, 2 * na_w)
        (q, k, v, u, ga, gb), (w_na, w_fn, w_o_b, w_m1, w_m2) = _inproj(
            x, scale1, shift1, norm1_w[l], w_in[l].astype(BF16), b_in[l], qk_w, na_w, f_w,
            (w_na_out[l], w_fn_out[l], w_o[l], w_mlp_in[l], w_mlp_out[l]))
        yfn = _fourier(u)
        t2 = _bias_tables(rpb[l], rows)
        x, h2 = _attn_out(q, k, v, t2, yfn, ga, gb, x, w_na, w_fn, b_fn_out[l], w_o_b, gate1,
                          scale2, shift2, norm2_w[l])
        x = _mlp(x, h2, gate2, w_m1, w_m2)
    return x
```

```python
import functools

import numpy as np
import jax
import jax.numpy as jnp
from jax import lax
from jax.experimental import pallas as pl
from jax.experimental.pallas import tpu as pltpu

F32 = jnp.float32
BF16 = jnp.bfloat16

GRID_W = 64
HEAD_DIM = 64
WIN_H = 8
WIN_W = 16
N_FOURIER_GROUPS = 4
N_MOD = 6
EPS = 1e-6
MASK_VALUE = -1e30
LOG2_E = 1.4426950408889634

SUBLANES = 8
LANES = 128
MXU_DIM = 256
SOFTMAX_SKEW = 2
PV_SKEW = 4
OUT_UNIT_EVERY = 4
NORM_UNIT_EVERY = 1
VMEM_LIMIT_BYTES = 56 * 1024 * 1024

INPROJ_TOKENS = 1024
INPROJ_CHUNK = 1024
FOURIER_BLOCKS_PER_STEP = 8
ATTN_ROWS_PER_STEP = 8
MLP_TOKENS = 1024
MLP_FF_CHUNK = 1024


def _dot(a, b):
    return jnp.dot(a, b, preferred_element_type=F32)


def _sigmoid(x):
    return 0.5 * jnp.tanh(0.5 * x) + 0.5


def _params(*semantics):
    return pltpu.CompilerParams(dimension_semantics=semantics, vmem_limit_bytes=VMEM_LIMIT_BYTES)


def _ada_kernel(ct_ref, w_ref, b_ref, o_ref):
    ct = ct_ref[...]
    st = ct * _sigmoid(ct)
    w = w_ref[...]
    rows = [jnp.sum(w * st[:, b:b + 1], axis=0, keepdims=True) for b in range(ct.shape[1])]
    o_ref[...] = jnp.concatenate(rows, axis=0) + b_ref[...]


def _ada(c, w, b):
    bsz, d = c.shape
    n = w.shape[1]
    tn = n // N_MOD
    return pl.pallas_call(
        _ada_kernel,
        grid=(n // tn,),
        in_specs=[pl.BlockSpec((d, bsz), lambda j: (0, 0)),
                  pl.BlockSpec((d, tn), lambda j: (0, j)),
                  pl.BlockSpec((1, tn), lambda j: (0, j))],
        out_specs=pl.BlockSpec((bsz, tn), lambda j: (0, j)),
        out_shape=jax.ShapeDtypeStruct((bsz, n), F32),
        compiler_params=_params("parallel"),
        name="ada",
    )(c.T, w, b.reshape(1, n))


def _inproj_kernel(x_ref, sc_ref, sh_ref, nw_ref, w_ref, b_ref, qkw_ref, gsum_ref, gexp_ref, *refs,
                   na_w, f_w, d, chunk, n_cast):
    cast_in = refs[:n_cast]
    q_ref, k_ref, v_ref, u_ref, ga_ref, gb_ref = refs[n_cast:n_cast + 6]
    cast_out = refs[n_cast + 6:]
    for src, dst in zip(cast_in, cast_out):
        dst[...] = src[...].astype(BF16)

    gain = nw_ref[...] * (1.0 + sc_ref[...])

    def normed(c0):
        x = x_ref[c0:c0 + chunk, :]
        ms = jnp.mean(x * x, axis=-1, keepdims=True)
        return (x * lax.rsqrt(ms + EPS) * gain + sh_ref[...]).astype(BF16)

    def proj(h, lo, width):
        return _dot(h, w_ref[:, lo:lo + width]) + b_ref[:, lo:lo + width]

    qk = 2 * na_w
    o_v, o_u, o_ga, o_gb = qk, qk + na_w, qk + na_w + f_w, qk + na_w + f_w + d
    n_rows = x_ref.shape[0]
    h = normed(0)
    for c0 in range(0, n_rows, chunk):
        rows = slice(c0, c0 + chunk)
        zqk = proj(h, 0, qk)
        h_next = normed(c0 + chunk) if c0 + chunk < n_rows else None
        v_ref[rows, :] = proj(h, o_v, na_w).astype(BF16)
        ssq = _dot((zqk * zqk).astype(BF16), gsum_ref[...])
        u_ref[rows, :] = proj(h, o_u, f_w)
        rinv = lax.rsqrt(ssq * (1.0 / HEAD_DIM) + EPS)
        rinv_hi = rinv.astype(BF16)
        rinv_lo = (rinv - rinv_hi.astype(F32)).astype(BF16)
        scale = _dot(jnp.concatenate([rinv_hi, rinv_lo], axis=1), gexp_ref[...])
        ga_ref[rows, :] = _sigmoid(proj(h, o_ga, d)).astype(BF16)
        zn = zqk * scale * qkw_ref[...]
        q_ref[rows, :] = zn[:, 0:na_w].astype(BF16)
        k_ref[rows, :] = zn[:, na_w:qk].astype(BF16)
        gb_ref[rows, :] = _sigmoid(proj(h, o_gb, d)).astype(BF16)
        h = h_next


def _inproj(x, scale1, shift1, norm_w, w_in, b_in, qk_w, na_w, f_w, later_weights):
    bsz, s, d = x.shape
    in_w = w_in.shape[1]
    tm = INPROJ_TOKENS
    qk = 2 * na_w
    n_grp = qk // HEAD_DIM
    grp = np.arange(qk) // HEAD_DIM
    gsum = jnp.asarray((grp[:, None] == np.arange(LANES)[None, :]).astype(np.float32), BF16)
    gexp1 = (np.arange(LANES)[:, None] == grp[None, :]).astype(np.float32)
    gexp = jnp.asarray(np.concatenate([gexp1, gexp1], axis=0), BF16)
    assert n_grp <= LANES

    tok = lambda w: pl.BlockSpec((None, tm, w), lambda b, i: (b, i, 0))
    per_batch = pl.BlockSpec((None, 1, d), lambda b, i: (b, 0, 0))
    full = lambda a: pl.BlockSpec(a.shape, lambda b, i: (0,) * a.ndim)
    nw = norm_w.reshape(1, d)
    bi = b_in.reshape(1, in_w)
    out_shapes = (
        jax.ShapeDtypeStruct((bsz, s, na_w), BF16),
        jax.ShapeDtypeStruct((bsz, s, na_w), BF16),
        jax.ShapeDtypeStruct((bsz, s, na_w), BF16),
        jax.ShapeDtypeStruct((bsz, s, f_w), F32),
        jax.ShapeDtypeStruct((bsz, s, d), BF16),
        jax.ShapeDtypeStruct((bsz, s, d), BF16),
    )
    n_steps = bsz * (s // tm)
    slab = lambda a: pl.BlockSpec((a.shape[0] // n_steps, a.shape[1]),
                                  lambda b, i: (b * (s // tm) + i, 0))
    for a in later_weights:
        assert a.shape[0] % (n_steps * 2 * SUBLANES) == 0, a.shape
    outs = pl.pallas_call(
        functools.partial(_inproj_kernel, na_w=na_w, f_w=f_w, d=d, chunk=INPROJ_CHUNK,
                          n_cast=len(later_weights)),
        grid=(bsz, s // tm),
        in_specs=[tok(d), per_batch, per_batch, full(nw), full(w_in), full(bi), full(qk_w),
                  full(gsum), full(gexp)] + [slab(a) for a in later_weights],
        out_specs=(tok(na_w), tok(na_w), tok(na_w), tok(f_w), tok(d), tok(d))
        + tuple(slab(a) for a in later_weights),
        out_shape=out_shapes + tuple(jax.ShapeDtypeStruct(a.shape, BF16) for a in later_weights),
        compiler_params=_params("parallel", "parallel"),
        name="inproj",
    )(x, scale1, shift1, nw, w_in, bi, qk_w, gsum, gexp, *later_weights)
    return outs[:6], outs[6:]


def _fourier_constants(rows, f_w):
    n = rows
    r8 = SUBLANES
    k = np.arange(n)
    ang_a = 2.0 * np.pi * np.outer(k, k) / n
    eye8 = np.eye(r8)
    norm = 1.0 / np.sqrt(n)
    half = n // 2 + 1
    fa_re = np.kron(np.cos(ang_a[:half]), eye8) * norm
    fa_im = np.kron(-np.sin(ang_a[:half]), eye8) * norm
    fa = np.concatenate([fa_re, fa_im], axis=0)

    nblk = n // r8
    s2 = k[None, :]
    s2p = k[:, None]
    g = np.zeros((r8 // 2, 4 * n, 4 * n))
    for pp in range(r8 // 2):
        gr = np.zeros((2, n, 2, n))
        gi = np.zeros((2, n, 2, n))
        for ll in range(2):
            l = 2 * pp + ll
            ang = 2.0 * np.pi * (s2 * l / (n * GRID_W) + s2 * s2p / n)
            gr[ll, :, ll, :] = np.cos(ang) * norm
            gi[ll, :, ll, :] = -np.sin(ang) * norm
        gr = gr.reshape(2 * n, 2 * n)
        gi = gi.reshape(2 * n, 2 * n)
        g[pp] = np.block([[gr, -gi], [gi, gr]])

    perm = np.zeros((n * r8, n * r8))
    s1p = np.repeat(k, r8)
    jj = np.tile(np.arange(r8), n)
    perm[jj * n + s1p, s1p * r8 + jj] = 1.0

    t_idx = np.arange(nblk)[:, None, None]
    blk_of_row = (k // r8)[None, :, None]
    j_idx = np.arange(r8)[None, None, :]
    ang_t = 2.0 * np.pi * (t_idx * r8 + j_idx) * blk_of_row * r8 / (n * GRID_W)
    ang_t = ang_t.reshape(nblk, n * r8, 1)
    tw_cos = np.broadcast_to(np.cos(ang_t), (nblk, n * r8, LANES))
    tw_sin = np.broadcast_to(np.sin(ang_t), (nblk, n * r8, LANES))

    gd = f_w // N_FOURIER_GROUPS
    c = np.arange(gd)
    ang_c = 2.0 * np.pi * np.outer(c, c) / gd
    assert MXU_DIM % gd == 0 and f_w % MXU_DIM == 0
    eye_g = np.eye(MXU_DIM // gd)
    cs = np.concatenate([np.kron(eye_g, np.cos(ang_c)), np.kron(eye_g, np.sin(ang_c))],
                        axis=0) / np.sqrt(gd)
    as_f32 = lambda a: jnp.asarray(np.ascontiguousarray(a, dtype=np.float32))
    to_bf16 = lambda a: as_f32(a).astype(BF16)
    return (to_bf16(fa), to_bf16(g), to_bf16(cs), to_bf16(perm), as_f32(tw_cos), as_f32(tw_sin))


def _fourier_kernel(u_ref, fa_ref, g_ref, cs_ref, perm_ref, twc_ref, tws_ref, y_ref,
                    sre_ref, sim_ref, *, steps_per_stage, blocks_per_step):
    t = pl.program_id(1)
    n, _, f_w = u_ref.shape
    r8 = SUBLANES
    m = n * r8

    @pl.when(t < steps_per_stage)
    def _stage_a():
        for j in range(blocks_per_step):
            xin = u_ref[:, j * r8:(j + 1) * r8, :].reshape(m, f_w).astype(BF16)
            a = _dot(fa_ref[...], xin)
            reps = f_w // LANES
            cos_t = jnp.concatenate([twc_ref[j]] * reps, axis=1)
            sin_t = jnp.concatenate([tws_ref[j]] * reps, axis=1)
            mh = (n // 2 + 1) * r8
            mirror = lambda z: [z[(n - s1) * r8:(n - s1 + 1) * r8] for s1 in range(n // 2 + 1, n)]
            a_re = jnp.concatenate([a[0:mh]] + mirror(a[0:mh]), axis=0)
            a_im = jnp.concatenate([a[mh:2 * mh]] + [-z for z in mirror(a[mh:2 * mh])], axis=0)
            b_re = a_re * cos_t + a_im * sin_t
            b_im = a_im * cos_t - a_re * sin_t
            off = pl.multiple_of((t * blocks_per_step + j) * r8, r8)
            sre_ref[:, pl.ds(off, r8), :] = b_re.reshape(n, r8, f_w)
            sim_ref[:, pl.ds(off, r8), :] = b_im.reshape(n, r8, f_w)

    @pl.when((t >= steps_per_stage) & (t < 2 * steps_per_stage))
    def _stage_b():
        pair_rows = 2 * n
        for j in range(blocks_per_step):
            off = pl.multiple_of(((t - steps_per_stage) * blocks_per_step + j) * r8, r8)
            br = sre_ref[pl.ds(off, r8)].reshape(m, f_w)
            bi = sim_ref[pl.ds(off, r8)].reshape(m, f_w)
            p_re, p_im = [], []
            for pp in range(r8 // 2):
                rs_ = slice(pp * pair_rows, (pp + 1) * pair_rows)
                bcat = jnp.concatenate([br[rs_], bi[rs_]], axis=0).astype(BF16)
                p = _dot(g_ref[pp], bcat)
                p_re.append(p[0:pair_rows])
                p_im.append(p[pair_rows:2 * pair_rows])
            p_re = jnp.concatenate(p_re, axis=0).astype(BF16)
            p_im = jnp.concatenate(p_im, axis=0).astype(BF16)
            y = jnp.concatenate(
                [_dot(jnp.concatenate([p_re[:, c0:c0 + MXU_DIM], p_im[:, c0:c0 + MXU_DIM]], axis=1),
                      cs_ref[...]) for c0 in range(0, f_w, MXU_DIM)], axis=1)
            sre_ref[pl.ds(off, r8)] = y.reshape(r8, n, f_w)

    @pl.when(t >= 2 * steps_per_stage)
    def _stage_c():
        for j in range(blocks_per_step):
            off = pl.multiple_of(((t - 2 * steps_per_stage) * blocks_per_step + j) * r8, r8)
            yin = sre_ref[:, pl.ds(off, r8), :].reshape(m, f_w).astype(BF16)
            y_ref[j * m:(j + 1) * m, :] = _dot(perm_ref[...], yin).astype(y_ref.dtype)


def _fourier(u):
    bsz, s, f_w = u.shape
    rows = s // GRID_W
    assert rows == GRID_W, "the two-stage position DFT assumes a square token grid"
    r8 = SUBLANES
    nblk = rows // r8
    fa, g, cs, perm, tw_cos, tw_sin = _fourier_constants(rows, f_w)
    u4 = u.reshape(bsz, rows, GRID_W, f_w)
    blocks_per_step = FOURIER_BLOCKS_PER_STEP
    steps = nblk // blocks_per_step
    width = blocks_per_step * r8
    stage_a_step = lambda b, t: jnp.minimum(t, steps - 1)
    stage_c_step = lambda b, t: jnp.maximum(t - 2 * steps, 0)
    full = lambda a: pl.BlockSpec(a.shape, lambda b, t: (0,) * a.ndim)
    tw_spec = pl.BlockSpec((blocks_per_step,) + tw_cos.shape[1:],
                           lambda b, t: (stage_a_step(b, t), 0, 0))
    return pl.pallas_call(
        functools.partial(_fourier_kernel, steps_per_stage=steps, blocks_per_step=blocks_per_step),
        grid=(bsz, 3 * steps),
        in_specs=[
            pl.BlockSpec((None, rows, width, f_w), lambda b, t: (b, 0, stage_a_step(b, t), 0)),
            full(fa), full(g), full(cs), full(perm), tw_spec, tw_spec,
        ],
        out_specs=pl.BlockSpec((None, width * GRID_W, f_w), lambda b, t: (b, stage_c_step(b, t), 0)),
        out_shape=jax.ShapeDtypeStruct((bsz, s, f_w), BF16),
        scratch_shapes=[pltpu.VMEM((rows, GRID_W, f_w), F32), pltpu.VMEM((rows, GRID_W, f_w), F32)],
        compiler_params=_params("arbitrary", "arbitrary"),
        name="fourier",
    )(u4, fa, g, cs, perm, tw_cos, tw_sin)


def _bias_tables(rpb, rows):
    kw = min(WIN_W, GRID_W)
    cols = np.arange(GRID_W)
    col_start = np.clip(cols - kw // 2, 0, GRID_W - kw)
    kj = cols[None, :]
    allowed = (kj >= col_start[:, None]) & (kj < col_start[:, None] + kw)
    rel = kj - cols[:, None] + (WIN_W - 1)
    nrel = 2 * WIN_W - 1
    onehot = ((rel[None] == np.arange(nrel)[:, None, None]) & allowed[None]).astype(np.float32)
    sel = np.zeros((2 * nrel + 1, GRID_W, 2 * GRID_W), np.float32)
    sel[0:nrel, :, 0:GRID_W] = onehot
    sel[nrel:2 * nrel, :, GRID_W:] = onehot
    sel[2 * nrel] = np.tile(~allowed, (1, 2))
    rpb = rpb.astype(F32) * LOG2_E
    mask_col = jnp.full(rpb.shape[:1] + (rpb.shape[1] - 1, 1), MASK_VALUE, F32)
    src = jnp.concatenate([rpb[:, :-1], rpb[:, 1:], mask_col], axis=-1)
    return jnp.einsum("hmd,dcx->hmcx", src, jnp.asarray(sel), precision=lax.Precision.HIGHEST)


def _attn_out_kernel(q_ref, k_ref, v_ref, t2_ref, yfn_ref, ga_ref, gb_ref, x_ref,
                     wna_ref, wfn_ref, bfn_ref, wo_ref, g1_ref, sc2_ref, sh2_ref, nw2_ref,
                     o_ref, h2_ref, yna_ref, s_ref, p_ref,
                     *, rows, kh, rows_per_step):
    t = pl.program_id(0)
    last_tile = pl.num_programs(0) - 2

    @pl.when(t == 0)
    def _init():
        yna_ref[...] = jnp.zeros_like(yna_ref)

    yna_prev = yna_ref[...].astype(BF16)
    yfn_prev = yfn_ref[...]

    rb = lax.rem(jnp.minimum(t, last_tile), rows // rows_per_step)
    n_pairs = q_ref.shape[1] // LANES
    lane = lax.broadcasted_iota(jnp.int32, (1, LANES), 1)
    low = lane < HEAD_DIM
    mask_lo = jnp.where(low, 1.0, 0.0).astype(BF16)
    mask_hi = jnp.where(low, 0.0, 1.0).astype(BF16)
    n_keys = kh * GRID_W

    key_start = []
    for j in range(rows_per_step):
        r = rb * rows_per_step + j
        rs = jnp.clip(r - kh // 2, 0, rows - kh)
        key_start.append((pl.multiple_of(rs * GRID_W, GRID_W), rs - r + (WIN_H - 1)))

    def qk_tile(ti):
        j, p = divmod(ti, n_pairs)
        ks, idx0 = key_start[j]
        cs = slice(LANES * p, LANES * (p + 1))
        q2 = q_ref[j * GRID_W:(j + 1) * GRID_W, cs]
        k2 = k_ref[pl.ds(ks, n_keys), cs]
        qs = jnp.concatenate([q2 * mask_lo, q2 * mask_hi], axis=0)
        s = lax.dot_general(qs, k2, (((1,), (1,)), ((), ())), preferred_element_type=F32)
        bias = jnp.concatenate(
            [jnp.concatenate([t2_ref[2 * p + hh, idx0 + 2 * m] for m in range(kh // 2)], axis=1)
             for hh in range(2)], axis=0)
        s_ref[ti] = s + bias

    def softmax_tile(ti):
        s = s_ref[ti]
        p_ref[ti] = jnp.exp2(s - jnp.max(s, axis=-1, keepdims=True)).astype(BF16)

    ones = jnp.ones((n_keys, LANES), BF16)
    row_outs = []

    def pv_tile(ti):
        j, p = divmod(ti, n_pairs)
        ks, _ = key_start[j]
        v2 = v_ref[pl.ds(ks, n_keys), LANES * p:LANES * (p + 1)]
        o = _dot(p_ref[ti], jnp.concatenate([v2, ones], axis=1))
        o = o[:, 0:LANES] * (1.0 / o[:, LANES:2 * LANES])
        row_outs.append(jnp.where(low, o[0:GRID_W], o[GRID_W:2 * GRID_W]))
        if p == n_pairs - 1:
            yna_ref[j * GRID_W:(j + 1) * GRID_W, :] = jnp.concatenate(row_outs, axis=1)
            row_outs.clear()

    d = o_ref.shape[1]
    merged = []

    def merge_chunk(c0):
        cols = slice(c0, c0 + MXU_DIM)
        ya = _dot(yna_prev, wna_ref[:, cols])
        yf = _dot(yfn_prev, wfn_ref[:, cols]) + bfn_ref[:, cols]
        merged.append((ga_ref[:, cols].astype(F32) * ya
                       + gb_ref[:, cols].astype(F32) * yf).astype(BF16))

    def out_chunk(c0):
        cols = slice(c0, c0 + MXU_DIM)
        lhs = jnp.concatenate(merged, axis=1)
        o_ref[:, cols] = x_ref[:, cols] + g1_ref[:, cols] * _dot(lhs, wo_ref[:, cols])

    norm_rows = o_ref.shape[0] // rows_per_step
    gain2 = nw2_ref[...] * (1.0 + sc2_ref[...])

    def norm_chunk(j):
        nr = slice(j * norm_rows, (j + 1) * norm_rows)
        x1 = o_ref[nr, :]
        ms = jnp.mean(x1 * x1, axis=-1, keepdims=True)
        h2_ref[nr, :] = (x1 * lax.rsqrt(ms + EPS) * gain2 + sh2_ref[...]).astype(BF16)

    matmul_units = ([functools.partial(merge_chunk, c0) for c0 in range(0, d, MXU_DIM)]
                    + [functools.partial(out_chunk, c0) for c0 in range(0, d, MXU_DIM)])
    norm_units = [functools.partial(norm_chunk, j) for j in range(rows_per_step)]

    n_tiles = rows_per_step * n_pairs
    for u in range(n_tiles + PV_SKEW):
        if u < n_tiles:
            qk_tile(u)
        if 0 <= u - SOFTMAX_SKEW < n_tiles:
            softmax_tile(u - SOFTMAX_SKEW)
        if 0 <= u - PV_SKEW < n_tiles:
            pv_tile(u - PV_SKEW)
        if matmul_units:
            if u % OUT_UNIT_EVERY == 1:
                matmul_units.pop(0)()
        elif norm_units and u % NORM_UNIT_EVERY == 0:
            norm_units.pop(0)()
    assert not matmul_units
    while norm_units:
        norm_units.pop(0)()


def _attn_out(q, k, v, t2, yfn, ga, gb, x, w_na, w_fn, b_fn, w_o, gate1, scale2, shift2, norm2_w):
    bsz, s, d = x.shape
    na_w = q.shape[-1]
    f_w = yfn.shape[-1]
    rows = s // GRID_W
    kh = min(WIN_H, rows)
    assert kh % 2 == 0 and (2 * HEAD_DIM) == LANES
    rows_per_step = ATTN_ROWS_PER_STEP
    tq = rows_per_step * GRID_W
    n_tiles = rows_per_step * (na_w // LANES)

    per_seq = s // tq
    n_steps = bsz * per_seq + 1
    attn_tile = lambda t: jnp.minimum(t, n_steps - 2)
    out_tile = lambda t: jnp.maximum(t - 1, 0)
    tok_a = lambda w: pl.BlockSpec(
        (None, tq, w), lambda t: (attn_tile(t) // per_seq, attn_tile(t) % per_seq, 0))
    tok_o = lambda w: pl.BlockSpec(
        (None, tq, w), lambda t: (out_tile(t) // per_seq, out_tile(t) % per_seq, 0))
    whole_seq = pl.BlockSpec((None, s, na_w), lambda t: (attn_tile(t) // per_seq, 0, 0))
    per_batch = pl.BlockSpec((None, 1, d), lambda t: (out_tile(t) // per_seq, 0, 0))
    full = lambda a: pl.BlockSpec(a.shape, lambda t: (0,) * a.ndim)
    bfn = b_fn.reshape(1, d)
    nw2 = norm2_w.reshape(1, d)
    return pl.pallas_call(
        functools.partial(_attn_out_kernel, rows=rows, kh=kh, rows_per_step=rows_per_step),
        grid=(n_steps,),
        in_specs=[tok_a(na_w), whole_seq, whole_seq, full(t2), tok_o(f_w), tok_o(d), tok_o(d),
                  tok_o(d), full(w_na), full(w_fn), full(bfn), full(w_o), per_batch, per_batch,
                  per_batch, full(nw2)],
        out_specs=(tok_o(d), tok_o(d)),
        out_shape=(jax.ShapeDtypeStruct((bsz, s, d), F32), jax.ShapeDtypeStruct((bsz, s, d), BF16)),
        scratch_shapes=[pltpu.VMEM((tq, na_w), F32),
                        pltpu.VMEM((n_tiles, 2 * GRID_W, kh * GRID_W), F32),
                        pltpu.VMEM((n_tiles, 2 * GRID_W, kh * GRID_W), BF16)],
        compiler_params=_params("arbitrary"),
        name="attn_out",
    )(q, k, v, t2, yfn, ga, gb, x, w_na, w_fn, bfn, w_o, gate1, scale2, shift2, nw2)


def _mlp_kernel(x_ref, h_ref, g2_ref, w1_ref, w2_ref, o_ref, *, chunk):
    h = h_ref[...]
    acc = None
    for c0 in range(0, w1_ref.shape[1], chunk):
        a = jnp.maximum(_dot(h, w1_ref[:, c0:c0 + chunk]), 0.0)
        term = _dot((a * a).astype(BF16), w2_ref[c0:c0 + chunk, :])
        acc = term if acc is None else acc + term
    o_ref[...] = x_ref[...] + g2_ref[...] * acc


def _mlp(x, h, gate2, w1, w2):
    bsz, s, d = x.shape
    tm = MLP_TOKENS
    tok = pl.BlockSpec((None, tm, d), lambda b, i: (b, i, 0))
    per_batch = pl.BlockSpec((None, 1, d), lambda b, i: (b, 0, 0))
    full = lambda a: pl.BlockSpec(a.shape, lambda b, i: (0,) * a.ndim)
    return pl.pallas_call(
        functools.partial(_mlp_kernel, chunk=MLP_FF_CHUNK),
        grid=(bsz, s // tm),
        in_specs=[tok, tok, per_batch, full(w1), full(w2)],
        out_specs=tok,
        out_shape=jax.ShapeDtypeStruct((bsz, s, d), F32),
        compiler_params=_params("parallel", "parallel"),
        name="mlp",
    )(x, h, gate2, w1, w2)


def kernel(x, c, norm1_w, norm2_w, w_ada, b_ada, w_in, b_in, q_norm_w, k_norm_w, rpb,
           w_na_out, w_fn_out, b_fn_out, w_o, w_mlp_in, w_mlp_out):
    bsz, s, d = x.shape
    depth = w_ada.shape[0]
    na_w = w_na_out.shape[1]
    f_w = w_fn_out.shape[1]
    n_heads = na_w // HEAD_DIM
    rows = s // GRID_W
    for l in range(depth):
        mod = _ada(c, w_ada[l], b_ada[l])
        shift1, scale1, gate1, shift2, scale2, gate2 = [
            mod[:, i * d:(i + 1) * d].reshape(bsz, 1, d) for i in range(N_MOD)]

        qk_w = jnp.concatenate([jnp.tile(q_norm_w[l], n_heads) * (HEAD_DIM ** -0.5 * LOG2_E),
                                jnp.tile(k_norm_w[l], n_heads)]).reshape(1, 2 * na_w)
        (q, k, v, u, ga, gb), (w_na, w_fn, w_o_b, w_m1, w_m2) = _inproj(
            x, scale1, shift1, norm1_w[l], w_in[l].astype(BF16), b_in[l], qk_w, na_w, f_w,
            (w_na_out[l], w_fn_out[l], w_o[l], w_mlp_in[l], w_mlp_out[l]))
        yfn = _fourier(u)
        t2 = _bias_tables(rpb[l], rows)
        x, h2 = _attn_out(q, k, v, t2, yfn, ga, gb, x, w_na, w_fn, b_fn_out[l], w_o_b, gate1,
                          scale2, shift2, norm2_w[l])
        x = _mlp(x, h2, gate2, w_m1, w_m2)
    return x
```

```python
import functools

import numpy as np
import jax
import jax.numpy as jnp
from jax import lax
from jax.experimental import pallas as pl
from jax.experimental.pallas import tpu as pltpu

F32 = jnp.float32
BF16 = jnp.bfloat16

GRID_W = 64
HEAD_DIM = 64
WIN_H = 8
WIN_W = 16
N_FOURIER_GROUPS = 4
N_MOD = 6
EPS = 1e-6
MASK_VALUE = -1e30
LOG2_E = 1.4426950408889634

SUBLANES = 8
LANES = 128
MXU_DIM = 256
SOFTMAX_SKEW = 3
PV_SKEW = 6
OUT_UNIT_EVERY = 4
NORM_UNIT_EVERY = 1
VMEM_LIMIT_BYTES = 56 * 1024 * 1024

INPROJ_TOKENS = 1024
INPROJ_CHUNK = 1024
FOURIER_BLOCKS_PER_STEP = 8
ATTN_ROWS_PER_STEP = 8
MLP_TOKENS = 1024
MLP_FF_CHUNK = 1024


def _dot(a, b):
    return jnp.dot(a, b, preferred_element_type=F32)


def _sigmoid(x):
    return 0.5 * jnp.tanh(0.5 * x) + 0.5


def _params(*semantics):
    return pltpu.CompilerParams(dimension_semantics=semantics, vmem_limit_bytes=VMEM_LIMIT_BYTES)


def _ada_kernel(ct_ref, w_ref, b_ref, o_ref):
    ct = ct_ref[...]
    st = ct * _sigmoid(ct)
    w = w_ref[...]
    rows = [jnp.sum(w * st[:, b:b + 1], axis=0, keepdims=True) for b in range(ct.shape[1])]
    o_ref[...] = jnp.concatenate(rows, axis=0) + b_ref[...]


def _ada(c, w, b):
    bsz, d = c.shape
    n = w.shape[1]
    tn = n // N_MOD
    return pl.pallas_call(
        _ada_kernel,
        grid=(n // tn,),
        in_specs=[pl.BlockSpec((d, bsz), lambda j: (0, 0)),
                  pl.BlockSpec((d, tn), lambda j: (0, j)),
                  pl.BlockSpec((1, tn), lambda j: (0, j))],
        out_specs=pl.BlockSpec((bsz, tn), lambda j: (0, j)),
        out_shape=jax.ShapeDtypeStruct((bsz, n), F32),
        compiler_params=_params("parallel"),
        name="ada",
    )(c.T, w, b.reshape(1, n))


def _inproj_kernel(x_ref, sc_ref, sh_ref, nw_ref, w_ref, b_ref, qkw_ref, gsum_ref, gexp_ref, *refs,
                   na_w, f_w, d, chunk, n_cast):
    cast_in = refs[:n_cast]
    q_ref, k_ref, v_ref, u_ref, ga_ref, gb_ref = refs[n_cast:n_cast + 6]
    cast_out = refs[n_cast + 6:]
    for src, dst in zip(cast_in, cast_out):
        dst[...] = src[...].astype(BF16)

    gain = nw_ref[...] * (1.0 + sc_ref[...])

    def normed(c0):
        x = x_ref[c0:c0 + chunk, :]
        ms = jnp.mean(x * x, axis=-1, keepdims=True)
        return (x * lax.rsqrt(ms + EPS) * gain + sh_ref[...]).astype(BF16)

    def proj(h, lo, width):
        return _dot(h, w_ref[:, lo:lo + width]) + b_ref[:, lo:lo + width]

    qk = 2 * na_w
    o_v, o_u, o_ga, o_gb = qk, qk + na_w, qk + na_w + f_w, qk + na_w + f_w + d
    n_rows = x_ref.shape[0]
    h = normed(0)
    for c0 in range(0, n_rows, chunk):
        rows = slice(c0, c0 + chunk)
        zqk = proj(h, 0, qk)
        h_next = normed(c0 + chunk) if c0 + chunk < n_rows else None
        v_ref[rows, :] = proj(h, o_v, na_w).astype(BF16)
        ssq = _dot((zqk * zqk).astype(BF16), gsum_ref[...])
        u_ref[rows, :] = proj(h, o_u, f_w)
        rinv = lax.rsqrt(ssq * (1.0 / HEAD_DIM) + EPS)
        rinv_hi = rinv.astype(BF16)
        rinv_lo = (rinv - rinv_hi.astype(F32)).astype(BF16)
        scale = _dot(jnp.concatenate([rinv_hi, rinv_lo], axis=1), gexp_ref[...])
        ga_ref[rows, :] = _sigmoid(proj(h, o_ga, d)).astype(BF16)
        zn = zqk * scale * qkw_ref[...]
        q_ref[rows, :] = zn[:, 0:na_w].astype(BF16)
        k_ref[rows, :] = zn[:, na_w:qk].astype(BF16)
        gb_ref[rows, :] = _sigmoid(proj(h, o_gb, d)).astype(BF16)
        h = h_next


def _inproj(x, scale1, shift1, norm_w, w_in, b_in, qk_w, na_w, f_w, later_weights):
    bsz, s, d = x.shape
    in_w = w_in.shape[1]
    tm = INPROJ_TOKENS
    qk = 2 * na_w
    n_grp = qk // HEAD_DIM
    grp = np.arange(qk) // HEAD_DIM
    gsum = jnp.asarray((grp[:, None] == np.arange(LANES)[None, :]).astype(np.float32), BF16)
    gexp1 = (np.arange(LANES)[:, None] == grp[None, :]).astype(np.float32)
    gexp = jnp.asarray(np.concatenate([gexp1, gexp1], axis=0), BF16)
    assert n_grp <= LANES

    tok = lambda w: pl.BlockSpec((None, tm, w), lambda b, i: (b, i, 0))
    per_batch = pl.BlockSpec((None, 1, d), lambda b, i: (b, 0, 0))
    full = lambda a: pl.BlockSpec(a.shape, lambda b, i: (0,) * a.ndim)
    nw = norm_w.reshape(1, d)
    bi = b_in.reshape(1, in_w)
    out_shapes = (
        jax.ShapeDtypeStruct((bsz, s, na_w), BF16),
        jax.ShapeDtypeStruct((bsz, s, na_w), BF16),
        jax.ShapeDtypeStruct((bsz, s, na_w), BF16),
        jax.ShapeDtypeStruct((bsz, s, f_w), F32),
        jax.ShapeDtypeStruct((bsz, s, d), BF16),
        jax.ShapeDtypeStruct((bsz, s, d), BF16),
    )
    n_steps = bsz * (s // tm)
    slab = lambda a: pl.BlockSpec((a.shape[0] // n_steps, a.shape[1]),
                                  lambda b, i: (b * (s // tm) + i, 0))
    for a in later_weights:
        assert a.shape[0] % (n_steps * 2 * SUBLANES) == 0, a.shape
    outs = pl.pallas_call(
        functools.partial(_inproj_kernel, na_w=na_w, f_w=f_w, d=d, chunk=INPROJ_CHUNK,
                          n_cast=len(later_weights)),
        grid=(bsz, s // tm),
        in_specs=[tok(d), per_batch, per_batch, full(nw), full(w_in), full(bi), full(qk_w),
                  full(gsum), full(gexp)] + [slab(a) for a in later_weights],
        out_specs=(tok(na_w), tok(na_w), tok(na_w), tok(f_w), tok(d), tok(d))
        + tuple(slab(a) for a in later_weights),
        out_shape=out_shapes + tuple(jax.ShapeDtypeStruct(a.shape, BF16) for a in later_weights),
        compiler_params=_params("parallel", "parallel"),
        name="inproj",
    )(x, scale1, shift1, nw, w_in, bi, qk_w, gsum, gexp, *later_weights)
    return outs[:6], outs[6:]


def _fourier_constants(rows, f_w):
    n = rows
    r8 = SUBLANES
    k = np.arange(n)
    ang_a = 2.0 * np.pi * np.outer(k, k) / n
    eye8 = np.eye(r8)
    norm = 1.0 / np.sqrt(n)
    half = n // 2 + 1
    fa_re = np.kron(np.cos(ang_a[:half]), eye8) * norm
    fa_im = np.kron(-np.sin(ang_a[:half]), eye8) * norm
    fa = np.concatenate([fa_re, fa_im], axis=0)

    nblk = n // r8
    s2 = k[None, :]
    s2p = k[:, None]
    g = np.zeros((r8 // 2, 4 * n, 4 * n))
    for pp in range(r8 // 2):
        gr = np.zeros((2, n, 2, n))
        gi = np.zeros((2, n, 2, n))
        for ll in range(2):
            l = 2 * pp + ll
            ang = 2.0 * np.pi * (s2 * l / (n * GRID_W) + s2 * s2p / n)
            gr[ll, :, ll, :] = np.cos(ang) * norm
            gi[ll, :, ll, :] = -np.sin(ang) * norm
        gr = gr.reshape(2 * n, 2 * n)
        gi = gi.reshape(2 * n, 2 * n)
        g[pp] = np.block([[gr, -gi], [gi, gr]])

    perm = np.zeros((n * r8, n * r8))
    s1p = np.repeat(k, r8)
    jj = np.tile(np.arange(r8), n)
    perm[jj * n + s1p, s1p * r8 + jj] = 1.0

    t_idx = np.arange(nblk)[:, None, None]
    blk_of_row = (k // r8)[None, :, None]
    j_idx = np.arange(r8)[None, None, :]
    ang_t = 2.0 * np.pi * (t_idx * r8 + j_idx) * blk_of_row * r8 / (n * GRID_W)
    ang_t = ang_t.reshape(nblk, n * r8, 1)
    tw_cos = np.broadcast_to(np.cos(ang_t), (nblk, n * r8, LANES))
    tw_sin = np.broadcast_to(np.sin(ang_t), (nblk, n * r8, LANES))

    gd = f_w // N_FOURIER_GROUPS
    c = np.arange(gd)
    ang_c = 2.0 * np.pi * np.outer(c, c) / gd
    assert MXU_DIM % gd == 0 and f_w % MXU_DIM == 0
    eye_g = np.eye(MXU_DIM // gd)
    cs = np.concatenate([np.kron(eye_g, np.cos(ang_c)), np.kron(eye_g, np.sin(ang_c))],
                        axis=0) / np.sqrt(gd)
    as_f32 = lambda a: jnp.asarray(np.ascontiguousarray(a, dtype=np.float32))
    to_bf16 = lambda a: as_f32(a).astype(BF16)
    return (to_bf16(fa), to_bf16(g), to_bf16(cs), to_bf16(perm), as_f32(tw_cos), as_f32(tw_sin))


def _fourier_kernel(u_ref, fa_ref, g_ref, cs_ref, perm_ref, twc_ref, tws_ref, y_ref,
                    sre_ref, sim_ref, *, steps_per_stage, blocks_per_step):
    t = pl.program_id(1)
    n, _, f_w = u_ref.shape
    r8 = SUBLANES
    m = n * r8

    @pl.when(t < steps_per_stage)
    def _stage_a():
        for j in range(blocks_per_step):
            xin = u_ref[:, j * r8:(j + 1) * r8, :].reshape(m, f_w).astype(BF16)
            a = _dot(fa_ref[...], xin)
            reps = f_w // LANES
            cos_t = jnp.concatenate([twc_ref[j]] * reps, axis=1)
            sin_t = jnp.concatenate([tws_ref[j]] * reps, axis=1)
            mh = (n // 2 + 1) * r8
            mirror = lambda z: [z[(n - s1) * r8:(n - s1 + 1) * r8] for s1 in range(n // 2 + 1, n)]
            a_re = jnp.concatenate([a[0:mh]] + mirror(a[0:mh]), axis=0)
            a_im = jnp.concatenate([a[mh:2 * mh]] + [-z for z in mirror(a[mh:2 * mh])], axis=0)
            b_re = a_re * cos_t + a_im * sin_t
            b_im = a_im * cos_t - a_re * sin_t
            off = pl.multiple_of((t * blocks_per_step + j) * r8, r8)
            sre_ref[:, pl.ds(off, r8), :] = b_re.reshape(n, r8, f_w)
            sim_ref[:, pl.ds(off, r8), :] = b_im.reshape(n, r8, f_w)

    @pl.when((t >= steps_per_stage) & (t < 2 * steps_per_stage))
    def _stage_b():
        pair_rows = 2 * n
        for j in range(blocks_per_step):
            off = pl.multiple_of(((t - steps_per_stage) * blocks_per_step + j) * r8, r8)
            br = sre_ref[pl.ds(off, r8)].reshape(m, f_w)
            bi = sim_ref[pl.ds(off, r8)].reshape(m, f_w)
            p_re, p_im = [], []
            for pp in range(r8 // 2):
                rs_ = slice(pp * pair_rows, (pp + 1) * pair_rows)
                bcat = jnp.concatenate([br[rs_], bi[rs_]], axis=0).astype(BF16)
                p = _dot(g_ref[pp], bcat)
                p_re.append(p[0:pair_rows])
                p_im.append(p[pair_rows:2 * pair_rows])
            p_re = jnp.concatenate(p_re, axis=0).astype(BF16)
            p_im = jnp.concatenate(p_im, axis=0).astype(BF16)
            y = jnp.concatenate(
                [_dot(jnp.concatenate([p_re[:, c0:c0 + MXU_DIM], p_im[:, c0:c0 + MXU_DIM]], axis=1),
                      cs_ref[...]) for c0 in range(0, f_w, MXU_DIM)], axis=1)
            sre_ref[pl.ds(off, r8)] = y.reshape(r8, n, f_w)

    @pl.when(t >= 2 * steps_per_stage)
    def _stage_c():
        for j in range(blocks_per_step):
            off = pl.multiple_of(((t - 2 * steps_per_stage) * blocks_per_step + j) * r8, r8)
            yin = sre_ref[:, pl.ds(off, r8), :].reshape(m, f_w).astype(BF16)
            y_ref[j * m:(j + 1) * m, :] = _dot(perm_ref[...], yin).astype(y_ref.dtype)


def _fourier(u):
    bsz, s, f_w = u.shape
    rows = s // GRID_W
    assert rows == GRID_W, "the two-stage position DFT assumes a square token grid"
    r8 = SUBLANES
    nblk = rows // r8
    fa, g, cs, perm, tw_cos, tw_sin = _fourier_constants(rows, f_w)
    u4 = u.reshape(bsz, rows, GRID_W, f_w)
    blocks_per_step = FOURIER_BLOCKS_PER_STEP
    steps = nblk // blocks_per_step
    width = blocks_per_step * r8
    stage_a_step = lambda b, t: jnp.minimum(t, steps - 1)
    stage_c_step = lambda b, t: jnp.maximum(t - 2 * steps, 0)
    full = lambda a: pl.BlockSpec(a.shape, lambda b, t: (0,) * a.ndim)
    tw_spec = pl.BlockSpec((blocks_per_step,) + tw_cos.shape[1:],
                           lambda b, t: (stage_a_step(b, t), 0, 0))
    return pl.pallas_call(
        functools.partial(_fourier_kernel, steps_per_stage=steps, blocks_per_step=blocks_per_step),
        grid=(bsz, 3 * steps),
        in_specs=[
            pl.BlockSpec((None, rows, width, f_w), lambda b, t: (b, 0, stage_a_step(b, t), 0)),
            full(fa), full(g), full(cs), full(perm), tw_spec, tw_spec,
        ],
        out_specs=pl.BlockSpec((None, width * GRID_W, f_w), lambda b, t: (b, stage_c_step(b, t), 0)),
        out_shape=jax.ShapeDtypeStruct((bsz, s, f_w), BF16),
        scratch_shapes=[pltpu.VMEM((rows, GRID_W, f_w), F32), pltpu.VMEM((rows, GRID_W, f_w), F32)],
        compiler_params=_params("arbitrary", "arbitrary"),
        name="fourier",
    )(u4, fa, g, cs, perm, tw_cos, tw_sin)


def _bias_tables(rpb, rows):
    kw = min(WIN_W, GRID_W)
    cols = np.arange(GRID_W)
    col_start = np.clip(cols - kw // 2, 0, GRID_W - kw)
    kj = cols[None, :]
    allowed = (kj >= col_start[:, None]) & (kj < col_start[:, None] + kw)
    rel = kj - cols[:, None] + (WIN_W - 1)
    nrel = 2 * WIN_W - 1
    onehot = ((rel[None] == np.arange(nrel)[:, None, None]) & allowed[None]).astype(np.float32)
    sel = np.zeros((2 * nrel + 1, GRID_W, 2 * GRID_W), np.float32)
    sel[0:nrel, :, 0:GRID_W] = onehot
    sel[nrel:2 * nrel, :, GRID_W:] = onehot
    sel[2 * nrel] = np.tile(~allowed, (1, 2))
    rpb = rpb.astype(F32) * LOG2_E
    mask_col = jnp.full(rpb.shape[:1] + (rpb.shape[1] - 1, 1), MASK_VALUE, F32)
    src = jnp.concatenate([rpb[:, :-1], rpb[:, 1:], mask_col], axis=-1)
    return jnp.einsum("hmd,dcx->hmcx", src, jnp.asarray(sel), precision=lax.Precision.HIGHEST)


def _attn_out_kernel(q_ref, k_ref, v_ref, t2_ref, yfn_ref, ga_ref, gb_ref, x_ref,
                     wna_ref, wfn_ref, bfn_ref, wo_ref, g1_ref, sc2_ref, sh2_ref, nw2_ref,
                     o_ref, h2_ref, yna_ref, s_ref, p_ref,
                     *, rows, kh, rows_per_step):
    t = pl.program_id(0)
    last_tile = pl.num_programs(0) - 2

    @pl.when(t == 0)
    def _init():
        yna_ref[...] = jnp.zeros_like(yna_ref)

    yna_prev = yna_ref[...].astype(BF16)
    yfn_prev = yfn_ref[...]

    rb = lax.rem(jnp.minimum(t, last_tile), rows // rows_per_step)
    n_pairs = q_ref.shape[1] // LANES
    lane = lax.broadcasted_iota(jnp.int32, (1, LANES), 1)
    low = lane < HEAD_DIM
    mask_lo = jnp.where(low, 1.0, 0.0).astype(BF16)
    mask_hi = jnp.where(low, 0.0, 1.0).astype(BF16)
    n_keys = kh * GRID_W

    key_start = []
    for j in range(rows_per_step):
        r = rb * rows_per_step + j
        rs = jnp.clip(r - kh // 2, 0, rows - kh)
        key_start.append((pl.multiple_of(rs * GRID_W, GRID_W), rs - r + (WIN_H - 1)))

    def qk_tile(ti):
        j, p = divmod(ti, n_pairs)
        ks, idx0 = key_start[j]
        cs = slice(LANES * p, LANES * (p + 1))
        q2 = q_ref[j * GRID_W:(j + 1) * GRID_W, cs]
        k2 = k_ref[pl.ds(ks, n_keys), cs]
        qs = jnp.concatenate([q2 * mask_lo, q2 * mask_hi], axis=0)
        s = lax.dot_general(qs, k2, (((1,), (1,)), ((), ())), preferred_element_type=F32)
        bias = jnp.concatenate(
            [jnp.concatenate([t2_ref[2 * p + hh, idx0 + 2 * m] for m in range(kh // 2)], axis=1)
             for hh in range(2)], axis=0)
        s_ref[ti] = s + bias

    def softmax_tile(ti):
        s = s_ref[ti]
        p_ref[ti] = jnp.exp2(s - jnp.max(s, axis=-1, keepdims=True)).astype(BF16)

    ones = jnp.ones((n_keys, LANES), BF16)
    row_outs = []

    def pv_tile(ti):
        j, p = divmod(ti, n_pairs)
        ks, _ = key_start[j]
        v2 = v_ref[pl.ds(ks, n_keys), LANES * p:LANES * (p + 1)]
        o = _dot(p_ref[ti], jnp.concatenate([v2, ones], axis=1))
        o = o[:, 0:LANES] * (1.0 / o[:, LANES:2 * LANES])
        row_outs.append(jnp.where(low, o[0:GRID_W], o[GRID_W:2 * GRID_W]))
        if p == n_pairs - 1:
            yna_ref[j * GRID_W:(j + 1) * GRID_W, :] = jnp.concatenate(row_outs, axis=1)
            row_outs.clear()

    d = o_ref.shape[1]
    merged = []

    def merge_chunk(c0):
        cols = slice(c0, c0 + MXU_DIM)
        ya = _dot(yna_prev, wna_ref[:, cols])
        yf = _dot(yfn_prev, wfn_ref[:, cols]) + bfn_ref[:, cols]
        merged.append((ga_ref[:, cols].astype(F32) * ya
                       + gb_ref[:, cols].astype(F32) * yf).astype(BF16))

    def out_chunk(c0):
        cols = slice(c0, c0 + MXU_DIM)
        lhs = jnp.concatenate(merged, axis=1)
        o_ref[:, cols] = x_ref[:, cols] + g1_ref[:, cols] * _dot(lhs, wo_ref[:, cols])

    norm_rows = o_ref.shape[0] // rows_per_step
    gain2 = nw2_ref[...] * (1.0 + sc2_ref[...])

    def norm_chunk(j):
        nr = slice(j * norm_rows, (j + 1) * norm_rows)
        x1 = o_ref[nr, :]
        ms = jnp.mean(x1 * x1, axis=-1, keepdims=True)
        h2_ref[nr, :] = (x1 * lax.rsqrt(ms + EPS) * gain2 + sh2_ref[...]).astype(BF16)

    matmul_units = ([functools.partial(merge_chunk, c0) for c0 in range(0, d, MXU_DIM)]
                    + [functools.partial(out_chunk, c0) for c0 in range(0, d, MXU_DIM)])
    norm_units = [functools.partial(norm_chunk, j) for j in range(rows_per_step)]

    n_tiles = rows_per_step * n_pairs
    for u in range(n_tiles + PV_SKEW):
        if u < n_tiles:
            qk_tile(u)
        if 0 <= u - SOFTMAX_SKEW < n_tiles:
            softmax_tile(u - SOFTMAX_SKEW)
        if 0 <= u - PV_SKEW < n_tiles:
            pv_tile(u - PV_SKEW)
        if matmul_units:
            if u % OUT_UNIT_EVERY == 1:
                matmul_units.pop(0)()
        elif norm_units and u % NORM_UNIT_EVERY == 0:
            norm_units.pop(0)()
    assert not matmul_units
    while norm_units:
        norm_units.pop(0)()


def _attn_out(q, k, v, t2, yfn, ga, gb, x, w_na, w_fn, b_fn, w_o, gate1, scale2, shift2, norm2_w):
    bsz, s, d = x.shape
    na_w = q.shape[-1]
    f_w = yfn.shape[-1]
    rows = s // GRID_W
    kh = min(WIN_H, rows)
    assert kh % 2 == 0 and (2 * HEAD_DIM) == LANES
    rows_per_step = ATTN_ROWS_PER_STEP
    tq = rows_per_step * GRID_W
    n_tiles = rows_per_step * (na_w // LANES)

    per_seq = s // tq
    n_steps = bsz * per_seq + 1
    attn_tile = lambda t: jnp.minimum(t, n_steps - 2)
    out_tile = lambda t: jnp.maximum(t - 1, 0)
    tok_a = lambda w: pl.BlockSpec(
        (None, tq, w), lambda t: (attn_tile(t) // per_seq, attn_tile(t) % per_seq, 0))
    tok_o = lambda w: pl.BlockSpec(
        (None, tq, w), lambda t: (out_tile(t) // per_seq, out_tile(t) % per_seq, 0))
    whole_seq = pl.BlockSpec((None, s, na_w), lambda t: (attn_tile(t) // per_seq, 0, 0))
    per_batch = pl.BlockSpec((None, 1, d), lambda t: (out_tile(t) // per_seq, 0, 0))
    full = lambda a: pl.BlockSpec(a.shape, lambda t: (0,) * a.ndim)
    bfn = b_fn.reshape(1, d)
    nw2 = norm2_w.reshape(1, d)
    return pl.pallas_call(
        functools.partial(_attn_out_kernel, rows=rows, kh=kh, rows_per_step=rows_per_step),
        grid=(n_steps,),
        in_specs=[tok_a(na_w), whole_seq, whole_seq, full(t2), tok_o(f_w), tok_o(d), tok_o(d),
                  tok_o(d), full(w_na), full(w_fn), full(bfn), full(w_o), per_batch, per_batch,
                  per_batch, full(nw2)],
        out_specs=(tok_o(d), tok_o(d)),
        out_shape=(jax.ShapeDtypeStruct((bsz, s, d), F32), jax.ShapeDtypeStruct((bsz, s, d), BF16)),
        scratch_shapes=[pltpu.VMEM((tq, na_w), F32),
                        pltpu.VMEM((n_tiles, 2 * GRID_W, kh * GRID_W), F32),
                        pltpu.VMEM((n_tiles, 2 * GRID_W, kh * GRID_W), BF16)],
        compiler_params=_params("arbitrary"),
        name="attn_out",
    )(q, k, v, t2, yfn, ga, gb, x, w_na, w_fn, bfn, w_o, gate1, scale2, shift2, nw2)


def _mlp_kernel(x_ref, h_ref, g2_ref, w1_ref, w2_ref, o_ref, *, chunk):
    h = h_ref[...]
    acc = None
    for c0 in range(0, w1_ref.shape[1], chunk):
        a = jnp.maximum(_dot(h, w1_ref[:, c0:c0 + chunk]), 0.0)
        term = _dot((a * a).astype(BF16), w2_ref[c0:c0 + chunk, :])
        acc = term if acc is None else acc + term
    o_ref[...] = x_ref[...] + g2_ref[...] * acc


def _mlp(x, h, gate2, w1, w2):
    bsz, s, d = x.shape
    tm = MLP_TOKENS
    tok = pl.BlockSpec((None, tm, d), lambda b, i: (b, i, 0))
    per_batch = pl.BlockSpec((None, 1, d), lambda b, i: (b, 0, 0))
    full = lambda a: pl.BlockSpec(a.shape, lambda b, i: (0,) * a.ndim)
    return pl.pallas_call(
        functools.partial(_mlp_kernel, chunk=MLP_FF_CHUNK),
        grid=(bsz, s // tm),
        in_specs=[tok, tok, per_batch, full(w1), full(w2)],
        out_specs=tok,
        out_shape=jax.ShapeDtypeStruct((bsz, s, d), F32),
        compiler_params=_params("parallel", "parallel"),
        name="mlp",
    )(x, h, gate2, w1, w2)


def kernel(x, c, norm1_w, norm2_w, w_ada, b_ada, w_in, b_in, q_norm_w, k_norm_w, rpb,
           w_na_out, w_fn_out, b_fn_out, w_o, w_mlp_in, w_mlp_out):
    bsz, s, d = x.shape
    depth = w_ada.shape[0]
    na_w = w_na_out.shape[1]
    f_w = w_fn_out.shape[1]
    n_heads = na_w // HEAD_DIM
    rows = s // GRID_W
    for l in range(depth):
        mod = _ada(c, w_ada[l], b_ada[l])
        shift1, scale1, gate1, shift2, scale2, gate2 = [
            mod[:, i * d:(i + 1) * d].reshape(bsz, 1, d) for i in range(N_MOD)]

        qk_w = jnp.concatenate([jnp.tile(q_norm_w[l], n_heads) * (HEAD_DIM ** -0.5 * LOG2_E),
                                jnp.tile(k_norm_w[l], n_heads)]).reshape(1, 2 * na_w)
        (q, k, v, u, ga, gb), (w_na, w_fn, w_o_b, w_m1, w_m2) = _inproj(
            x, scale1, shift1, norm1_w[l], w_in[l].astype(BF16), b_in[l], qk_w, na_w, f_w,
            (w_na_out[l], w_fn_out[l], w_o[l], w_mlp_in[l], w_mlp_out[l]))
        yfn = _fourier(u)
        t2 = _bias_tables(rpb[l], rows)
        x, h2 = _attn_out(q, k, v, t2, yfn, ga, gb, x, w_na, w_fn, b_fn_out[l], w_o_b, gate1,
                          scale2, shift2, norm2_w[l])
        x = _mlp(x, h2, gate2, w_m1, w_m2)
    return x
```

```python
import functools

import numpy as np
import jax
import jax.numpy as jnp
from jax import lax
from jax.experimental import pallas as pl
from jax.experimental.pallas import tpu as pltpu

F32 = jnp.float32
BF16 = jnp.bfloat16

GRID_W = 64
HEAD_DIM = 64
WIN_H = 8
WIN_W = 16
N_FOURIER_GROUPS = 4
N_MOD = 6
MOD_SHIFT1, MOD_SCALE1, MOD_GATE1, MOD_SHIFT2, MOD_SCALE2, MOD_GATE2 = range(N_MOD)
EPS = 1e-6
MASK_VALUE = -1e30
LOG2_E = 1.4426950408889634

SUBLANES = 8
LANES = 128
MXU_DIM = 256
SOFTMAX_SKEW = 3
PV_SKEW = 6
OUT_UNIT_EVERY = 4
NORM_UNIT_EVERY = 1
VMEM_LIMIT_BYTES = 56 * 1024 * 1024

ADA_STEPS = 8
INPROJ_TOKENS = 1024
INPROJ_CHUNK = 1024
FOURIER_BLOCKS_PER_STEP = 8
ATTN_ROWS_PER_STEP = 8
MLP_TOKENS = 1024
MLP_FF_CHUNK = 1024


def _dot(a, b):
    return jnp.dot(a, b, preferred_element_type=F32)


def _sigmoid(x):
    return 0.5 * jnp.tanh(0.5 * x) + 0.5


def _params(*semantics):
    return pltpu.CompilerParams(dimension_semantics=semantics, vmem_limit_bytes=VMEM_LIMIT_BYTES)


def _ada_kernel(ct_ref, w_ref, b_ref, win_ref, o_ref, win_bf16_ref):
    ct = ct_ref[...]
    st = ct * _sigmoid(ct)
    w = w_ref[...]
    rows = [jnp.sum(w * st[:, b:b + 1], axis=0, keepdims=True) for b in range(ct.shape[1])]
    o_ref[...] = jnp.concatenate(rows, axis=0) + b_ref[...]
    win_bf16_ref[...] = win_ref[...].astype(BF16)


def _ada(c, w, b, w_in):
    bsz, d = c.shape
    n = w.shape[1]
    n_steps = ADA_STEPS
    tn = n // n_steps
    slab = w_in.shape[0] // n_steps
    assert n % (n_steps * LANES) == 0 and w_in.shape[0] % (n_steps * 2 * SUBLANES) == 0
    return pl.pallas_call(
        _ada_kernel,
        grid=(n_steps,),
        in_specs=[pl.BlockSpec((d, bsz), lambda j: (0, 0)),
                  pl.BlockSpec((d, tn), lambda j: (0, j)),
                  pl.BlockSpec((1, tn), lambda j: (0, j)),
                  pl.BlockSpec((slab, w_in.shape[1]), lambda j: (j, 0))],
        out_specs=(pl.BlockSpec((bsz, tn), lambda j: (0, j)),
                   pl.BlockSpec((slab, w_in.shape[1]), lambda j: (j, 0))),
        out_shape=(jax.ShapeDtypeStruct((bsz, n), F32),
                   jax.ShapeDtypeStruct(w_in.shape, BF16)),
        compiler_params=_params("parallel"),
        name="ada",
    )(c.T, w, b.reshape(1, n), w_in)


def _inproj_kernel(x_ref, sc_ref, sh_ref, nw_ref, w_ref, b_ref, qkw_ref, gsum_ref, gexp_ref, *refs,
                   na_w, f_w, d, chunk, n_cast):
    cast_in = refs[:n_cast]
    q_ref, k_ref, v_ref, u_ref, ga_ref, gb_ref = refs[n_cast:n_cast + 6]
    cast_out = refs[n_cast + 6:]
    for src, dst in zip(cast_in, cast_out):
        dst[...] = src[...].astype(BF16)

    gain = nw_ref[...] * (1.0 + sc_ref[...])

    def normed(c0):
        x = x_ref[c0:c0 + chunk, :]
        ms = jnp.mean(x * x, axis=-1, keepdims=True)
        return (x * lax.rsqrt(ms + EPS) * gain + sh_ref[...]).astype(BF16)

    def proj(h, lo, width):
        return _dot(h, w_ref[:, lo:lo + width]) + b_ref[:, lo:lo + width]

    qk = 2 * na_w
    o_v, o_u, o_ga, o_gb = qk, qk + na_w, qk + na_w + f_w, qk + na_w + f_w + d
    n_rows = x_ref.shape[0]
    h = normed(0)
    for c0 in range(0, n_rows, chunk):
        rows = slice(c0, c0 + chunk)
        zqk = proj(h, 0, qk)
        h_next = normed(c0 + chunk) if c0 + chunk < n_rows else None
        ga_ref[rows, :] = _sigmoid(proj(h, o_ga, d)).astype(BF16)
        ssq = _dot((zqk * zqk).astype(BF16), gsum_ref[...])
        gb_ref[rows, :] = _sigmoid(proj(h, o_gb, d)).astype(BF16)
        rinv = lax.rsqrt(ssq * (1.0 / HEAD_DIM) + EPS)
        rinv_hi = rinv.astype(BF16)
        rinv_lo = (rinv - rinv_hi.astype(F32)).astype(BF16)
        scale = _dot(jnp.concatenate([rinv_hi, rinv_lo], axis=1), gexp_ref[...])
        v_ref[rows, :] = proj(h, o_v, na_w).astype(BF16)
        zn = zqk * scale * qkw_ref[...]
        q_ref[rows, :] = zn[:, 0:na_w].astype(BF16)
        k_ref[rows, :] = zn[:, na_w:qk].astype(BF16)
        u_ref[rows, :] = proj(h, o_u, f_w)
        h = h_next


def _inproj(x, mod, norm_w, w_in, b_in, qk_w, na_w, f_w, later_weights):
    bsz, s, d = x.shape
    in_w = w_in.shape[1]
    tm = INPROJ_TOKENS
    qk = 2 * na_w
    n_grp = qk // HEAD_DIM
    grp = np.arange(qk) // HEAD_DIM
    gsum = jnp.asarray((grp[:, None] == np.arange(LANES)[None, :]).astype(np.float32), BF16)
    gexp1 = (np.arange(LANES)[:, None] == grp[None, :]).astype(np.float32)
    gexp = jnp.asarray(np.concatenate([gexp1, gexp1], axis=0), BF16)
    assert n_grp <= LANES

    tok = lambda w: pl.BlockSpec((None, tm, w), lambda b, i: (b, i, 0))
    mod_chunk = lambda c: pl.BlockSpec((None, 1, d), lambda b, i: (b, 0, c))
    full = lambda a: pl.BlockSpec(a.shape, lambda b, i: (0,) * a.ndim)
    nw = norm_w.reshape(1, d)
    bi = b_in.reshape(1, in_w)
    out_shapes = (
        jax.ShapeDtypeStruct((bsz, s, na_w), BF16),
        jax.ShapeDtypeStruct((bsz, s, na_w), BF16),
        jax.ShapeDtypeStruct((bsz, s, na_w), BF16),
        jax.ShapeDtypeStruct((bsz, s, f_w), F32),
        jax.ShapeDtypeStruct((bsz, s, d), BF16),
        jax.ShapeDtypeStruct((bsz, s, d), BF16),
    )
    n_steps = bsz * (s // tm)
    slab = lambda a: pl.BlockSpec((a.shape[0] // n_steps, a.shape[1]),
                                  lambda b, i: (b * (s // tm) + i, 0))
    for a in later_weights:
        assert a.shape[0] % (n_steps * 2 * SUBLANES) == 0, a.shape
    outs = pl.pallas_call(
        functools.partial(_inproj_kernel, na_w=na_w, f_w=f_w, d=d, chunk=INPROJ_CHUNK,
                          n_cast=len(later_weights)),
        grid=(bsz, s // tm),
        in_specs=[tok(d), mod_chunk(MOD_SCALE1), mod_chunk(MOD_SHIFT1), full(nw), full(w_in),
                  full(bi), full(qk_w), full(gsum), full(gexp)] + [slab(a) for a in later_weights],
        out_specs=(tok(na_w), tok(na_w), tok(na_w), tok(f_w), tok(d), tok(d))
        + tuple(slab(a) for a in later_weights),
        out_shape=out_shapes + tuple(jax.ShapeDtypeStruct(a.shape, BF16) for a in later_weights),
        compiler_params=_params("parallel", "parallel"),
        name="inproj",
    )(x, mod, mod, nw, w_in, bi, qk_w, gsum, gexp, *later_weights)
    return outs[:6], outs[6:]


def _fourier_constants(rows, f_w):
    n = rows
    r8 = SUBLANES
    k = np.arange(n)
    ang_a = 2.0 * np.pi * np.outer(k, k) / n
    eye8 = np.eye(r8)
    norm = 1.0 / np.sqrt(n)
    half = n // 2 + 1
    fa_re = np.kron(np.cos(ang_a[:half]), eye8) * norm
    fa_im = np.kron(-np.sin(ang_a[:half]), eye8) * norm
    fa = np.concatenate([fa_re, fa_im], axis=0)

    nblk = n // r8
    s2 = k[None, :]
    s2p = k[:, None]
    g = np.zeros((r8 // 2, 4 * n, 4 * n))
    for pp in range(r8 // 2):
        gr = np.zeros((2, n, 2, n))
        gi = np.zeros((2, n, 2, n))
        for ll in range(2):
            l = 2 * pp + ll
            ang = 2.0 * np.pi * (s2 * l / (n * GRID_W) + s2 * s2p / n)
            gr[ll, :, ll, :] = np.cos(ang) * norm
            gi[ll, :, ll, :] = -np.sin(ang) * norm
        gr = gr.reshape(2 * n, 2 * n)
        gi = gi.reshape(2 * n, 2 * n)
        g[pp] = np.block([[gr, -gi], [gi, gr]])

    perm = np.zeros((n * r8, n * r8))
    s1p = np.repeat(k, r8)
    jj = np.tile(np.arange(r8), n)
    perm[jj * n + s1p, s1p * r8 + jj] = 1.0

    t_idx = np.arange(nblk)[:, None, None]
    blk_of_row = (k // r8)[None, :, None]
    j_idx = np.arange(r8)[None, None, :]
    ang_t = 2.0 * np.pi * (t_idx * r8 + j_idx) * blk_of_row * r8 / (n * GRID_W)
    ang_t = ang_t.reshape(nblk, n * r8, 1)
    tw_cos = np.broadcast_to(np.cos(ang_t), (nblk, n * r8, LANES))
    tw_sin = np.broadcast_to(np.sin(ang_t), (nblk, n * r8, LANES))

    gd = f_w // N_FOURIER_GROUPS
    c = np.arange(gd)
    ang_c = 2.0 * np.pi * np.outer(c, c) / gd
    assert MXU_DIM % gd == 0 and f_w % MXU_DIM == 0
    eye_g = np.eye(MXU_DIM // gd)
    cs = np.concatenate([np.kron(eye_g, np.cos(ang_c)), np.kron(eye_g, np.sin(ang_c))],
                        axis=0) / np.sqrt(gd)
    as_f32 = lambda a: jnp.asarray(np.ascontiguousarray(a, dtype=np.float32))
    to_bf16 = lambda a: as_f32(a).astype(BF16)
    return (to_bf16(fa), to_bf16(g), to_bf16(cs), to_bf16(perm), as_f32(tw_cos), as_f32(tw_sin))


def _fourier_kernel(u_ref, fa_ref, g_ref, cs_ref, perm_ref, twc_ref, tws_ref, y_ref,
                    sre_ref, sim_ref, *, steps_per_stage, blocks_per_step):
    t = pl.program_id(1)
    n, _, f_w = u_ref.shape
    r8 = SUBLANES
    m = n * r8

    @pl.when(t < steps_per_stage)
    def _stage_a():
        for j in range(blocks_per_step):
            xin = u_ref[:, j * r8:(j + 1) * r8, :].reshape(m, f_w).astype(BF16)
            a = _dot(fa_ref[...], xin)
            reps = f_w // LANES
            cos_t = jnp.concatenate([twc_ref[j]] * reps, axis=1)
            sin_t = jnp.concatenate([tws_ref[j]] * reps, axis=1)
            mh = (n // 2 + 1) * r8
            mirror = lambda z: [z[(n - s1) * r8:(n - s1 + 1) * r8] for s1 in range(n // 2 + 1, n)]
            a_re = jnp.concatenate([a[0:mh]] + mirror(a[0:mh]), axis=0)
            a_im = jnp.concatenate([a[mh:2 * mh]] + [-z for z in mirror(a[mh:2 * mh])], axis=0)
            b_re = a_re * cos_t + a_im * sin_t
            b_im = a_im * cos_t - a_re * sin_t
            off = pl.multiple_of((t * blocks_per_step + j) * r8, r8)
            sre_ref[:, pl.ds(off, r8), :] = b_re.reshape(n, r8, f_w)
            sim_ref[:, pl.ds(off, r8), :] = b_im.reshape(n, r8, f_w)

    @pl.when((t >= steps_per_stage) & (t < 2 * steps_per_stage))
    def _stage_b():
        pair_rows = 2 * n
        for j in range(blocks_per_step):
            off = pl.multiple_of(((t - steps_per_stage) * blocks_per_step + j) * r8, r8)
            br = sre_ref[pl.ds(off, r8)].reshape(m, f_w)
            bi = sim_ref[pl.ds(off, r8)].reshape(m, f_w)
            p_re, p_im = [], []
            for pp in range(r8 // 2):
                rs_ = slice(pp * pair_rows, (pp + 1) * pair_rows)
                bcat = jnp.concatenate([br[rs_], bi[rs_]], axis=0).astype(BF16)
                p = _dot(g_ref[pp], bcat)
                p_re.append(p[0:pair_rows])
                p_im.append(p[pair_rows:2 * pair_rows])
            p_re = jnp.concatenate(p_re, axis=0).astype(BF16)
            p_im = jnp.concatenate(p_im, axis=0).astype(BF16)
            y = jnp.concatenate(
                [_dot(jnp.concatenate([p_re[:, c0:c0 + MXU_DIM], p_im[:, c0:c0 + MXU_DIM]], axis=1),
                      cs_ref[...]) for c0 in range(0, f_w, MXU_DIM)], axis=1)
            sre_ref[pl.ds(off, r8)] = y.reshape(r8, n, f_w)

    @pl.when(t >= 2 * steps_per_stage)
    def _stage_c():
        for j in range(blocks_per_step):
            off = pl.multiple_of(((t - 2 * steps_per_stage) * blocks_per_step + j) * r8, r8)
            yin = sre_ref[:, pl.ds(off, r8), :].reshape(m, f_w).astype(BF16)
            y_ref[j * m:(j + 1) * m, :] = _dot(perm_ref[...], yin).astype(y_ref.dtype)


def _fourier(u):
    bsz, s, f_w = u.shape
    rows = s // GRID_W
    assert rows == GRID_W, "the two-stage position DFT assumes a square token grid"
    r8 = SUBLANES
    nblk = rows // r8
    fa, g, cs, perm, tw_cos, tw_sin = _fourier_constants(rows, f_w)
    u4 = u.reshape(bsz, rows, GRID_W, f_w)
    blocks_per_step = FOURIER_BLOCKS_PER_STEP
    steps = nblk // blocks_per_step
    width = blocks_per_step * r8
    stage_a_step = lambda b, t: jnp.minimum(t, steps - 1)
    stage_c_step = lambda b, t: jnp.maximum(t - 2 * steps, 0)
    full = lambda a: pl.BlockSpec(a.shape, lambda b, t: (0,) * a.ndim)
    tw_spec = pl.BlockSpec((blocks_per_step,) + tw_cos.shape[1:],
                           lambda b, t: (stage_a_step(b, t), 0, 0))
    return pl.pallas_call(
        functools.partial(_fourier_kernel, steps_per_stage=steps, blocks_per_step=blocks_per_step),
        grid=(bsz, 3 * steps),
        in_specs=[
            pl.BlockSpec((None, rows, width, f_w), lambda b, t: (b, 0, stage_a_step(b, t), 0)),
            full(fa), full(g), full(cs), full(perm), tw_spec, tw_spec,
        ],
        out_specs=pl.BlockSpec((None, width * GRID_W, f_w), lambda b, t: (b, stage_c_step(b, t), 0)),
        out_shape=jax.ShapeDtypeStruct((bsz, s, f_w), BF16),
        scratch_shapes=[pltpu.VMEM((rows, GRID_W, f_w), F32), pltpu.VMEM((rows, GRID_W, f_w), F32)],
        compiler_params=_params("arbitrary", "arbitrary"),
        name="fourier",
    )(u4, fa, g, cs, perm, tw_cos, tw_sin)


def _bias_tables(rpb, rows):
    kw = min(WIN_W, GRID_W)
    cols = np.arange(GRID_W)
    col_start = np.clip(cols - kw // 2, 0, GRID_W - kw)
    kj = cols[None, :]
    allowed = (kj >= col_start[:, None]) & (kj < col_start[:, None] + kw)
    rel = kj - cols[:, None] + (WIN_W - 1)
    nrel = 2 * WIN_W - 1
    onehot = ((rel[None] == np.arange(nrel)[:, None, None]) & allowed[None]).astype(np.float32)
    sel = np.zeros((2 * nrel + 1, GRID_W, 2 * GRID_W), np.float32)
    sel[0:nrel, :, 0:GRID_W] = onehot
    sel[nrel:2 * nrel, :, GRID_W:] = onehot
    sel[2 * nrel] = np.tile(~allowed, (1, 2))
    rpb = rpb.astype(F32) * LOG2_E
    mask_col = jnp.full(rpb.shape[:1] + (rpb.shape[1] - 1, 1), MASK_VALUE, F32)
    src = jnp.concatenate([rpb[:, :-1], rpb[:, 1:], mask_col], axis=-1)
    return jnp.einsum("hmd,dcx->hmcx", src, jnp.asarray(sel), precision=lax.Precision.HIGHEST)


def _attn_out_kernel(q_ref, k_ref, v_ref, t2_ref, yfn_ref, ga_ref, gb_ref, x_ref,
                     wna_ref, wfn_ref, bfn_ref, wo_ref, g1_ref, sc2_ref, sh2_ref, nw2_ref,
                     o_ref, h2_ref, yna_ref, s_ref, p_ref,
                     *, rows, kh, rows_per_step):
    t = pl.program_id(0)
    last_tile = pl.num_programs(0) - 2

    @pl.when(t == 0)
    def _init():
        yna_ref[...] = jnp.zeros_like(yna_ref)

    yna_prev = yna_ref[...].astype(BF16)
    yfn_prev = yfn_ref[...]

    rb = lax.rem(jnp.minimum(t, last_tile), rows // rows_per_step)
    n_pairs = q_ref.shape[1] // LANES
    lane = lax.broadcasted_iota(jnp.int32, (1, LANES), 1)
    low = lane < HEAD_DIM
    mask_lo = jnp.where(low, 1.0, 0.0).astype(BF16)
    mask_hi = jnp.where(low, 0.0, 1.0).astype(BF16)
    n_keys = kh * GRID_W

    key_start = []
    for j in range(rows_per_step):
        r = rb * rows_per_step + j
        rs = jnp.clip(r - kh // 2, 0, rows - kh)
        key_start.append((pl.multiple_of(rs * GRID_W, GRID_W), rs - r + (WIN_H - 1)))

    def qk_tile(ti):
        j, p = divmod(ti, n_pairs)
        ks, idx0 = key_start[j]
        cs = slice(LANES * p, LANES * (p + 1))
        q2 = q_ref[j * GRID_W:(j + 1) * GRID_W, cs]
        k2 = k_ref[pl.ds(ks, n_keys), cs]
        qs = jnp.concatenate([q2 * mask_lo, q2 * mask_hi], axis=0)
        s = lax.dot_general(qs, k2, (((1,), (1,)), ((), ())), preferred_element_type=F32)
        bias = jnp.concatenate(
            [jnp.concatenate([t2_ref[2 * p + hh, idx0 + 2 * m] for m in range(kh // 2)], axis=1)
             for hh in range(2)], axis=0)
        s_ref[ti] = s + bias

    def softmax_tile(ti):
        s = s_ref[ti]
        p_ref[ti] = jnp.exp2(s - jnp.max(s, axis=-1, keepdims=True)).astype(BF16)

    ones = jnp.ones((n_keys, LANES), BF16)
    row_outs = []

    def pv_tile(ti):
        j, p = divmod(ti, n_pairs)
        ks, _ = key_start[j]
        v2 = v_ref[pl.ds(ks, n_keys), LANES * p:LANES * (p + 1)]
        o = _dot(p_ref[ti], jnp.concatenate([v2, ones], axis=1))
        o = o[:, 0:LANES] * (1.0 / o[:, LANES:2 * LANES])
        row_outs.append(jnp.where(low, o[0:GRID_W], o[GRID_W:2 * GRID_W]))
        if p == n_pairs - 1:
            yna_ref[j * GRID_W:(j + 1) * GRID_W, :] = jnp.concatenate(row_outs, axis=1)
            row_outs.clear()

    d = o_ref.shape[1]
    merged = []

    def merge_chunk(c0):
        cols = slice(c0, c0 + MXU_DIM)
        ya = _dot(yna_prev, wna_ref[:, cols])
        yf = _dot(yfn_prev, wfn_ref[:, cols]) + bfn_ref[:, cols]
        merged.append((ga_ref[:, cols].astype(F32) * ya
                       + gb_ref[:, cols].astype(F32) * yf).astype(BF16))

    def out_chunk(c0):
        cols = slice(c0, c0 + MXU_DIM)
        lhs = jnp.concatenate(merged, axis=1)
        o_ref[:, cols] = x_ref[:, cols] + g1_ref[:, cols] * _dot(lhs, wo_ref[:, cols])

    norm_rows = o_ref.shape[0] // rows_per_step
    gain2 = nw2_ref[...] * (1.0 + sc2_ref[...])

    def norm_chunk(j):
        nr = slice(j * norm_rows, (j + 1) * norm_rows)
        x1 = o_ref[nr, :]
        ms = jnp.mean(x1 * x1, axis=-1, keepdims=True)
        h2_ref[nr, :] = (x1 * lax.rsqrt(ms + EPS) * gain2 + sh2_ref[...]).astype(BF16)

    matmul_units = ([functools.partial(merge_chunk, c0) for c0 in range(0, d, MXU_DIM)]
                    + [functools.partial(out_chunk, c0) for c0 in range(0, d, MXU_DIM)])
    norm_units = [functools.partial(norm_chunk, j) for j in range(rows_per_step)]

    n_tiles = rows_per_step * n_pairs
    for u in range(n_tiles + PV_SKEW):
        if u < n_tiles:
            qk_tile(u)
        if 0 <= u - SOFTMAX_SKEW < n_tiles:
            softmax_tile(u - SOFTMAX_SKEW)
        if 0 <= u - PV_SKEW < n_tiles:
            pv_tile(u - PV_SKEW)
        if matmul_units:
            if u % OUT_UNIT_EVERY == 1:
                matmul_units.pop(0)()
        elif norm_units and u % NORM_UNIT_EVERY == 0:
            norm_units.pop(0)()
    assert not matmul_units
    while norm_units:
        norm_units.pop(0)()


def _attn_out(q, k, v, t2, yfn, ga, gb, x, w_na, w_fn, b_fn, w_o, mod, norm2_w):
    bsz, s, d = x.shape
    na_w = q.shape[-1]
    f_w = yfn.shape[-1]
    rows = s // GRID_W
    kh = min(WIN_H, rows)
    assert kh % 2 == 0 and (2 * HEAD_DIM) == LANES
    rows_per_step = ATTN_ROWS_PER_STEP
    tq = rows_per_step * GRID_W
    n_tiles = rows_per_step * (na_w // LANES)

    per_seq = s // tq
    n_steps = bsz * per_seq + 1
    attn_tile = lambda t: jnp.minimum(t, n_steps - 2)
    out_tile = lambda t: jnp.maximum(t - 1, 0)
    tok_a = lambda w: pl.BlockSpec(
        (None, tq, w), lambda t: (attn_tile(t) // per_seq, attn_tile(t) % per_seq, 0))
    tok_o = lambda w: pl.BlockSpec(
        (None, tq, w), lambda t: (out_tile(t) // per_seq, out_tile(t) % per_seq, 0))
    whole_seq = pl.BlockSpec((None, s, na_w), lambda t: (attn_tile(t) // per_seq, 0, 0))
    mod_chunk = lambda c: pl.BlockSpec((None, 1, d), lambda t: (out_tile(t) // per_seq, 0, c))
    full = lambda a: pl.BlockSpec(a.shape, lambda t: (0,) * a.ndim)
    bfn = b_fn.reshape(1, d)
    nw2 = norm2_w.reshape(1, d)
    return pl.pallas_call(
        functools.partial(_attn_out_kernel, rows=rows, kh=kh, rows_per_step=rows_per_step),
        grid=(n_steps,),
        in_specs=[tok_a(na_w), whole_seq, whole_seq, full(t2), tok_o(f_w), tok_o(d), tok_o(d),
                  tok_o(d), full(w_na), full(w_fn), full(bfn), full(w_o), mod_chunk(MOD_GATE1),
                  mod_chunk(MOD_SCALE2), mod_chunk(MOD_SHIFT2), full(nw2)],
        out_specs=(tok_o(d), tok_o(d)),
        out_shape=(jax.ShapeDtypeStruct((bsz, s, d), F32), jax.ShapeDtypeStruct((bsz, s, d), BF16)),
        scratch_shapes=[pltpu.VMEM((tq, na_w), F32),
                        pltpu.VMEM((n_tiles, 2 * GRID_W, kh * GRID_W), F32),
                        pltpu.VMEM((n_tiles, 2 * GRID_W, kh * GRID_W), BF16)],
        compiler_params=_params("arbitrary"),
        name="attn_out",
    )(q, k, v, t2, yfn, ga, gb, x, w_na, w_fn, bfn, w_o, mod, mod, mod, nw2)


def _mlp_kernel(x_ref, h_ref, g2_ref, w1_ref, w2_ref, o_ref, *, chunk):
    h = h_ref[...]
    acc = None
    for c0 in range(0, w1_ref.shape[1], chunk):
        a = jnp.maximum(_dot(h, w1_ref[:, c0:c0 + chunk]), 0.0)
        term = _dot((a * a).astype(BF16), w2_ref[c0:c0 + chunk, :])
        acc = term if acc is None else acc + term
    o_ref[...] = x_ref[...] + g2_ref[...] * acc


def _mlp(x, h, mod, w1, w2):
    bsz, s, d = x.shape
    tm = MLP_TOKENS
    tok = pl.BlockSpec((None, tm, d), lambda b, i: (b, i, 0))
    gate2 = pl.BlockSpec((None, 1, d), lambda b, i: (b, 0, MOD_GATE2))
    full = lambda a: pl.BlockSpec(a.shape, lambda b, i: (0,) * a.ndim)
    return pl.pallas_call(
        functools.partial(_mlp_kernel, chunk=MLP_FF_CHUNK),
        grid=(bsz, s // tm),
        in_specs=[tok, tok, gate2, full(w1), full(w2)],
        out_specs=tok,
        out_shape=jax.ShapeDtypeStruct((bsz, s, d), F32),
        compiler_params=_params("parallel", "parallel"),
        name="mlp",
    )(x, h, mod, w1, w2)


def kernel(x, c, norm1_w, norm2_w, w_ada, b_ada, w_in, b_in, q_norm_w, k_norm_w, rpb,
           w_na_out, w_fn_out, b_fn_out, w_o, w_mlp_in, w_mlp_out):
    bsz, s, d = x.shape
    depth = w_ada.shape[0]
    na_w = w_na_out.shape[1]
    f_w = w_fn_out.shape[1]
    n_heads = na_w // HEAD_DIM
    rows = s // GRID_W
    for l in range(depth):
        mod, w_in_b = _ada(c, w_ada[l], b_ada[l], w_in[l])
        mod = mod.reshape(bsz, 1, N_MOD * d)

        qk_w = jnp.concatenate([jnp.tile(q_norm_w[l], n_heads) * (HEAD_DIM ** -0.5 * LOG2_E),
                                jnp.tile(k_norm_w[l], n_heads)]).reshape(1, 2 * na_w)
        (q, k, v, u, ga, gb), (w_na, w_fn, w_o_b, w_m1, w_m2) = _inproj(
            x, mod, norm1_w[l], w_in_b, b_in[l], qk_w, na_w, f_w,
            (w_na_out[l], w_fn_out[l], w_o[l], w_mlp_in[l], w_mlp_out[l]))
        yfn = _fourier(u)
        t2 = _bias_tables(rpb[l], rows)
        x, h2 = _attn_out(q, k, v, t2, yfn, ga, gb, x, w_na, w_fn, b_fn_out[l], w_o_b, mod,
                          norm2_w[l])
        x = _mlp(x, h2, mod, w_m1, w_m2)
    return x
```

```python
import functools

import numpy as np
import jax
import jax.numpy as jnp
from jax import lax
from jax.experimental import pallas as pl
from jax.experimental.pallas import tpu as pltpu

F32 = jnp.float32
BF16 = jnp.bfloat16

GRID_W = 64
HEAD_DIM = 64
WIN_H = 8
WIN_W = 16
N_FOURIER_GROUPS = 4
N_MOD = 6
MOD_SHIFT1, MOD_SCALE1, MOD_GATE1, MOD_SHIFT2, MOD_SCALE2, MOD_GATE2 = range(N_MOD)
EPS = 1e-6
MASK_VALUE = -1e30
LOG2_E = 1.4426950408889634

SUBLANES = 8
LANES = 128
MXU_DIM = 256
SOFTMAX_SKEW = 3
PV_SKEW = 6
OUT_UNIT_EVERY = 4
NORM_UNIT_EVERY = 1
VMEM_LIMIT_BYTES = 56 * 1024 * 1024

ADA_STEPS = 8
INPROJ_TOKENS = 1024
INPROJ_CHUNK = 1024
FOURIER_BLOCKS_PER_STEP = 8
ATTN_ROWS_PER_STEP = 8
MLP_TOKENS = 1024
MLP_FF_CHUNK = 1024


def _dot(a, b):
    return jnp.dot(a, b, preferred_element_type=F32)


def _sigmoid(x):
    return 0.5 * jnp.tanh(0.5 * x) + 0.5


def _params(*semantics):
    return pltpu.CompilerParams(dimension_semantics=semantics, vmem_limit_bytes=VMEM_LIMIT_BYTES)


def _ada_kernel(ct_ref, w_ref, b_ref, win_ref, o_ref, win_bf16_ref):
    ct = ct_ref[...]
    st = ct * _sigmoid(ct)
    w = w_ref[...]
    for b in range(ct.shape[1]):
        o_ref[b] = jnp.sum(w * st[:, b:b + 1], axis=0, keepdims=True) + b_ref[...]
    win_bf16_ref[...] = win_ref[...].astype(BF16)


def _ada(c, w, b, w_in):
    bsz, d = c.shape
    n = w.shape[1]
    n_steps = ADA_STEPS
    tn = n // n_steps
    slab = w_in.shape[0] // n_steps
    assert n % (n_steps * LANES) == 0 and w_in.shape[0] % (n_steps * 2 * SUBLANES) == 0
    return pl.pallas_call(
        _ada_kernel,
        grid=(n_steps,),
        in_specs=[pl.BlockSpec((d, bsz), lambda j: (0, 0)),
                  pl.BlockSpec((d, tn), lambda j: (0, j)),
                  pl.BlockSpec((1, tn), lambda j: (0, j)),
                  pl.BlockSpec((slab, w_in.shape[1]), lambda j: (j, 0))],
        out_specs=(pl.BlockSpec((bsz, 1, tn), lambda j: (0, 0, j)),
                   pl.BlockSpec((slab, w_in.shape[1]), lambda j: (j, 0))),
        out_shape=(jax.ShapeDtypeStruct((bsz, 1, n), F32),
                   jax.ShapeDtypeStruct(w_in.shape, BF16)),
        compiler_params=_params("parallel"),
        name="ada",
    )(c.T, w, b.reshape(1, n), w_in)


def _inproj_kernel(x_ref, sc_ref, sh_ref, nw_ref, w_ref, b_ref, qkw_ref, gsum_ref, gexp_ref, *refs,
                   na_w, f_w, d, chunk, n_cast):
    cast_in = refs[:n_cast]
    q_ref, k_ref, v_ref, u_ref, ga_ref, gb_ref = refs[n_cast:n_cast + 6]
    cast_out = refs[n_cast + 6:]
    for src, dst in zip(cast_in, cast_out):
        dst[...] = src[...].astype(BF16)

    gain = nw_ref[...] * (1.0 + sc_ref[...])

    def normed(c0):
        x = x_ref[c0:c0 + chunk, :]
        ms = jnp.mean(x * x, axis=-1, keepdims=True)
        return (x * lax.rsqrt(ms + EPS) * gain + sh_ref[...]).astype(BF16)

    def proj(h, lo, width):
        return _dot(h, w_ref[:, lo:lo + width]) + b_ref[:, lo:lo + width]

    qk = 2 * na_w
    o_v, o_u, o_ga, o_gb = qk, qk + na_w, qk + na_w + f_w, qk + na_w + f_w + d
    n_rows = x_ref.shape[0]
    h = normed(0)
    for c0 in range(0, n_rows, chunk):
        rows = slice(c0, c0 + chunk)
        zqk = proj(h, 0, qk)
        h_next = normed(c0 + chunk) if c0 + chunk < n_rows else None
        ga_ref[rows, :] = _sigmoid(proj(h, o_ga, d)).astype(BF16)
        ssq = _dot((zqk * zqk).astype(BF16), gsum_ref[...])
        gb_ref[rows, :] = _sigmoid(proj(h, o_gb, d)).astype(BF16)
        rinv = lax.rsqrt(ssq * (1.0 / HEAD_DIM) + EPS)
        rinv_hi = rinv.astype(BF16)
        rinv_lo = (rinv - rinv_hi.astype(F32)).astype(BF16)
        scale = _dot(jnp.concatenate([rinv_hi, rinv_lo], axis=1), gexp_ref[...])
        v_ref[rows, :] = proj(h, o_v, na_w).astype(BF16)
        zn = zqk * scale * qkw_ref[...]
        q_ref[rows, :] = zn[:, 0:na_w].astype(BF16)
        k_ref[rows, :] = zn[:, na_w:qk].astype(BF16)
        u_ref[rows, :] = proj(h, o_u, f_w)
        h = h_next


def _inproj(x, mod, norm_w, w_in, b_in, qk_w, na_w, f_w, later_weights):
    bsz, s, d = x.shape
    in_w = w_in.shape[1]
    tm = INPROJ_TOKENS
    qk = 2 * na_w
    n_grp = qk // HEAD_DIM
    grp = np.arange(qk) // HEAD_DIM
    gsum = jnp.asarray((grp[:, None] == np.arange(LANES)[None, :]).astype(np.float32), BF16)
    gexp1 = (np.arange(LANES)[:, None] == grp[None, :]).astype(np.float32)
    gexp = jnp.asarray(np.concatenate([gexp1, gexp1], axis=0), BF16)
    assert n_grp <= LANES

    tok = lambda w: pl.BlockSpec((None, tm, w), lambda b, i: (b, i, 0))
    mod_chunk = lambda c: pl.BlockSpec((None, 1, d), lambda b, i: (b, 0, c))
    full = lambda a: pl.BlockSpec(a.shape, lambda b, i: (0,) * a.ndim)
    nw = norm_w.reshape(1, d)
    bi = b_in.reshape(1, in_w)
    out_shapes = (
        jax.ShapeDtypeStruct((bsz, s, na_w), BF16),
        jax.ShapeDtypeStruct((bsz, s, na_w), BF16),
        jax.ShapeDtypeStruct((bsz, s, na_w), BF16),
        jax.ShapeDtypeStruct((bsz, s, f_w), F32),
        jax.ShapeDtypeStruct((bsz, s, d), BF16),
        jax.ShapeDtypeStruct((bsz, s, d), BF16),
    )
    n_steps = bsz * (s // tm)
    slab = lambda a: pl.BlockSpec((a.shape[0] // n_steps, a.shape[1]),
                                  lambda b, i: (b * (s // tm) + i, 0))
    for a in later_weights:
        assert a.shape[0] % (n_steps * 2 * SUBLANES) == 0, a.shape
    outs = pl.pallas_call(
        functools.partial(_inproj_kernel, na_w=na_w, f_w=f_w, d=d, chunk=INPROJ_CHUNK,
                          n_cast=len(later_weights)),
        grid=(bsz, s // tm),
        in_specs=[tok(d), mod_chunk(MOD_SCALE1), mod_chunk(MOD_SHIFT1), full(nw), full(w_in),
                  full(bi), full(qk_w), full(gsum), full(gexp)] + [slab(a) for a in later_weights],
        out_specs=(tok(na_w), tok(na_w), tok(na_w), tok(f_w), tok(d), tok(d))
        + tuple(slab(a) for a in later_weights),
        out_shape=out_shapes + tuple(jax.ShapeDtypeStruct(a.shape, BF16) for a in later_weights),
        compiler_params=_params("parallel", "parallel"),
        name="inproj",
    )(x, mod, mod, nw, w_in, bi, qk_w, gsum, gexp, *later_weights)
    return outs[:6], outs[6:]


def _fourier_constants(rows, f_w):
    n = rows
    r8 = SUBLANES
    k = np.arange(n)
    ang_a = 2.0 * np.pi * np.outer(k, k) / n
    eye8 = np.eye(r8)
    norm = 1.0 / np.sqrt(n)
    half = n // 2 + 1
    fa_re = np.kron(np.cos(ang_a[:half]), eye8) * norm
    fa_im = np.kron(-np.sin(ang_a[:half]), eye8) * norm
    fa = np.concatenate([fa_re, fa_im], axis=0)

    nblk = n // r8
    s2 = k[None, :]
    s2p = k[:, None]
    g = np.zeros((r8 // 2, 4 * n, 4 * n))
    for pp in range(r8 // 2):
        gr = np.zeros((2, n, 2, n))
        gi = np.zeros((2, n, 2, n))
        for ll in range(2):
            l = 2 * pp + ll
            ang = 2.0 * np.pi * (s2 * l / (n * GRID_W) + s2 * s2p / n)
            gr[ll, :, ll, :] = np.cos(ang) * norm
            gi[ll, :, ll, :] = -np.sin(ang) * norm
        gr = gr.reshape(2 * n, 2 * n)
        gi = gi.reshape(2 * n, 2 * n)
        g[pp] = np.block([[gr, -gi], [gi, gr]])

    perm = np.zeros((n * r8, n * r8))
    s1p = np.repeat(k, r8)
    jj = np.tile(np.arange(r8), n)
    perm[jj * n + s1p, s1p * r8 + jj] = 1.0

    t_idx = np.arange(nblk)[:, None, None]
    blk_of_row = (k // r8)[None, :, None]
    j_idx = np.arange(r8)[None, None, :]
    ang_t = 2.0 * np.pi * (t_idx * r8 + j_idx) * blk_of_row * r8 / (n * GRID_W)
    ang_t = ang_t.reshape(nblk, n * r8, 1)
    tw_cos = np.broadcast_to(np.cos(ang_t), (nblk, n * r8, LANES))
    tw_sin = np.broadcast_to(np.sin(ang_t), (nblk, n * r8, LANES))

    gd = f_w // N_FOURIER_GROUPS
    c = np.arange(gd)
    ang_c = 2.0 * np.pi * np.outer(c, c) / gd
    assert MXU_DIM % gd == 0 and f_w % MXU_DIM == 0
    eye_g = np.eye(MXU_DIM // gd)
    cs = np.concatenate([np.kron(eye_g, np.cos(ang_c)), np.kron(eye_g, np.sin(ang_c))],
                        axis=0) / np.sqrt(gd)
    as_f32 = lambda a: jnp.asarray(np.ascontiguousarray(a, dtype=np.float32))
    to_bf16 = lambda a: as_f32(a).astype(BF16)
    return (to_bf16(fa), to_bf16(g), to_bf16(cs), to_bf16(perm), as_f32(tw_cos), as_f32(tw_sin))


def _fourier_kernel(u_ref, fa_ref, g_ref, cs_ref, perm_ref, twc_ref, tws_ref, y_ref,
                    sre_ref, sim_ref, *, steps_per_stage, blocks_per_step):
    t = pl.program_id(1)
    n, _, f_w = u_ref.shape
    r8 = SUBLANES
    m = n * r8

    @pl.when(t < steps_per_stage)
    def _stage_a():
        for j in range(blocks_per_step):
            xin = u_ref[:, j * r8:(j + 1) * r8, :].reshape(m, f_w).astype(BF16)
            a = _dot(fa_ref[...], xin)
            reps = f_w // LANES
            cos_t = jnp.concatenate([twc_ref[j]] * reps, axis=1)
            sin_t = jnp.concatenate([tws_ref[j]] * reps, axis=1)
            mh = (n // 2 + 1) * r8
            mirror = lambda z: [z[(n - s1) * r8:(n - s1 + 1) * r8] for s1 in range(n // 2 + 1, n)]
            a_re = jnp.concatenate([a[0:mh]] + mirror(a[0:mh]), axis=0)
            a_im = jnp.concatenate([a[mh:2 * mh]] + [-z for z in mirror(a[mh:2 * mh])], axis=0)
            b_re = a_re * cos_t + a_im * sin_t
            b_im = a_im * cos_t - a_re * sin_t
            off = pl.multiple_of((t * blocks_per_step + j) * r8, r8)
            sre_ref[:, pl.ds(off, r8), :] = b_re.reshape(n, r8, f_w)
            sim_ref[:, pl.ds(off, r8), :] = b_im.reshape(n, r8, f_w)

    @pl.when((t >= steps_per_stage) & (t < 2 * steps_per_stage))
    def _stage_b():
        pair_rows = 2 * n
        for j in range(blocks_per_step):
            off = pl.multiple_of(((t - steps_per_stage) * blocks_per_step + j) * r8, r8)
            br = sre_ref[pl.ds(off, r8)].reshape(m, f_w)
            bi = sim_ref[pl.ds(off, r8)].reshape(m, f_w)
            p_re, p_im = [], []
            for pp in range(r8 // 2):
                rs_ = slice(pp * pair_rows, (pp + 1) * pair_rows)
                bcat = jnp.concatenate([br[rs_], bi[rs_]], axis=0).astype(BF16)
                p = _dot(g_ref[pp], bcat)
                p_re.append(p[0:pair_rows])
                p_im.append(p[pair_rows:2 * pair_rows])
            p_re = jnp.concatenate(p_re, axis=0).astype(BF16)
            p_im = jnp.concatenate(p_im, axis=0).astype(BF16)
            y = jnp.concatenate(
                [_dot(jnp.concatenate([p_re[:, c0:c0 + MXU_DIM], p_im[:, c0:c0 + MXU_DIM]], axis=1),
                      cs_ref[...]) for c0 in range(0, f_w, MXU_DIM)], axis=1)
            sre_ref[pl.ds(off, r8)] = y.reshape(r8, n, f_w)

    @pl.when(t >= 2 * steps_per_stage)
    def _stage_c():
        for j in range(blocks_per_step):
            off = pl.multiple_of(((t - 2 * steps_per_stage) * blocks_per_step + j) * r8, r8)
            yin = sre_ref[:, pl.ds(off, r8), :].reshape(m, f_w).astype(BF16)
            y_ref[j * m:(j + 1) * m, :] = _dot(perm_ref[...], yin).astype(y_ref.dtype)


def _fourier(u):
    bsz, s, f_w = u.shape
    rows = s // GRID_W
    assert rows == GRID_W, "the two-stage position DFT assumes a square token grid"
    r8 = SUBLANES
    nblk = rows // r8
    fa, g, cs, perm, tw_cos, tw_sin = _fourier_constants(rows, f_w)
    u4 = u.reshape(bsz, rows, GRID_W, f_w)
    blocks_per_step = FOURIER_BLOCKS_PER_STEP
    steps = nblk // blocks_per_step
    width = blocks_per_step * r8
    stage_a_step = lambda b, t: jnp.minimum(t, steps - 1)
    stage_c_step = lambda b, t: jnp.maximum(t - 2 * steps, 0)
    full = lambda a: pl.BlockSpec(a.shape, lambda b, t: (0,) * a.ndim)
    tw_spec = pl.BlockSpec((blocks_per_step,) + tw_cos.shape[1:],
                           lambda b, t: (stage_a_step(b, t), 0, 0))
    return pl.pallas_call(
        functools.partial(_fourier_kernel, steps_per_stage=steps, blocks_per_step=blocks_per_step),
        grid=(bsz, 3 * steps),
        in_specs=[
            pl.BlockSpec((None, rows, width, f_w), lambda b, t: (b, 0, stage_a_step(b, t), 0)),
            full(fa), full(g), full(cs), full(perm), tw_spec, tw_spec,
        ],
        out_specs=pl.BlockSpec((None, width * GRID_W, f_w), lambda b, t: (b, stage_c_step(b, t), 0)),
        out_shape=jax.ShapeDtypeStruct((bsz, s, f_w), BF16),
        scratch_shapes=[pltpu.VMEM((rows, GRID_W, f_w), F32), pltpu.VMEM((rows, GRID_W, f_w), F32)],
        compiler_params=_params("arbitrary", "arbitrary"),
        name="fourier",
    )(u4, fa, g, cs, perm, tw_cos, tw_sin)


def _bias_tables(rpb, rows):
    kw = min(WIN_W, GRID_W)
    cols = np.arange(GRID_W)
    col_start = np.clip(cols - kw // 2, 0, GRID_W - kw)
    kj = cols[None, :]
    allowed = (kj >= col_start[:, None]) & (kj < col_start[:, None] + kw)
    rel = kj - cols[:, None] + (WIN_W - 1)
    nrel = 2 * WIN_W - 1
    onehot = ((rel[None] == np.arange(nrel)[:, None, None]) & allowed[None]).astype(np.float32)
    sel = np.zeros((2 * nrel + 1, GRID_W, 2 * GRID_W), np.float32)
    sel[0:nrel, :, 0:GRID_W] = onehot
    sel[nrel:2 * nrel, :, GRID_W:] = onehot
    sel[2 * nrel] = np.tile(~allowed, (1, 2))
    rpb = rpb.astype(F32) * LOG2_E
    mask_col = jnp.full(rpb.shape[:1] + (rpb.shape[1] - 1, 1), MASK_VALUE, F32)
    src = jnp.concatenate([rpb[:, :-1], rpb[:, 1:], mask_col], axis=-1)
    return jnp.einsum("hmd,dcx->hmcx", src, jnp.asarray(sel), precision=lax.Precision.HIGHEST)


def _attn_out_kernel(q_ref, k_ref, v_ref, t2_ref, yfn_ref, ga_ref, gb_ref, x_ref,
                     wna_ref, wfn_ref, bfn_ref, wo_ref, g1_ref, sc2_ref, sh2_ref, nw2_ref,
                     o_ref, h2_ref, yna_ref, s_ref, p_ref,
                     *, rows, kh, rows_per_step):
    t = pl.program_id(0)
    last_tile = pl.num_programs(0) - 2

    @pl.when(t == 0)
    def _init():
        yna_ref[...] = jnp.zeros_like(yna_ref)

    yna_prev = yna_ref[...].astype(BF16)
    yfn_prev = yfn_ref[...]

    rb = lax.rem(jnp.minimum(t, last_tile), rows // rows_per_step)
    n_pairs = q_ref.shape[1] // LANES
    lane = lax.broadcasted_iota(jnp.int32, (1, LANES), 1)
    low = lane < HEAD_DIM
    mask_lo = jnp.where(low, 1.0, 0.0).astype(BF16)
    mask_hi = jnp.where(low, 0.0, 1.0).astype(BF16)
    n_keys = kh * GRID_W

    key_start = []
    for j in range(rows_per_step):
        r = rb * rows_per_step + j
        rs = jnp.clip(r - kh // 2, 0, rows - kh)
        key_start.append((pl.multiple_of(rs * GRID_W, GRID_W), rs - r + (WIN_H - 1)))

    def qk_tile(ti):
        j, p = divmod(ti, n_pairs)
        ks, idx0 = key_start[j]
        cs = slice(LANES * p, LANES * (p + 1))
        q2 = q_ref[j * GRID_W:(j + 1) * GRID_W, cs]
        k2 = k_ref[pl.ds(ks, n_keys), cs]
        qs = jnp.concatenate([q2 * mask_lo, q2 * mask_hi], axis=0)
        s = lax.dot_general(qs, k2, (((1,), (1,)), ((), ())), preferred_element_type=F32)
        bias = jnp.concatenate(
            [jnp.concatenate([t2_ref[2 * p + hh, idx0 + 2 * m] for m in range(kh // 2)], axis=1)
             for hh in range(2)], axis=0)
        s_ref[ti] = s + bias

    def softmax_tile(ti):
        s = s_ref[ti]
        p_ref[ti] = jnp.exp2(s - jnp.max(s, axis=-1, keepdims=True)).astype(BF16)

    ones = jnp.ones((n_keys, LANES), BF16)
    row_outs = []

    def pv_tile(ti):
        j, p = divmod(ti, n_pairs)
        ks, _ = key_start[j]
        v2 = v_ref[pl.ds(ks, n_keys), LANES * p:LANES * (p + 1)]
        o = _dot(p_ref[ti], jnp.concatenate([v2, ones], axis=1))
        o = o[:, 0:LANES] * (1.0 / o[:, LANES:2 * LANES])
        row_outs.append(jnp.where(low, o[0:GRID_W], o[GRID_W:2 * GRID_W]))
        if p == n_pairs - 1:
            yna_ref[j * GRID_W:(j + 1) * GRID_W, :] = jnp.concatenate(row_outs, axis=1)
            row_outs.clear()

    d = o_ref.shape[1]
    merged = []

    def merge_chunk(c0):
        cols = slice(c0, c0 + MXU_DIM)
        ya = _dot(yna_prev, wna_ref[:, cols])
        yf = _dot(yfn_prev, wfn_ref[:, cols]) + bfn_ref[:, cols]
        merged.append((ga_ref[:, cols].astype(F32) * ya
                       + gb_ref[:, cols].astype(F32) * yf).astype(BF16))

    def out_chunk(c0):
        cols = slice(c0, c0 + MXU_DIM)
        lhs = jnp.concatenate(merged, axis=1)
        o_ref[:, cols] = x_ref[:, cols] + g1_ref[:, cols] * _dot(lhs, wo_ref[:, cols])

    norm_rows = o_ref.shape[0] // rows_per_step
    gain2 = nw2_ref[...] * (1.0 + sc2_ref[...])

    def norm_chunk(j):
        nr = slice(j * norm_rows, (j + 1) * norm_rows)
        x1 = o_ref[nr, :]
        ms = jnp.mean(x1 * x1, axis=-1, keepdims=True)
        h2_ref[nr, :] = (x1 * lax.rsqrt(ms + EPS) * gain2 + sh2_ref[...]).astype(BF16)

    matmul_units = ([functools.partial(merge_chunk, c0) for c0 in range(0, d, MXU_DIM)]
                    + [functools.partial(out_chunk, c0) for c0 in range(0, d, MXU_DIM)])
    norm_units = [functools.partial(norm_chunk, j) for j in range(rows_per_step)]

    n_tiles = rows_per_step * n_pairs
    for u in range(n_tiles + PV_SKEW):
        if u < n_tiles:
            qk_tile(u)
        if 0 <= u - SOFTMAX_SKEW < n_tiles:
            softmax_tile(u - SOFTMAX_SKEW)
        if 0 <= u - PV_SKEW < n_tiles:
            pv_tile(u - PV_SKEW)
        if matmul_units:
            if u % OUT_UNIT_EVERY == 1:
                matmul_units.pop(0)()
        elif norm_units and u % NORM_UNIT_EVERY == 0:
            norm_units.pop(0)()
    assert not matmul_units
    while norm_units:
        norm_units.pop(0)()


def _attn_out(q, k, v, t2, yfn, ga, gb, x, w_na, w_fn, b_fn, w_o, mod, norm2_w):
    bsz, s, d = x.shape
    na_w = q.shape[-1]
    f_w = yfn.shape[-1]
    rows = s // GRID_W
    kh = min(WIN_H, rows)
    assert kh % 2 == 0 and (2 * HEAD_DIM) == LANES
    rows_per_step = ATTN_ROWS_PER_STEP
    tq = rows_per_step * GRID_W
    n_tiles = rows_per_step * (na_w // LANES)

    per_seq = s // tq
    n_steps = bsz * per_seq + 1
    attn_tile = lambda t: jnp.minimum(t, n_steps - 2)
    out_tile = lambda t: jnp.maximum(t - 1, 0)
    tok_a = lambda w: pl.BlockSpec(
        (None, tq, w), lambda t: (attn_tile(t) // per_seq, attn_tile(t) % per_seq, 0))
    tok_o = lambda w: pl.BlockSpec(
        (None, tq, w), lambda t: (out_tile(t) // per_seq, out_tile(t) % per_seq, 0))
    whole_seq = pl.BlockSpec((None, s, na_w), lambda t: (attn_tile(t) // per_seq, 0, 0))
    mod_chunk = lambda c: pl.BlockSpec((None, 1, d), lambda t: (out_tile(t) // per_seq, 0, c))
    full = lambda a: pl.BlockSpec(a.shape, lambda t: (0,) * a.ndim)
    bfn = b_fn.reshape(1, d)
    nw2 = norm2_w.reshape(1, d)
    return pl.pallas_call(
        functools.partial(_attn_out_kernel, rows=rows, kh=kh, rows_per_step=rows_per_step),
        grid=(n_steps,),
        in_specs=[tok_a(na_w), whole_seq, whole_seq, full(t2), tok_o(f_w), tok_o(d), tok_o(d),
                  tok_o(d), full(w_na), full(w_fn), full(bfn), full(w_o), mod_chunk(MOD_GATE1),
                  mod_chunk(MOD_SCALE2), mod_chunk(MOD_SHIFT2), full(nw2)],
        out_specs=(tok_o(d), tok_o(d)),
        out_shape=(jax.ShapeDtypeStruct((bsz, s, d), F32), jax.ShapeDtypeStruct((bsz, s, d), BF16)),
        scratch_shapes=[pltpu.VMEM((tq, na_w), F32),
                        pltpu.VMEM((n_tiles, 2 * GRID_W, kh * GRID_W), F32),
                        pltpu.VMEM((n_tiles, 2 * GRID_W, kh * GRID_W), BF16)],
        compiler_params=_params("arbitrary"),
        name="attn_out",
    )(q, k, v, t2, yfn, ga, gb, x, w_na, w_fn, bfn, w_o, mod, mod, mod, nw2)


def _mlp_kernel(x_ref, h_ref, g2_ref, w1_ref, w2_ref, o_ref, *, chunk):
    h = h_ref[...]
    acc = None
    for c0 in range(0, w1_ref.shape[1], chunk):
        a = jnp.maximum(_dot(h, w1_ref[:, c0:c0 + chunk]), 0.0)
        term = _dot((a * a).astype(BF16), w2_ref[c0:c0 + chunk, :])
        acc = term if acc is None else acc + term
    o_ref[...] = x_ref[...] + g2_ref[...] * acc


def _mlp(x, h, mod, w1, w2):
    bsz, s, d = x.shape
    tm = MLP_TOKENS
    tok = pl.BlockSpec((None, tm, d), lambda b, i: (b, i, 0))
    gate2 = pl.BlockSpec((None, 1, d), lambda b, i: (b, 0, MOD_GATE2))
    full = lambda a: pl.BlockSpec(a.shape, lambda b, i: (0,) * a.ndim)
    return pl.pallas_call(
        functools.partial(_mlp_kernel, chunk=MLP_FF_CHUNK),
        grid=(bsz, s // tm),
        in_specs=[tok, tok, gate2, full(w1), full(w2)],
        out_specs=tok,
        out_shape=jax.ShapeDtypeStruct((bsz, s, d), F32),
        compiler_params=_params("parallel", "parallel"),
        name="mlp",
    )(x, h, mod, w1, w2)


def kernel(x, c, norm1_w, norm2_w, w_ada, b_ada, w_in, b_in, q_norm_w, k_norm_w, rpb,
           w_na_out, w_fn_out, b_fn_out, w_o, w_mlp_in, w_mlp_out):
    bsz, s, d = x.shape
    depth = w_ada.shape[0]
    na_w = w_na_out.shape[1]
    f_w = w_fn_out.shape[1]
    n_heads = na_w // HEAD_DIM
    rows = s // GRID_W
    for l in range(depth):
        mod, w_in_b = _ada(c, w_ada[l], b_ada[l], w_in[l])

        qk_w = jnp.concatenate([jnp.tile(q_norm_w[l], n_heads) * (HEAD_DIM ** -0.5 * LOG2_E),
                                jnp.tile(k_norm_w[l], n_heads)]).reshape(1, 2 * na_w)
        (q, k, v, u, ga, gb), (w_na, w_fn, w_o_b, w_m1, w_m2) = _inproj(
            x, mod, norm1_w[l], w_in_b, b_in[l], qk_w, na_w, f_w,
            (w_na_out[l], w_fn_out[l], w_o[l], w_mlp_in[l], w_mlp_out[l]))
        yfn = _fourier(u)
        t2 = _bias_tables(rpb[l], rows)
        x, h2 = _attn_out(q, k, v, t2, yfn, ga, gb, x, w_na, w_fn, b_fn_out[l], w_o_b, mod,
                          norm2_w[l])
        x = _mlp(x, h2, mod, w_m1, w_m2)
    return x
```

```python
import functools

import numpy as np
import jax
import jax.numpy as jnp
from jax import lax
from jax.experimental import pallas as pl
from jax.experimental.pallas import tpu as pltpu

F32 = jnp.float32
BF16 = jnp.bfloat16

GRID_W = 64
HEAD_DIM = 64
WIN_H = 8
WIN_W = 16
N_FOURIER_GROUPS = 4
N_MOD = 6
MOD_SHIFT1, MOD_SCALE1, MOD_GATE1, MOD_SHIFT2, MOD_SCALE2, MOD_GATE2 = range(N_MOD)
EPS = 1e-6
MASK_VALUE = -1e30
LOG2_E = 1.4426950408889634

SUBLANES = 8
LANES = 128
MXU_DIM = 256
SOFTMAX_SKEW = 3
PV_SKEW = 6
OUT_UNIT_EVERY = 4
NORM_UNIT_EVERY = 1
VMEM_LIMIT_BYTES = 56 * 1024 * 1024

ADA_STEPS = 8
INPROJ_TOKENS = 1024
INPROJ_CHUNK = 1024
FOURIER_BLOCKS_PER_STEP = 8
ATTN_ROWS_PER_STEP = 8
MLP_TOKENS = 1024
MLP_FF_CHUNK = 1024


def _dot(a, b):
    return jnp.dot(a, b, preferred_element_type=F32)


def _sigmoid(x):
    return 0.5 * jnp.tanh(0.5 * x) + 0.5


def _params(*semantics):
    return pltpu.CompilerParams(dimension_semantics=semantics, vmem_limit_bytes=VMEM_LIMIT_BYTES)


def _ada_kernel(ct_ref, w_ref, b_ref, win_ref, o_ref, win_bf16_ref):
    ct = ct_ref[...]
    st = ct * _sigmoid(ct)
    w = w_ref[...]
    for b in range(ct.shape[1]):
        o_ref[b] = jnp.sum(w * st[:, b:b + 1], axis=0, keepdims=True) + b_ref[...]
    win_bf16_ref[...] = win_ref[...].astype(BF16)


def _ada(c, w, b, w_in):
    bsz, d = c.shape
    n = w.shape[1]
    n_steps = ADA_STEPS
    tn = n // n_steps
    slab = w_in.shape[0] // n_steps
    assert n % (n_steps * LANES) == 0 and w_in.shape[0] % (n_steps * 2 * SUBLANES) == 0
    return pl.pallas_call(
        _ada_kernel,
        grid=(n_steps,),
        in_specs=[pl.BlockSpec((d, bsz), lambda j: (0, 0)),
                  pl.BlockSpec((d, tn), lambda j: (0, j)),
                  pl.BlockSpec((1, tn), lambda j: (0, j)),
                  pl.BlockSpec((slab, w_in.shape[1]), lambda j: (j, 0))],
        out_specs=(pl.BlockSpec((bsz, 1, tn), lambda j: (0, 0, j)),
                   pl.BlockSpec((slab, w_in.shape[1]), lambda j: (j, 0))),
        out_shape=(jax.ShapeDtypeStruct((bsz, 1, n), F32),
                   jax.ShapeDtypeStruct(w_in.shape, BF16)),
        compiler_params=_params("parallel"),
        name="ada",
    )(c.T, w, b.reshape(1, n), w_in)


def _inproj_kernel(x_ref, sc_ref, sh_ref, nw_ref, w_ref, b_ref, qkw_ref, gsum_ref, *refs,
                   na_w, f_w, d, chunk, n_cast):
    cast_in = refs[:n_cast]
    q_ref, k_ref, v_ref, u_ref, ga_ref, gb_ref = refs[n_cast:n_cast + 6]
    cast_out = refs[n_cast + 6:]
    for src, dst in zip(cast_in, cast_out):
        dst[...] = src[...].astype(BF16)

    gain = nw_ref[...] * (1.0 + sc_ref[...])

    def normed(c0):
        x = x_ref[c0:c0 + chunk, :]
        ms = jnp.mean(x * x, axis=-1, keepdims=True)
        return (x * lax.rsqrt(ms + EPS) * gain + sh_ref[...]).astype(BF16)

    def proj(h, lo, width):
        return _dot(h, w_ref[:, lo:lo + width]) + b_ref[:, lo:lo + width]

    qk = 2 * na_w
    o_v, o_u, o_ga, o_gb = qk, qk + na_w, qk + na_w + f_w, qk + na_w + f_w + d
    n_rows = x_ref.shape[0]
    first_head = lax.broadcasted_iota(jnp.int32, (1, LANES), 1) < HEAD_DIM
    h = normed(0)
    for c0 in range(0, n_rows, chunk):
        rows = slice(c0, c0 + chunk)
        zqk = proj(h, 0, qk)
        h_next = normed(c0 + chunk) if c0 + chunk < n_rows else None
        ga_ref[rows, :] = _sigmoid(proj(h, o_ga, d)).astype(BF16)
        ssq = _dot((zqk * zqk).astype(BF16), gsum_ref[...])
        gb_ref[rows, :] = _sigmoid(proj(h, o_gb, d)).astype(BF16)
        rinv = lax.rsqrt(ssq * (1.0 / HEAD_DIM) + EPS)
        scale = jnp.concatenate(
            [jnp.where(first_head, rinv[:, 2 * g:2 * g + 1], rinv[:, 2 * g + 1:2 * g + 2])
             for g in range(qk // LANES)], axis=1)
        v_ref[rows, :] = proj(h, o_v, na_w).astype(BF16)
        zn = zqk * scale * qkw_ref[...]
        q_ref[rows, :] = zn[:, 0:na_w].astype(BF16)
        k_ref[rows, :] = zn[:, na_w:qk].astype(BF16)
        u_ref[rows, :] = proj(h, o_u, f_w)
        h = h_next


def _inproj(x, mod, norm_w, w_in, b_in, qk_w, na_w, f_w, later_weights):
    bsz, s, d = x.shape
    in_w = w_in.shape[1]
    tm = INPROJ_TOKENS
    qk = 2 * na_w
    n_grp = qk // HEAD_DIM
    grp = np.arange(qk) // HEAD_DIM
    gsum = jnp.asarray((grp[:, None] == np.arange(LANES)[None, :]).astype(np.float32), BF16)
    assert n_grp <= LANES and 2 * HEAD_DIM == LANES

    tok = lambda w: pl.BlockSpec((None, tm, w), lambda b, i: (b, i, 0))
    mod_chunk = lambda c: pl.BlockSpec((None, 1, d), lambda b, i: (b, 0, c))
    full = lambda a: pl.BlockSpec(a.shape, lambda b, i: (0,) * a.ndim)
    nw = norm_w.reshape(1, d)
    bi = b_in.reshape(1, in_w)
    out_shapes = (
        jax.ShapeDtypeStruct((bsz, s, na_w), BF16),
        jax.ShapeDtypeStruct((bsz, s, na_w), BF16),
        jax.ShapeDtypeStruct((bsz, s, na_w), BF16),
        jax.ShapeDtypeStruct((bsz, s, f_w), F32),
        jax.ShapeDtypeStruct((bsz, s, d), BF16),
        jax.ShapeDtypeStruct((bsz, s, d), BF16),
    )
    n_steps = bsz * (s // tm)
    slab = lambda a: pl.BlockSpec((a.shape[0] // n_steps, a.shape[1]),
                                  lambda b, i: (b * (s // tm) + i, 0))
    for a in later_weights:
        assert a.shape[0] % (n_steps * 2 * SUBLANES) == 0, a.shape
    outs = pl.pallas_call(
        functools.partial(_inproj_kernel, na_w=na_w, f_w=f_w, d=d, chunk=INPROJ_CHUNK,
                          n_cast=len(later_weights)),
        grid=(bsz, s // tm),
        in_specs=[tok(d), mod_chunk(MOD_SCALE1), mod_chunk(MOD_SHIFT1), full(nw), full(w_in),
                  full(bi), full(qk_w), full(gsum)] + [slab(a) for a in later_weights],
        out_specs=(tok(na_w), tok(na_w), tok(na_w), tok(f_w), tok(d), tok(d))
        + tuple(slab(a) for a in later_weights),
        out_shape=out_shapes + tuple(jax.ShapeDtypeStruct(a.shape, BF16) for a in later_weights),
        compiler_params=_params("parallel", "parallel"),
        name="inproj",
    )(x, mod, mod, nw, w_in, bi, qk_w, gsum, *later_weights)
    return outs[:6], outs[6:]


def _fourier_constants(rows, f_w):
    n = rows
    r8 = SUBLANES
    k = np.arange(n)
    ang_a = 2.0 * np.pi * np.outer(k, k) / n
    eye8 = np.eye(r8)
    norm = 1.0 / np.sqrt(n)
    half = n // 2 + 1
    fa_re = np.kron(np.cos(ang_a[:half]), eye8) * norm
    fa_im = np.kron(-np.sin(ang_a[:half]), eye8) * norm
    fa = np.concatenate([fa_re, fa_im], axis=0)

    nblk = n // r8
    s2 = k[None, :]
    s2p = k[:, None]
    g = np.zeros((r8 // 2, 4 * n, 4 * n))
    for pp in range(r8 // 2):
        gr = np.zeros((2, n, 2, n))
        gi = np.zeros((2, n, 2, n))
        for ll in range(2):
            l = 2 * pp + ll
            ang = 2.0 * np.pi * (s2 * l / (n * GRID_W) + s2 * s2p / n)
            gr[ll, :, ll, :] = np.cos(ang) * norm
            gi[ll, :, ll, :] = -np.sin(ang) * norm
        gr = gr.reshape(2 * n, 2 * n)
        gi = gi.reshape(2 * n, 2 * n)
        g[pp] = np.block([[gr, -gi], [gi, gr]])

    perm = np.zeros((n * r8, n * r8))
    s1p = np.repeat(k, r8)
    jj = np.tile(np.arange(r8), n)
    perm[jj * n + s1p, s1p * r8 + jj] = 1.0

    t_idx = np.arange(nblk)[:, None, None]
    blk_of_row = (k // r8)[None, :, None]
    j_idx = np.arange(r8)[None, None, :]
    ang_t = 2.0 * np.pi * (t_idx * r8 + j_idx) * blk_of_row * r8 / (n * GRID_W)
    ang_t = ang_t.reshape(nblk, n * r8, 1)
    tw_cos = np.broadcast_to(np.cos(ang_t), (nblk, n * r8, LANES))
    tw_sin = np.broadcast_to(np.sin(ang_t), (nblk, n * r8, LANES))

    gd = f_w // N_FOURIER_GROUPS
    c = np.arange(gd)
    ang_c = 2.0 * np.pi * np.outer(c, c) / gd
    assert MXU_DIM % gd == 0 and f_w % MXU_DIM == 0
    eye_g = np.eye(MXU_DIM // gd)
    cs = np.concatenate([np.kron(eye_g, np.cos(ang_c)), np.kron(eye_g, np.sin(ang_c))],
                        axis=0) / np.sqrt(gd)
    as_f32 = lambda a: jnp.asarray(np.ascontiguousarray(a, dtype=np.float32))
    to_bf16 = lambda a: as_f32(a).astype(BF16)
    return (to_bf16(fa), to_bf16(g), to_bf16(cs), to_bf16(perm), as_f32(tw_cos), as_f32(tw_sin))


def _fourier_kernel(u_ref, fa_ref, g_ref, cs_ref, perm_ref, twc_ref, tws_ref, y_ref,
                    sre_ref, sim_ref, *, steps_per_stage, blocks_per_step):
    t = pl.program_id(1)
    n, _, f_w = u_ref.shape
    r8 = SUBLANES
    m = n * r8

    @pl.when(t < steps_per_stage)
    def _stage_a():
        for j in range(blocks_per_step):
            xin = u_ref[:, j * r8:(j + 1) * r8, :].reshape(m, f_w).astype(BF16)
            a = _dot(fa_ref[...], xin)
            reps = f_w // LANES
            cos_t = jnp.concatenate([twc_ref[j]] * reps, axis=1)
            sin_t = jnp.concatenate([tws_ref[j]] * reps, axis=1)
            mh = (n // 2 + 1) * r8
            mirror = lambda z: [z[(n - s1) * r8:(n - s1 + 1) * r8] for s1 in range(n // 2 + 1, n)]
            a_re = jnp.concatenate([a[0:mh]] + mirror(a[0:mh]), axis=0)
            a_im = jnp.concatenate([a[mh:2 * mh]] + [-z for z in mirror(a[mh:2 * mh])], axis=0)
            b_re = a_re * cos_t + a_im * sin_t
            b_im = a_im * cos_t - a_re * sin_t
            off = pl.multiple_of((t * blocks_per_step + j) * r8, r8)
            sre_ref[:, pl.ds(off, r8), :] = b_re.reshape(n, r8, f_w)
            sim_ref[:, pl.ds(off, r8), :] = b_im.reshape(n, r8, f_w)

    @pl.when((t >= steps_per_stage) & (t < 2 * steps_per_stage))
    def _stage_b():
        pair_rows = 2 * n
        for j in range(blocks_per_step):
            off = pl.multiple_of(((t - steps_per_stage) * blocks_per_step + j) * r8, r8)
            br = sre_ref[pl.ds(off, r8)].reshape(m, f_w)
            bi = sim_ref[pl.ds(off, r8)].reshape(m, f_w)
            p_re, p_im = [], []
            for pp in range(r8 // 2):
                rs_ = slice(pp * pair_rows, (pp + 1) * pair_rows)
                bcat = jnp.concatenate([br[rs_], bi[rs_]], axis=0).astype(BF16)
                p = _dot(g_ref[pp], bcat)
                p_re.append(p[0:pair_rows])
                p_im.append(p[pair_rows:2 * pair_rows])
            p_re = jnp.concatenate(p_re, axis=0).astype(BF16)
            p_im = jnp.concatenate(p_im, axis=0).astype(BF16)
            y = jnp.concatenate(
                [_dot(jnp.concatenate([p_re[:, c0:c0 + MXU_DIM], p_im[:, c0:c0 + MXU_DIM]], axis=1),
                      cs_ref[...]) for c0 in range(0, f_w, MXU_DIM)], axis=1)
            sre_ref[pl.ds(off, r8)] = y.reshape(r8, n, f_w)

    @pl.when(t >= 2 * steps_per_stage)
    def _stage_c():
        for j in range(blocks_per_step):
            off = pl.multiple_of(((t - 2 * steps_per_stage) * blocks_per_step + j) * r8, r8)
            yin = sre_ref[:, pl.ds(off, r8), :].reshape(m, f_w).astype(BF16)
            y_ref[j * m:(j + 1) * m, :] = _dot(perm_ref[...], yin).astype(y_ref.dtype)


def _fourier(u):
    bsz, s, f_w = u.shape
    rows = s // GRID_W
    assert rows == GRID_W, "the two-stage position DFT assumes a square token grid"
    r8 = SUBLANES
    nblk = rows // r8
    fa, g, cs, perm, tw_cos, tw_sin = _fourier_constants(rows, f_w)
    u4 = u.reshape(bsz, rows, GRID_W, f_w)
    blocks_per_step = FOURIER_BLOCKS_PER_STEP
    steps = nblk // blocks_per_step
    width = blocks_per_step * r8
    stage_a_step = lambda b, t: jnp.minimum(t, steps - 1)
    stage_c_step = lambda b, t: jnp.maximum(t - 2 * steps, 0)
    full = lambda a: pl.BlockSpec(a.shape, lambda b, t: (0,) * a.ndim)
    tw_spec = pl.BlockSpec((blocks_per_step,) + tw_cos.shape[1:],
                           lambda b, t: (stage_a_step(b, t), 0, 0))
    return pl.pallas_call(
        functools.partial(_fourier_kernel, steps_per_stage=steps, blocks_per_step=blocks_per_step),
        grid=(bsz, 3 * steps),
        in_specs=[
            pl.BlockSpec((None, rows, width, f_w), lambda b, t: (b, 0, stage_a_step(b, t), 0)),
            full(fa), full(g), full(cs), full(perm), tw_spec, tw_spec,
        ],
        out_specs=pl.BlockSpec((None, width * GRID_W, f_w), lambda b, t: (b, stage_c_step(b, t), 0)),
        out_shape=jax.ShapeDtypeStruct((bsz, s, f_w), BF16),
        scratch_shapes=[pltpu.VMEM((rows, GRID_W, f_w), F32), pltpu.VMEM((rows, GRID_W, f_w), F32)],
        compiler_params=_params("arbitrary", "arbitrary"),
        name="fourier",
    )(u4, fa, g, cs, perm, tw_cos, tw_sin)


def _bias_tables(rpb, rows):
    kw = min(WIN_W, GRID_W)
    cols = np.arange(GRID_W)
    col_start = np.clip(cols - kw // 2, 0, GRID_W - kw)
    kj = cols[None, :]
    allowed = (kj >= col_start[:, None]) & (kj < col_start[:, None] + kw)
    rel = kj - cols[:, None] + (WIN_W - 1)
    nrel = 2 * WIN_W - 1
    onehot = ((rel[None] == np.arange(nrel)[:, None, None]) & allowed[None]).astype(np.float32)
    sel = np.zeros((2 * nrel + 1, GRID_W, 2 * GRID_W), np.float32)
    sel[0:nrel, :, 0:GRID_W] = onehot
    sel[nrel:2 * nrel, :, GRID_W:] = onehot
    sel[2 * nrel] = np.tile(~allowed, (1, 2))
    rpb = rpb.astype(F32) * LOG2_E
    mask_col = jnp.full(rpb.shape[:1] + (rpb.shape[1] - 1, 1), MASK_VALUE, F32)
    src = jnp.concatenate([rpb[:, :-1], rpb[:, 1:], mask_col], axis=-1)
    return jnp.einsum("hmd,dcx->hmcx", src, jnp.asarray(sel), precision=lax.Precision.HIGHEST)


def _attn_out_kernel(q_ref, k_ref, v_ref, t2_ref, yfn_ref, ga_ref, gb_ref, x_ref,
                     wna_ref, wfn_ref, bfn_ref, wo_ref, g1_ref, sc2_ref, sh2_ref, nw2_ref,
                     o_ref, h2_ref, yna_ref, s_ref, p_ref,
                     *, rows, kh, rows_per_step):
    t = pl.program_id(0)
    last_tile = pl.num_programs(0) - 2

    @pl.when(t == 0)
    def _init():
        yna_ref[...] = jnp.zeros_like(yna_ref)

    yna_prev = yna_ref[...].astype(BF16)
    yfn_prev = yfn_ref[...]

    rb = lax.rem(jnp.minimum(t, last_tile), rows // rows_per_step)
    n_pairs = q_ref.shape[1] // LANES
    lane = lax.broadcasted_iota(jnp.int32, (1, LANES), 1)
    low = lane < HEAD_DIM
    mask_lo = jnp.where(low, 1.0, 0.0).astype(BF16)
    mask_hi = jnp.where(low, 0.0, 1.0).astype(BF16)
    n_keys = kh * GRID_W

    key_start = []
    for j in range(rows_per_step):
        r = rb * rows_per_step + j
        rs = jnp.clip(r - kh // 2, 0, rows - kh)
        key_start.append((pl.multiple_of(rs * GRID_W, GRID_W), rs - r + (WIN_H - 1)))

    def qk_tile(ti):
        j, p = divmod(ti, n_pairs)
        ks, idx0 = key_start[j]
        cs = slice(LANES * p, LANES * (p + 1))
        q2 = q_ref[j * GRID_W:(j + 1) * GRID_W, cs]
        k2 = k_ref[pl.ds(ks, n_keys), cs]
        qs = jnp.concatenate([q2 * mask_lo, q2 * mask_hi], axis=0)
        s = lax.dot_general(qs, k2, (((1,), (1,)), ((), ())), preferred_element_type=F32)
        bias = jnp.concatenate(
            [jnp.concatenate([t2_ref[2 * p + hh, idx0 + 2 * m] for m in range(kh // 2)], axis=1)
             for hh in range(2)], axis=0)
        s_ref[ti] = s + bias

    def softmax_tile(ti):
        s = s_ref[ti]
        p_ref[ti] = jnp.exp2(s - jnp.max(s, axis=-1, keepdims=True)).astype(BF16)

    ones = jnp.ones((n_keys, LANES), BF16)
    row_outs = []

    def pv_tile(ti):
        j, p = divmod(ti, n_pairs)
        ks, _ = key_start[j]
        v2 = v_ref[pl.ds(ks, n_keys), LANES * p:LANES * (p + 1)]
        o = _dot(p_ref[ti], jnp.concatenate([v2, ones], axis=1))
        o = o[:, 0:LANES] * (1.0 / o[:, LANES:2 * LANES])
        row_outs.append(jnp.where(low, o[0:GRID_W], o[GRID_W:2 * GRID_W]))
        if p == n_pairs - 1:
            yna_ref[j * GRID_W:(j + 1) * GRID_W, :] = jnp.concatenate(row_outs, axis=1)
            row_outs.clear()

    d = o_ref.shape[1]
    merged = []

    def merge_chunk(c0):
        cols = slice(c0, c0 + MXU_DIM)
        ya = _dot(yna_prev, wna_ref[:, cols])
        yf = _dot(yfn_prev, wfn_ref[:, cols]) + bfn_ref[:, cols]
        merged.append((ga_ref[:, cols].astype(F32) * ya
                       + gb_ref[:, cols].astype(F32) * yf).astype(BF16))

    def out_chunk(c0):
        cols = slice(c0, c0 + MXU_DIM)
        lhs = jnp.concatenate(merged, axis=1)
        o_ref[:, cols] = x_ref[:, cols] + g1_ref[:, cols] * _dot(lhs, wo_ref[:, cols])

    norm_rows = o_ref.shape[0] // rows_per_step
    gain2 = nw2_ref[...] * (1.0 + sc2_ref[...])

    def norm_chunk(j):
        nr = slice(j * norm_rows, (j + 1) * norm_rows)
        x1 = o_ref[nr, :]
        ms = jnp.mean(x1 * x1, axis=-1, keepdims=True)
        h2_ref[nr, :] = (x1 * lax.rsqrt(ms + EPS) * gain2 + sh2_ref[...]).astype(BF16)

    matmul_units = ([functools.partial(merge_chunk, c0) for c0 in range(0, d, MXU_DIM)]
                    + [functools.partial(out_chunk, c0) for c0 in range(0, d, MXU_DIM)])
    norm_units = [functools.partial(norm_chunk, j) for j in range(rows_per_step)]

    n_tiles = rows_per_step * n_pairs
    for u in range(n_tiles + PV_SKEW):
        if u < n_tiles:
            qk_tile(u)
        if 0 <= u - SOFTMAX_SKEW < n_tiles:
            softmax_tile(u - SOFTMAX_SKEW)
        if 0 <= u - PV_SKEW < n_tiles:
            pv_tile(u - PV_SKEW)
        if matmul_units:
            if u % OUT_UNIT_EVERY == 1:
                matmul_units.pop(0)()
        elif norm_units and u % NORM_UNIT_EVERY == 0:
            norm_units.pop(0)()
    assert not matmul_units
    while norm_units:
        norm_units.pop(0)()


def _attn_out(q, k, v, t2, yfn, ga, gb, x, w_na, w_fn, b_fn, w_o, mod, norm2_w):
    bsz, s, d = x.shape
    na_w = q.shape[-1]
    f_w = yfn.shape[-1]
    rows = s // GRID_W
    kh = min(WIN_H, rows)
    assert kh % 2 == 0 and (2 * HEAD_DIM) == LANES
    rows_per_step = ATTN_ROWS_PER_STEP
    tq = rows_per_step * GRID_W
    n_tiles = rows_per_step * (na_w // LANES)

    per_seq = s // tq
    n_steps = bsz * per_seq + 1
    attn_tile = lambda t: jnp.minimum(t, n_steps - 2)
    out_tile = lambda t: jnp.maximum(t - 1, 0)
    tok_a = lambda w: pl.BlockSpec(
        (None, tq, w), lambda t: (attn_tile(t) // per_seq, attn_tile(t) % per_seq, 0))
    tok_o = lambda w: pl.BlockSpec(
        (None, tq, w), lambda t: (out_tile(t) // per_seq, out_tile(t) % per_seq, 0))
    whole_seq = pl.BlockSpec((None, s, na_w), lambda t: (attn_tile(t) // per_seq, 0, 0))
    mod_chunk = lambda c: pl.BlockSpec((None, 1, d), lambda t: (out_tile(t) // per_seq, 0, c))
    full = lambda a: pl.BlockSpec(a.shape, lambda t: (0,) * a.ndim)
    bfn = b_fn.reshape(1, d)
    nw2 = norm2_w.reshape(1, d)
    return pl.pallas_call(
        functools.partial(_attn_out_kernel, rows=rows, kh=kh, rows_per_step=rows_per_step),
        grid=(n_steps,),
        in_specs=[tok_a(na_w), whole_seq, whole_seq, full(t2), tok_o(f_w), tok_o(d), tok_o(d),
                  tok_o(d), full(w_na), full(w_fn), full(bfn), full(w_o), mod_chunk(MOD_GATE1),
                  mod_chunk(MOD_SCALE2), mod_chunk(MOD_SHIFT2), full(nw2)],
        out_specs=(tok_o(d), tok_o(d)),
        out_shape=(jax.ShapeDtypeStruct((bsz, s, d), F32), jax.ShapeDtypeStruct((bsz, s, d), BF16)),
        scratch_shapes=[pltpu.VMEM((tq, na_w), F32),
                        pltpu.VMEM((n_tiles, 2 * GRID_W, kh * GRID_W), F32),
                        pltpu.VMEM((n_tiles, 2 * GRID_W, kh * GRID_W), BF16)],
        compiler_params=_params("arbitrary"),
        name="attn_out",
    )(q, k, v, t2, yfn, ga, gb, x, w_na, w_fn, bfn, w_o, mod, mod, mod, nw2)


def _mlp_kernel(x_ref, h_ref, g2_ref, w1_ref, w2_ref, o_ref, *, chunk):
    h = h_ref[...]
    acc = None
    for c0 in range(0, w1_ref.shape[1], chunk):
        a = jnp.maximum(_dot(h, w1_ref[:, c0:c0 + chunk]), 0.0)
        term = _dot((a * a).astype(BF16), w2_ref[c0:c0 + chunk, :])
        acc = term if acc is None else acc + term
    o_ref[...] = x_ref[...] + g2_ref[...] * acc


def _mlp(x, h, mod, w1, w2):
    bsz, s, d = x.shape
    tm = MLP_TOKENS
    tok = pl.BlockSpec((None, tm, d), lambda b, i: (b, i, 0))
    gate2 = pl.BlockSpec((None, 1, d), lambda b, i: (b, 0, MOD_GATE2))
    full = lambda a: pl.BlockSpec(a.shape, lambda b, i: (0,) * a.ndim)
    return pl.pallas_call(
        functools.partial(_mlp_kernel, chunk=MLP_FF_CHUNK),
        grid=(bsz, s // tm),
        in_specs=[tok, tok, gate2, full(w1), full(w2)],
        out_specs=tok,
        out_shape=jax.ShapeDtypeStruct((bsz, s, d), F32),
        compiler_params=_params("parallel", "parallel"),
        name="mlp",
    )(x, h, mod, w1, w2)


def kernel(x, c, norm1_w, norm2_w, w_ada, b_ada, w_in, b_in, q_norm_w, k_norm_w, rpb,
           w_na_out, w_fn_out, b_fn_out, w_o, w_mlp_in, w_mlp_out):
    bsz, s, d = x.shape
    depth = w_ada.shape[0]
    na_w = w_na_out.shape[1]
    f_w = w_fn_out.shape[1]
    n_heads = na_w // HEAD_DIM
    rows = s // GRID_W
    for l in range(depth):
        mod, w_in_b = _ada(c, w_ada[l], b_ada[l], w_in[l])

        qk_w = jnp.concatenate([jnp.tile(q_norm_w[l], n_heads) * (HEAD_DIM ** -0.5 * LOG2_E),
                                jnp.tile(k_norm_w[l], n_heads)]).reshape(1, 2 * na_w)
        (q, k, v, u, ga, gb), (w_na, w_fn, w_o_b, w_m1, w_m2) = _inproj(
            x, mod, norm1_w[l], w_in_b, b_in[l], qk_w, na_w, f_w,
            (w_na_out[l], w_fn_out[l], w_o[l], w_mlp_in[l], w_mlp_out[l]))
        yfn = _fourier(u)
        t2 = _bias_tables(rpb[l], rows)
        x, h2 = _attn_out(q, k, v, t2, yfn, ga, gb, x, w_na, w_fn, b_fn_out[l], w_o_b, mod,
                          norm2_w[l])
        x = _mlp(x, h2, mod, w_m1, w_m2)
    return x
```

```python
import functools

import numpy as np
import jax
import jax.numpy as jnp
from jax import lax
from jax.experimental import pallas as pl
from jax.experimental.pallas import tpu as pltpu

F32 = jnp.float32
BF16 = jnp.bfloat16

GRID_W = 64
HEAD_DIM = 64
WIN_H = 8
WIN_W = 16
N_FOURIER_GROUPS = 4
N_MOD = 6
MOD_SHIFT1, MOD_SCALE1, MOD_GATE1, MOD_SHIFT2, MOD_SCALE2, MOD_GATE2 = range(N_MOD)
EPS = 1e-6
MASK_VALUE = -1e30
LOG2_E = 1.4426950408889634

SUBLANES = 8
LANES = 128
MXU_DIM = 256
SOFTMAX_SKEW = 3
PV_SKEW = 6
OUT_UNIT_EVERY = 4
NORM_UNIT_EVERY = 1
VMEM_LIMIT_BYTES = 56 * 1024 * 1024

ADA_STEPS = 8
INPROJ_TOKENS = 1024
INPROJ_CHUNK = 1024
FOURIER_BLOCKS_PER_STEP = 8
ATTN_ROWS_PER_STEP = 8
MLP_TOKENS = 1024
MLP_FF_CHUNK = 1024


def _dot(a, b):
    return jnp.dot(a, b, preferred_element_type=F32)


def _sigmoid(x):
    return 0.5 * jnp.tanh(0.5 * x) + 0.5


def _params(*semantics):
    return pltpu.CompilerParams(dimension_semantics=semantics, vmem_limit_bytes=VMEM_LIMIT_BYTES)


def _ada_kernel(ct_ref, w_ref, b_ref, win_ref, o_ref, win_bf16_ref):
    ct = ct_ref[...]
    st = ct * _sigmoid(ct)
    w = w_ref[...]
    for b in range(ct.shape[1]):
        o_ref[b] = jnp.sum(w * st[:, b:b + 1], axis=0, keepdims=True) + b_ref[...]
    win_bf16_ref[...] = win_ref[...].astype(BF16)


def _ada(c, w, b, w_in):
    bsz, d = c.shape
    n = w.shape[1]
    n_steps = ADA_STEPS
    tn = n // n_steps
    slab = w_in.shape[0] // n_steps
    assert n % (n_steps * LANES) == 0 and w_in.shape[0] % (n_steps * 2 * SUBLANES) == 0
    return pl.pallas_call(
        _ada_kernel,
        grid=(n_steps,),
        in_specs=[pl.BlockSpec((d, bsz), lambda j: (0, 0)),
                  pl.BlockSpec((d, tn), lambda j: (0, j)),
                  pl.BlockSpec((1, tn), lambda j: (0, j)),
                  pl.BlockSpec((slab, w_in.shape[1]), lambda j: (j, 0))],
        out_specs=(pl.BlockSpec((bsz, 1, tn), lambda j: (0, 0, j)),
                   pl.BlockSpec((slab, w_in.shape[1]), lambda j: (j, 0))),
        out_shape=(jax.ShapeDtypeStruct((bsz, 1, n), F32),
                   jax.ShapeDtypeStruct(w_in.shape, BF16)),
        compiler_params=_params("parallel"),
        name="ada",
    )(c.T, w, b.reshape(1, n), w_in)


def _inproj_kernel(x_ref, sc_ref, sh_ref, nw_ref, w_ref, b_ref, qkw_ref, *refs,
                   na_w, f_w, d, chunk, n_cast):
    cast_in = refs[:n_cast]
    q_ref, k_ref, v_ref, u_ref, ga_ref, gb_ref = refs[n_cast:n_cast + 6]
    cast_out = refs[n_cast + 6:]
    for src, dst in zip(cast_in, cast_out):
        dst[...] = src[...].astype(BF16)

    gain = nw_ref[...] * (1.0 + sc_ref[...])

    def normed(c0):
        x = x_ref[c0:c0 + chunk, :]
        ms = jnp.mean(x * x, axis=-1, keepdims=True)
        return (x * lax.rsqrt(ms + EPS) * gain + sh_ref[...]).astype(BF16)

    def proj(h, lo, width):
        return _dot(h, w_ref[:, lo:lo + width]) + b_ref[:, lo:lo + width]

    qk = 2 * na_w
    o_v, o_u, o_ga, o_gb = qk, qk + na_w, qk + na_w + f_w, qk + na_w + f_w + d
    n_rows = x_ref.shape[0]
    first_head = lax.broadcasted_iota(jnp.int32, (1, LANES), 1) < HEAD_DIM
    h = normed(0)
    for c0 in range(0, n_rows, chunk):
        rows = slice(c0, c0 + chunk)
        zqk = proj(h, 0, qk)
        h_next = normed(c0 + chunk) if c0 + chunk < n_rows else None
        ga_ref[rows, :] = _sigmoid(proj(h, o_ga, d)).astype(BF16)
        z2 = zqk * zqk
        gb_ref[rows, :] = _sigmoid(proj(h, o_gb, d)).astype(BF16)
        scale = []
        for g in range(qk // LANES):
            blk = z2[:, g * LANES:(g + 1) * LANES]
            ms_a = jnp.sum(jnp.where(first_head, blk, 0.0), axis=-1, keepdims=True) * (1.0 / HEAD_DIM)
            ms_b = jnp.sum(jnp.where(first_head, 0.0, blk), axis=-1, keepdims=True) * (1.0 / HEAD_DIM)
            scale.append(jnp.where(first_head, lax.rsqrt(ms_a + EPS), lax.rsqrt(ms_b + EPS)))
        scale = jnp.concatenate(scale, axis=1)
        v_ref[rows, :] = proj(h, o_v, na_w).astype(BF16)
        zn = zqk * scale * qkw_ref[...]
        q_ref[rows, :] = zn[:, 0:na_w].astype(BF16)
        k_ref[rows, :] = zn[:, na_w:qk].astype(BF16)
        u_ref[rows, :] = proj(h, o_u, f_w)
        h = h_next


def _inproj(x, mod, norm_w, w_in, b_in, qk_w, na_w, f_w, later_weights):
    bsz, s, d = x.shape
    in_w = w_in.shape[1]
    tm = INPROJ_TOKENS
    assert 2 * HEAD_DIM == LANES

    tok = lambda w: pl.BlockSpec((None, tm, w), lambda b, i: (b, i, 0))
    mod_chunk = lambda c: pl.BlockSpec((None, 1, d), lambda b, i: (b, 0, c))
    full = lambda a: pl.BlockSpec(a.shape, lambda b, i: (0,) * a.ndim)
    nw = norm_w.reshape(1, d)
    bi = b_in.reshape(1, in_w)
    out_shapes = (
        jax.ShapeDtypeStruct((bsz, s, na_w), BF16),
        jax.ShapeDtypeStruct((bsz, s, na_w), BF16),
        jax.ShapeDtypeStruct((bsz, s, na_w), BF16),
        jax.ShapeDtypeStruct((bsz, s, f_w), F32),
        jax.ShapeDtypeStruct((bsz, s, d), BF16),
        jax.ShapeDtypeStruct((bsz, s, d), BF16),
    )
    n_steps = bsz * (s // tm)
    slab = lambda a: pl.BlockSpec((a.shape[0] // n_steps, a.shape[1]),
                                  lambda b, i: (b * (s // tm) + i, 0))
    for a in later_weights:
        assert a.shape[0] % (n_steps * 2 * SUBLANES) == 0, a.shape
    outs = pl.pallas_call(
        functools.partial(_inproj_kernel, na_w=na_w, f_w=f_w, d=d, chunk=INPROJ_CHUNK,
                          n_cast=len(later_weights)),
        grid=(bsz, s // tm),
        in_specs=[tok(d), mod_chunk(MOD_SCALE1), mod_chunk(MOD_SHIFT1), full(nw), full(w_in),
                  full(bi), full(qk_w)] + [slab(a) for a in later_weights],
        out_specs=(tok(na_w), tok(na_w), tok(na_w), tok(f_w), tok(d), tok(d))
        + tuple(slab(a) for a in later_weights),
        out_shape=out_shapes + tuple(jax.ShapeDtypeStruct(a.shape, BF16) for a in later_weights),
        compiler_params=_params("parallel", "parallel"),
        name="inproj",
    )(x, mod, mod, nw, w_in, bi, qk_w, *later_weights)
    return outs[:6], outs[6:]


def _fourier_constants(rows, f_w):
    n = rows
    r8 = SUBLANES
    k = np.arange(n)
    ang_a = 2.0 * np.pi * np.outer(k, k) / n
    eye8 = np.eye(r8)
    norm = 1.0 / np.sqrt(n)
    half = n // 2 + 1
    fa_re = np.kron(np.cos(ang_a[:half]), eye8) * norm
    fa_im = np.kron(-np.sin(ang_a[:half]), eye8) * norm
    fa = np.concatenate([fa_re, fa_im], axis=0)

    nblk = n // r8
    s2 = k[None, :]
    s2p = k[:, None]
    g = np.zeros((r8 // 2, 4 * n, 4 * n))
    for pp in range(r8 // 2):
        gr = np.zeros((2, n, 2, n))
        gi = np.zeros((2, n, 2, n))
        for ll in range(2):
            l = 2 * pp + ll
            ang = 2.0 * np.pi * (s2 * l / (n * GRID_W) + s2 * s2p / n)
            gr[ll, :, ll, :] = np.cos(ang) * norm
            gi[ll, :, ll, :] = -np.sin(ang) * norm
        gr = gr.reshape(2 * n, 2 * n)
        gi = gi.reshape(2 * n, 2 * n)
        g[pp] = np.block([[gr, -gi], [gi, gr]])

    perm = np.zeros((n * r8, n * r8))
    s1p = np.repeat(k, r8)
    jj = np.tile(np.arange(r8), n)
    perm[jj * n + s1p, s1p * r8 + jj] = 1.0

    t_idx = np.arange(nblk)[:, None, None]
    blk_of_row = (k // r8)[None, :, None]
    j_idx = np.arange(r8)[None, None, :]
    ang_t = 2.0 * np.pi * (t_idx * r8 + j_idx) * blk_of_row * r8 / (n * GRID_W)
    ang_t = ang_t.reshape(nblk, n * r8, 1)
    tw_cos = np.broadcast_to(np.cos(ang_t), (nblk, n * r8, LANES))
    tw_sin = np.broadcast_to(np.sin(ang_t), (nblk, n * r8, LANES))

    gd = f_w // N_FOURIER_GROUPS
    c = np.arange(gd)
    ang_c = 2.0 * np.pi * np.outer(c, c) / gd
    assert MXU_DIM % gd == 0 and f_w % MXU_DIM == 0
    eye_g = np.eye(MXU_DIM // gd)
    cs = np.concatenate([np.kron(eye_g, np.cos(ang_c)), np.kron(eye_g, np.sin(ang_c))],
                        axis=0) / np.sqrt(gd)
    as_f32 = lambda a: jnp.asarray(np.ascontiguousarray(a, dtype=np.float32))
    to_bf16 = lambda a: as_f32(a).astype(BF16)
    return (to_bf16(fa), to_bf16(g), to_bf16(cs), to_bf16(perm), as_f32(tw_cos), as_f32(tw_sin))


def _fourier_kernel(u_ref, fa_ref, g_ref, cs_ref, perm_ref, twc_ref, tws_ref, y_ref,
                    sre_ref, sim_ref, *, steps_per_stage, blocks_per_step):
    t = pl.program_id(1)
    n, _, f_w = u_ref.shape
    r8 = SUBLANES
    m = n * r8

    @pl.when(t < steps_per_stage)
    def _stage_a():
        for j in range(blocks_per_step):
            xin = u_ref[:, j * r8:(j + 1) * r8, :].reshape(m, f_w).astype(BF16)
            a = _dot(fa_ref[...], xin)
            reps = f_w // LANES
            cos_t = jnp.concatenate([twc_ref[j]] * reps, axis=1)
            sin_t = jnp.concatenate([tws_ref[j]] * reps, axis=1)
            mh = (n // 2 + 1) * r8
            mirror = lambda z: [z[(n - s1) * r8:(n - s1 + 1) * r8] for s1 in range(n // 2 + 1, n)]
            a_re = jnp.concatenate([a[0:mh]] + mirror(a[0:mh]), axis=0)
            a_im = jnp.concatenate([a[mh:2 * mh]] + [-z for z in mirror(a[mh:2 * mh])], axis=0)
            b_re = a_re * cos_t + a_im * sin_t
            b_im = a_im * cos_t - a_re * sin_t
            off = pl.multiple_of((t * blocks_per_step + j) * r8, r8)
            sre_ref[:, pl.ds(off, r8), :] = b_re.reshape(n, r8, f_w)
            sim_ref[:, pl.ds(off, r8), :] = b_im.reshape(n, r8, f_w)

    @pl.when((t >= steps_per_stage) & (t < 2 * steps_per_stage))
    def _stage_b():
        pair_rows = 2 * n
        for j in range(blocks_per_step):
            off = pl.multiple_of(((t - steps_per_stage) * blocks_per_step + j) * r8, r8)
            br = sre_ref[pl.ds(off, r8)].reshape(m, f_w)
            bi = sim_ref[pl.ds(off, r8)].reshape(m, f_w)
            p_re, p_im = [], []
            for pp in range(r8 // 2):
                rs_ = slice(pp * pair_rows, (pp + 1) * pair_rows)
                bcat = jnp.concatenate([br[rs_], bi[rs_]], axis=0).astype(BF16)
                p = _dot(g_ref[pp], bcat)
                p_re.append(p[0:pair_rows])
                p_im.append(p[pair_rows:2 * pair_rows])
            p_re = jnp.concatenate(p_re, axis=0).astype(BF16)
            p_im = jnp.concatenate(p_im, axis=0).astype(BF16)
            y = jnp.concatenate(
                [_dot(jnp.concatenate([p_re[:, c0:c0 + MXU_DIM], p_im[:, c0:c0 + MXU_DIM]], axis=1),
                      cs_ref[...]) for c0 in range(0, f_w, MXU_DIM)], axis=1)
            sre_ref[pl.ds(off, r8)] = y.reshape(r8, n, f_w)

    @pl.when(t >= 2 * steps_per_stage)
    def _stage_c():
        for j in range(blocks_per_step):
            off = pl.multiple_of(((t - 2 * steps_per_stage) * blocks_per_step + j) * r8, r8)
            yin = sre_ref[:, pl.ds(off, r8), :].reshape(m, f_w).astype(BF16)
            y_ref[j * m:(j + 1) * m, :] = _dot(perm_ref[...], yin).astype(y_ref.dtype)


def _fourier(u):
    bsz, s, f_w = u.shape
    rows = s // GRID_W
    assert rows == GRID_W, "the two-stage position DFT assumes a square token grid"
    r8 = SUBLANES
    nblk = rows // r8
    fa, g, cs, perm, tw_cos, tw_sin = _fourier_constants(rows, f_w)
    u4 = u.reshape(bsz, rows, GRID_W, f_w)
    blocks_per_step = FOURIER_BLOCKS_PER_STEP
    steps = nblk // blocks_per_step
    width = blocks_per_step * r8
    stage_a_step = lambda b, t: jnp.minimum(t, steps - 1)
    stage_c_step = lambda b, t: jnp.maximum(t - 2 * steps, 0)
    full = lambda a: pl.BlockSpec(a.shape, lambda b, t: (0,) * a.ndim)
    tw_spec = pl.BlockSpec((blocks_per_step,) + tw_cos.shape[1:],
                           lambda b, t: (stage_a_step(b, t), 0, 0))
    return pl.pallas_call(
        functools.partial(_fourier_kernel, steps_per_stage=steps, blocks_per_step=blocks_per_step),
        grid=(bsz, 3 * steps),
        in_specs=[
            pl.BlockSpec((None, rows, width, f_w), lambda b, t: (b, 0, stage_a_step(b, t), 0)),
            full(fa), full(g), full(cs), full(perm), tw_spec, tw_spec,
        ],
        out_specs=pl.BlockSpec((None, width * GRID_W, f_w), lambda b, t: (b, stage_c_step(b, t), 0)),
        out_shape=jax.ShapeDtypeStruct((bsz, s, f_w), BF16),
        scratch_shapes=[pltpu.VMEM((rows, GRID_W, f_w), F32), pltpu.VMEM((rows, GRID_W, f_w), F32)],
        compiler_params=_params("arbitrary", "arbitrary"),
        name="fourier",
    )(u4, fa, g, cs, perm, tw_cos, tw_sin)


def _bias_tables(rpb, rows):
    kw = min(WIN_W, GRID_W)
    cols = np.arange(GRID_W)
    col_start = np.clip(cols - kw // 2, 0, GRID_W - kw)
    kj = cols[None, :]
    allowed = (kj >= col_start[:, None]) & (kj < col_start[:, None] + kw)
    rel = kj - cols[:, None] + (WIN_W - 1)
    nrel = 2 * WIN_W - 1
    onehot = ((rel[None] == np.arange(nrel)[:, None, None]) & allowed[None]).astype(np.float32)
    sel = np.zeros((2 * nrel + 1, GRID_W, 2 * GRID_W), np.float32)
    sel[0:nrel, :, 0:GRID_W] = onehot
    sel[nrel:2 * nrel, :, GRID_W:] = onehot
    sel[2 * nrel] = np.tile(~allowed, (1, 2))
    rpb = rpb.astype(F32) * LOG2_E
    mask_col = jnp.full(rpb.shape[:1] + (rpb.shape[1] - 1, 1), MASK_VALUE, F32)
    src = jnp.concatenate([rpb[:, :-1], rpb[:, 1:], mask_col], axis=-1)
    return jnp.einsum("hmd,dcx->hmcx", src, jnp.asarray(sel), precision=lax.Precision.HIGHEST)


def _attn_out_kernel(q_ref, k_ref, v_ref, t2_ref, yfn_ref, ga_ref, gb_ref, x_ref,
                     wna_ref, wfn_ref, bfn_ref, wo_ref, g1_ref, sc2_ref, sh2_ref, nw2_ref,
                     o_ref, h2_ref, yna_ref, s_ref, p_ref,
                     *, rows, kh, rows_per_step):
    t = pl.program_id(0)
    last_tile = pl.num_programs(0) - 2

    @pl.when(t == 0)
    def _init():
        yna_ref[...] = jnp.zeros_like(yna_ref)

    yna_prev = yna_ref[...].astype(BF16)
    yfn_prev = yfn_ref[...]

    rb = lax.rem(jnp.minimum(t, last_tile), rows // rows_per_step)
    n_pairs = q_ref.shape[1] // LANES
    lane = lax.broadcasted_iota(jnp.int32, (1, LANES), 1)
    low = lane < HEAD_DIM
    mask_lo = jnp.where(low, 1.0, 0.0).astype(BF16)
    mask_hi = jnp.where(low, 0.0, 1.0).astype(BF16)
    n_keys = kh * GRID_W

    key_start = []
    for j in range(rows_per_step):
        r = rb * rows_per_step + j
        rs = jnp.clip(r - kh // 2, 0, rows - kh)
        key_start.append((pl.multiple_of(rs * GRID_W, GRID_W), rs - r + (WIN_H - 1)))

    def qk_tile(ti):
        j, p = divmod(ti, n_pairs)
        ks, idx0 = key_start[j]
        cs = slice(LANES * p, LANES * (p + 1))
        q2 = q_ref[j * GRID_W:(j + 1) * GRID_W, cs]
        k2 = k_ref[pl.ds(ks, n_keys), cs]
        qs = jnp.concatenate([q2 * mask_lo, q2 * mask_hi], axis=0)
        s = lax.dot_general(qs, k2, (((1,), (1,)), ((), ())), preferred_element_type=F32)
        bias = jnp.concatenate(
            [jnp.concatenate([t2_ref[2 * p + hh, idx0 + 2 * m] for m in range(kh // 2)], axis=1)
             for hh in range(2)], axis=0)
        s_ref[ti] = s + bias

    def softmax_tile(ti):
        s = s_ref[ti]
        p_ref[ti] = jnp.exp2(s - jnp.max(s, axis=-1, keepdims=True)).astype(BF16)

    ones = jnp.ones((n_keys, LANES), BF16)
    row_outs = []

    def pv_tile(ti):
        j, p = divmod(ti, n_pairs)
        ks, _ = key_start[j]
        v2 = v_ref[pl.ds(ks, n_keys), LANES * p:LANES * (p + 1)]
        o = _dot(p_ref[ti], jnp.concatenate([v2, ones], axis=1))
        o = o[:, 0:LANES] * (1.0 / o[:, LANES:2 * LANES])
        row_outs.append(jnp.where(low, o[0:GRID_W], o[GRID_W:2 * GRID_W]))
        if p == n_pairs - 1:
            yna_ref[j * GRID_W:(j + 1) * GRID_W, :] = jnp.concatenate(row_outs, axis=1)
            row_outs.clear()

    d = o_ref.shape[1]
    merged = []

    def merge_chunk(c0):
        cols = slice(c0, c0 + MXU_DIM)
        ya = _dot(yna_prev, wna_ref[:, cols])
        yf = _dot(yfn_prev, wfn_ref[:, cols]) + bfn_ref[:, cols]
        merged.append((ga_ref[:, cols].astype(F32) * ya
                       + gb_ref[:, cols].astype(F32) * yf).astype(BF16))

    def out_chunk(c0):
        cols = slice(c0, c0 + MXU_DIM)
        lhs = jnp.concatenate(merged, axis=1)
        o_ref[:, cols] = x_ref[:, cols] + g1_ref[:, cols] * _dot(lhs, wo_ref[:, cols])

    norm_rows = o_ref.shape[0] // rows_per_step
    gain2 = nw2_ref[...] * (1.0 + sc2_ref[...])

    def norm_chunk(j):
        nr = slice(j * norm_rows, (j + 1) * norm_rows)
        x1 = o_ref[nr, :]
        ms = jnp.mean(x1 * x1, axis=-1, keepdims=True)
        h2_ref[nr, :] = (x1 * lax.rsqrt(ms + EPS) * gain2 + sh2_ref[...]).astype(BF16)

    matmul_units = ([functools.partial(merge_chunk, c0) for c0 in range(0, d, MXU_DIM)]
                    + [functools.partial(out_chunk, c0) for c0 in range(0, d, MXU_DIM)])
    norm_units = [functools.partial(norm_chunk, j) for j in range(rows_per_step)]

    n_tiles = rows_per_step * n_pairs
    for u in range(n_tiles + PV_SKEW):
        if u < n_tiles:
            qk_tile(u)
        if 0 <= u - SOFTMAX_SKEW < n_tiles:
            softmax_tile(u - SOFTMAX_SKEW)
        if 0 <= u - PV_SKEW < n_tiles:
            pv_tile(u - PV_SKEW)
        if matmul_units:
            if u % OUT_UNIT_EVERY == 1:
                matmul_units.pop(0)()
        elif norm_units and u % NORM_UNIT_EVERY == 0:
            norm_units.pop(0)()
    assert not matmul_units
    while norm_units:
        norm_units.pop(0)()


def _attn_out(q, k, v, t2, yfn, ga, gb, x, w_na, w_fn, b_fn, w_o, mod, norm2_w):
    bsz, s, d = x.shape
    na_w = q.shape[-1]
    f_w = yfn.shape[-1]
    rows = s // GRID_W
    kh = min(WIN_H, rows)
    assert kh % 2 == 0 and (2 * HEAD_DIM) == LANES
    rows_per_step = ATTN_ROWS_PER_STEP
    tq = rows_per_step * GRID_W
    n_tiles = rows_per_step * (na_w // LANES)

    per_seq = s // tq
    n_steps = bsz * per_seq + 1
    attn_tile = lambda t: jnp.minimum(t, n_steps - 2)
    out_tile = lambda t: jnp.maximum(t - 1, 0)
    tok_a = lambda w: pl.BlockSpec(
        (None, tq, w), lambda t: (attn_tile(t) // per_seq, attn_tile(t) % per_seq, 0))
    tok_o = lambda w: pl.BlockSpec(
        (None, tq, w), lambda t: (out_tile(t) // per_seq, out_tile(t) % per_seq, 0))
    whole_seq = pl.BlockSpec((None, s, na_w), lambda t: (attn_tile(t) // per_seq, 0, 0))
    mod_chunk = lambda c: pl.BlockSpec((None, 1, d), lambda t: (out_tile(t) // per_seq, 0, c))
    full = lambda a: pl.BlockSpec(a.shape, lambda t: (0,) * a.ndim)
    bfn = b_fn.reshape(1, d)
    nw2 = norm2_w.reshape(1, d)
    return pl.pallas_call(
        functools.partial(_attn_out_kernel, rows=rows, kh=kh, rows_per_step=rows_per_step),
        grid=(n_steps,),
        in_specs=[tok_a(na_w), whole_seq, whole_seq, full(t2), tok_o(f_w), tok_o(d), tok_o(d),
                  tok_o(d), full(w_na), full(w_fn), full(bfn), full(w_o), mod_chunk(MOD_GATE1),
                  mod_chunk(MOD_SCALE2), mod_chunk(MOD_SHIFT2), full(nw2)],
        out_specs=(tok_o(d), tok_o(d)),
        out_shape=(jax.ShapeDtypeStruct((bsz, s, d), F32), jax.ShapeDtypeStruct((bsz, s, d), BF16)),
        scratch_shapes=[pltpu.VMEM((tq, na_w), F32),
                        pltpu.VMEM((n_tiles, 2 * GRID_W, kh * GRID_W), F32),
                        pltpu.VMEM((n_tiles, 2 * GRID_W, kh * GRID_W), BF16)],
        compiler_params=_params("arbitrary"),
        name="attn_out",
    )(q, k, v, t2, yfn, ga, gb, x, w_na, w_fn, bfn, w_o, mod, mod, mod, nw2)


def _mlp_kernel(x_ref, h_ref, g2_ref, w1_ref, w2_ref, o_ref, *, chunk):
    h = h_ref[...]
    acc = None
    for c0 in range(0, w1_ref.shape[1], chunk):
        a = jnp.maximum(_dot(h, w1_ref[:, c0:c0 + chunk]), 0.0)
        term = _dot((a * a).astype(BF16), w2_ref[c0:c0 + chunk, :])
        acc = term if acc is None else acc + term
    o_ref[...] = x_ref[...] + g2_ref[...] * acc


def _mlp(x, h, mod, w1, w2):
    bsz, s, d = x.shape
    tm = MLP_TOKENS
    tok = pl.BlockSpec((None, tm, d), lambda b, i: (b, i, 0))
    gate2 = pl.BlockSpec((None, 1, d), lambda b, i: (b, 0, MOD_GATE2))
    full = lambda a: pl.BlockSpec(a.shape, lambda b, i: (0,) * a.ndim)
    return pl.pallas_call(
        functools.partial(_mlp_kernel, chunk=MLP_FF_CHUNK),
        grid=(bsz, s // tm),
        in_specs=[tok, tok, gate2, full(w1), full(w2)],
        out_specs=tok,
        out_shape=jax.ShapeDtypeStruct((bsz, s, d), F32),
        compiler_params=_params("parallel", "parallel"),
        name="mlp",
    )(x, h, mod, w1, w2)


def kernel(x, c, norm1_w, norm2_w, w_ada, b_ada, w_in, b_in, q_norm_w, k_norm_w, rpb,
           w_na_out, w_fn_out, b_fn_out, w_o, w_mlp_in, w_mlp_out):
    bsz, s, d = x.shape
    depth = w_ada.shape[0]
    na_w = w_na_out.shape[1]
    f_w = w_fn_out.shape[1]
    n_heads = na_w // HEAD_DIM
    rows = s // GRID_W
    for l in range(depth):
        mod, w_in_b = _ada(c, w_ada[l], b_ada[l], w_in[l])

        qk_w = jnp.concatenate([jnp.tile(q_norm_w[l], n_heads) * (HEAD_DIM ** -0.5 * LOG2_E),
                                jnp.tile(k_norm_w[l], n_heads)]).reshape(1, 2 * na_w)
        (q, k, v, u, ga, gb), (w_na, w_fn, w_o_b, w_m1, w_m2) = _inproj(
            x, mod, norm1_w[l], w_in_b, b_in[l], qk_w, na_w, f_w,
            (w_na_out[l], w_fn_out[l], w_o[l], w_mlp_in[l], w_mlp_out[l]))
        yfn = _fourier(u)
        t2 = _bias_tables(rpb[l], rows)
        x, h2 = _attn_out(q, k, v, t2, yfn, ga, gb, x, w_na, w_fn, b_fn_out[l], w_o_b, mod,
                          norm2_w[l])
        x = _mlp(x, h2, mod, w_m1, w_m2)
    return x
```

```python
import functools

import numpy as np
import jax
import jax.numpy as jnp
from jax import lax
from jax.experimental import pallas as pl
from jax.experimental.pallas import tpu as pltpu

F32 = jnp.float32
BF16 = jnp.bfloat16

GRID_W = 64
HEAD_DIM = 64
WIN_H = 8
WIN_W = 16
N_FOURIER_GROUPS = 4
N_MOD = 6
MOD_SHIFT1, MOD_SCALE1, MOD_GATE1, MOD_SHIFT2, MOD_SCALE2, MOD_GATE2 = range(N_MOD)
EPS = 1e-6
MASK_VALUE = -1e30
LOG2_E = 1.4426950408889634

SUBLANES = 8
LANES = 128
MXU_DIM = 256
SOFTMAX_SKEW = 3
PV_SKEW = 6
OUT_UNIT_EVERY = 4
NORM_UNIT_EVERY = 1
VMEM_LIMIT_BYTES = 56 * 1024 * 1024

ADA_STEPS = 8
INPROJ_TOKENS = 1024
INPROJ_CHUNK = 1024
FOURIER_BLOCKS_PER_STEP = 8
ATTN_ROWS_PER_STEP = 8
MLP_TOKENS = 1024
MLP_FF_CHUNK = 1024


def _dot(a, b):
    return jnp.dot(a, b, preferred_element_type=F32)


def _sigmoid(x):
    return 0.5 * jnp.tanh(0.5 * x) + 0.5


def _params(*semantics):
    return pltpu.CompilerParams(dimension_semantics=semantics, vmem_limit_bytes=VMEM_LIMIT_BYTES)


def _ada_kernel(ct_ref, w_ref, b_ref, win_ref, o_ref, win_bf16_ref):
    ct = ct_ref[...]
    st = ct * _sigmoid(ct)
    w = w_ref[...]
    for b in range(ct.shape[1]):
        o_ref[b] = jnp.sum(w * st[:, b:b + 1], axis=0, keepdims=True) + b_ref[...]
    win_bf16_ref[...] = win_ref[...].astype(BF16)


def _ada(c, w, b, w_in):
    bsz, d = c.shape
    n = w.shape[1]
    n_steps = ADA_STEPS
    tn = n // n_steps
    slab = w_in.shape[0] // n_steps
    assert n % (n_steps * LANES) == 0 and w_in.shape[0] % (n_steps * 2 * SUBLANES) == 0
    return pl.pallas_call(
        _ada_kernel,
        grid=(n_steps,),
        in_specs=[pl.BlockSpec((d, bsz), lambda j: (0, 0)),
                  pl.BlockSpec((d, tn), lambda j: (0, j)),
                  pl.BlockSpec((1, tn), lambda j: (0, j)),
                  pl.BlockSpec((slab, w_in.shape[1]), lambda j: (j, 0))],
        out_specs=(pl.BlockSpec((bsz, 1, tn), lambda j: (0, 0, j)),
                   pl.BlockSpec((slab, w_in.shape[1]), lambda j: (j, 0))),
        out_shape=(jax.ShapeDtypeStruct((bsz, 1, n), F32),
                   jax.ShapeDtypeStruct(w_in.shape, BF16)),
        compiler_params=_params("parallel"),
        name="ada",
    )(c.T, w, b.reshape(1, n), w_in)


def _inproj_kernel(x_ref, sc_ref, sh_ref, nw_ref, w_ref, b_ref, qkw_ref, *refs,
                   na_w, f_w, d, chunk, n_cast):
    cast_in = refs[:n_cast]
    q_ref, k_ref, v_ref, u_ref, ga_ref, gb_ref = refs[n_cast:n_cast + 6]
    cast_out = refs[n_cast + 6:]
    for src, dst in zip(cast_in, cast_out):
        dst[...] = src[...].astype(BF16)

    gain = nw_ref[...] * (1.0 + sc_ref[...])

    def normed(c0):
        x = x_ref[c0:c0 + chunk, :]
        ms = jnp.mean(x * x, axis=-1, keepdims=True)
        return (x * lax.rsqrt(ms + EPS) * gain + sh_ref[...]).astype(BF16)

    def proj(h, lo, width):
        return _dot(h, w_ref[:, lo:lo + width]) + b_ref[:, lo:lo + width]

    qk = 2 * na_w
    o_v, o_u, o_ga, o_gb = qk, qk + na_w, qk + na_w + f_w, qk + na_w + f_w + d
    n_rows = x_ref.shape[0]
    first_head = lax.broadcasted_iota(jnp.int32, (1, LANES), 1) < HEAD_DIM
    h = normed(0)
    for c0 in range(0, n_rows, chunk):
        rows = slice(c0, c0 + chunk)
        zqk = proj(h, 0, qk)
        h_next = normed(c0 + chunk) if c0 + chunk < n_rows else None
        ga_ref[rows, :] = _sigmoid(proj(h, o_ga, d)).astype(BF16)
        z2 = zqk * zqk
        gb_ref[rows, :] = _sigmoid(proj(h, o_gb, d)).astype(BF16)
        scale = []
        for g in range(qk // LANES):
            blk = z2[:, g * LANES:(g + 1) * LANES]
            ms_a = jnp.sum(jnp.where(first_head, blk, 0.0), axis=-1, keepdims=True) * (1.0 / HEAD_DIM)
            ms_b = jnp.sum(jnp.where(first_head, 0.0, blk), axis=-1, keepdims=True) * (1.0 / HEAD_DIM)
            scale.append(jnp.where(first_head, lax.rsqrt(ms_a + EPS), lax.rsqrt(ms_b + EPS)))
        scale = jnp.concatenate(scale, axis=1)
        v_ref[rows, :] = proj(h, o_v, na_w).astype(BF16)
        zn = zqk * scale * qkw_ref[...]
        q_ref[rows, :] = zn[:, 0:na_w].astype(BF16)
        k_ref[rows, :] = zn[:, na_w:qk].astype(BF16)
        u_ref[rows, :] = proj(h, o_u, f_w)
        h = h_next


def _inproj(x, mod, norm_w, w_in, b_in, qk_w, na_w, f_w, later_weights):
    bsz, s, d = x.shape
    in_w = w_in.shape[1]
    tm = INPROJ_TOKENS
    assert 2 * HEAD_DIM == LANES

    tok = lambda w: pl.BlockSpec((None, tm, w), lambda b, i: (b, i, 0))
    mod_chunk = lambda c: pl.BlockSpec((None, 1, d), lambda b, i: (b, 0, c))
    full = lambda a: pl.BlockSpec(a.shape, lambda b, i: (0,) * a.ndim)
    nw = norm_w.reshape(1, d)
    bi = b_in.reshape(1, in_w)
    out_shapes = (
        jax.ShapeDtypeStruct((bsz, s, na_w), BF16),
        jax.ShapeDtypeStruct((bsz, s, na_w), BF16),
        jax.ShapeDtypeStruct((bsz, s, na_w), BF16),
        jax.ShapeDtypeStruct((bsz, s, f_w), F32),
        jax.ShapeDtypeStruct((bsz, s, d), BF16),
        jax.ShapeDtypeStruct((bsz, s, d), BF16),
    )
    n_steps = bsz * (s // tm)
    slab = lambda a: pl.BlockSpec((a.shape[0] // n_steps, a.shape[1]),
                                  lambda b, i: (b * (s // tm) + i, 0))
    for a in later_weights:
        assert a.shape[0] % (n_steps * 2 * SUBLANES) == 0, a.shape
    outs = pl.pallas_call(
        functools.partial(_inproj_kernel, na_w=na_w, f_w=f_w, d=d, chunk=INPROJ_CHUNK,
                          n_cast=len(later_weights)),
        grid=(bsz, s // tm),
        in_specs=[tok(d), mod_chunk(MOD_SCALE1), mod_chunk(MOD_SHIFT1), full(nw), full(w_in),
                  full(bi), full(qk_w)] + [slab(a) for a in later_weights],
        out_specs=(tok(na_w), tok(na_w), tok(na_w), tok(f_w), tok(d), tok(d))
        + tuple(slab(a) for a in later_weights),
        out_shape=out_shapes + tuple(jax.ShapeDtypeStruct(a.shape, BF16) for a in later_weights),
        compiler_params=_params("parallel", "parallel"),
        name="inproj",
    )(x, mod, mod, nw, w_in, bi, qk_w, *later_weights)
    return outs[:6], outs[6:]


def _fourier_constants(rows, f_w):
    n = rows
    r8 = SUBLANES
    k = np.arange(n)
    ang_a = 2.0 * np.pi * np.outer(k, k) / n
    eye8 = np.eye(r8)
    norm = 1.0 / np.sqrt(n)
    half = n // 2 + 1
    fa_re = np.kron(np.cos(ang_a[:half]), eye8) * norm
    fa_im = np.kron(-np.sin(ang_a[:half]), eye8) * norm
    fa = np.concatenate([fa_re, fa_im], axis=0)

    nblk = n // r8
    s2 = k[None, :]
    s2p = k[:, None]
    g = np.zeros((r8 // 2, 4 * n, 4 * n))
    for pp in range(r8 // 2):
        gr = np.zeros((2, n, 2, n))
        gi = np.zeros((2, n, 2, n))
        for ll in range(2):
            l = 2 * pp + ll
            ang = 2.0 * np.pi * (s2 * l / (n * GRID_W) + s2 * s2p / n)
            gr[ll, :, ll, :] = np.cos(ang) * norm
            gi[ll, :, ll, :] = -np.sin(ang) * norm
        gr = gr.reshape(2 * n, 2 * n)
        gi = gi.reshape(2 * n, 2 * n)
        g[pp] = np.block([[gr, -gi], [gi, gr]])

    perm = np.zeros((n * r8, n * r8))
    s1p = np.repeat(k, r8)
    jj = np.tile(np.arange(r8), n)
    perm[jj * n + s1p, s1p * r8 + jj] = 1.0

    t_idx = np.arange(nblk)[:, None, None]
    blk_of_row = (k // r8)[None, :, None]
    j_idx = np.arange(r8)[None, None, :]
    ang_t = 2.0 * np.pi * (t_idx * r8 + j_idx) * blk_of_row * r8 / (n * GRID_W)
    ang_t = ang_t.reshape(nblk, n * r8, 1)
    tw_cos = np.broadcast_to(np.cos(ang_t), (nblk, n * r8, LANES))
    tw_sin = np.broadcast_to(np.sin(ang_t), (nblk, n * r8, LANES))

    gd = f_w // N_FOURIER_GROUPS
    c = np.arange(gd)
    ang_c = 2.0 * np.pi * np.outer(c, c) / gd
    assert MXU_DIM % gd == 0 and f_w % MXU_DIM == 0
    eye_g = np.eye(MXU_DIM // gd)
    cs = np.concatenate([np.kron(eye_g, np.cos(ang_c)), np.kron(eye_g, np.sin(ang_c))],
                        axis=0) / np.sqrt(gd)
    as_f32 = lambda a: jnp.asarray(np.ascontiguousarray(a, dtype=np.float32))
    to_bf16 = lambda a: as_f32(a).astype(BF16)
    return (to_bf16(fa), to_bf16(g), to_bf16(cs), to_bf16(perm), as_f32(tw_cos), as_f32(tw_sin))


def _fourier_kernel(u_ref, fa_ref, g_ref, cs_ref, perm_ref, twc_ref, tws_ref, y_ref,
                    sre_ref, sim_ref, *, steps_per_stage, blocks_per_step):
    t = pl.program_id(1)
    n, _, f_w = u_ref.shape
    r8 = SUBLANES
    m = n * r8

    @pl.when(t < steps_per_stage)
    def _stage_a():
        for j, c0 in [(j, c0) for j in range(blocks_per_step) for c0 in range(0, f_w, MXU_DIM)]:
            cols = slice(c0, c0 + MXU_DIM)
            xin = u_ref[:, j * r8:(j + 1) * r8, cols].reshape(m, MXU_DIM).astype(BF16)
            a = _dot(fa_ref[...], xin)
            reps = MXU_DIM // LANES
            cos_t = jnp.concatenate([twc_ref[j]] * reps, axis=1)
            sin_t = jnp.concatenate([tws_ref[j]] * reps, axis=1)
            mh = (n // 2 + 1) * r8
            mirror = lambda z: [z[(n - s1) * r8:(n - s1 + 1) * r8] for s1 in range(n // 2 + 1, n)]
            a_re = jnp.concatenate([a[0:mh]] + mirror(a[0:mh]), axis=0)
            a_im = jnp.concatenate([a[mh:2 * mh]] + [-z for z in mirror(a[mh:2 * mh])], axis=0)
            b_re = a_re * cos_t + a_im * sin_t
            b_im = a_im * cos_t - a_re * sin_t
            off = pl.multiple_of((t * blocks_per_step + j) * r8, r8)
            sre_ref[:, pl.ds(off, r8), cols] = b_re.reshape(n, r8, MXU_DIM)
            sim_ref[:, pl.ds(off, r8), cols] = b_im.reshape(n, r8, MXU_DIM)

    @pl.when((t >= steps_per_stage) & (t < 2 * steps_per_stage))
    def _stage_b():
        pair_rows = 2 * n
        for j, c0 in [(j, c0) for j in range(blocks_per_step) for c0 in range(0, f_w, MXU_DIM)]:
            cols = slice(c0, c0 + MXU_DIM)
            off = pl.multiple_of(((t - steps_per_stage) * blocks_per_step + j) * r8, r8)
            br = sre_ref[pl.ds(off, r8), :, cols].reshape(m, MXU_DIM)
            bi = sim_ref[pl.ds(off, r8), :, cols].reshape(m, MXU_DIM)
            p_re, p_im = [], []
            for pp in range(r8 // 2):
                rs_ = slice(pp * pair_rows, (pp + 1) * pair_rows)
                bcat = jnp.concatenate([br[rs_], bi[rs_]], axis=0).astype(BF16)
                p = _dot(g_ref[pp], bcat)
                p_re.append(p[0:pair_rows])
                p_im.append(p[pair_rows:2 * pair_rows])
            p_re = jnp.concatenate(p_re, axis=0).astype(BF16)
            p_im = jnp.concatenate(p_im, axis=0).astype(BF16)
            y = _dot(jnp.concatenate([p_re, p_im], axis=1), cs_ref[...])
            sre_ref[pl.ds(off, r8), :, cols] = y.reshape(r8, n, MXU_DIM)

    @pl.when(t >= 2 * steps_per_stage)
    def _stage_c():
        for j in range(blocks_per_step):
            off = pl.multiple_of(((t - 2 * steps_per_stage) * blocks_per_step + j) * r8, r8)
            yin = sre_ref[:, pl.ds(off, r8), :].reshape(m, f_w).astype(BF16)
            y_ref[j * m:(j + 1) * m, :] = _dot(perm_ref[...], yin).astype(y_ref.dtype)


def _fourier(u):
    bsz, s, f_w = u.shape
    rows = s // GRID_W
    assert rows == GRID_W, "the two-stage position DFT assumes a square token grid"
    r8 = SUBLANES
    nblk = rows // r8
    fa, g, cs, perm, tw_cos, tw_sin = _fourier_constants(rows, f_w)
    u4 = u.reshape(bsz, rows, GRID_W, f_w)
    blocks_per_step = FOURIER_BLOCKS_PER_STEP
    steps = nblk // blocks_per_step
    width = blocks_per_step * r8
    stage_a_step = lambda b, t: jnp.minimum(t, steps - 1)
    stage_c_step = lambda b, t: jnp.maximum(t - 2 * steps, 0)
    full = lambda a: pl.BlockSpec(a.shape, lambda b, t: (0,) * a.ndim)
    tw_spec = pl.BlockSpec((blocks_per_step,) + tw_cos.shape[1:],
                           lambda b, t: (stage_a_step(b, t), 0, 0))
    return pl.pallas_call(
        functools.partial(_fourier_kernel, steps_per_stage=steps, blocks_per_step=blocks_per_step),
        grid=(bsz, 3 * steps),
        in_specs=[
            pl.BlockSpec((None, rows, width, f_w), lambda b, t: (b, 0, stage_a_step(b, t), 0)),
            full(fa), full(g), full(cs), full(perm), tw_spec, tw_spec,
        ],
        out_specs=pl.BlockSpec((None, width * GRID_W, f_w), lambda b, t: (b, stage_c_step(b, t), 0)),
        out_shape=jax.ShapeDtypeStruct((bsz, s, f_w), BF16),
        scratch_shapes=[pltpu.VMEM((rows, GRID_W, f_w), F32), pltpu.VMEM((rows, GRID_W, f_w), F32)],
        compiler_params=_params("arbitrary", "arbitrary"),
        name="fourier",
    )(u4, fa, g, cs, perm, tw_cos, tw_sin)


def _bias_tables(rpb, rows):
    kw = min(WIN_W, GRID_W)
    cols = np.arange(GRID_W)
    col_start = np.clip(cols - kw // 2, 0, GRID_W - kw)
    kj = cols[None, :]
    allowed = (kj >= col_start[:, None]) & (kj < col_start[:, None] + kw)
    rel = kj - cols[:, None] + (WIN_W - 1)
    nrel = 2 * WIN_W - 1
    onehot = ((rel[None] == np.arange(nrel)[:, None, None]) & allowed[None]).astype(np.float32)
    sel = np.zeros((2 * nrel + 1, GRID_W, 2 * GRID_W), np.float32)
    sel[0:nrel, :, 0:GRID_W] = onehot
    sel[nrel:2 * nrel, :, GRID_W:] = onehot
    sel[2 * nrel] = np.tile(~allowed, (1, 2))
    rpb = rpb.astype(F32) * LOG2_E
    mask_col = jnp.full(rpb.shape[:1] + (rpb.shape[1] - 1, 1), MASK_VALUE, F32)
    src = jnp.concatenate([rpb[:, :-1], rpb[:, 1:], mask_col], axis=-1)
    return jnp.einsum("hmd,dcx->hmcx", src, jnp.asarray(sel), precision=lax.Precision.HIGHEST)


def _attn_out_kernel(q_ref, k_ref, v_ref, t2_ref, yfn_ref, ga_ref, gb_ref, x_ref,
                     wna_ref, wfn_ref, bfn_ref, wo_ref, g1_ref, sc2_ref, sh2_ref, nw2_ref,
                     o_ref, h2_ref, yna_ref, s_ref, p_ref,
                     *, rows, kh, rows_per_step):
    t = pl.program_id(0)
    last_tile = pl.num_programs(0) - 2

    @pl.when(t == 0)
    def _init():
        yna_ref[...] = jnp.zeros_like(yna_ref)

    yna_prev = yna_ref[...].astype(BF16)
    yfn_prev = yfn_ref[...]

    rb = lax.rem(jnp.minimum(t, last_tile), rows // rows_per_step)
    n_pairs = q_ref.shape[1] // LANES
    lane = lax.broadcasted_iota(jnp.int32, (1, LANES), 1)
    low = lane < HEAD_DIM
    mask_lo = jnp.where(low, 1.0, 0.0).astype(BF16)
    mask_hi = jnp.where(low, 0.0, 1.0).astype(BF16)
    n_keys = kh * GRID_W

    key_start = []
    for j in range(rows_per_step):
        r = rb * rows_per_step + j
        rs = jnp.clip(r - kh // 2, 0, rows - kh)
        key_start.append((pl.multiple_of(rs * GRID_W, GRID_W), rs - r + (WIN_H - 1)))

    def qk_tile(ti):
        j, p = divmod(ti, n_pairs)
        ks, idx0 = key_start[j]
        cs = slice(LANES * p, LANES * (p + 1))
        q2 = q_ref[j * GRID_W:(j + 1) * GRID_W, cs]
        k2 = k_ref[pl.ds(ks, n_keys), cs]
        qs = jnp.concatenate([q2 * mask_lo, q2 * mask_hi], axis=0)
        s = lax.dot_general(qs, k2, (((1,), (1,)), ((), ())), preferred_element_type=F32)
        bias = jnp.concatenate(
            [jnp.concatenate([t2_ref[2 * p + hh, idx0 + 2 * m] for m in range(kh // 2)], axis=1)
             for hh in range(2)], axis=0)
        s_ref[ti] = s + bias

    def softmax_tile(ti):
        s = s_ref[ti]
        p_ref[ti] = jnp.exp2(s - jnp.max(s, axis=-1, keepdims=True)).astype(BF16)

    ones = jnp.ones((n_keys, LANES), BF16)
    row_outs = []

    def pv_tile(ti):
        j, p = divmod(ti, n_pairs)
        ks, _ = key_start[j]
        v2 = v_ref[pl.ds(ks, n_keys), LANES * p:LANES * (p + 1)]
        o = _dot(p_ref[ti], jnp.concatenate([v2, ones], axis=1))
        o = o[:, 0:LANES] * (1.0 / o[:, LANES:2 * LANES])
        row_outs.append(jnp.where(low, o[0:GRID_W], o[GRID_W:2 * GRID_W]))
        if p == n_pairs - 1:
            yna_ref[j * GRID_W:(j + 1) * GRID_W, :] = jnp.concatenate(row_outs, axis=1)
            row_outs.clear()

    d = o_ref.shape[1]
    merged = []

    def merge_chunk(c0):
        cols = slice(c0, c0 + MXU_DIM)
        ya = _dot(yna_prev, wna_ref[:, cols])
        yf = _dot(yfn_prev, wfn_ref[:, cols]) + bfn_ref[:, cols]
        merged.append((ga_ref[:, cols].astype(F32) * ya
                       + gb_ref[:, cols].astype(F32) * yf).astype(BF16))

    def out_chunk(c0):
        cols = slice(c0, c0 + MXU_DIM)
        lhs = jnp.concatenate(merged, axis=1)
        o_ref[:, cols] = x_ref[:, cols] + g1_ref[:, cols] * _dot(lhs, wo_ref[:, cols])

    norm_rows = o_ref.shape[0] // rows_per_step
    gain2 = nw2_ref[...] * (1.0 + sc2_ref[...])

    def norm_chunk(j):
        nr = slice(j * norm_rows, (j + 1) * norm_rows)
        x1 = o_ref[nr, :]
        ms = jnp.mean(x1 * x1, axis=-1, keepdims=True)
        h2_ref[nr, :] = (x1 * lax.rsqrt(ms + EPS) * gain2 + sh2_ref[...]).astype(BF16)

    matmul_units = ([functools.partial(merge_chunk, c0) for c0 in range(0, d, MXU_DIM)]
                    + [functools.partial(out_chunk, c0) for c0 in range(0, d, MXU_DIM)])
    norm_units = [functools.partial(norm_chunk, j) for j in range(rows_per_step)]

    n_tiles = rows_per_step * n_pairs
    for u in range(n_tiles + PV_SKEW):
        if u < n_tiles:
            qk_tile(u)
        if 0 <= u - SOFTMAX_SKEW < n_tiles:
            softmax_tile(u - SOFTMAX_SKEW)
        if 0 <= u - PV_SKEW < n_tiles:
            pv_tile(u - PV_SKEW)
        if matmul_units:
            if u % OUT_UNIT_EVERY == 1:
                matmul_units.pop(0)()
        elif norm_units and u % NORM_UNIT_EVERY == 0:
            norm_units.pop(0)()
    assert not matmul_units
    while norm_units:
        norm_units.pop(0)()


def _attn_out(q, k, v, t2, yfn, ga, gb, x, w_na, w_fn, b_fn, w_o, mod, norm2_w):
    bsz, s, d = x.shape
    na_w = q.shape[-1]
    f_w = yfn.shape[-1]
    rows = s // GRID_W
    kh = min(WIN_H, rows)
    assert kh % 2 == 0 and (2 * HEAD_DIM) == LANES
    rows_per_step = ATTN_ROWS_PER_STEP
    tq = rows_per_step * GRID_W
    n_tiles = rows_per_step * (na_w // LANES)

    per_seq = s // tq
    n_steps = bsz * per_seq + 1
    attn_tile = lambda t: jnp.minimum(t, n_steps - 2)
    out_tile = lambda t: jnp.maximum(t - 1, 0)
    tok_a = lambda w: pl.BlockSpec(
        (None, tq, w), lambda t: (attn_tile(t) // per_seq, attn_tile(t) % per_seq, 0))
    tok_o = lambda w: pl.BlockSpec(
        (None, tq, w), lambda t: (out_tile(t) // per_seq, out_tile(t) % per_seq, 0))
    whole_seq = pl.BlockSpec((None, s, na_w), lambda t: (attn_tile(t) // per_seq, 0, 0))
    mod_chunk = lambda c: pl.BlockSpec((None, 1, d), lambda t: (out_tile(t) // per_seq, 0, c))
    full = lambda a: pl.BlockSpec(a.shape, lambda t: (0,) * a.ndim)
    bfn = b_fn.reshape(1, d)
    nw2 = norm2_w.reshape(1, d)
    return pl.pallas_call(
        functools.partial(_attn_out_kernel, rows=rows, kh=kh, rows_per_step=rows_per_step),
        grid=(n_steps,),
        in_specs=[tok_a(na_w), whole_seq, whole_seq, full(t2), tok_o(f_w), tok_o(d), tok_o(d),
                  tok_o(d), full(w_na), full(w_fn), full(bfn), full(w_o), mod_chunk(MOD_GATE1),
                  mod_chunk(MOD_SCALE2), mod_chunk(MOD_SHIFT2), full(nw2)],
        out_specs=(tok_o(d), tok_o(d)),
        out_shape=(jax.ShapeDtypeStruct((bsz, s, d), F32), jax.ShapeDtypeStruct((bsz, s, d), BF16)),
        scratch_shapes=[pltpu.VMEM((tq, na_w), F32),
                        pltpu.VMEM((n_tiles, 2 * GRID_W, kh * GRID_W), F32),
                        pltpu.VMEM((n_tiles, 2 * GRID_W, kh * GRID_W), BF16)],
        compiler_params=_params("arbitrary"),
        name="attn_out",
    )(q, k, v, t2, yfn, ga, gb, x, w_na, w_fn, bfn, w_o, mod, mod, mod, nw2)


def _mlp_kernel(x_ref, h_ref, g2_ref, w1_ref, w2_ref, o_ref, *, chunk):
    h = h_ref[...]
    acc = None
    for c0 in range(0, w1_ref.shape[1], chunk):
        a = jnp.maximum(_dot(h, w1_ref[:, c0:c0 + chunk]), 0.0)
        term = _dot((a * a).astype(BF16), w2_ref[c0:c0 + chunk, :])
        acc = term if acc is None else acc + term
    o_ref[...] = x_ref[...] + g2_ref[...] * acc


def _mlp(x, h, mod, w1, w2):
    bsz, s, d = x.shape
    tm = MLP_TOKENS
    tok = pl.BlockSpec((None, tm, d), lambda b, i: (b, i, 0))
    gate2 = pl.BlockSpec((None, 1, d), lambda b, i: (b, 0, MOD_GATE2))
    full = lambda a: pl.BlockSpec(a.shape, lambda b, i: (0,) * a.ndim)
    return pl.pallas_call(
        functools.partial(_mlp_kernel, chunk=MLP_FF_CHUNK),
        grid=(bsz, s // tm),
        in_specs=[tok, tok, gate2, full(w1), full(w2)],
        out_specs=tok,
        out_shape=jax.ShapeDtypeStruct((bsz, s, d), F32),
        compiler_params=_params("parallel", "parallel"),
        name="mlp",
    )(x, h, mod, w1, w2)


def kernel(x, c, norm1_w, norm2_w, w_ada, b_ada, w_in, b_in, q_norm_w, k_norm_w, rpb,
           w_na_out, w_fn_out, b_fn_out, w_o, w_mlp_in, w_mlp_out):
    bsz, s, d = x.shape
    depth = w_ada.shape[0]
    na_w = w_na_out.shape[1]
    f_w = w_fn_out.shape[1]
    n_heads = na_w // HEAD_DIM
    rows = s // GRID_W
    for l in range(depth):
        mod, w_in_b = _ada(c, w_ada[l], b_ada[l], w_in[l])

        qk_w = jnp.concatenate([jnp.tile(q_norm_w[l], n_heads) * (HEAD_DIM ** -0.5 * LOG2_E),
                                jnp.tile(k_norm_w[l], n_heads)]).reshape(1, 2 * na_w)
        (q, k, v, u, ga, gb), (w_na, w_fn, w_o_b, w_m1, w_m2) = _inproj(
            x, mod, norm1_w[l], w_in_b, b_in[l], qk_w, na_w, f_w,
            (w_na_out[l], w_fn_out[l], w_o[l], w_mlp_in[l], w_mlp_out[l]))
        yfn = _fourier(u)
        t2 = _bias_tables(rpb[l], rows)
        x, h2 = _attn_out(q, k, v, t2, yfn, ga, gb, x, w_na, w_fn, b_fn_out[l], w_o_b, mod,
                          norm2_w[l])
        x = _mlp(x, h2, mod, w_m1, w_m2)
    return x
```

```python
import functools

import numpy as np
import jax
import jax.numpy as jnp
from jax import lax
from jax.experimental import pallas as pl
from jax.experimental.pallas import tpu as pltpu

F32 = jnp.float32
BF16 = jnp.bfloat16

GRID_W = 64
HEAD_DIM = 64
WIN_H = 8
WIN_W = 16
N_FOURIER_GROUPS = 4
N_MOD = 6
MOD_SHIFT1, MOD_SCALE1, MOD_GATE1, MOD_SHIFT2, MOD_SCALE2, MOD_GATE2 = range(N_MOD)
EPS = 1e-6
MASK_VALUE = -1e30
LOG2_E = 1.4426950408889634

SUBLANES = 8
LANES = 128
MXU_DIM = 256
SOFTMAX_SKEW = 3
PV_SKEW = 6
OUT_UNIT_EVERY = 4
NORM_UNIT_EVERY = 1
VMEM_LIMIT_BYTES = 56 * 1024 * 1024

ADA_STEPS = 4
INPROJ_TOKENS = 1024
INPROJ_CHUNK = 1024
FOURIER_BLOCKS_PER_STEP = 8
ATTN_ROWS_PER_STEP = 8
MLP_TOKENS = 1024
MLP_FF_CHUNK = 1024


def _dot(a, b):
    return jnp.dot(a, b, preferred_element_type=F32)


def _sigmoid(x):
    return 0.5 * jnp.tanh(0.5 * x) + 0.5


def _params(*semantics):
    return pltpu.CompilerParams(dimension_semantics=semantics, vmem_limit_bytes=VMEM_LIMIT_BYTES)


def _ada_kernel(ct_ref, w_ref, b_ref, win_ref, o_ref, win_bf16_ref):
    ct = ct_ref[...]
    st = ct * _sigmoid(ct)
    w = w_ref[...]
    for b in range(ct.shape[1]):
        o_ref[b] = jnp.sum(w * st[:, b:b + 1], axis=0, keepdims=True) + b_ref[...]
    win_bf16_ref[...] = win_ref[...].astype(BF16)


def _ada(c, w, b, w_in):
    bsz, d = c.shape
    n = w.shape[1]
    n_steps = ADA_STEPS
    tn = n // n_steps
    slab = w_in.shape[0] // n_steps
    assert n % (n_steps * LANES) == 0 and w_in.shape[0] % (n_steps * 2 * SUBLANES) == 0
    return pl.pallas_call(
        _ada_kernel,
        grid=(n_steps,),
        in_specs=[pl.BlockSpec((d, bsz), lambda j: (0, 0)),
                  pl.BlockSpec((d, tn), lambda j: (0, j)),
                  pl.BlockSpec((1, tn), lambda j: (0, j)),
                  pl.BlockSpec((slab, w_in.shape[1]), lambda j: (j, 0))],
        out_specs=(pl.BlockSpec((bsz, 1, tn), lambda j: (0, 0, j)),
                   pl.BlockSpec((slab, w_in.shape[1]), lambda j: (j, 0))),
        out_shape=(jax.ShapeDtypeStruct((bsz, 1, n), F32),
                   jax.ShapeDtypeStruct(w_in.shape, BF16)),
        compiler_params=_params("parallel"),
        name="ada",
    )(c.T, w, b.reshape(1, n), w_in)


def _inproj_kernel(x_ref, sc_ref, sh_ref, nw_ref, w_ref, b_ref, qkw_ref, *refs,
                   na_w, f_w, d, chunk, n_cast):
    cast_in = refs[:n_cast]
    q_ref, k_ref, v_ref, u_ref, ga_ref, gb_ref = refs[n_cast:n_cast + 6]
    cast_out = refs[n_cast + 6:]
    for src, dst in zip(cast_in, cast_out):
        dst[...] = src[...].astype(BF16)

    gain = nw_ref[...] * (1.0 + sc_ref[...])

    def normed(c0):
        x = x_ref[c0:c0 + chunk, :]
        ms = jnp.mean(x * x, axis=-1, keepdims=True)
        return (x * lax.rsqrt(ms + EPS) * gain + sh_ref[...]).astype(BF16)

    def proj(h, lo, width):
        return _dot(h, w_ref[:, lo:lo + width]) + b_ref[:, lo:lo + width]

    qk = 2 * na_w
    o_v, o_u, o_ga, o_gb = qk, qk + na_w, qk + na_w + f_w, qk + na_w + f_w + d
    n_rows = x_ref.shape[0]
    first_head = lax.broadcasted_iota(jnp.int32, (1, LANES), 1) < HEAD_DIM
    h = normed(0)
    for c0 in range(0, n_rows, chunk):
        rows = slice(c0, c0 + chunk)
        zqk = proj(h, 0, qk)
        h_next = normed(c0 + chunk) if c0 + chunk < n_rows else None
        ga_ref[rows, :] = _sigmoid(proj(h, o_ga, d)).astype(BF16)
        z2 = zqk * zqk
        gb_ref[rows, :] = _sigmoid(proj(h, o_gb, d)).astype(BF16)
        scale = []
        for g in range(qk // LANES):
            blk = z2[:, g * LANES:(g + 1) * LANES]
            ms_a = jnp.sum(jnp.where(first_head, blk, 0.0), axis=-1, keepdims=True) * (1.0 / HEAD_DIM)
            ms_b = jnp.sum(jnp.where(first_head, 0.0, blk), axis=-1, keepdims=True) * (1.0 / HEAD_DIM)
            scale.append(jnp.where(first_head, lax.rsqrt(ms_a + EPS), lax.rsqrt(ms_b + EPS)))
        scale = jnp.concatenate(scale, axis=1)
        v_ref[rows, :] = proj(h, o_v, na_w).astype(BF16)
        zn = zqk * scale * qkw_ref[...]
        q_ref[rows, :] = zn[:, 0:na_w].astype(BF16)
        k_ref[rows, :] = zn[:, na_w:qk].astype(BF16)
        u_ref[rows, :] = proj(h, o_u, f_w)
        h = h_next


def _inproj(x, mod, norm_w, w_in, b_in, qk_w, na_w, f_w, later_weights):
    bsz, s, d = x.shape
    in_w = w_in.shape[1]
    tm = INPROJ_TOKENS
    assert 2 * HEAD_DIM == LANES

    tok = lambda w: pl.BlockSpec((None, tm, w), lambda b, i: (b, i, 0))
    mod_chunk = lambda c: pl.BlockSpec((None, 1, d), lambda b, i: (b, 0, c))
    full = lambda a: pl.BlockSpec(a.shape, lambda b, i: (0,) * a.ndim)
    nw = norm_w.reshape(1, d)
    bi = b_in.reshape(1, in_w)
    out_shapes = (
        jax.ShapeDtypeStruct((bsz, s, na_w), BF16),
        jax.ShapeDtypeStruct((bsz, s, na_w), BF16),
        jax.ShapeDtypeStruct((bsz, s, na_w), BF16),
        jax.ShapeDtypeStruct((bsz, s, f_w), F32),
        jax.ShapeDtypeStruct((bsz, s, d), BF16),
        jax.ShapeDtypeStruct((bsz, s, d), BF16),
    )
    n_steps = bsz * (s // tm)
    slab = lambda a: pl.BlockSpec((a.shape[0] // n_steps, a.shape[1]),
                                  lambda b, i: (b * (s // tm) + i, 0))
    for a in later_weights:
        assert a.shape[0] % (n_steps * 2 * SUBLANES) == 0, a.shape
    outs = pl.pallas_call(
        functools.partial(_inproj_kernel, na_w=na_w, f_w=f_w, d=d, chunk=INPROJ_CHUNK,
                          n_cast=len(later_weights)),
        grid=(bsz, s // tm),
        in_specs=[tok(d), mod_chunk(MOD_SCALE1), mod_chunk(MOD_SHIFT1), full(nw), full(w_in),
                  full(bi), full(qk_w)] + [slab(a) for a in later_weights],
        out_specs=(tok(na_w), tok(na_w), tok(na_w), tok(f_w), tok(d), tok(d))
        + tuple(slab(a) for a in later_weights),
        out_shape=out_shapes + tuple(jax.ShapeDtypeStruct(a.shape, BF16) for a in later_weights),
        compiler_params=_params("parallel", "parallel"),
        name="inproj",
    )(x, mod, mod, nw, w_in, bi, qk_w, *later_weights)
    return outs[:6], outs[6:]


def _fourier_constants(rows, f_w):
    n = rows
    r8 = SUBLANES
    k = np.arange(n)
    ang_a = 2.0 * np.pi * np.outer(k, k) / n
    eye8 = np.eye(r8)
    norm = 1.0 / np.sqrt(n)
    half = n // 2 + 1
    fa_re = np.kron(np.cos(ang_a[:half]), eye8) * norm
    fa_im = np.kron(-np.sin(ang_a[:half]), eye8) * norm
    fa = np.concatenate([fa_re, fa_im], axis=0)

    nblk = n // r8
    s2 = k[None, :]
    s2p = k[:, None]
    g = np.zeros((r8 // 2, 4 * n, 4 * n))
    for pp in range(r8 // 2):
        gr = np.zeros((2, n, 2, n))
        gi = np.zeros((2, n, 2, n))
        for ll in range(2):
            l = 2 * pp + ll
            ang = 2.0 * np.pi * (s2 * l / (n * GRID_W) + s2 * s2p / n)
            gr[ll, :, ll, :] = np.cos(ang) * norm
            gi[ll, :, ll, :] = -np.sin(ang) * norm
        gr = gr.reshape(2 * n, 2 * n)
        gi = gi.reshape(2 * n, 2 * n)
        g[pp] = np.block([[gr, -gi], [gi, gr]])

    perm = np.zeros((n * r8, n * r8))
    s1p = np.repeat(k, r8)
    jj = np.tile(np.arange(r8), n)
    perm[jj * n + s1p, s1p * r8 + jj] = 1.0

    t_idx = np.arange(nblk)[:, None, None]
    blk_of_row = (k // r8)[None, :, None]
    j_idx = np.arange(r8)[None, None, :]
    ang_t = 2.0 * np.pi * (t_idx * r8 + j_idx) * blk_of_row * r8 / (n * GRID_W)
    ang_t = ang_t.reshape(nblk, n * r8, 1)
    tw_cos = np.broadcast_to(np.cos(ang_t), (nblk, n * r8, LANES))
    tw_sin = np.broadcast_to(np.sin(ang_t), (nblk, n * r8, LANES))

    gd = f_w // N_FOURIER_GROUPS
    c = np.arange(gd)
    ang_c = 2.0 * np.pi * np.outer(c, c) / gd
    assert MXU_DIM % gd == 0 and f_w % MXU_DIM == 0
    eye_g = np.eye(MXU_DIM // gd)
    cs = np.concatenate([np.kron(eye_g, np.cos(ang_c)), np.kron(eye_g, np.sin(ang_c))],
                        axis=0) / np.sqrt(gd)
    as_f32 = lambda a: jnp.asarray(np.ascontiguousarray(a, dtype=np.float32))
    to_bf16 = lambda a: as_f32(a).astype(BF16)
    return (to_bf16(fa), to_bf16(g), to_bf16(cs), to_bf16(perm), as_f32(tw_cos), as_f32(tw_sin))


def _fourier_kernel(u_ref, fa_ref, g_ref, cs_ref, perm_ref, twc_ref, tws_ref, y_ref,
                    sre_ref, sim_ref, *, steps_per_stage, blocks_per_step):
    t = pl.program_id(1)
    n, _, f_w = u_ref.shape
    r8 = SUBLANES
    m = n * r8

    @pl.when(t < steps_per_stage)
    def _stage_a():
        for j in range(blocks_per_step):
            xin = u_ref[:, j * r8:(j + 1) * r8, :].reshape(m, f_w).astype(BF16)
            a = _dot(fa_ref[...], xin)
            reps = f_w // LANES
            cos_t = jnp.concatenate([twc_ref[j]] * reps, axis=1)
            sin_t = jnp.concatenate([tws_ref[j]] * reps, axis=1)
            mh = (n // 2 + 1) * r8
            mirror = lambda z: [z[(n - s1) * r8:(n - s1 + 1) * r8] for s1 in range(n // 2 + 1, n)]
            a_re = jnp.concatenate([a[0:mh]] + mirror(a[0:mh]), axis=0)
            a_im = jnp.concatenate([a[mh:2 * mh]] + [-z for z in mirror(a[mh:2 * mh])], axis=0)
            b_re = a_re * cos_t + a_im * sin_t
            b_im = a_im * cos_t - a_re * sin_t
            off = pl.multiple_of((t * blocks_per_step + j) * r8, r8)
            sre_ref[:, pl.ds(off, r8), :] = b_re.reshape(n, r8, f_w)
            sim_ref[:, pl.ds(off, r8), :] = b_im.reshape(n, r8, f_w)

    @pl.when((t >= steps_per_stage) & (t < 2 * steps_per_stage))
    def _stage_b():
        pair_rows = 2 * n
        for j in range(blocks_per_step):
            off = pl.multiple_of(((t - steps_per_stage) * blocks_per_step + j) * r8, r8)
            br = sre_ref[pl.ds(off, r8)].reshape(m, f_w)
            bi = sim_ref[pl.ds(off, r8)].reshape(m, f_w)
            p_re, p_im = [], []
            for pp in range(r8 // 2):
                rs_ = slice(pp * pair_rows, (pp + 1) * pair_rows)
                bcat = jnp.concatenate([br[rs_], bi[rs_]], axis=0).astype(BF16)
                p = _dot(g_ref[pp], bcat)
                p_re.append(p[0:pair_rows])
                p_im.append(p[pair_rows:2 * pair_rows])
            p_re = jnp.concatenate(p_re, axis=0).astype(BF16)
            p_im = jnp.concatenate(p_im, axis=0).astype(BF16)
            y = jnp.concatenate(
                [_dot(jnp.concatenate([p_re[:, c0:c0 + MXU_DIM], p_im[:, c0:c0 + MXU_DIM]], axis=1),
                      cs_ref[...]) for c0 in range(0, f_w, MXU_DIM)], axis=1)
            sre_ref[pl.ds(off, r8)] = y.reshape(r8, n, f_w)

    @pl.when(t >= 2 * steps_per_stage)
    def _stage_c():
        for j in range(blocks_per_step):
            off = pl.multiple_of(((t - 2 * steps_per_stage) * blocks_per_step + j) * r8, r8)
            yin = sre_ref[:, pl.ds(off, r8), :].reshape(m, f_w).astype(BF16)
            y_ref[j * m:(j + 1) * m, :] = _dot(perm_ref[...], yin).astype(y_ref.dtype)


def _fourier(u):
    bsz, s, f_w = u.shape
    rows = s // GRID_W
    assert rows == GRID_W, "the two-stage position DFT assumes a square token grid"
    r8 = SUBLANES
    nblk = rows // r8
    fa, g, cs, perm, tw_cos, tw_sin = _fourier_constants(rows, f_w)
    u4 = u.reshape(bsz, rows, GRID_W, f_w)
    blocks_per_step = FOURIER_BLOCKS_PER_STEP
    steps = nblk // blocks_per_step
    width = blocks_per_step * r8
    stage_a_step = lambda b, t: jnp.minimum(t, steps - 1)
    stage_c_step = lambda b, t: jnp.maximum(t - 2 * steps, 0)
    full = lambda a: pl.BlockSpec(a.shape, lambda b, t: (0,) * a.ndim)
    tw_spec = pl.BlockSpec((blocks_per_step,) + tw_cos.shape[1:],
                           lambda b, t: (stage_a_step(b, t), 0, 0))
    return pl.pallas_call(
        functools.partial(_fourier_kernel, steps_per_stage=steps, blocks_per_step=blocks_per_step),
        grid=(bsz, 3 * steps),
        in_specs=[
            pl.BlockSpec((None, rows, width, f_w), lambda b, t: (b, 0, stage_a_step(b, t), 0)),
            full(fa), full(g), full(cs), full(perm), tw_spec, tw_spec,
        ],
        out_specs=pl.BlockSpec((None, width * GRID_W, f_w), lambda b, t: (b, stage_c_step(b, t), 0)),
        out_shape=jax.ShapeDtypeStruct((bsz, s, f_w), BF16),
        scratch_shapes=[pltpu.VMEM((rows, GRID_W, f_w), F32), pltpu.VMEM((rows, GRID_W, f_w), F32)],
        compiler_params=_params("arbitrary", "arbitrary"),
        name="fourier",
    )(u4, fa, g, cs, perm, tw_cos, tw_sin)


def _bias_tables(rpb, rows):
    kw = min(WIN_W, GRID_W)
    cols = np.arange(GRID_W)
    col_start = np.clip(cols - kw // 2, 0, GRID_W - kw)
    kj = cols[None, :]
    allowed = (kj >= col_start[:, None]) & (kj < col_start[:, None] + kw)
    rel = kj - cols[:, None] + (WIN_W - 1)
    nrel = 2 * WIN_W - 1
    onehot = ((rel[None] == np.arange(nrel)[:, None, None]) & allowed[None]).astype(np.float32)
    sel = np.zeros((2 * nrel + 1, GRID_W, 2 * GRID_W), np.float32)
    sel[0:nrel, :, 0:GRID_W] = onehot
    sel[nrel:2 * nrel, :, GRID_W:] = onehot
    sel[2 * nrel] = np.tile(~allowed, (1, 2))
    rpb = rpb.astype(F32) * LOG2_E
    mask_col = jnp.full(rpb.shape[:1] + (rpb.shape[1] - 1, 1), MASK_VALUE, F32)
    src = jnp.concatenate([rpb[:, :-1], rpb[:, 1:], mask_col], axis=-1)
    return jnp.einsum("hmd,dcx->hmcx", src, jnp.asarray(sel), precision=lax.Precision.HIGHEST)


def _attn_out_kernel(q_ref, k_ref, v_ref, t2_ref, yfn_ref, ga_ref, gb_ref, x_ref,
                     wna_ref, wfn_ref, bfn_ref, wo_ref, g1_ref, sc2_ref, sh2_ref, nw2_ref,
                     o_ref, h2_ref, yna_ref, s_ref, p_ref,
                     *, rows, kh, rows_per_step):
    t = pl.program_id(0)
    last_tile = pl.num_programs(0) - 2

    @pl.when(t == 0)
    def _init():
        yna_ref[...] = jnp.zeros_like(yna_ref)

    yna_prev = yna_ref[...].astype(BF16)
    yfn_prev = yfn_ref[...]

    rb = lax.rem(jnp.minimum(t, last_tile), rows // rows_per_step)
    n_pairs = q_ref.shape[1] // LANES
    lane = lax.broadcasted_iota(jnp.int32, (1, LANES), 1)
    low = lane < HEAD_DIM
    mask_lo = jnp.where(low, 1.0, 0.0).astype(BF16)
    mask_hi = jnp.where(low, 0.0, 1.0).astype(BF16)
    n_keys = kh * GRID_W

    key_start = []
    for j in range(rows_per_step):
        r = rb * rows_per_step + j
        rs = jnp.clip(r - kh // 2, 0, rows - kh)
        key_start.append((pl.multiple_of(rs * GRID_W, GRID_W), rs - r + (WIN_H - 1)))

    def qk_tile(ti):
        j, p = divmod(ti, n_pairs)
        ks, idx0 = key_start[j]
        cs = slice(LANES * p, LANES * (p + 1))
        q2 = q_ref[j * GRID_W:(j + 1) * GRID_W, cs]
        k2 = k_ref[pl.ds(ks, n_keys), cs]
        qs = jnp.concatenate([q2 * mask_lo, q2 * mask_hi], axis=0)
        s = lax.dot_general(qs, k2, (((1,), (1,)), ((), ())), preferred_element_type=F32)
        bias = jnp.concatenate(
            [jnp.concatenate([t2_ref[2 * p + hh, idx0 + 2 * m] for m in range(kh // 2)], axis=1)
             for hh in range(2)], axis=0)
        s_ref[ti] = s + bias

    def softmax_tile(ti):
        s = s_ref[ti]
        p_ref[ti] = jnp.exp2(s - jnp.max(s, axis=-1, keepdims=True)).astype(BF16)

    ones = jnp.ones((n_keys, LANES), BF16)
    row_outs = []

    def pv_tile(ti):
        j, p = divmod(ti, n_pairs)
        ks, _ = key_start[j]
        v2 = v_ref[pl.ds(ks, n_keys), LANES * p:LANES * (p + 1)]
        o = _dot(p_ref[ti], jnp.concatenate([v2, ones], axis=1))
        o = o[:, 0:LANES] * (1.0 / o[:, LANES:2 * LANES])
        row_outs.append(jnp.where(low, o[0:GRID_W], o[GRID_W:2 * GRID_W]))
        if p == n_pairs - 1:
            yna_ref[j * GRID_W:(j + 1) * GRID_W, :] = jnp.concatenate(row_outs, axis=1)
            row_outs.clear()

    d = o_ref.shape[1]
    merged = []

    def merge_chunk(c0):
        cols = slice(c0, c0 + MXU_DIM)
        ya = _dot(yna_prev, wna_ref[:, cols])
        yf = _dot(yfn_prev, wfn_ref[:, cols]) + bfn_ref[:, cols]
        merged.append((ga_ref[:, cols].astype(F32) * ya
                       + gb_ref[:, cols].astype(F32) * yf).astype(BF16))

    def out_chunk(c0):
        cols = slice(c0, c0 + MXU_DIM)
        lhs = jnp.concatenate(merged, axis=1)
        o_ref[:, cols] = x_ref[:, cols] + g1_ref[:, cols] * _dot(lhs, wo_ref[:, cols])

    norm_rows = o_ref.shape[0] // rows_per_step
    gain2 = nw2_ref[...] * (1.0 + sc2_ref[...])

    def norm_chunk(j):
        nr = slice(j * norm_rows, (j + 1) * norm_rows)
        x1 = o_ref[nr, :]
        ms = jnp.mean(x1 * x1, axis=-1, keepdims=True)
        h2_ref[nr, :] = (x1 * lax.rsqrt(ms + EPS) * gain2 + sh2_ref[...]).astype(BF16)

    matmul_units = ([functools.partial(merge_chunk, c0) for c0 in range(0, d, MXU_DIM)]
                    + [functools.partial(out_chunk, c0) for c0 in range(0, d, MXU_DIM)])
    norm_units = [functools.partial(norm_chunk, j) for j in range(rows_per_step)]

    n_tiles = rows_per_step * n_pairs
    for u in range(n_tiles + PV_SKEW):
        if u < n_tiles:
            qk_tile(u)
        if 0 <= u - SOFTMAX_SKEW < n_tiles:
            softmax_tile(u - SOFTMAX_SKEW)
        if 0 <= u - PV_SKEW < n_tiles:
            pv_tile(u - PV_SKEW)
        if matmul_units:
            if u % OUT_UNIT_EVERY == 1:
                matmul_units.pop(0)()
        elif norm_units and u % NORM_UNIT_EVERY == 0:
            norm_units.pop(0)()
    assert not matmul_units
    while norm_units:
        norm_units.pop(0)()


def _attn_out(q, k, v, t2, yfn, ga, gb, x, w_na, w_fn, b_fn, w_o, mod, norm2_w):
    bsz, s, d = x.shape
    na_w = q.shape[-1]
    f_w = yfn.shape[-1]
    rows = s // GRID_W
    kh = min(WIN_H, rows)
    assert kh % 2 == 0 and (2 * HEAD_DIM) == LANES
    rows_per_step = ATTN_ROWS_PER_STEP
    tq = rows_per_step * GRID_W
    n_tiles = rows_per_step * (na_w // LANES)

    per_seq = s // tq
    n_steps = bsz * per_seq + 1
    attn_tile = lambda t: jnp.minimum(t, n_steps - 2)
    out_tile = lambda t: jnp.maximum(t - 1, 0)
    tok_a = lambda w: pl.BlockSpec(
        (None, tq, w), lambda t: (attn_tile(t) // per_seq, attn_tile(t) % per_seq, 0))
    tok_o = lambda w: pl.BlockSpec(
        (None, tq, w), lambda t: (out_tile(t) // per_seq, out_tile(t) % per_seq, 0))
    whole_seq = pl.BlockSpec((None, s, na_w), lambda t: (attn_tile(t) // per_seq, 0, 0))
    mod_chunk = lambda c: pl.BlockSpec((None, 1, d), lambda t: (out_tile(t) // per_seq, 0, c))
    full = lambda a: pl.BlockSpec(a.shape, lambda t: (0,) * a.ndim)
    bfn = b_fn.reshape(1, d)
    nw2 = norm2_w.reshape(1, d)
    return pl.pallas_call(
        functools.partial(_attn_out_kernel, rows=rows, kh=kh, rows_per_step=rows_per_step),
        grid=(n_steps,),
        in_specs=[tok_a(na_w), whole_seq, whole_seq, full(t2), tok_o(f_w), tok_o(d), tok_o(d),
                  tok_o(d), full(w_na), full(w_fn), full(bfn), full(w_o), mod_chunk(MOD_GATE1),
                  mod_chunk(MOD_SCALE2), mod_chunk(MOD_SHIFT2), full(nw2)],
        out_specs=(tok_o(d), tok_o(d)),
        out_shape=(jax.ShapeDtypeStruct((bsz, s, d), F32), jax.ShapeDtypeStruct((bsz, s, d), BF16)),
        scratch_shapes=[pltpu.VMEM((tq, na_w), F32),
                        pltpu.VMEM((n_tiles, 2 * GRID_W, kh * GRID_W), F32),
                        pltpu.VMEM((n_tiles, 2 * GRID_W, kh * GRID_W), BF16)],
        compiler_params=_params("arbitrary"),
        name="attn_out",
    )(q, k, v, t2, yfn, ga, gb, x, w_na, w_fn, bfn, w_o, mod, mod, mod, nw2)


def _mlp_kernel(x_ref, h_ref, g2_ref, w1_ref, w2_ref, o_ref, *, chunk):
    h = h_ref[...]
    acc = None
    for c0 in range(0, w1_ref.shape[1], chunk):
        a = jnp.maximum(_dot(h, w1_ref[:, c0:c0 + chunk]), 0.0)
        term = _dot((a * a).astype(BF16), w2_ref[c0:c0 + chunk, :])
        acc = term if acc is None else acc + term
    o_ref[...] = x_ref[...] + g2_ref[...] * acc


def _mlp(x, h, mod, w1, w2):
    bsz, s, d = x.shape
    tm = MLP_TOKENS
    tok = pl.BlockSpec((None, tm, d), lambda b, i: (b, i, 0))
    gate2 = pl.BlockSpec((None, 1, d), lambda b, i: (b, 0, MOD_GATE2))
    full = lambda a: pl.BlockSpec(a.shape, lambda b, i: (0,) * a.ndim)
    return pl.pallas_call(
        functools.partial(_mlp_kernel, chunk=MLP_FF_CHUNK),
        grid=(bsz, s // tm),
        in_specs=[tok, tok, gate2, full(w1), full(w2)],
        out_specs=tok,
        out_shape=jax.ShapeDtypeStruct((bsz, s, d), F32),
        compiler_params=_params("parallel", "parallel"),
        name="mlp",
    )(x, h, mod, w1, w2)


def kernel(x, c, norm1_w, norm2_w, w_ada, b_ada, w_in, b_in, q_norm_w, k_norm_w, rpb,
           w_na_out, w_fn_out, b_fn_out, w_o, w_mlp_in, w_mlp_out):
    bsz, s, d = x.shape
    depth = w_ada.shape[0]
    na_w = w_na_out.shape[1]
    f_w = w_fn_out.shape[1]
    n_heads = na_w // HEAD_DIM
    rows = s // GRID_W
    for l in range(depth):
        mod, w_in_b = _ada(c, w_ada[l], b_ada[l], w_in[l])

        qk_w = jnp.concatenate([jnp.tile(q_norm_w[l], n_heads) * (HEAD_DIM ** -0.5 * LOG2_E),
                                jnp.tile(k_norm_w[l], n_heads)]).reshape(1, 2 * na_w)
        (q, k, v, u, ga, gb), (w_na, w_fn, w_o_b, w_m1, w_m2) = _inproj(
            x, mod, norm1_w[l], w_in_b, b_in[l], qk_w, na_w, f_w,
            (w_na_out[l], w_fn_out[l], w_o[l], w_mlp_in[l], w_mlp_out[l]))
        yfn = _fourier(u)
        t2 = _bias_tables(rpb[l], rows)
        x, h2 = _attn_out(q, k, v, t2, yfn, ga, gb, x, w_na, w_fn, b_fn_out[l], w_o_b, mod,
                          norm2_w[l])
        x = _mlp(x, h2, mod, w_m1, w_m2)
    return x
```

```python
import functools

import numpy as np
import jax
import jax.numpy as jnp
from jax import lax
from jax.experimental import pallas as pl
from jax.experimental.pallas import tpu as pltpu

F32 = jnp.float32
BF16 = jnp.bfloat16

GRID_W = 64
HEAD_DIM = 64
WIN_H = 8
WIN_W = 16
N_FOURIER_GROUPS = 4
N_MOD = 6
MOD_SHIFT1, MOD_SCALE1, MOD_GATE1, MOD_SHIFT2, MOD_SCALE2, MOD_GATE2 = range(N_MOD)
EPS = 1e-6
MASK_VALUE = -1e30
LOG2_E = 1.4426950408889634

SUBLANES = 8
LANES = 128
MXU_DIM = 256
SOFTMAX_SKEW = 3
PV_SKEW = 6
OUT_UNIT_EVERY = 4
NORM_UNIT_EVERY = 1
VMEM_LIMIT_BYTES = 56 * 1024 * 1024

ADA_STEPS = 16
INPROJ_TOKENS = 1024
INPROJ_CHUNK = 1024
FOURIER_BLOCKS_PER_STEP = 8
ATTN_ROWS_PER_STEP = 8
MLP_TOKENS = 1024
MLP_FF_CHUNK = 1024


def _dot(a, b):
    return jnp.dot(a, b, preferred_element_type=F32)


def _sigmoid(x):
    return 0.5 * jnp.tanh(0.5 * x) + 0.5


def _params(*semantics):
    return pltpu.CompilerParams(dimension_semantics=semantics, vmem_limit_bytes=VMEM_LIMIT_BYTES)


def _ada_kernel(ct_ref, w_ref, b_ref, win_ref, o_ref, win_bf16_ref):
    ct = ct_ref[...]
    st = ct * _sigmoid(ct)
    w = w_ref[...]
    for b in range(ct.shape[1]):
        o_ref[b] = jnp.sum(w * st[:, b:b + 1], axis=0, keepdims=True) + b_ref[...]
    win_bf16_ref[...] = win_ref[...].astype(BF16)


def _ada(c, w, b, w_in):
    bsz, d = c.shape
    n = w.shape[1]
    n_steps = ADA_STEPS
    tn = n // n_steps
    slab = w_in.shape[0] // n_steps
    assert n % (n_steps * LANES) == 0 and w_in.shape[0] % (n_steps * 2 * SUBLANES) == 0
    return pl.pallas_call(
        _ada_kernel,
        grid=(n_steps,),
        in_specs=[pl.BlockSpec((d, bsz), lambda j: (0, 0)),
                  pl.BlockSpec((d, tn), lambda j: (0, j)),
                  pl.BlockSpec((1, tn), lambda j: (0, j)),
                  pl.BlockSpec((slab, w_in.shape[1]), lambda j: (j, 0))],
        out_specs=(pl.BlockSpec((bsz, 1, tn), lambda j: (0, 0, j)),
                   pl.BlockSpec((slab, w_in.shape[1]), lambda j: (j, 0))),
        out_shape=(jax.ShapeDtypeStruct((bsz, 1, n), F32),
                   jax.ShapeDtypeStruct(w_in.shape, BF16)),
        compiler_params=_params("parallel"),
        name="ada",
    )(c.T, w, b.reshape(1, n), w_in)


def _inproj_kernel(x_ref, sc_ref, sh_ref, nw_ref, w_ref, b_ref, qkw_ref, *refs,
                   na_w, f_w, d, chunk, n_cast):
    cast_in = refs[:n_cast]
    q_ref, k_ref, v_ref, u_ref, ga_ref, gb_ref = refs[n_cast:n_cast + 6]
    cast_out = refs[n_cast + 6:]
    for src, dst in zip(cast_in, cast_out):
        dst[...] = src[...].astype(BF16)

    gain = nw_ref[...] * (1.0 + sc_ref[...])

    def normed(c0):
        x = x_ref[c0:c0 + chunk, :]
        ms = jnp.mean(x * x, axis=-1, keepdims=True)
        return (x * lax.rsqrt(ms + EPS) * gain + sh_ref[...]).astype(BF16)

    def proj(h, lo, width):
        return _dot(h, w_ref[:, lo:lo + width]) + b_ref[:, lo:lo + width]

    qk = 2 * na_w
    o_v, o_u, o_ga, o_gb = qk, qk + na_w, qk + na_w + f_w, qk + na_w + f_w + d
    n_rows = x_ref.shape[0]
    first_head = lax.broadcasted_iota(jnp.int32, (1, LANES), 1) < HEAD_DIM
    h = normed(0)
    for c0 in range(0, n_rows, chunk):
        rows = slice(c0, c0 + chunk)
        zqk = proj(h, 0, qk)
        h_next = normed(c0 + chunk) if c0 + chunk < n_rows else None
        ga_ref[rows, :] = _sigmoid(proj(h, o_ga, d)).astype(BF16)
        z2 = zqk * zqk
        gb_ref[rows, :] = _sigmoid(proj(h, o_gb, d)).astype(BF16)
        scale = []
        for g in range(qk // LANES):
            blk = z2[:, g * LANES:(g + 1) * LANES]
            ms_a = jnp.sum(jnp.where(first_head, blk, 0.0), axis=-1, keepdims=True) * (1.0 / HEAD_DIM)
            ms_b = jnp.sum(jnp.where(first_head, 0.0, blk), axis=-1, keepdims=True) * (1.0 / HEAD_DIM)
            scale.append(jnp.where(first_head, lax.rsqrt(ms_a + EPS), lax.rsqrt(ms_b + EPS)))
        scale = jnp.concatenate(scale, axis=1)
        v_ref[rows, :] = proj(h, o_v, na_w).astype(BF16)
        zn = zqk * scale * qkw_ref[...]
        q_ref[rows, :] = zn[:, 0:na_w].astype(BF16)
        k_ref[rows, :] = zn[:, na_w:qk].astype(BF16)
        u_ref[rows, :] = proj(h, o_u, f_w)
        h = h_next


def _inproj(x, mod, norm_w, w_in, b_in, qk_w, na_w, f_w, later_weights):
    bsz, s, d = x.shape
    in_w = w_in.shape[1]
    tm = INPROJ_TOKENS
    assert 2 * HEAD_DIM == LANES

    tok = lambda w: pl.BlockSpec((None, tm, w), lambda b, i: (b, i, 0))
    mod_chunk = lambda c: pl.BlockSpec((None, 1, d), lambda b, i: (b, 0, c))
    full = lambda a: pl.BlockSpec(a.shape, lambda b, i: (0,) * a.ndim)
    nw = norm_w.reshape(1, d)
    bi = b_in.reshape(1, in_w)
    out_shapes = (
        jax.ShapeDtypeStruct((bsz, s, na_w), BF16),
        jax.ShapeDtypeStruct((bsz, s, na_w), BF16),
        jax.ShapeDtypeStruct((bsz, s, na_w), BF16),
        jax.ShapeDtypeStruct((bsz, s, f_w), F32),
        jax.ShapeDtypeStruct((bsz, s, d), BF16),
        jax.ShapeDtypeStruct((bsz, s, d), BF16),
    )
    n_steps = bsz * (s // tm)
    slab = lambda a: pl.BlockSpec((a.shape[0] // n_steps, a.shape[1]),
                                  lambda b, i: (b * (s // tm) + i, 0))
    for a in later_weights:
        assert a.shape[0] % (n_steps * 2 * SUBLANES) == 0, a.shape
    outs = pl.pallas_call(
        functools.partial(_inproj_kernel, na_w=na_w, f_w=f_w, d=d, chunk=INPROJ_CHUNK,
                          n_cast=len(later_weights)),
        grid=(bsz, s // tm),
        in_specs=[tok(d), mod_chunk(MOD_SCALE1), mod_chunk(MOD_SHIFT1), full(nw), full(w_in),
                  full(bi), full(qk_w)] + [slab(a) for a in later_weights],
        out_specs=(tok(na_w), tok(na_w), tok(na_w), tok(f_w), tok(d), tok(d))
        + tuple(slab(a) for a in later_weights),
        out_shape=out_shapes + tuple(jax.ShapeDtypeStruct(a.shape, BF16) for a in later_weights),
        compiler_params=_params("parallel", "parallel"),
        name="inproj",
    )(x, mod, mod, nw, w_in, bi, qk_w, *later_weights)
    return outs[:6], outs[6:]


def _fourier_constants(rows, f_w):
    n = rows
    r8 = SUBLANES
    k = np.arange(n)
    ang_a = 2.0 * np.pi * np.outer(k, k) / n
    eye8 = np.eye(r8)
    norm = 1.0 / np.sqrt(n)
    half = n // 2 + 1
    fa_re = np.kron(np.cos(ang_a[:half]), eye8) * norm
    fa_im = np.kron(-np.sin(ang_a[:half]), eye8) * norm
    fa = np.concatenate([fa_re, fa_im], axis=0)

    nblk = n // r8
    s2 = k[None, :]
    s2p = k[:, None]
    g = np.zeros((r8 // 2, 4 * n, 4 * n))
    for pp in range(r8 // 2):
        gr = np.zeros((2, n, 2, n))
        gi = np.zeros((2, n, 2, n))
        for ll in range(2):
            l = 2 * pp + ll
            ang = 2.0 * np.pi * (s2 * l / (n * GRID_W) + s2 * s2p / n)
            gr[ll, :, ll, :] = np.cos(ang) * norm
            gi[ll, :, ll, :] = -np.sin(ang) * norm
        gr = gr.reshape(2 * n, 2 * n)
        gi = gi.reshape(2 * n, 2 * n)
        g[pp] = np.block([[gr, -gi], [gi, gr]])

    perm = np.zeros((n * r8, n * r8))
    s1p = np.repeat(k, r8)
    jj = np.tile(np.arange(r8), n)
    perm[jj * n + s1p, s1p * r8 + jj] = 1.0

    t_idx = np.arange(nblk)[:, None, None]
    blk_of_row = (k // r8)[None, :, None]
    j_idx = np.arange(r8)[None, None, :]
    ang_t = 2.0 * np.pi * (t_idx * r8 + j_idx) * blk_of_row * r8 / (n * GRID_W)
    ang_t = ang_t.reshape(nblk, n * r8, 1)
    tw_cos = np.broadcast_to(np.cos(ang_t), (nblk, n * r8, LANES))
    tw_sin = np.broadcast_to(np.sin(ang_t), (nblk, n * r8, LANES))

    gd = f_w // N_FOURIER_GROUPS
    c = np.arange(gd)
    ang_c = 2.0 * np.pi * np.outer(c, c) / gd
    assert MXU_DIM % gd == 0 and f_w % MXU_DIM == 0
    eye_g = np.eye(MXU_DIM // gd)
    cs = np.concatenate([np.kron(eye_g, np.cos(ang_c)), np.kron(eye_g, np.sin(ang_c))],
                        axis=0) / np.sqrt(gd)
    as_f32 = lambda a: jnp.asarray(np.ascontiguousarray(a, dtype=np.float32))
    to_bf16 = lambda a: as_f32(a).astype(BF16)
    return (to_bf16(fa), to_bf16(g), to_bf16(cs), to_bf16(perm), as_f32(tw_cos), as_f32(tw_sin))


def _fourier_kernel(u_ref, fa_ref, g_ref, cs_ref, perm_ref, twc_ref, tws_ref, y_ref,
                    sre_ref, sim_ref, *, steps_per_stage, blocks_per_step):
    t = pl.program_id(1)
    n, _, f_w = u_ref.shape
    r8 = SUBLANES
    m = n * r8

    @pl.when(t < steps_per_stage)
    def _stage_a():
        for j in range(blocks_per_step):
            xin = u_ref[:, j * r8:(j + 1) * r8, :].reshape(m, f_w).astype(BF16)
            a = _dot(fa_ref[...], xin)
            reps = f_w // LANES
            cos_t = jnp.concatenate([twc_ref[j]] * reps, axis=1)
            sin_t = jnp.concatenate([tws_ref[j]] * reps, axis=1)
            mh = (n // 2 + 1) * r8
            mirror = lambda z: [z[(n - s1) * r8:(n - s1 + 1) * r8] for s1 in range(n // 2 + 1, n)]
            a_re = jnp.concatenate([a[0:mh]] + mirror(a[0:mh]), axis=0)
            a_im = jnp.concatenate([a[mh:2 * mh]] + [-z for z in mirror(a[mh:2 * mh])], axis=0)
            b_re = a_re * cos_t + a_im * sin_t
            b_im = a_im * cos_t - a_re * sin_t
            off = pl.multiple_of((t * blocks_per_step + j) * r8, r8)
            sre_ref[:, pl.ds(off, r8), :] = b_re.reshape(n, r8, f_w)
            sim_ref[:, pl.ds(off, r8), :] = b_im.reshape(n, r8, f_w)

    @pl.when((t >= steps_per_stage) & (t < 2 * steps_per_stage))
    def _stage_b():
        pair_rows = 2 * n
        for j in range(blocks_per_step):
            off = pl.multiple_of(((t - steps_per_stage) * blocks_per_step + j) * r8, r8)
            br = sre_ref[pl.ds(off, r8)].reshape(m, f_w)
            bi = sim_ref[pl.ds(off, r8)].reshape(m, f_w)
            p_re, p_im = [], []
            for pp in range(r8 // 2):
                rs_ = slice(pp * pair_rows, (pp + 1) * pair_rows)
                bcat = jnp.concatenate([br[rs_], bi[rs_]], axis=0).astype(BF16)
                p = _dot(g_ref[pp], bcat)
                p_re.append(p[0:pair_rows])
                p_im.append(p[pair_rows:2 * pair_rows])
            p_re = jnp.concatenate(p_re, axis=0).astype(BF16)
            p_im = jnp.concatenate(p_im, axis=0).astype(BF16)
            y = jnp.concatenate(
                [_dot(jnp.concatenate([p_re[:, c0:c0 + MXU_DIM], p_im[:, c0:c0 + MXU_DIM]], axis=1),
                      cs_ref[...]) for c0 in range(0, f_w, MXU_DIM)], axis=1)
            sre_ref[pl.ds(off, r8)] = y.reshape(r8, n, f_w)

    @pl.when(t >= 2 * steps_per_stage)
    def _stage_c():
        for j in range(blocks_per_step):
            off = pl.multiple_of(((t - 2 * steps_per_stage) * blocks_per_step + j) * r8, r8)
            yin = sre_ref[:, pl.ds(off, r8), :].reshape(m, f_w).astype(BF16)
            y_ref[j * m:(j + 1) * m, :] = _dot(perm_ref[...], yin).astype(y_ref.dtype)


def _fourier(u):
    bsz, s, f_w = u.shape
    rows = s // GRID_W
    assert rows == GRID_W, "the two-stage position DFT assumes a square token grid"
    r8 = SUBLANES
    nblk = rows // r8
    fa, g, cs, perm, tw_cos, tw_sin = _fourier_constants(rows, f_w)
    u4 = u.reshape(bsz, rows, GRID_W, f_w)
    blocks_per_step = FOURIER_BLOCKS_PER_STEP
    steps = nblk // blocks_per_step
    width = blocks_per_step * r8
    stage_a_step = lambda b, t: jnp.minimum(t, steps - 1)
    stage_c_step = lambda b, t: jnp.maximum(t - 2 * steps, 0)
    full = lambda a: pl.BlockSpec(a.shape, lambda b, t: (0,) * a.ndim)
    tw_spec = pl.BlockSpec((blocks_per_step,) + tw_cos.shape[1:],
                           lambda b, t: (stage_a_step(b, t), 0, 0))
    return pl.pallas_call(
        functools.partial(_fourier_kernel, steps_per_stage=steps, blocks_per_step=blocks_per_step),
        grid=(bsz, 3 * steps),
        in_specs=[
            pl.BlockSpec((None, rows, width, f_w), lambda b, t: (b, 0, stage_a_step(b, t), 0)),
            full(fa), full(g), full(cs), full(perm), tw_spec, tw_spec,
        ],
        out_specs=pl.BlockSpec((None, width * GRID_W, f_w), lambda b, t: (b, stage_c_step(b, t), 0)),
        out_shape=jax.ShapeDtypeStruct((bsz, s, f_w), BF16),
        scratch_shapes=[pltpu.VMEM((rows, GRID_W, f_w), F32), pltpu.VMEM((rows, GRID_W, f_w), F32)],
        compiler_params=_params("arbitrary", "arbitrary"),
        name="fourier",
    )(u4, fa, g, cs, perm, tw_cos, tw_sin)


def _bias_tables(rpb, rows):
    kw = min(WIN_W, GRID_W)
    cols = np.arange(GRID_W)
    col_start = np.clip(cols - kw // 2, 0, GRID_W - kw)
    kj = cols[None, :]
    allowed = (kj >= col_start[:, None]) & (kj < col_start[:, None] + kw)
    rel = kj - cols[:, None] + (WIN_W - 1)
    nrel = 2 * WIN_W - 1
    onehot = ((rel[None] == np.arange(nrel)[:, None, None]) & allowed[None]).astype(np.float32)
    sel = np.zeros((2 * nrel + 1, GRID_W, 2 * GRID_W), np.float32)
    sel[0:nrel, :, 0:GRID_W] = onehot
    sel[nrel:2 * nrel, :, GRID_W:] = onehot
    sel[2 * nrel] = np.tile(~allowed, (1, 2))
    rpb = rpb.astype(F32) * LOG2_E
    mask_col = jnp.full(rpb.shape[:1] + (rpb.shape[1] - 1, 1), MASK_VALUE, F32)
    src = jnp.concatenate([rpb[:, :-1], rpb[:, 1:], mask_col], axis=-1)
    return jnp.einsum("hmd,dcx->hmcx", src, jnp.asarray(sel), precision=lax.Precision.HIGHEST)


def _attn_out_kernel(q_ref, k_ref, v_ref, t2_ref, yfn_ref, ga_ref, gb_ref, x_ref,
                     wna_ref, wfn_ref, bfn_ref, wo_ref, g1_ref, sc2_ref, sh2_ref, nw2_ref,
                     o_ref, h2_ref, yna_ref, s_ref, p_ref,
                     *, rows, kh, rows_per_step):
    t = pl.program_id(0)
    last_tile = pl.num_programs(0) - 2

    @pl.when(t == 0)
    def _init():
        yna_ref[...] = jnp.zeros_like(yna_ref)

    yna_prev = yna_ref[...].astype(BF16)
    yfn_prev = yfn_ref[...]

    rb = lax.rem(jnp.minimum(t, last_tile), rows // rows_per_step)
    n_pairs = q_ref.shape[1] // LANES
    lane = lax.broadcasted_iota(jnp.int32, (1, LANES), 1)
    low = lane < HEAD_DIM
    mask_lo = jnp.where(low, 1.0, 0.0).astype(BF16)
    mask_hi = jnp.where(low, 0.0, 1.0).astype(BF16)
    n_keys = kh * GRID_W

    key_start = []
    for j in range(rows_per_step):
        r = rb * rows_per_step + j
        rs = jnp.clip(r - kh // 2, 0, rows - kh)
        key_start.append((pl.multiple_of(rs * GRID_W, GRID_W), rs - r + (WIN_H - 1)))

    def qk_tile(ti):
        j, p = divmod(ti, n_pairs)
        ks, idx0 = key_start[j]
        cs = slice(LANES * p, LANES * (p + 1))
        q2 = q_ref[j * GRID_W:(j + 1) * GRID_W, cs]
        k2 = k_ref[pl.ds(ks, n_keys), cs]
        qs = jnp.concatenate([q2 * mask_lo, q2 * mask_hi], axis=0)
        s = lax.dot_general(qs, k2, (((1,), (1,)), ((), ())), preferred_element_type=F32)
        bias = jnp.concatenate(
            [jnp.concatenate([t2_ref[2 * p + hh, idx0 + 2 * m] for m in range(kh // 2)], axis=1)
             for hh in range(2)], axis=0)
        s_ref[ti] = s + bias

    def softmax_tile(ti):
        s = s_ref[ti]
        p_ref[ti] = jnp.exp2(s - jnp.max(s, axis=-1, keepdims=True)).astype(BF16)

    ones = jnp.ones((n_keys, LANES), BF16)
    row_outs = []

    def pv_tile(ti):
        j, p = divmod(ti, n_pairs)
        ks, _ = key_start[j]
        v2 = v_ref[pl.ds(ks, n_keys), LANES * p:LANES * (p + 1)]
        o = _dot(p_ref[ti], jnp.concatenate([v2, ones], axis=1))
        o = o[:, 0:LANES] * (1.0 / o[:, LANES:2 * LANES])
        row_outs.append(jnp.where(low, o[0:GRID_W], o[GRID_W:2 * GRID_W]))
        if p == n_pairs - 1:
            yna_ref[j * GRID_W:(j + 1) * GRID_W, :] = jnp.concatenate(row_outs, axis=1)
            row_outs.clear()

    d = o_ref.shape[1]
    merged = []

    def merge_chunk(c0):
        cols = slice(c0, c0 + MXU_DIM)
        ya = _dot(yna_prev, wna_ref[:, cols])
        yf = _dot(yfn_prev, wfn_ref[:, cols]) + bfn_ref[:, cols]
        merged.append((ga_ref[:, cols].astype(F32) * ya
                       + gb_ref[:, cols].astype(F32) * yf).astype(BF16))

    def out_chunk(c0):
        cols = slice(c0, c0 + MXU_DIM)
        lhs = jnp.concatenate(merged, axis=1)
        o_ref[:, cols] = x_ref[:, cols] + g1_ref[:, cols] * _dot(lhs, wo_ref[:, cols])

    norm_rows = o_ref.shape[0] // rows_per_step
    gain2 = nw2_ref[...] * (1.0 + sc2_ref[...])

    def norm_chunk(j):
        nr = slice(j * norm_rows, (j + 1) * norm_rows)
        x1 = o_ref[nr, :]
        ms = jnp.mean(x1 * x1, axis=-1, keepdims=True)
        h2_ref[nr, :] = (x1 * lax.rsqrt(ms + EPS) * gain2 + sh2_ref[...]).astype(BF16)

    matmul_units = ([functools.partial(merge_chunk, c0) for c0 in range(0, d, MXU_DIM)]
                    + [functools.partial(out_chunk, c0) for c0 in range(0, d, MXU_DIM)])
    norm_units = [functools.partial(norm_chunk, j) for j in range(rows_per_step)]

    n_tiles = rows_per_step * n_pairs
    for u in range(n_tiles + PV_SKEW):
        if u < n_tiles:
            qk_tile(u)
        if 0 <= u - SOFTMAX_SKEW < n_tiles:
            softmax_tile(u - SOFTMAX_SKEW)
        if 0 <= u - PV_SKEW < n_tiles:
            pv_tile(u - PV_SKEW)
        if matmul_units:
            if u % OUT_UNIT_EVERY == 1:
                matmul_units.pop(0)()
        elif norm_units and u % NORM_UNIT_EVERY == 0:
            norm_units.pop(0)()
    assert not matmul_units
    while norm_units:
        norm_units.pop(0)()


def _attn_out(q, k, v, t2, yfn, ga, gb, x, w_na, w_fn, b_fn, w_o, mod, norm2_w):
    bsz, s, d = x.shape
    na_w = q.shape[-1]
    f_w = yfn.shape[-1]
    rows = s // GRID_W
    kh = min(WIN_H, rows)
    assert kh % 2 == 0 and (2 * HEAD_DIM) == LANES
    rows_per_step = ATTN_ROWS_PER_STEP
    tq = rows_per_step * GRID_W
    n_tiles = rows_per_step * (na_w // LANES)

    per_seq = s // tq
    n_steps = bsz * per_seq + 1
    attn_tile = lambda t: jnp.minimum(t, n_steps - 2)
    out_tile = lambda t: jnp.maximum(t - 1, 0)
    tok_a = lambda w: pl.BlockSpec(
        (None, tq, w), lambda t: (attn_tile(t) // per_seq, attn_tile(t) % per_seq, 0))
    tok_o = lambda w: pl.BlockSpec(
        (None, tq, w), lambda t: (out_tile(t) // per_seq, out_tile(t) % per_seq, 0))
    whole_seq = pl.BlockSpec((None, s, na_w), lambda t: (attn_tile(t) // per_seq, 0, 0))
    mod_chunk = lambda c: pl.BlockSpec((None, 1, d), lambda t: (out_tile(t) // per_seq, 0, c))
    full = lambda a: pl.BlockSpec(a.shape, lambda t: (0,) * a.ndim)
    bfn = b_fn.reshape(1, d)
    nw2 = norm2_w.reshape(1, d)
    return pl.pallas_call(
        functools.partial(_attn_out_kernel, rows=rows, kh=kh, rows_per_step=rows_per_step),
        grid=(n_steps,),
        in_specs=[tok_a(na_w), whole_seq, whole_seq, full(t2), tok_o(f_w), tok_o(d), tok_o(d),
                  tok_o(d), full(w_na), full(w_fn), full(bfn), full(w_o), mod_chunk(MOD_GATE1),
                  mod_chunk(MOD_SCALE2), mod_chunk(MOD_SHIFT2), full(nw2)],
        out_specs=(tok_o(d), tok_o(d)),
        out_shape=(jax.ShapeDtypeStruct((bsz, s, d), F32), jax.ShapeDtypeStruct((bsz, s, d), BF16)),
        scratch_shapes=[pltpu.VMEM((tq, na_w), F32),
                        pltpu.VMEM((n_tiles, 2 * GRID_W, kh * GRID_W), F32),
                        pltpu.VMEM((n_tiles, 2 * GRID_W, kh * GRID_W), BF16)],
        compiler_params=_params("arbitrary"),
        name="attn_out",
    )(q, k, v, t2, yfn, ga, gb, x, w_na, w_fn, bfn, w_o, mod, mod, mod, nw2)


def _mlp_kernel(x_ref, h_ref, g2_ref, w1_ref, w2_ref, o_ref, *, chunk):
    h = h_ref[...]
    acc = None
    for c0 in range(0, w1_ref.shape[1], chunk):
        a = jnp.maximum(_dot(h, w1_ref[:, c0:c0 + chunk]), 0.0)
        term = _dot((a * a).astype(BF16), w2_ref[c0:c0 + chunk, :])
        acc = term if acc is None else acc + term
    o_ref[...] = x_ref[...] + g2_ref[...] * acc


def _mlp(x, h, mod, w1, w2):
    bsz, s, d = x.shape
    tm = MLP_TOKENS
    tok = pl.BlockSpec((None, tm, d), lambda b, i: (b, i, 0))
    gate2 = pl.BlockSpec((None, 1, d), lambda b, i: (b, 0, MOD_GATE2))
    full = lambda a: pl.BlockSpec(a.shape, lambda b, i: (0,) * a.ndim)
    return pl.pallas_call(
        functools.partial(_mlp_kernel, chunk=MLP_FF_CHUNK),
        grid=(bsz, s // tm),
        in_specs=[tok, tok, gate2, full(w1), full(w2)],
        out_specs=tok,
        out_shape=jax.ShapeDtypeStruct((bsz, s, d), F32),
        compiler_params=_params("parallel", "parallel"),
        name="mlp",
    )(x, h, mod, w1, w2)


def kernel(x, c, norm1_w, norm2_w, w_ada, b_ada, w_in, b_in, q_norm_w, k_norm_w, rpb,
           w_na_out, w_fn_out, b_fn_out, w_o, w_mlp_in, w_mlp_out):
    bsz, s, d = x.shape
    depth = w_ada.shape[0]
    na_w = w_na_out.shape[1]
    f_w = w_fn_out.shape[1]
    n_heads = na_w // HEAD_DIM
    rows = s // GRID_W
    for l in range(depth):
        mod, w_in_b = _ada(c, w_ada[l], b_ada[l], w_in[l])

        qk_w = jnp.concatenate([jnp.tile(q_norm_w[l], n_heads) * (HEAD_DIM ** -0.5 * LOG2_E),
                                jnp.tile(k_norm_w[l], n_heads)]).reshape(1, 2 * na_w)
        (q, k, v, u, ga, gb), (w_na, w_fn, w_o_b, w_m1, w_m2) = _inproj(
            x, mod, norm1_w[l], w_in_b, b_in[l], qk_w, na_w, f_w,
            (w_na_out[l], w_fn_out[l], w_o[l], w_mlp_in[l], w_mlp_out[l]))
        yfn = _fourier(u)
        t2 = _bias_tables(rpb[l], rows)
        x, h2 = _attn_out(q, k, v, t2, yfn, ga, gb, x, w_na, w_fn, b_fn_out[l], w_o_b, mod,
                          norm2_w[l])
        x = _mlp(x, h2, mod, w_m1, w_m2)
    return x
```

```python
import functools

import numpy as np
import jax
import jax.numpy as jnp
from jax import lax
from jax.experimental import pallas as pl
from jax.experimental.pallas import tpu as pltpu

F32 = jnp.float32
BF16 = jnp.bfloat16

GRID_W = 64
HEAD_DIM = 64
WIN_H = 8
WIN_W = 16
N_FOURIER_GROUPS = 4
N_MOD = 6
MOD_SHIFT1, MOD_SCALE1, MOD_GATE1, MOD_SHIFT2, MOD_SCALE2, MOD_GATE2 = range(N_MOD)
EPS = 1e-6
MASK_VALUE = -1e30
LOG2_E = 1.4426950408889634

SUBLANES = 8
LANES = 128
MXU_DIM = 256
SOFTMAX_SKEW = 3
PV_SKEW = 6
OUT_UNIT_EVERY = 4
NORM_UNIT_EVERY = 1
VMEM_LIMIT_BYTES = 56 * 1024 * 1024

ADA_STEPS = 8
INPROJ_TOKENS = 1024
INPROJ_CHUNK = 1024
FOURIER_BLOCKS_PER_STEP = 8
ATTN_ROWS_PER_STEP = 8
MLP_TOKENS = 1024
MLP_FF_CHUNK = 1024


def _dot(a, b):
    return jnp.dot(a, b, preferred_element_type=F32)


def _sigmoid(x):
    return 0.5 * jnp.tanh(0.5 * x) + 0.5


def _params(*semantics):
    return pltpu.CompilerParams(dimension_semantics=semantics, vmem_limit_bytes=VMEM_LIMIT_BYTES)


def _ada_kernel(ct_ref, w_ref, b_ref, win_ref, o_ref, win_bf16_ref, *, gate_start):
    ct = ct_ref[...]
    st = ct * _sigmoid(ct)
    w = w_ref[...]
    for b in range(ct.shape[1]):
        o_ref[b] = jnp.sum(w * st[:, b:b + 1], axis=0, keepdims=True) + b_ref[...]
    col = lax.broadcasted_iota(jnp.int32, (1, win_ref.shape[1]), 1)
    win_bf16_ref[...] = (win_ref[...] * jnp.where(col >= gate_start, 0.5, 1.0)).astype(BF16)


def _ada(c, w, b, w_in, gate_start):
    bsz, d = c.shape
    n = w.shape[1]
    n_steps = ADA_STEPS
    tn = n // n_steps
    slab = w_in.shape[0] // n_steps
    assert n % (n_steps * LANES) == 0 and w_in.shape[0] % (n_steps * 2 * SUBLANES) == 0
    return pl.pallas_call(
        functools.partial(_ada_kernel, gate_start=gate_start),
        grid=(n_steps,),
        in_specs=[pl.BlockSpec((d, bsz), lambda j: (0, 0)),
                  pl.BlockSpec((d, tn), lambda j: (0, j)),
                  pl.BlockSpec((1, tn), lambda j: (0, j)),
                  pl.BlockSpec((slab, w_in.shape[1]), lambda j: (j, 0))],
        out_specs=(pl.BlockSpec((bsz, 1, tn), lambda j: (0, 0, j)),
                   pl.BlockSpec((slab, w_in.shape[1]), lambda j: (j, 0))),
        out_shape=(jax.ShapeDtypeStruct((bsz, 1, n), F32),
                   jax.ShapeDtypeStruct(w_in.shape, BF16)),
        compiler_params=_params("parallel"),
        name="ada",
    )(c.T, w, b.reshape(1, n), w_in)


def _inproj_kernel(x_ref, sc_ref, sh_ref, nw_ref, w_ref, b_ref, qkw_ref, *refs,
                   na_w, f_w, d, chunk, n_cast):
    cast_in = refs[:n_cast]
    q_ref, k_ref, v_ref, u_ref, ga_ref, gb_ref = refs[n_cast:n_cast + 6]
    cast_out = refs[n_cast + 6:]
    for src, dst in zip(cast_in, cast_out):
        dst[...] = src[...].astype(BF16)

    gain = nw_ref[...] * (1.0 + sc_ref[...])

    def normed(c0):
        x = x_ref[c0:c0 + chunk, :]
        ms = jnp.mean(x * x, axis=-1, keepdims=True)
        return (x * lax.rsqrt(ms + EPS) * gain + sh_ref[...]).astype(BF16)

    def proj(h, lo, width):
        return _dot(h, w_ref[:, lo:lo + width]) + b_ref[:, lo:lo + width]

    def half_sigmoid(h, lo):
        z_half = _dot(h, w_ref[:, lo:lo + d]) + 0.5 * b_ref[:, lo:lo + d]
        return 0.5 * jnp.tanh(z_half) + 0.5

    qk = 2 * na_w
    o_v, o_u, o_ga, o_gb = qk, qk + na_w, qk + na_w + f_w, qk + na_w + f_w + d
    n_rows = x_ref.shape[0]
    first_head = lax.broadcasted_iota(jnp.int32, (1, LANES), 1) < HEAD_DIM
    h = normed(0)
    for c0 in range(0, n_rows, chunk):
        rows = slice(c0, c0 + chunk)
        zqk = proj(h, 0, qk)
        h_next = normed(c0 + chunk) if c0 + chunk < n_rows else None
        ga_ref[rows, :] = half_sigmoid(h, o_ga).astype(BF16)
        z2 = zqk * zqk
        gb_ref[rows, :] = half_sigmoid(h, o_gb).astype(BF16)
        scale = []
        for g in range(qk // LANES):
            blk = z2[:, g * LANES:(g + 1) * LANES]
            ms_a = jnp.sum(jnp.where(first_head, blk, 0.0), axis=-1, keepdims=True) * (1.0 / HEAD_DIM)
            ms_b = jnp.sum(jnp.where(first_head, 0.0, blk), axis=-1, keepdims=True) * (1.0 / HEAD_DIM)
            scale.append(jnp.where(first_head, lax.rsqrt(ms_a + EPS), lax.rsqrt(ms_b + EPS)))
        scale = jnp.concatenate(scale, axis=1)
        v_ref[rows, :] = proj(h, o_v, na_w).astype(BF16)
        zn = zqk * scale * qkw_ref[...]
        q_ref[rows, :] = zn[:, 0:na_w].astype(BF16)
        k_ref[rows, :] = zn[:, na_w:qk].astype(BF16)
        u_ref[rows, :] = proj(h, o_u, f_w)
        h = h_next


def _inproj(x, mod, norm_w, w_in, b_in, qk_w, na_w, f_w, later_weights):
    bsz, s, d = x.shape
    in_w = w_in.shape[1]
    tm = INPROJ_TOKENS
    assert 2 * HEAD_DIM == LANES

    tok = lambda w: pl.BlockSpec((None, tm, w), lambda b, i: (b, i, 0))
    mod_chunk = lambda c: pl.BlockSpec((None, 1, d), lambda b, i: (b, 0, c))
    full = lambda a: pl.BlockSpec(a.shape, lambda b, i: (0,) * a.ndim)
    nw = norm_w.reshape(1, d)
    bi = b_in.reshape(1, in_w)
    out_shapes = (
        jax.ShapeDtypeStruct((bsz, s, na_w), BF16),
        jax.ShapeDtypeStruct((bsz, s, na_w), BF16),
        jax.ShapeDtypeStruct((bsz, s, na_w), BF16),
        jax.ShapeDtypeStruct((bsz, s, f_w), F32),
        jax.ShapeDtypeStruct((bsz, s, d), BF16),
        jax.ShapeDtypeStruct((bsz, s, d), BF16),
    )
    n_steps = bsz * (s // tm)
    slab = lambda a: pl.BlockSpec((a.shape[0] // n_steps, a.shape[1]),
                                  lambda b, i: (b * (s // tm) + i, 0))
    for a in later_weights:
        assert a.shape[0] % (n_steps * 2 * SUBLANES) == 0, a.shape
    outs = pl.pallas_call(
        functools.partial(_inproj_kernel, na_w=na_w, f_w=f_w, d=d, chunk=INPROJ_CHUNK,
                          n_cast=len(later_weights)),
        grid=(bsz, s // tm),
        in_specs=[tok(d), mod_chunk(MOD_SCALE1), mod_chunk(MOD_SHIFT1), full(nw), full(w_in),
                  full(bi), full(qk_w)] + [slab(a) for a in later_weights],
        out_specs=(tok(na_w), tok(na_w), tok(na_w), tok(f_w), tok(d), tok(d))
        + tuple(slab(a) for a in later_weights),
        out_shape=out_shapes + tuple(jax.ShapeDtypeStruct(a.shape, BF16) for a in later_weights),
        compiler_params=_params("parallel", "parallel"),
        name="inproj",
    )(x, mod, mod, nw, w_in, bi, qk_w, *later_weights)
    return outs[:6], outs[6:]


def _fourier_constants(rows, f_w):
    n = rows
    r8 = SUBLANES
    k = np.arange(n)
    ang_a = 2.0 * np.pi * np.outer(k, k) / n
    eye8 = np.eye(r8)
    norm = 1.0 / np.sqrt(n)
    half = n // 2 + 1
    fa_re = np.kron(np.cos(ang_a[:half]), eye8) * norm
    fa_im = np.kron(-np.sin(ang_a[:half]), eye8) * norm
    fa = np.concatenate([fa_re, fa_im], axis=0)

    nblk = n // r8
    s2 = k[None, :]
    s2p = k[:, None]
    g = np.zeros((r8 // 2, 4 * n, 4 * n))
    for pp in range(r8 // 2):
        gr = np.zeros((2, n, 2, n))
        gi = np.zeros((2, n, 2, n))
        for ll in range(2):
            l = 2 * pp + ll
            ang = 2.0 * np.pi * (s2 * l / (n * GRID_W) + s2 * s2p / n)
            gr[ll, :, ll, :] = np.cos(ang) * norm
            gi[ll, :, ll, :] = -np.sin(ang) * norm
        gr = gr.reshape(2 * n, 2 * n)
        gi = gi.reshape(2 * n, 2 * n)
        g[pp] = np.block([[gr, -gi], [gi, gr]])

    perm = np.zeros((n * r8, n * r8))
    s1p = np.repeat(k, r8)
    jj = np.tile(np.arange(r8), n)
    perm[jj * n + s1p, s1p * r8 + jj] = 1.0

    t_idx = np.arange(nblk)[:, None, None]
    blk_of_row = (k // r8)[None, :, None]
    j_idx = np.arange(r8)[None, None, :]
    ang_t = 2.0 * np.pi * (t_idx * r8 + j_idx) * blk_of_row * r8 / (n * GRID_W)
    ang_t = ang_t.reshape(nblk, n * r8, 1)
    tw_cos = np.broadcast_to(np.cos(ang_t), (nblk, n * r8, LANES))
    tw_sin = np.broadcast_to(np.sin(ang_t), (nblk, n * r8, LANES))

    gd = f_w // N_FOURIER_GROUPS
    c = np.arange(gd)
    ang_c = 2.0 * np.pi * np.outer(c, c) / gd
    assert MXU_DIM % gd == 0 and f_w % MXU_DIM == 0
    eye_g = np.eye(MXU_DIM // gd)
    cs = np.concatenate([np.kron(eye_g, np.cos(ang_c)), np.kron(eye_g, np.sin(ang_c))],
                        axis=0) / np.sqrt(gd)
    as_f32 = lambda a: jnp.asarray(np.ascontiguousarray(a, dtype=np.float32))
    to_bf16 = lambda a: as_f32(a).astype(BF16)
    return (to_bf16(fa), to_bf16(g), to_bf16(cs), to_bf16(perm), as_f32(tw_cos), as_f32(tw_sin))


def _fourier_kernel(u_ref, fa_ref, g_ref, cs_ref, perm_ref, twc_ref, tws_ref, y_ref,
                    sre_ref, sim_ref, *, steps_per_stage, blocks_per_step):
    t = pl.program_id(1)
    n, _, f_w = u_ref.shape
    r8 = SUBLANES
    m = n * r8

    @pl.when(t < steps_per_stage)
    def _stage_a():
        for j in range(blocks_per_step):
            xin = u_ref[:, j * r8:(j + 1) * r8, :].reshape(m, f_w).astype(BF16)
            a = _dot(fa_ref[...], xin)
            reps = f_w // LANES
            cos_t = jnp.concatenate([twc_ref[j]] * reps, axis=1)
            sin_t = jnp.concatenate([tws_ref[j]] * reps, axis=1)
            mh = (n // 2 + 1) * r8
            mirror = lambda z: [z[(n - s1) * r8:(n - s1 + 1) * r8] for s1 in range(n // 2 + 1, n)]
            a_re = jnp.concatenate([a[0:mh]] + mirror(a[0:mh]), axis=0)
            a_im = jnp.concatenate([a[mh:2 * mh]] + [-z for z in mirror(a[mh:2 * mh])], axis=0)
            b_re = a_re * cos_t + a_im * sin_t
            b_im = a_im * cos_t - a_re * sin_t
            off = pl.multiple_of((t * blocks_per_step + j) * r8, r8)
            sre_ref[:, pl.ds(off, r8), :] = b_re.reshape(n, r8, f_w)
            sim_ref[:, pl.ds(off, r8), :] = b_im.reshape(n, r8, f_w)

    @pl.when((t >= steps_per_stage) & (t < 2 * steps_per_stage))
    def _stage_b():
        pair_rows = 2 * n
        for j in range(blocks_per_step):
            off = pl.multiple_of(((t - steps_per_stage) * blocks_per_step + j) * r8, r8)
            br = sre_ref[pl.ds(off, r8)].reshape(m, f_w)
            bi = sim_ref[pl.ds(off, r8)].reshape(m, f_w)
            p_re, p_im = [], []
            for pp in range(r8 // 2):
                rs_ = slice(pp * pair_rows, (pp + 1) * pair_rows)
                bcat = jnp.concatenate([br[rs_], bi[rs_]], axis=0).astype(BF16)
                p = _dot(g_ref[pp], bcat)
                p_re.append(p[0:pair_rows])
                p_im.append(p[pair_rows:2 * pair_rows])
            p_re = jnp.concatenate(p_re, axis=0).astype(BF16)
            p_im = jnp.concatenate(p_im, axis=0).astype(BF16)
            y = jnp.concatenate(
                [_dot(jnp.concatenate([p_re[:, c0:c0 + MXU_DIM], p_im[:, c0:c0 + MXU_DIM]], axis=1),
                      cs_ref[...]) for c0 in range(0, f_w, MXU_DIM)], axis=1)
            sre_ref[pl.ds(off, r8)] = y.reshape(r8, n, f_w)

    @pl.when(t >= 2 * steps_per_stage)
    def _stage_c():
        for j in range(blocks_per_step):
            off = pl.multiple_of(((t - 2 * steps_per_stage) * blocks_per_step + j) * r8, r8)
            yin = sre_ref[:, pl.ds(off, r8), :].reshape(m, f_w).astype(BF16)
            y_ref[j * m:(j + 1) * m, :] = _dot(perm_ref[...], yin).astype(y_ref.dtype)


def _fourier(u):
    bsz, s, f_w = u.shape
    rows = s // GRID_W
    assert rows == GRID_W, "the two-stage position DFT assumes a square token grid"
    r8 = SUBLANES
    nblk = rows // r8
    fa, g, cs, perm, tw_cos, tw_sin = _fourier_constants(rows, f_w)
    u4 = u.reshape(bsz, rows, GRID_W, f_w)
    blocks_per_step = FOURIER_BLOCKS_PER_STEP
    steps = nblk // blocks_per_step
    width = blocks_per_step * r8
    stage_a_step = lambda b, t: jnp.minimum(t, steps - 1)
    stage_c_step = lambda b, t: jnp.maximum(t - 2 * steps, 0)
    full = lambda a: pl.BlockSpec(a.shape, lambda b, t: (0,) * a.ndim)
    tw_spec = pl.BlockSpec((blocks_per_step,) + tw_cos.shape[1:],
                           lambda b, t: (stage_a_step(b, t), 0, 0))
    return pl.pallas_call(
        functools.partial(_fourier_kernel, steps_per_stage=steps, blocks_per_step=blocks_per_step),
        grid=(bsz, 3 * steps),
        in_specs=[
            pl.BlockSpec((None, rows, width, f_w), lambda b, t: (b, 0, stage_a_step(b, t), 0)),
            full(fa), full(g), full(cs), full(perm), tw_spec, tw_spec,
        ],
        out_specs=pl.BlockSpec((None, width * GRID_W, f_w), lambda b, t: (b, stage_c_step(b, t), 0)),
        out_shape=jax.ShapeDtypeStruct((bsz, s, f_w), BF16),
        scratch_shapes=[pltpu.VMEM((rows, GRID_W, f_w), F32), pltpu.VMEM((rows, GRID_W, f_w), F32)],
        compiler_params=_params("arbitrary", "arbitrary"),
        name="fourier",
    )(u4, fa, g, cs, perm, tw_cos, tw_sin)


def _bias_tables(rpb, rows):
    kw = min(WIN_W, GRID_W)
    cols = np.arange(GRID_W)
    col_start = np.clip(cols - kw // 2, 0, GRID_W - kw)
    kj = cols[None, :]
    allowed = (kj >= col_start[:, None]) & (kj < col_start[:, None] + kw)
    rel = kj - cols[:, None] + (WIN_W - 1)
    nrel = 2 * WIN_W - 1
    onehot = ((rel[None] == np.arange(nrel)[:, None, None]) & allowed[None]).astype(np.float32)
    sel = np.zeros((2 * nrel + 1, GRID_W, 2 * GRID_W), np.float32)
    sel[0:nrel, :, 0:GRID_W] = onehot
    sel[nrel:2 * nrel, :, GRID_W:] = onehot
    sel[2 * nrel] = np.tile(~allowed, (1, 2))
    rpb = rpb.astype(F32) * LOG2_E
    mask_col = jnp.full(rpb.shape[:1] + (rpb.shape[1] - 1, 1), MASK_VALUE, F32)
    src = jnp.concatenate([rpb[:, :-1], rpb[:, 1:], mask_col], axis=-1)
    return jnp.einsum("hmd,dcx->hmcx", src, jnp.asarray(sel), precision=lax.Precision.HIGHEST)


def _attn_out_kernel(q_ref, k_ref, v_ref, t2_ref, yfn_ref, ga_ref, gb_ref, x_ref,
                     wna_ref, wfn_ref, bfn_ref, wo_ref, g1_ref, sc2_ref, sh2_ref, nw2_ref,
                     o_ref, h2_ref, yna_ref, s_ref, p_ref,
                     *, rows, kh, rows_per_step):
    t = pl.program_id(0)
    last_tile = pl.num_programs(0) - 2

    @pl.when(t == 0)
    def _init():
        yna_ref[...] = jnp.zeros_like(yna_ref)

    yna_prev = yna_ref[...].astype(BF16)
    yfn_prev = yfn_ref[...]

    rb = lax.rem(jnp.minimum(t, last_tile), rows // rows_per_step)
    n_pairs = q_ref.shape[1] // LANES
    lane = lax.broadcasted_iota(jnp.int32, (1, LANES), 1)
    low = lane < HEAD_DIM
    mask_lo = jnp.where(low, 1.0, 0.0).astype(BF16)
    mask_hi = jnp.where(low, 0.0, 1.0).astype(BF16)
    n_keys = kh * GRID_W

    key_start = []
    for j in range(rows_per_step):
        r = rb * rows_per_step + j
        rs = jnp.clip(r - kh // 2, 0, rows - kh)
        key_start.append((pl.multiple_of(rs * GRID_W, GRID_W), rs - r + (WIN_H - 1)))

    def qk_tile(ti):
        j, p = divmod(ti, n_pairs)
        ks, idx0 = key_start[j]
        cs = slice(LANES * p, LANES * (p + 1))
        q2 = q_ref[j * GRID_W:(j + 1) * GRID_W, cs]
        k2 = k_ref[pl.ds(ks, n_keys), cs]
        qs = jnp.concatenate([q2 * mask_lo, q2 * mask_hi], axis=0)
        s = lax.dot_general(qs, k2, (((1,), (1,)), ((), ())), preferred_element_type=F32)
        bias = jnp.concatenate(
            [jnp.concatenate([t2_ref[2 * p + hh, idx0 + 2 * m] for m in range(kh // 2)], axis=1)
             for hh in range(2)], axis=0)
        s_ref[ti] = s + bias

    def softmax_tile(ti):
        s = s_ref[ti]
        p_ref[ti] = jnp.exp2(s - jnp.max(s, axis=-1, keepdims=True)).astype(BF16)

    ones = jnp.ones((n_keys, LANES), BF16)
    row_outs = []

    def pv_tile(ti):
        j, p = divmod(ti, n_pairs)
        ks, _ = key_start[j]
        v2 = v_ref[pl.ds(ks, n_keys), LANES * p:LANES * (p + 1)]
        o = _dot(p_ref[ti], jnp.concatenate([v2, ones], axis=1))
        o = o[:, 0:LANES] * (1.0 / o[:, LANES:2 * LANES])
        row_outs.append(jnp.where(low, o[0:GRID_W], o[GRID_W:2 * GRID_W]))
        if p == n_pairs - 1:
            yna_ref[j * GRID_W:(j + 1) * GRID_W, :] = jnp.concatenate(row_outs, axis=1)
            row_outs.clear()

    d = o_ref.shape[1]
    merged = []

    def merge_chunk(c0):
        cols = slice(c0, c0 + MXU_DIM)
        ya = _dot(yna_prev, wna_ref[:, cols])
        yf = _dot(yfn_prev, wfn_ref[:, cols]) + bfn_ref[:, cols]
        merged.append((ga_ref[:, cols].astype(F32) * ya
                       + gb_ref[:, cols].astype(F32) * yf).astype(BF16))

    def out_chunk(c0):
        cols = slice(c0, c0 + MXU_DIM)
        lhs = jnp.concatenate(merged, axis=1)
        o_ref[:, cols] = x_ref[:, cols] + g1_ref[:, cols] * _dot(lhs, wo_ref[:, cols])

    norm_rows = o_ref.shape[0] // rows_per_step
    gain2 = nw2_ref[...] * (1.0 + sc2_ref[...])

    def norm_chunk(j):
        nr = slice(j * norm_rows, (j + 1) * norm_rows)
        x1 = o_ref[nr, :]
        ms = jnp.mean(x1 * x1, axis=-1, keepdims=True)
        h2_ref[nr, :] = (x1 * lax.rsqrt(ms + EPS) * gain2 + sh2_ref[...]).astype(BF16)

    matmul_units = ([functools.partial(merge_chunk, c0) for c0 in range(0, d, MXU_DIM)]
                    + [functools.partial(out_chunk, c0) for c0 in range(0, d, MXU_DIM)])
    norm_units = [functools.partial(norm_chunk, j) for j in range(rows_per_step)]

    n_tiles = rows_per_step * n_pairs
    for u in range(n_tiles + PV_SKEW):
        if u < n_tiles:
            qk_tile(u)
        if 0 <= u - SOFTMAX_SKEW < n_tiles:
            softmax_tile(u - SOFTMAX_SKEW)
        if 0 <= u - PV_SKEW < n_tiles:
            pv_tile(u - PV_SKEW)
        if matmul_units:
            if u % OUT_UNIT_EVERY == 1:
                matmul_units.pop(0)()
        elif norm_units and u % NORM_UNIT_EVERY == 0:
            norm_units.pop(0)()
    assert not matmul_units
    while norm_units:
        norm_units.pop(0)()


def _attn_out(q, k, v, t2, yfn, ga, gb, x, w_na, w_fn, b_fn, w_o, mod, norm2_w):
    bsz, s, d = x.shape
    na_w = q.shape[-1]
    f_w = yfn.shape[-1]
    rows = s // GRID_W
    kh = min(WIN_H, rows)
    assert kh % 2 == 0 and (2 * HEAD_DIM) == LANES
    rows_per_step = ATTN_ROWS_PER_STEP
    tq = rows_per_step * GRID_W
    n_tiles = rows_per_step * (na_w // LANES)

    per_seq = s // tq
    n_steps = bsz * per_seq + 1
    attn_tile = lambda t: jnp.minimum(t, n_steps - 2)
    out_tile = lambda t: jnp.maximum(t - 1, 0)
    tok_a = lambda w: pl.BlockSpec(
        (None, tq, w), lambda t: (attn_tile(t) // per_seq, attn_tile(t) % per_seq, 0))
    tok_o = lambda w: pl.BlockSpec(
        (None, tq, w), lambda t: (out_tile(t) // per_seq, out_tile(t) % per_seq, 0))
    whole_seq = pl.BlockSpec((None, s, na_w), lambda t: (attn_tile(t) // per_seq, 0, 0))
    mod_chunk = lambda c: pl.BlockSpec((None, 1, d), lambda t: (out_tile(t) // per_seq, 0, c))
    full = lambda a: pl.BlockSpec(a.shape, lambda t: (0,) * a.ndim)
    bfn = b_fn.reshape(1, d)
    nw2 = norm2_w.reshape(1, d)
    return pl.pallas_call(
        functools.partial(_attn_out_kernel, rows=rows, kh=kh, rows_per_step=rows_per_step),
        grid=(n_steps,),
        in_specs=[tok_a(na_w), whole_seq, whole_seq, full(t2), tok_o(f_w), tok_o(d), tok_o(d),
                  tok_o(d), full(w_na), full(w_fn), full(bfn), full(w_o), mod_chunk(MOD_GATE1),
                  mod_chunk(MOD_SCALE2), mod_chunk(MOD_SHIFT2), full(nw2)],
        out_specs=(tok_o(d), tok_o(d)),
        out_shape=(jax.ShapeDtypeStruct((bsz, s, d), F32), jax.ShapeDtypeStruct((bsz, s, d), BF16)),
        scratch_shapes=[pltpu.VMEM((tq, na_w), F32),
                        pltpu.VMEM((n_tiles, 2 * GRID_W, kh * GRID_W), F32),
                        pltpu.VMEM((n_tiles, 2 * GRID_W, kh * GRID_W), BF16)],
        compiler_params=_params("arbitrary"),
        name="attn_out",
    )(q, k, v, t2, yfn, ga, gb, x, w_na, w_fn, bfn, w_o, mod, mod, mod, nw2)


def _mlp_kernel(x_ref, h_ref, g2_ref, w1_ref, w2_ref, o_ref, *, chunk):
    h = h_ref[...]
    acc = None
    for c0 in range(0, w1_ref.shape[1], chunk):
        a = jnp.maximum(_dot(h, w1_ref[:, c0:c0 + chunk]), 0.0)
        term = _dot((a * a).astype(BF16), w2_ref[c0:c0 + chunk, :])
        acc = term if acc is None else acc + term
    o_ref[...] = x_ref[...] + g2_ref[...] * acc


def _mlp(x, h, mod, w1, w2):
    bsz, s, d = x.shape
    tm = MLP_TOKENS
    tok = pl.BlockSpec((None, tm, d), lambda b, i: (b, i, 0))
    gate2 = pl.BlockSpec((None, 1, d), lambda b, i: (b, 0, MOD_GATE2))
    full = lambda a: pl.BlockSpec(a.shape, lambda b, i: (0,) * a.ndim)
    return pl.pallas_call(
        functools.partial(_mlp_kernel, chunk=MLP_FF_CHUNK),
        grid=(bsz, s // tm),
        in_specs=[tok, tok, gate2, full(w1), full(w2)],
        out_specs=tok,
        out_shape=jax.ShapeDtypeStruct((bsz, s, d), F32),
        compiler_params=_params("parallel", "parallel"),
        name="mlp",
    )(x, h, mod, w1, w2)


def kernel(x, c, norm1_w, norm2_w, w_ada, b_ada, w_in, b_in, q_norm_w, k_norm_w, rpb,
           w_na_out, w_fn_out, b_fn_out, w_o, w_mlp_in, w_mlp_out):
    bsz, s, d = x.shape
    depth = w_ada.shape[0]
    na_w = w_na_out.shape[1]
    f_w = w_fn_out.shape[1]
    n_heads = na_w // HEAD_DIM
    rows = s // GRID_W
    for l in range(depth):
        mod, w_in_b = _ada(c, w_ada[l], b_ada[l], w_in[l], gate_start=3 * na_w + f_w)

        qk_w = jnp.concatenate([jnp.tile(q_norm_w[l], n_heads) * (HEAD_DIM ** -0.5 * LOG2_E),
                                jnp.tile(k_norm_w[l], n_heads)]).reshape(1, 2 * na_w)
        (q, k, v, u, ga, gb), (w_na, w_fn, w_o_b, w_m1, w_m2) = _inproj(
            x, mod, norm1_w[l], w_in_b, b_in[l], qk_w, na_w, f_w,
            (w_na_out[l], w_fn_out[l], w_o[l], w_mlp_in[l], w_mlp_out[l]))
        yfn = _fourier(u)
        t2 = _bias_tables(rpb[l], rows)
        x, h2 = _attn_out(q, k, v, t2, yfn, ga, gb, x, w_na, w_fn, b_fn_out[l], w_o_b, mod,
                          norm2_w[l])
        x = _mlp(x, h2, mod, w_m1, w_m2)
    return x
```
